```python
import jax
import jax.numpy as jnp
from jax import lax
import numpy as np

D_MODEL = 1024
BATCH = 2
SEQ = 8192
DEPTH = 2

GRID_W = 64
CTX_LEN = 256
EPS = 1e-6
D_FF = 4 * D_MODEL
N_MOD = 6

NA_HEADS = 8
NA_HEAD_DIM = D_MODEL // 16
NA_WIDTH = NA_HEADS * NA_HEAD_DIM
NA_KH = 8
NA_KW = 16

LRU_WIDTH = D_MODEL // 2
LRU_HEADS = 8
LRU_BLOCK = LRU_WIDTH // LRU_HEADS
LRU_CONV = 4
LRU_C = 8.0

GLA_HEADS = 4
GLA_DK = D_MODEL // 16
GLA_DV = D_MODEL // 8
GLA_KEY = GLA_HEADS * GLA_DK
GLA_VAL = GLA_HEADS * GLA_DV
GLA_RANK = 16
GLA_TAU = 16.0
GLA_CHUNK = 64

GQA_HEADS = 8
GQA_KV_HEADS = 2
GQA_HEAD_DIM = D_MODEL // 16
GQA_Q = GQA_HEADS * GQA_HEAD_DIM
GQA_KV = GQA_KV_HEADS * GQA_HEAD_DIM
Q_BLOCK = 128
ROPE_THETA = 10000.0

EVEN_IN = 3 * NA_WIDTH + 2 * LRU_WIDTH
EVEN_SPLITS = (NA_WIDTH, 2 * NA_WIDTH, 3 * NA_WIDTH, 3 * NA_WIDTH + LRU_WIDTH)
ODD_IN = 2 * GLA_KEY + 2 * GLA_VAL + 2 * GLA_RANK + GQA_Q + 2 * GQA_KV
ODD_SPLITS = (GLA_KEY, 2 * GLA_KEY, 2 * GLA_KEY + GLA_VAL, 2 * GLA_KEY + 2 * GLA_VAL,
              2 * GLA_KEY + 2 * GLA_VAL + 2 * GLA_RANK,
              2 * GLA_KEY + 2 * GLA_VAL + 2 * GLA_RANK + GQA_Q,
              2 * GLA_KEY + 2 * GLA_VAL + 2 * GLA_RANK + GQA_Q + GQA_KV)

kernel_name = 'hybrid_natten_rglru_gla_gqa_prefix'

F32 = jnp.float32


def rms_norm(x, w):
    xf = x.astype(F32)
    y = xf * lax.rsqrt(jnp.mean(xf * xf, axis=-1, keepdims=True) + EPS)
    return (y * w.astype(F32)).astype(x.dtype)


def modulate(x, shift, scale):
    return x * (1.0 + scale) + shift


def _split_heads(t, n_heads):
    return t.reshape(*t.shape[:-1], n_heads, t.shape[-1] // n_heads)


def _flip(t, direction):
    return t[:, ::-1] if direction else t


def squared_relu_mlp(u, w1, w2):
    return jnp.square(jax.nn.relu(u @ w1)) @ w2


def _rope_1d(x, pos):
    f = x.shape[-1] // 2
    inv = ROPE_THETA ** (-jnp.arange(f, dtype=F32) / f)
    ang = pos.astype(F32)[:, None] * inv[None, :]
    cos = jnp.cos(ang)[None, :, None, :]
    sin = jnp.sin(ang)[None, :, None, :]
    x1 = x[..., :f].astype(F32)
    x2 = x[..., f:].astype(F32)
    return jnp.concatenate([x1 * cos - x2 * sin, x2 * cos + x1 * sin], axis=-1)


def rope_2d(x, pos_row, pos_col):
    half = x.shape[-1] // 2
    return jnp.concatenate([_rope_1d(x[..., :half], pos_row),
                            _rope_1d(x[..., half:], pos_col)], axis=-1).astype(x.dtype)


def neighbourhood_attention(q_c, k_c, v_c, q_l, k_l, v_l, rpb, need_ctx):
    bsz, s, h, dh = q_l.shape
    rows = s // GRID_W
    kh = min(NA_KH, rows)
    scale = dh ** -0.5
    kg = k_l.reshape(bsz, rows, GRID_W, h, dh)
    vg = v_l.reshape(bsz, rows, GRID_W, h, dh)
    cols = jnp.arange(GRID_W)
    col_start = jnp.clip(cols - NA_KW // 2, 0, GRID_W - NA_KW)
    col_valid = (cols[None, :] >= col_start[:, None]) & (cols[None, :] < col_start[:, None] + NA_KW)
    d_col = jnp.clip(cols[None, :] - cols[:, None], 1 - NA_KW, NA_KW - 1) + (NA_KW - 1)

    def row_block(args):
        q_row, r = args
        start = jnp.clip(r - kh // 2, 0, rows - kh)
        k_rows = lax.dynamic_slice_in_dim(kg, start, kh, axis=1)
        v_rows = lax.dynamic_slice_in_dim(vg, start, kh, axis=1)
        d_row = start + jnp.arange(kh) - r + (NA_KH - 1)
        bias = rpb[:, d_row[None, :, None], d_col[:, None, :]]
        s_loc = jnp.einsum('bwhd,bkvhd->bhwkv', q_row, k_rows).astype(F32) * scale + bias.astype(F32)
        s_loc = jnp.where(col_valid[:, None, :], s_loc, -jnp.inf)
        s_ctx = jnp.einsum('bwhd,bchd->bhwc', q_row, k_c).astype(F32) * scale
        n_loc = kh * GRID_W
        p = jax.nn.softmax(jnp.concatenate([s_loc.reshape(bsz, h, GRID_W, n_loc), s_ctx], axis=-1), axis=-1)
        p = p.astype(v_l.dtype)
        p_loc = p[..., :n_loc].reshape(bsz, h, GRID_W, kh, GRID_W)
        return (jnp.einsum('bhwkv,bkvhd->bwhd', p_loc, v_rows)
                + jnp.einsum('bhwc,bchd->bwhd', p[..., n_loc:], v_c))

    q_rows = jnp.moveaxis(q_l.reshape(bsz, rows, GRID_W, h, dh), 1, 0)
    o = lax.map(row_block, (q_rows, jnp.arange(rows)))
    o_l = jnp.moveaxis(o, 0, 1).reshape(bsz, s, h * dh)
    o_c = None
    if need_ctx:
        sc = jnp.einsum('bqhd,bkhd->bhqk', q_c, k_c).astype(F32) * scale
        pc = jax.nn.softmax(sc, axis=-1).astype(v_c.dtype)
        o_c = jnp.einsum('bhqk,bkhd->bqhd', pc, v_c).reshape(bsz, q_c.shape[1], h * dh)
    return o_c, o_l


def depthwise_conv_centred(x, w, b):
    k, ch = w.shape
    y = lax.conv_general_dilated(x, w[:, None, :].astype(x.dtype), window_strides=(1,),
                                 padding=[(k // 2, k - 1 - k // 2)],
                                 dimension_numbers=('NWC', 'WIO', 'NWC'),
                                 feature_group_count=ch)
    return y + b


def rglru_gates(x, wa, ba, wx, bx, lam):
    bsz, t, ch = x.shape
    xb = x.reshape(bsz, t, LRU_HEADS, LRU_BLOCK)
    r = jax.nn.sigmoid((jnp.einsum('bthi,hij->bthj', xb, wa).reshape(bsz, t, ch) + ba).astype(F32))
    i = jax.nn.sigmoid((jnp.einsum('bthi,hij->bthj', xb, wx).reshape(bsz, t, ch) + bx).astype(F32))
    log_a = -LRU_C * r * jax.nn.softplus(-lam.astype(F32))
    a = jnp.exp(log_a)
    b = jnp.sqrt(-jnp.expm1(2.0 * log_a)) * (i * x.astype(F32))
    return a, b


def linear_recurrence(a, b, h0):
    b = b.at[:, 0].add(a[:, 0] * h0)

    def combine(left, right):
        a_l, b_l = left
        a_r, b_r = right
        return a_l * a_r, a_r * b_l + b_r

    _, h = lax.associative_scan(combine, (a, b), axis=1)
    return h


def rglru_mixer(x_c, g_c, x_l, g_l, conv_w, conv_b, wa, ba, wx, bx, lam, need_ctx):
    xc = depthwise_conv_centred(x_c, conv_w, conv_b)
    xl = depthwise_conv_centred(x_l, conv_w, conv_b)
    bsz = x_l.shape[0]
    ys_c, ys_l = [], []
    for d in range(2):
        a_c, b_c = rglru_gates(_flip(xc, d), wa[d], ba[d], wx[d], bx[d], lam[d])
        h_c = linear_recurrence(a_c, b_c, jnp.zeros((bsz, LRU_WIDTH), F32))
        a_l, b_l = rglru_gates(_flip(xl, d), wa[d], ba[d], wx[d], bx[d], lam[d])
        h_l = linear_recurrence(a_l, b_l, h_c[:, -1])
        ys_c.append(_flip(h_c, d))
        ys_l.append(_flip(h_l, d))
    out_l = (ys_l[0] + ys_l[1]).astype(x_l.dtype) * jax.nn.gelu(g_l)
    out_c = (ys_c[0] + ys_c[1]).astype(x_c.dtype) * jax.nn.gelu(g_c) if need_ctx else None
    return out_c, out_l


def even_mixer(u_c, u_l, w_in, rpb, conv_w, conv_b, wa, ba, wx, bx, lam, w_out, need_ctx):
    qc, kc, vc, xc, gc = jnp.split(u_c @ w_in, EVEN_SPLITS, axis=-1)
    ql, kl, vl, xl, gl = jnp.split(u_l @ w_in, EVEN_SPLITS, axis=-1)
    hs = lambda t: _split_heads(t, NA_HEADS)
    na_c, na_l = neighbourhood_attention(hs(qc), hs(kc), hs(vc), hs(ql), hs(kl), hs(vl), rpb, need_ctx)
    lru_c, lru_l = rglru_mixer(xc, gc, xl, gl, conv_w, conv_b, wa, ba, wx, bx, lam, need_ctx)
    y_l = jnp.concatenate([na_l, lru_l], axis=-1) @ w_out
    y_c = jnp.concatenate([na_c, lru_c], axis=-1) @ w_out if need_ctx else None
    return y_c, y_l


def gla_chunked(q, k, v, log_a, s0):
    bsz, t, h, dk = q.shape
    dv = v.shape[-1]
    n = t // GLA_CHUNK
    rs = lambda z: z.reshape(bsz, n, GLA_CHUNK, h, z.shape[-1]).transpose(1, 0, 3, 2, 4)
    q, k, v, g = rs(q), rs(k), rs(v), rs(log_a)
    b = jnp.cumsum(g, axis=3)
    b_last = b[..., -1:, :]
    q_in = q * jnp.exp(b)
    k_in = k * jnp.exp(-b)
    mask = jnp.tril(jnp.ones((GLA_CHUNK, GLA_CHUNK), dtype=bool))
    att = jnp.where(mask, jnp.einsum('nbhtd,nbhsd->nbhts', q_in, k_in), 0.0)
    o_intra = jnp.einsum('nbhts,nbhsv->nbhtv', att, v)
    u = jnp.einsum('nbhsd,nbhsv->nbhdv', k * jnp.exp(b_last - b), v)
    decay = jnp.exp(b_last[..., 0, :])

    def step(state, inp):
        dec, u_c = inp
        return dec[..., None] * state + u_c, state

    s_final, s_prev = lax.scan(step, s0, (decay, u))
    o = o_intra + jnp.einsum('nbhtd,nbhdv->nbhtv', q_in, s_prev)
    return o.transpose(1, 0, 3, 2, 4).reshape(bsz, t, h, dv), s_final


def gla_mixer(zc, zl, wa2, ba, norm_w, need_ctx):
    def prep(z, d):
        q, k, v, _, lr = z
        qf = _split_heads(q, GLA_HEADS).astype(F32) * (GLA_DK ** -0.5)
        kf = _split_heads(k, GLA_HEADS).astype(F32)
        vf = _split_heads(v, GLA_HEADS).astype(F32)
        lr_d = lr[..., d * GLA_RANK:(d + 1) * GLA_RANK]
        log_a = jax.nn.log_sigmoid((lr_d @ wa2[d] + ba[d]).astype(F32)) / GLA_TAU
        log_a = _split_heads(log_a, GLA_HEADS)
        return _flip(qf, d), _flip(kf, d), _flip(vf, d), _flip(log_a, d)

    bsz = zl[0].shape[0]
    outs_c, outs_l = [], []
    for d in range(2):
        s0 = jnp.zeros((bsz, GLA_HEADS, GLA_DK, GLA_DV), F32)
        o_c, s_ctx = gla_chunked(*prep(zc, d), s0)
        o_l, _ = gla_chunked(*prep(zl, d), s_ctx)
        outs_c.append(_flip(o_c, d))
        outs_l.append(_flip(o_l, d))

    def finish(o, gate):
        o = rms_norm(o.astype(gate.dtype), norm_w) * jax.nn.silu(_split_heads(gate, GLA_HEADS))
        return o.reshape(*o.shape[:-2], GLA_VAL)

    y_l = finish(outs_l[0] + outs_l[1], zl[3])
    y_c = finish(outs_c[0] + outs_c[1], zc[3]) if need_ctx else None
    return y_c, y_l


def gqa_mixer(zc, zl, q_norm_w, k_norm_w, pos_row, pos_col, need_ctx):
    group = GQA_HEADS // GQA_KV_HEADS
    scale = GQA_HEAD_DIM ** -0.5

    def heads(z):
        q, k, v = z
        q = rms_norm(_split_heads(q, GQA_HEADS), q_norm_w)
        k = rms_norm(_split_heads(k, GQA_KV_HEADS), k_norm_w)
        return q, k, _split_heads(v, GQA_KV_HEADS)

    qc, kc, vc = heads(zc)
    ql, kl, vl = heads(zl)
    ql = rope_2d(ql, pos_row, pos_col)
    kl = rope_2d(kl, pos_row, pos_col)
    bsz, s = ql.shape[:2]
    k_all = jnp.concatenate([kc, kl], axis=1)
    v_all = jnp.concatenate([vc, vl], axis=1)

    def to_groups(q):
        return q.reshape(q.shape[0], q.shape[1], GQA_KV_HEADS, group, GQA_HEAD_DIM).transpose(0, 2, 3, 1, 4)

    def attend(q_grp, k, v):
        sc = jnp.einsum('bkgqd,bskd->bkgqs', q_grp, k).astype(F32) * scale
        p = jax.nn.softmax(sc, axis=-1).astype(v.dtype)
        return jnp.einsum('bkgqs,bskd->bkgqd', p, v)

    nb = s // Q_BLOCK
    q_blocks = to_groups(ql).reshape(bsz, GQA_KV_HEADS, group, nb, Q_BLOCK, GQA_HEAD_DIM).transpose(3, 0, 1, 2, 4, 5)
    o = lax.map(lambda qb: attend(qb, k_all, v_all), q_blocks)
    o_l = o.transpose(1, 0, 4, 2, 3, 5).reshape(bsz, s, GQA_Q)
    o_c = None
    if need_ctx:
        oc = attend(to_groups(qc), kc, vc)
        o_c = oc.transpose(0, 3, 1, 2, 4).reshape(bsz, qc.shape[1], GQA_Q)
    return o_c, o_l


def odd_mixer(u_c, u_l, w_in, wa2, ba, gla_norm_w, q_norm_w, k_norm_w, w_out, pos_row, pos_col, need_ctx):
    zc = jnp.split(u_c @ w_in, ODD_SPLITS, axis=-1)
    zl = jnp.split(u_l @ w_in, ODD_SPLITS, axis=-1)
    gla_c, gla_l = gla_mixer(zc[:5], zl[:5], wa2, ba, gla_norm_w, need_ctx)
    gqa_c, gqa_l = gqa_mixer(zc[5:], zl[5:], q_norm_w, k_norm_w, pos_row, pos_col, need_ctx)
    y_l = jnp.concatenate([gla_l, gqa_l], axis=-1) @ w_out
    y_c = jnp.concatenate([gla_c, gqa_c], axis=-1) @ w_out if need_ctx else None
    return y_c, y_l


def setup_inputs(seed: int = 0) -> dict:
    key = jax.random.key(seed)
    keys = jax.random.split(key, 32)
    counter = iter(range(32))
    n_even = (DEPTH + 1) // 2
    n_odd = DEPTH // 2

    def nrm(shape, scale):
        return scale * jax.random.normal(keys[next(counter)], shape, F32)

    def gain(shape):
        return 1.0 + 0.1 * jax.random.normal(keys[next(counter)], shape, F32)

    a_pow = jax.random.uniform(keys[next(counter)], (n_even, 2, LRU_WIDTH), F32, 0.9, 0.999)
    a = a_pow ** (1.0 / LRU_C)
    lru_lambda = jnp.log(a) - jnp.log1p(-a)
    return {
        'x': nrm((BATCH, SEQ, D_MODEL), 1.0),
        'c': nrm((BATCH, D_MODEL), 1.0),
        'ctx': nrm((BATCH, CTX_LEN, D_MODEL), 1.0),
        'c_ctx': nrm((D_MODEL,), 0.5),
        'norm1_w': gain((DEPTH, D_MODEL)),
        'norm2_w': gain((DEPTH, D_MODEL)),
        'w_mod': nrm((DEPTH, D_MODEL, N_MOD * D_MODEL), 0.5 * D_MODEL ** -0.5),
        'b_mod': nrm((DEPTH, N_MOD * D_MODEL), 0.02),
        'w_ff1': nrm((DEPTH, D_MODEL, D_FF), D_MODEL ** -0.5),
        'w_ff2': nrm((DEPTH, D_FF, D_MODEL), D_FF ** -0.5),
        'w_in_even': nrm((n_even, D_MODEL, EVEN_IN), D_MODEL ** -0.5),
        'na_rpb': nrm((n_even, NA_HEADS, 2 * NA_KH - 1, 2 * NA_KW - 1), 0.1),
        'lru_conv_w': nrm((n_even, LRU_CONV, LRU_WIDTH), LRU_CONV ** -0.5),
        'lru_conv_b': nrm((n_even, LRU_WIDTH), 0.02),
        'lru_wa': nrm((n_even, 2, LRU_HEADS, LRU_BLOCK, LRU_BLOCK), LRU_BLOCK ** -0.5),
        'lru_ba': nrm((n_even, 2, LRU_WIDTH), 0.02),
        'lru_wx': nrm((n_even, 2, LRU_HEADS, LRU_BLOCK, LRU_BLOCK), LRU_BLOCK ** -0.5),
        'lru_bx': nrm((n_even, 2, LRU_WIDTH), 0.02),
        'lru_lambda': lru_lambda,
        'w_out_even': nrm((n_even, NA_WIDTH + LRU_WIDTH, D_MODEL), (NA_WIDTH + LRU_WIDTH) ** -0.5),
        'w_in_odd': nrm((n_odd, D_MODEL, ODD_IN), D_MODEL ** -0.5),
        'gla_wa2': nrm((n_odd, 2, GLA_RANK, GLA_KEY), GLA_RANK ** -0.5),
        'gla_ba': nrm((n_odd, 2, GLA_KEY), 0.1),
        'gla_norm_w': gain((n_odd, GLA_DV)),
        'gqa_q_norm_w': gain((n_odd, GQA_HEAD_DIM)),
        'gqa_k_norm_w': gain((n_odd, GQA_HEAD_DIM)),
        'w_out_odd': nrm((n_odd, GLA_VAL + GQA_Q, D_MODEL), (GLA_VAL + GQA_Q) ** -0.5),
        'final_norm_w': gain((D_MODEL,)),
    }


def reference(x, c, ctx, c_ctx, norm1_w, norm2_w, w_mod, b_mod, w_ff1, w_ff2,
              w_in_even, na_rpb, lru_conv_w, lru_conv_b, lru_wa, lru_ba, lru_wx, lru_bx, lru_lambda, w_out_even,
              w_in_odd, gla_wa2, gla_ba, gla_norm_w, gqa_q_norm_w, gqa_k_norm_w, w_out_odd, final_norm_w):
    s = x.shape[1]
    t = jnp.arange(s, dtype=jnp.int32)
    pos_row = t // GRID_W
    pos_col = t % GRID_W
    silu_c = jax.nn.silu(c)
    silu_cc = jax.nn.silu(c_ctx)
    h_l, h_c = x, ctx
    for i in range(DEPTH):
        need_ctx = i < DEPTH - 1
        mod_l = jnp.split((silu_c @ w_mod[i] + b_mod[i])[:, None, :], N_MOD, axis=-1)
        mod_c = jnp.split(silu_cc @ w_mod[i] + b_mod[i], N_MOD, axis=-1)
        u_l = modulate(rms_norm(h_l, norm1_w[i]), mod_l[0], mod_l[1])
        u_c = modulate(rms_norm(h_c, norm1_w[i]), mod_c[0], mod_c[1])
        j = i // 2
        if i % 2 == 0:
            y_c, y_l = even_mixer(u_c, u_l, w_in_even[j], na_rpb[j], lru_conv_w[j], lru_conv_b[j],
                                  lru_wa[j], lru_ba[j], lru_wx[j], lru_bx[j], lru_lambda[j],
                                  w_out_even[j], need_ctx)
        else:
            y_c, y_l = odd_mixer(u_c, u_l, w_in_odd[j], gla_wa2[j], gla_ba[j], gla_norm_w[j],
                                 gqa_q_norm_w[j], gqa_k_norm_w[j], w_out_odd[j], pos_row, pos_col, need_ctx)
        h_l = h_l + mod_l[2] * y_l
        h_l = h_l + mod_l[5] * squared_relu_mlp(
            modulate(rms_norm(h_l, norm2_w[i]), mod_l[3], mod_l[4]), w_ff1[i], w_ff2[i])
        if need_ctx:
            h_c = h_c + mod_c[2] * y_c
            h_c = h_c + mod_c[5] * squared_relu_mlp(
                modulate(rms_norm(h_c, norm2_w[i]), mod_c[3], mod_c[4]), w_ff1[i], w_ff2[i])
    return rms_norm(h_l, final_norm_w)
```

```python
import functools

import numpy as np
import jax
import jax.numpy as jnp
from jax import lax
from jax.experimental import pallas as pl
from jax.experimental.pallas import tpu as pltpu

F32 = jnp.float32
BF16 = jnp.bfloat16

EPS = 1e-6
GRID_W = 64
LANES = 128
N_MOD = 6
TOK = 256
NEG = -1e30

NA_HEADS = 8
NA_DH = 64
NA_KH = 8
NA_KW = 16
LRU_C = 8.0
LRU_SEG = 8
GLA_HEADS = 4
GLA_DK = 64
GLA_DV = 128
GLA_RANK = 16
GLA_TAU = 16.0
GLA_CHUNK = 64
GQA_HEADS = 8
GQA_KV_HEADS = 2
GQA_DH = 64
ROPE_THETA = 10000.0

VMEM_LIMIT = 56 * 1024 * 1024


def _cparams(sem):
    return pltpu.CompilerParams(dimension_semantics=sem, vmem_limit_bytes=VMEM_LIMIT)


def _const_spec(shape):
    nd = len(shape)
    return pl.BlockSpec(shape, lambda *_: (0,) * nd, pipeline_mode=pl.Buffered(1))


def _dot(a, b):
    return jnp.dot(a, b, preferred_element_type=F32)


def _dot_nt(a, b):
    return lax.dot_general(a, b, (((1,), (1,)), ((), ())), preferred_element_type=F32)


def _dot_tn(a, b):
    return lax.dot_general(a, b, (((0,), (0,)), ((), ())), preferred_element_type=F32)


def _split_hi_lo(x):
    hi = x.astype(BF16)
    lo = (x - hi.astype(F32)).astype(BF16)
    return hi, lo


def _rms(x, w):
    ms = jnp.mean(x * x, axis=-1, keepdims=True)
    return x * lax.rsqrt(ms + EPS) * w


def _softplus(z):
    return jnp.maximum(z, 0.0) + jnp.log1p(jnp.exp(-jnp.abs(z)))


def _mod_kernel(c_ref, w_ref, b_ref, o_ref):
    c = c_ref[...]
    s = c * jax.nn.sigmoid(c)
    o_ref[0] = jnp.dot(s, w_ref[0], precision=lax.Precision.HIGHEST, preferred_element_type=F32) + b_ref[0]


def _modulation(cvec, w_mod, b_mod):
    depth, d, _ = w_mod.shape
    rows = cvec.shape[0]
    out = pl.pallas_call(
        _mod_kernel,
        grid=(depth, N_MOD),
        in_specs=[
            pl.BlockSpec((rows, d), lambda i, j: (0, 0)),
            pl.BlockSpec((1, d, d), lambda i, j: (i, 0, j)),
            pl.BlockSpec((1, 1, d), lambda i, j: (i, 0, j)),
        ],
        out_specs=pl.BlockSpec((1, rows, d), lambda i, j: (i, 0, j)),
        out_shape=jax.ShapeDtypeStruct((depth, rows, N_MOD * d), F32),
        compiler_params=_cparams(("arbitrary", "arbitrary")),
        name="modulation",
    )(cvec, w_mod, b_mod.reshape(depth, 1, N_MOD * d))
    return out.reshape(depth, rows, N_MOD, d)


def _mod_spec(d, n_lat_tiles):
    return pl.BlockSpec((1, N_MOD, d), lambda b, t: (jnp.where(t >= n_lat_tiles, 0, b + 1), 0, 0))


def _inproj_kernel(h_ref, nw_ref, mod_ref, w_ref, *o_refs, splits):
    mod = mod_ref[0]
    u = _rms(h_ref[0], nw_ref[...]) * (1.0 + mod[1:2]) + mod[0:1]
    z = _dot(u.astype(BF16), w_ref[...])
    for o_ref, (lo, hi) in zip(o_refs, splits):
        o_ref[0] = z[:, lo:hi].astype(o_ref.dtype)


def _inproj(h, norm_w, mods, w, splits, dtypes, n_lat_tiles):
    bsz, t, d = h.shape
    n = w.shape[1]
    nt = t // TOK
    return pl.pallas_call(
        functools.partial(_inproj_kernel, splits=splits),
        grid=(bsz, nt),
        in_specs=[
            pl.BlockSpec((1, TOK, d), lambda b, i: (b, i, 0)),
            _const_spec((1, d)),
            _mod_spec(d, n_lat_tiles),
            _const_spec((d, n)),
        ],
        out_specs=[pl.BlockSpec((1, TOK, hi - lo), lambda b, i: (b, i, 0)) for lo, hi in splits],
        out_shape=[jax.ShapeDtypeStruct((bsz, t, hi - lo), dt) for (lo, hi), dt in zip(splits, dtypes)],
        compiler_params=_cparams(("parallel", "parallel")),
        name="inproj",
    )(h, norm_w.reshape(1, d), mods, w)


def _mlp_kernel(h_ref, ma_ref, mb_ref, mod_ref, woa_ref, wob_ref, nw_ref, w1_ref, w2_ref, fw_ref, o_ref,
                *, ff_chunk, final):
    mod = mod_ref[0]
    y = _dot(ma_ref[0].astype(BF16), woa_ref[...]) + _dot(mb_ref[0].astype(BF16), wob_ref[...])
    h1 = h_ref[0] + mod[2:3] * y
    u = (_rms(h1, nw_ref[...]) * (1.0 + mod[4:5]) + mod[3:4]).astype(BF16)
    d_ff = w1_ref.shape[1]
    acc = jnp.zeros(h1.shape, F32)
    for c in range(d_ff // ff_chunk):
        sl = slice(c * ff_chunk, (c + 1) * ff_chunk)
        a = jnp.maximum(_dot(u, w1_ref[:, sl]), 0.0)
        acc = acc + _dot((a * a).astype(BF16), w2_ref[sl, :])
    h2 = h1 + mod[5:6] * acc
    if final:
        h2 = _rms(h2, fw_ref[...])
    o_ref[0] = h2


def _mlp(h, mix_a, mix_b, mods, wo_a, wo_b, norm_w, w1, w2, final_w, n_lat_tiles, n_tiles, final):
    bsz, _, d = h.shape
    da, db = mix_a.shape[-1], mix_b.shape[-1]
    d_ff = w1.shape[1]
    tok_spec = lambda width: pl.BlockSpec((1, TOK, width), lambda b, i: (b, i, 0))
    return pl.pallas_call(
        functools.partial(_mlp_kernel, ff_chunk=512, final=final),
        grid=(bsz, n_tiles),
        in_specs=[
            tok_spec(d), tok_spec(da), tok_spec(db),
            _mod_spec(d, n_lat_tiles),
            _const_spec((da, d)), _const_spec((db, d)), _const_spec((1, d)),
            _const_spec((d, d_ff)), _const_spec((d_ff, d)), _const_spec((1, d)),
        ],
        out_specs=tok_spec(d),
        out_shape=jax.ShapeDtypeStruct((bsz, n_tiles * TOK, d), F32),
        compiler_params=_cparams(("parallel", "parallel")),
        name="mlp",
    )(h, mix_a, mix_b, mods, wo_a, wo_b, norm_w.reshape(1, d), w1, w2, final_w.reshape(1, d))


def _na_bias_table(rpb):
    cols = np.arange(GRID_W)
    col_start = np.clip(cols - NA_KW // 2, 0, GRID_W - NA_KW)
    col_valid = (cols[None, :] >= col_start[:, None]) & (cols[None, :] < col_start[:, None] + NA_KW)
    d_col = np.clip(cols[None, :] - cols[:, None], 1 - NA_KW, NA_KW - 1) + (NA_KW - 1)
    d_row = np.arange(NA_KH)[:, None] + np.arange(NA_KH)[None, :]
    bias = rpb[:, d_row[:, None, :, None], d_col[None, :, None, :]]
    bias = jnp.where(col_valid[None, None, :, None, :], bias, NEG)
    return bias.transpose(1, 0, 2, 3, 4).reshape(NA_KH, rpb.shape[0], GRID_W, NA_KH * GRID_W)


def _na_window_start(r, rows):
    return jnp.clip(r - NA_KH // 2, 0, rows - NA_KH)


def _na_kernel(q_ref, k_ref, v_ref, kc_ref, vc_ref, bias_ref, o_ref, *, rows):
    r = pl.program_id(1)
    is_lat = r < rows
    start = jnp.where(is_lat, _na_window_start(r, rows), 0)
    row0 = pl.multiple_of(start * GRID_W, GRID_W)
    win = NA_KH * GRID_W
    q = q_ref[0] * 0.125
    kw = k_ref[0, pl.ds(row0, win), :]
    vw = v_ref[0, pl.ds(row0, win), :]
    kc = kc_ref[0]
    vc = vc_ref[0]
    for h in range(NA_HEADS):
        sl = slice(h * NA_DH, (h + 1) * NA_DH)
        qh = q[:, sl]
        s_ctx = _dot_nt(qh, kc[:, sl])
        s_loc = jnp.where(is_lat, _dot_nt(qh, kw[:, sl]) + bias_ref[0, h], NEG)
        m = jnp.maximum(jnp.max(s_loc, axis=-1, keepdims=True), jnp.max(s_ctx, axis=-1, keepdims=True))
        p_loc = jnp.exp(s_loc - m)
        p_ctx = jnp.exp(s_ctx - m)
        denom = jnp.sum(p_loc, axis=-1, keepdims=True) + jnp.sum(p_ctx, axis=-1, keepdims=True)
        o = _dot(p_loc.astype(BF16), vw[:, sl]) + _dot(p_ctx.astype(BF16), vc[:, sl])
        o_ref[0, :, sl] = o / denom


def _na(qkv, rpb, s_len, c_len):
    bsz, t, _ = qkv.shape
    width = NA_HEADS * NA_DH
    rows = s_len // GRID_W
    assert rows >= NA_KH and s_len % c_len == 0 and c_len % GRID_W == 0
    bias = _na_bias_table(rpb)

    def bias_idx(b, r):
        off = jnp.where(r < rows, _na_window_start(r, rows) - r + (NA_KH - 1), 0)
        return (off, 0, 0, 0)

    return pl.pallas_call(
        functools.partial(_na_kernel, rows=rows),
        grid=(bsz, t // GRID_W),
        in_specs=[
            pl.BlockSpec((1, GRID_W, width), lambda b, r: (b, r, 0)),
            pl.BlockSpec((1, s_len, width), lambda b, r: (b, 0, 1), pipeline_mode=pl.Buffered(1)),
            pl.BlockSpec((1, s_len, width), lambda b, r: (b, 0, 2), pipeline_mode=pl.Buffered(1)),
            pl.BlockSpec((1, c_len, width), lambda b, r: (b, s_len // c_len, 1)),
            pl.BlockSpec((1, c_len, width), lambda b, r: (b, s_len // c_len, 2)),
            pl.BlockSpec((1, NA_HEADS, GRID_W, NA_KH * GRID_W), bias_idx),
        ],
        out_specs=pl.BlockSpec((1, GRID_W, width), lambda b, r: (b, r, 0)),
        out_shape=jax.ShapeDtypeStruct((bsz, t, width), F32),
        compiler_params=_cparams(("parallel", "arbitrary")),
        name="na",
    )(qkv, qkv, qkv, qkv, qkv, bias)


def _lru_tile_index(i, n_lat, n_tiles, direction):
    if direction == 0:
        return (i + n_lat) % n_tiles
    return n_tiles - 1 - i


def _lru_kernel(*refs, direction, n_lat, n_tiles, combine):
    if combine:
        (x_ref, prev_ref, next_ref, cw_ref, cb_ref, wg_ref, bg_ref, lam_ref, g_ref, other_ref,
         o_ref, ext_s, a_s, b_s, carry_s) = refs
    else:
        (x_ref, prev_ref, next_ref, cw_ref, cb_ref, wg_ref, bg_ref, lam_ref,
         o_ref, ext_s, a_s, b_s, carry_s) = refs
    i = pl.program_id(1)
    tile = _lru_tile_index(i, n_lat, n_tiles, direction)
    ch = x_ref.shape[-1]
    halo = prev_ref.shape[2]
    seg = TOK // LRU_SEG

    @pl.when(i == 0)
    def _():
        carry_s[...] = jnp.zeros_like(carry_s)

    first = jnp.logical_or(tile == 0, tile == n_lat)
    last = jnp.logical_or(tile == n_lat - 1, tile == n_tiles - 1)
    x = x_ref[0]
    ext_s[0:halo, :] = jnp.where(first, 0.0, prev_ref[0, 0])
    ext_s[halo:halo + TOK, :] = x
    ext_s[halo + TOK:, :] = jnp.where(last, 0.0, next_ref[0, 0])
    cw = cw_ref[...]
    xc = (cw[0:1] * ext_s[halo - 2:halo - 2 + TOK, :] + cw[1:2] * ext_s[halo - 1:halo - 1 + TOK, :]
          + cw[2:3] * x + cw[3:4] * ext_s[halo + 1:halo + 1 + TOK, :] + cb_ref[...])

    pre = _dot(xc.astype(BF16), wg_ref[0]) + bg_ref[0]
    r_gate = jax.nn.sigmoid(pre[:, :ch])
    i_gate = jax.nn.sigmoid(pre[:, ch:])
    log_a = (-LRU_C) * r_gate * _softplus(-lam_ref[0])
    a = jnp.exp(log_a)
    bb = jnp.sqrt(1.0 - a * a) * (i_gate * xc)
    state_all = carry_s[...]
    seg_order = range(LRU_SEG) if direction == 0 else range(LRU_SEG - 1, -1, -1)
    step_order = range(seg) if direction == 0 else range(seg - 1, -1, -1)
    new_state = []
    for cs in range(ch // LANES):
        lanes = slice(cs * LANES, (cs + 1) * LANES)
        a_s[cs] = a[:, lanes]
        b_s[cs] = bb[:, lanes]
        h = jnp.zeros((LRU_SEG, LANES), F32)
        cum = jnp.ones((LRU_SEG, LANES), F32)
        for j in step_order:
            rows_j = pl.ds(j, LRU_SEG, stride=seg)
            a_j = a_s[cs, rows_j, :]
            h = a_j * h + b_s[cs, rows_j, :]
            cum = cum * a_j
            b_s[cs, rows_j, :] = h
            a_s[cs, rows_j, :] = cum
        state = state_all[:, lanes]
        for s in seg_order:
            rows_s = slice(s * seg, (s + 1) * seg)
            y = b_s[cs, rows_s, :] + a_s[cs, rows_s, :] * state
            state = h[s:s + 1, :] + cum[s:s + 1, :] * state
            if combine:
                y = (y + other_ref[0, rows_s, lanes]) * jax.nn.gelu(g_ref[0, rows_s, lanes])
            o_ref[0, rows_s, lanes] = y
        new_state.append(state)
    state = jnp.concatenate(new_state, axis=1)
    carry_s[...] = state


def _lru(xg, conv_w, conv_b, w_gate, b_gate, lam, n_lat, direction, other=None):
    bsz, t, two_ch = xg.shape
    ch = two_ch // 2
    n_tiles = t // TOK
    halo = 8
    combine = other is not None
    tile_of = functools.partial(_lru_tile_index, n_lat=n_lat, n_tiles=n_tiles, direction=direction)
    per_tile = TOK // halo
    xg_halo = xg.reshape(bsz, t // halo, halo, two_ch)
    in_specs = [
        pl.BlockSpec((1, TOK, ch), lambda b, i: (b, tile_of(i), 0)),
        pl.BlockSpec((1, 1, halo, ch), lambda b, i: (b, jnp.maximum(tile_of(i) * per_tile - 1, 0), 0, 0)),
        pl.BlockSpec((1, 1, halo, ch),
                     lambda b, i: (b, jnp.minimum((tile_of(i) + 1) * per_tile, t // halo - 1), 0, 0)),
        _const_spec((4, ch)), _const_spec((1, ch)),
        pl.BlockSpec((1, ch, 2 * ch), lambda b, i: (direction, 0, 0), pipeline_mode=pl.Buffered(1)),
        pl.BlockSpec((1, 1, 2 * ch), lambda b, i: (direction, 0, 0), pipeline_mode=pl.Buffered(1)),
        pl.BlockSpec((1, 1, ch), lambda b, i: (direction, 0, 0), pipeline_mode=pl.Buffered(1)),
    ]
    args = [xg, xg_halo, xg_halo, conv_w, conv_b.reshape(1, ch), w_gate, b_gate, lam.reshape(2, 1, ch)]
    if combine:
        in_specs += [pl.BlockSpec((1, TOK, ch), lambda b, i: (b, tile_of(i), 1)),
                     pl.BlockSpec((1, TOK, ch), lambda b, i: (b, tile_of(i), 0))]
        args += [xg, other]
    return pl.pallas_call(
        functools.partial(_lru_kernel, direction=direction, n_lat=n_lat, n_tiles=n_tiles, combine=combine),
        grid=(bsz, n_tiles),
        in_specs=in_specs,
        out_specs=pl.BlockSpec((1, TOK, ch), lambda b, i: (b, tile_of(i), 0)),
        out_shape=jax.ShapeDtypeStruct((bsz, t, ch), F32),
        scratch_shapes=[pltpu.VMEM((TOK + 2 * halo, ch), F32), pltpu.VMEM((ch // LANES, TOK, LANES), F32),
                        pltpu.VMEM((ch // LANES, TOK, LANES), F32), pltpu.VMEM((1, ch), F32)],
        compiler_params=_cparams(("parallel", "arbitrary")),
        name="lru_fwd" if direction == 0 else "lru_bwd",
    )(*args)


def _block_diag(w):
    heads, n, _ = w.shape
    eye = jnp.eye(heads, dtype=w.dtype)
    return (eye[:, None, :, None] * w[:, :, None, :]).reshape(heads * n, heads * n)


def _gla_kernel(*refs, direction, combine):
    if combine:
        qkvg_ref, lr_ref, wa_ref, ba_ref, tri_ref, other_ref, nw_ref, o_ref, state_s = refs
    else:
        qkvg_ref, lr_ref, wa_ref, ba_ref, tri_ref, o_ref, state_s = refs
    key = GLA_HEADS * GLA_DK
    val = GLA_HEADS * GLA_DV
    nchunk = TOK // GLA_CHUNK

    @pl.when(pl.program_id(1) == 0)
    def _():
        state_s[...] = jnp.zeros_like(state_s)

    z = qkvg_ref[0]
    q = z[:, :key] * (GLA_DK ** -0.5)
    k = z[:, key:2 * key]
    v = z[:, 2 * key:2 * key + val].astype(BF16)
    lr = lr_ref[0][:, direction * GLA_RANK:(direction + 1) * GLA_RANK]
    logit = _dot(lr.astype(BF16), wa_ref[0].astype(BF16)) + ba_ref[0]
    log_a = (jnp.minimum(logit, 0.0) - jnp.log1p(jnp.exp(-jnp.abs(logit)))) / GLA_TAU

    g_hi, g_lo = _split_hi_lo(log_a)
    tri = tri_ref[...]
    b = _dot(tri, g_hi) + _dot(tri, g_lo)
    q_in = (q * jnp.exp(b)).astype(BF16)
    k_in = (k * jnp.exp(-b)).astype(BF16)
    t_idx = lax.broadcasted_iota(jnp.int32, (GLA_CHUNK, GLA_CHUNK), 0)
    s_idx = lax.broadcasted_iota(jnp.int32, (GLA_CHUNK, GLA_CHUNK), 1)
    causal = (s_idx <= t_idx) if direction == 0 else (s_idx >= t_idx)
    ones = jnp.ones((GLA_CHUNK, GLA_DV), BF16)

    chunk_order = range(nchunk) if direction == 0 else range(nchunk - 1, -1, -1)
    for c in chunk_order:
        rows_c = slice(c * GLA_CHUNK, (c + 1) * GLA_CHUNK)
        end = (c + 1) * GLA_CHUNK - 1 if direction == 0 else c * GLA_CHUNK
        b_c = b[rows_c]
        b_end = b[end:end + 1]
        k_out = (k[rows_c] * jnp.exp(b_end - b_c)).astype(BF16)
        for h in range(GLA_HEADS):
            ks = slice(h * GLA_DK, (h + 1) * GLA_DK)
            vs = slice(h * GLA_DV, (h + 1) * GLA_DV)
            v_h = v[rows_c, vs]
            att = jnp.where(causal, _dot_nt(q_in[rows_c, ks], k_in[rows_c, ks]), 0.0)
            state = state_s[h]
            o = _dot(att.astype(BF16), v_h) + _dot(q_in[rows_c, ks], state.astype(BF16))
            dec = _dot_tn(g_hi[rows_c, ks], ones) + _dot_tn(g_lo[rows_c, ks], ones)
            state_s[h] = jnp.exp(dec) * state + _dot_tn(k_out[:, ks], v_h)
            if combine:
                o = o + other_ref[0, rows_c, vs]
                gate = z[rows_c, 2 * key + val + h * GLA_DV:2 * key + val + (h + 1) * GLA_DV]
                o = _rms(o, nw_ref[...]) * (gate * jax.nn.sigmoid(gate))
            o_ref[0, rows_c, vs] = o


def _gla(qkvg, lr, wa2, ba, n_lat, direction, other=None, norm_w=None):
    bsz, t, width = qkvg.shape
    key = GLA_HEADS * GLA_DK
    val = GLA_HEADS * GLA_DV
    n_tiles = t // TOK
    combine = other is not None
    tile_of = functools.partial(_lru_tile_index, n_lat=n_lat, n_tiles=n_tiles, direction=direction)
    pos = np.arange(TOK)
    same = (pos[:, None] // GLA_CHUNK) == (pos[None, :] // GLA_CHUNK)
    tri = same & ((pos[None, :] <= pos[:, None]) if direction == 0 else (pos[None, :] >= pos[:, None]))
    tri = jnp.asarray(tri, BF16)
    in_specs = [
        pl.BlockSpec((1, TOK, width), lambda b, i: (b, tile_of(i), 0)),
        pl.BlockSpec((1, TOK, lr.shape[-1]), lambda b, i: (b, tile_of(i), 0)),
        pl.BlockSpec((1, GLA_RANK, key), lambda b, i: (direction, 0, 0), pipeline_mode=pl.Buffered(1)),
        pl.BlockSpec((1, 1, key), lambda b, i: (direction, 0, 0), pipeline_mode=pl.Buffered(1)),
        _const_spec((TOK, TOK)),
    ]
    args = [qkvg, lr, wa2, ba.reshape(2, 1, key), tri]
    if combine:
        in_specs += [pl.BlockSpec((1, TOK, val), lambda b, i: (b, tile_of(i), 0)), _const_spec((1, GLA_DV))]
        args += [other, norm_w.reshape(1, GLA_DV)]
    return pl.pallas_call(
        functools.partial(_gla_kernel, direction=direction, combine=combine),
        grid=(bsz, n_tiles),
        in_specs=in_specs,
        out_specs=pl.BlockSpec((1, TOK, val), lambda b, i: (b, tile_of(i), 0)),
        out_shape=jax.ShapeDtypeStruct((bsz, t, val), F32),
        scratch_shapes=[pltpu.VMEM((GLA_HEADS, GLA_DK, GLA_DV), F32)],
        compiler_params=_cparams(("parallel", "arbitrary")),
        name="gla_fwd" if direction == 0 else "gla_bwd",
    )(*args)


def _rope_tables(s_len, c_len, reps):
    quarter = GQA_DH // 4
    inv = ROPE_THETA ** (-np.arange(quarter, dtype=np.float32) / quarter)
    t = np.arange(s_len)
    ang_r = (t // GRID_W).astype(np.float32)[:, None] * inv[None, :]
    ang_c = (t % GRID_W).astype(np.float32)[:, None] * inv[None, :]
    cos = np.concatenate([np.cos(ang_r)] * 2 + [np.cos(ang_c)] * 2, axis=1)
    sin = np.concatenate([-np.sin(ang_r), np.sin(ang_r), -np.sin(ang_c), np.sin(ang_c)], axis=1)
    cos = np.concatenate([cos, np.ones((c_len, GQA_DH), np.float32)], axis=0)
    sin = np.concatenate([sin, np.zeros((c_len, GQA_DH), np.float32)], axis=0)
    return (jnp.asarray(np.tile(cos, (1, reps)), F32), jnp.asarray(np.tile(sin, (1, reps)), F32))


def _rope_swap(x):
    lane = lax.broadcasted_iota(jnp.int32, x.shape, 1)
    return jnp.where(lane % 32 < 16, pltpu.roll(x, 128 - 16, axis=1), pltpu.roll(x, 16, axis=1))


def _norm_rope(x, nw, cos, sin, gmean):
    hi, lo = _split_hi_lo(x * x)
    ms = _dot(hi, gmean) + _dot(lo, gmean)
    y = x * lax.rsqrt(ms + EPS) * nw
    slabs = [_rope_swap(y[:, j:j + 128]) for j in range(0, y.shape[1], 128)]
    swapped = slabs[0] if len(slabs) == 1 else jnp.concatenate(slabs, axis=1)
    return y * cos + swapped * sin


def _gqa_prep_kernel(z_ref, qw_ref, kw_ref, cos_ref, sin_ref, gq_ref, gk_ref, q_ref, k_ref, v_ref):
    qd = GQA_HEADS * GQA_DH
    kd = GQA_KV_HEADS * GQA_DH
    z = z_ref[0]
    cos_k = cos_ref[...]
    sin_k = sin_ref[...]
    reps = qd // kd
    cos_q = jnp.concatenate([cos_k] * reps, axis=1)
    sin_q = jnp.concatenate([sin_k] * reps, axis=1)
    q = _norm_rope(z[:, :qd], qw_ref[...], cos_q, sin_q, gq_ref[...]) * (GQA_DH ** -0.5)
    k = _norm_rope(z[:, qd:qd + kd], kw_ref[...], cos_k, sin_k, gk_ref[...])
    v = z[:, qd + kd:qd + 2 * kd]
    for h in range(GQA_HEADS):
        q_ref[0, h] = q[:, h * GQA_DH:(h + 1) * GQA_DH].astype(BF16)
    for h in range(GQA_KV_HEADS):
        k_ref[0, h] = k[:, h * GQA_DH:(h + 1) * GQA_DH].astype(BF16)
        v_ref[0, h] = v[:, h * GQA_DH:(h + 1) * GQA_DH].astype(BF16)


def _group_mean_matrix(n, group):
    idx = np.arange(n) // group
    return jnp.asarray((idx[:, None] == idx[None, :]) / group, BF16)


def _gqa_prep(z, q_norm_w, k_norm_w, s_len, c_len):
    bsz, t, width = z.shape
    qd = GQA_HEADS * GQA_DH
    kd = GQA_KV_HEADS * GQA_DH
    cos, sin = _rope_tables(s_len, c_len, GQA_KV_HEADS)
    head_spec = lambda n: pl.BlockSpec((1, n, TOK, GQA_DH), lambda b, i: (b, 0, i, 0))
    return pl.pallas_call(
        _gqa_prep_kernel,
        grid=(bsz, t // TOK),
        in_specs=[
            pl.BlockSpec((1, TOK, width), lambda b, i: (b, i, 0)),
            _const_spec((1, qd)), _const_spec((1, kd)),
            pl.BlockSpec((TOK, kd), lambda b, i: (i, 0)),
            pl.BlockSpec((TOK, kd), lambda b, i: (i, 0)),
            _const_spec((qd, qd)), _const_spec((kd, kd)),
        ],
        out_specs=[head_spec(GQA_HEADS), head_spec(GQA_KV_HEADS), head_spec(GQA_KV_HEADS)],
        out_shape=[jax.ShapeDtypeStruct((bsz, GQA_HEADS, t, GQA_DH), BF16),
                   jax.ShapeDtypeStruct((bsz, GQA_KV_HEADS, t, GQA_DH), BF16),
                   jax.ShapeDtypeStruct((bsz, GQA_KV_HEADS, t, GQA_DH), BF16)],
        compiler_params=_cparams(("parallel", "parallel")),
        name="gqa_prep",
    )(z, jnp.tile(q_norm_w, GQA_HEADS).reshape(1, qd), jnp.tile(k_norm_w, GQA_KV_HEADS).reshape(1, kd),
      cos, sin, _group_mean_matrix(qd, GQA_DH), _group_mean_matrix(kd, GQA_DH))


def _flash_kernel(q_ref, k_ref, v_ref, o_ref, m_s, l_s, acc_s, *, group):
    ki = pl.program_id(3)

    @pl.when(ki == 0)
    def _():
        m_s[...] = jnp.full_like(m_s, -jnp.inf)
        l_s[...] = jnp.zeros_like(l_s)
        acc_s[...] = jnp.zeros_like(acc_s)

    k = k_ref[0, 0]
    v = v_ref[0, 0]
    for g in range(group):
        s = _dot_nt(q_ref[0, g], k)
        m_prev = m_s[g]
        m_new = jnp.maximum(m_prev, jnp.max(s, axis=-1, keepdims=True))
        alpha = jnp.exp(m_prev - m_new)
        p = jnp.exp(s - m_new)
        l_s[g] = alpha * l_s[g] + jnp.sum(p, axis=-1, keepdims=True)
        acc_s[g] = alpha * acc_s[g] + _dot(p.astype(BF16), v)
        m_s[g] = m_new

    @pl.when(ki == pl.num_programs(3) - 1)
    def _():
        for g in range(group):
            o_ref[0, :, g * GQA_DH:(g + 1) * GQA_DH] = acc_s[g] / l_s[g]


def _kv_tile(t):
    return max(n * TOK for n in (1, 2, 3) if t % (n * TOK) == 0)


def _flash(q, k, v, s_len, tq, tk):
    bsz, _, t, dh = q.shape
    assert t % tk == 0 and s_len % tq == 0
    group = GQA_HEADS // GQA_KV_HEADS
    return pl.pallas_call(
        functools.partial(_flash_kernel, group=group),
        grid=(bsz, GQA_KV_HEADS, s_len // tq, t // tk),
        in_specs=[
            pl.BlockSpec((1, group, tq, dh), lambda b, h, i, j: (b, h, i, 0)),
            pl.BlockSpec((1, 1, tk, dh), lambda b, h, i, j: (b, h, j, 0)),
            pl.BlockSpec((1, 1, tk, dh), lambda b, h, i, j: (b, h, j, 0)),
        ],
        out_specs=pl.BlockSpec((1, tq, group * dh), lambda b, h, i, j: (b, i, h)),
        out_shape=jax.ShapeDtypeStruct((bsz, s_len, GQA_HEADS * dh), F32),
        scratch_shapes=[pltpu.VMEM((group, tq, 1), F32), pltpu.VMEM((group, tq, 1), F32),
                        pltpu.VMEM((group, tq, dh), F32)],
        compiler_params=_cparams(("parallel", "parallel", "parallel", "arbitrary")),
        name="gqa_flash",
    )(q, k, v)


def kernel(x, c, ctx, c_ctx, norm1_w, norm2_w, w_mod, b_mod, w_ff1, w_ff2,
           w_in_even, na_rpb, lru_conv_w, lru_conv_b, lru_wa, lru_ba, lru_wx, lru_bx, lru_lambda, w_out_even,
           w_in_odd, gla_wa2, gla_ba, gla_norm_w, gqa_q_norm_w, gqa_k_norm_w, w_out_odd, final_norm_w):
    bsz, s_len, d = x.shape
    c_len = ctx.shape[1]
    depth = w_mod.shape[0]
    assert s_len % TOK == 0 and c_len % TOK == 0 and depth == 2
    n_lat = s_len // TOK
    n_tiles = (s_len + c_len) // TOK

    pad = (-(bsz + 1)) % 8
    cvec = jnp.concatenate([c_ctx[None], c, jnp.zeros((pad, d), F32)], axis=0)
    mods = _modulation(cvec, w_mod, b_mod)
    h = jnp.concatenate([x, ctx], axis=1)

    na_w = NA_HEADS * NA_DH
    ch = lru_lambda.shape[-1]
    qkv, xg = _inproj(h, norm1_w[0], mods[0], w_in_even[0].astype(BF16),
                      ((0, 3 * na_w), (3 * na_w, 3 * na_w + 2 * ch)), (BF16, F32), n_lat)
    na_out = _na(qkv, na_rpb[0], s_len, c_len)
    w_gate = jnp.stack([jnp.concatenate([_block_diag(lru_wa[0, dr]), _block_diag(lru_wx[0, dr])], axis=1)
                        for dr in range(2)]).astype(BF16)
    b_gate = jnp.concatenate([lru_ba[0], lru_bx[0]], axis=-1).reshape(2, 1, 2 * ch)
    lru_args = (xg, lru_conv_w[0], lru_conv_b[0], w_gate, b_gate, lru_lambda[0], n_lat)
    lru_b = _lru(*lru_args, direction=1)
    lru_out = _lru(*lru_args, direction=0, other=lru_b)
    wo = w_out_even[0].astype(BF16)
    h = _mlp(h, na_out, lru_out, mods[0], wo[:na_w], wo[na_w:], norm2_w[0],
             w_ff1[0].astype(BF16), w_ff2[0].astype(BF16), final_norm_w, n_lat, n_tiles, final=False)

    key = GLA_HEADS * GLA_DK
    val = GLA_HEADS * GLA_DV
    gla_w = 2 * key + 2 * val
    gqa_w = (GQA_HEADS + 2 * GQA_KV_HEADS) * GQA_DH
    w_in = w_in_odd[0]
    w_in = jnp.concatenate([w_in[:, :gla_w], w_in[:, gla_w + 2 * GLA_RANK:], w_in[:, gla_w:gla_w + 2 * GLA_RANK]],
                           axis=1).astype(BF16)
    qkvg, gqa_z, lr = _inproj(h, norm1_w[1], mods[1], w_in,
                              ((0, gla_w), (gla_w, gla_w + gqa_w), (gla_w + gqa_w, gla_w + gqa_w + 2 * GLA_RANK)),
                              (F32, F32, F32), n_lat)
    gla_b = _gla(qkvg, lr, gla_wa2[0], gla_ba[0], n_lat, direction=1)
    gla_out = _gla(qkvg, lr, gla_wa2[0], gla_ba[0], n_lat, direction=0, other=gla_b, norm_w=gla_norm_w[0])
    q, k, v = _gqa_prep(gqa_z, gqa_q_norm_w[0], gqa_k_norm_w[0], s_len, c_len)
    gqa_out = _flash(q, k, v, s_len, tq=TOK, tk=_kv_tile(s_len + c_len))
    wo = w_out_odd[0].astype(BF16)
    return _mlp(h, gla_out, gqa_out, mods[1], wo[:val], wo[val:], norm2_w[1],
                w_ff1[1].astype(BF16), w_ff2[1].astype(BF16), final_norm_w, n_lat, n_lat, final=True)
```

```python
import functools

import numpy as np
import jax
import jax.numpy as jnp
from jax import lax
from jax.experimental import pallas as pl
from jax.experimental.pallas import tpu as pltpu

F32 = jnp.float32
BF16 = jnp.bfloat16

EPS = 1e-6
GRID_W = 64
LANES = 128
N_MOD = 6
TOK = 256
NEG = -1e30

NA_HEADS = 8
NA_DH = 64
NA_KH = 8
NA_KW = 16
LRU_C = 8.0
LRU_SEG = 8
GLA_HEADS = 4
GLA_DK = 64
GLA_DV = 128
GLA_RANK = 16
GLA_TAU = 16.0
GLA_CHUNK = 64
GQA_HEADS = 8
GQA_KV_HEADS = 2
GQA_DH = 64
ROPE_THETA = 10000.0
LOG2E = 1.4426950408889634

VMEM_LIMIT = 56 * 1024 * 1024


def _cparams(sem):
    return pltpu.CompilerParams(dimension_semantics=sem, vmem_limit_bytes=VMEM_LIMIT)


def _const_spec(shape):
    nd = len(shape)
    return pl.BlockSpec(shape, lambda *_: (0,) * nd, pipeline_mode=pl.Buffered(1))


def _dot(a, b):
    return jnp.dot(a, b, preferred_element_type=F32)


def _dot_nt(a, b):
    return lax.dot_general(a, b, (((1,), (1,)), ((), ())), preferred_element_type=F32)


def _dot_tn(a, b):
    return lax.dot_general(a, b, (((0,), (0,)), ((), ())), preferred_element_type=F32)


def _split_hi_lo(x):
    hi = x.astype(BF16)
    lo = (x - hi.astype(F32)).astype(BF16)
    return hi, lo


def _rms(x, w):
    ms = jnp.mean(x * x, axis=-1, keepdims=True)
    return x * lax.rsqrt(ms + EPS) * w


def _softplus(z):
    return jnp.maximum(z, 0.0) + jnp.log1p(jnp.exp(-jnp.abs(z)))


def _mod_kernel(c_ref, w_ref, b_ref, o_ref):
    c = c_ref[...]
    s = c * jax.nn.sigmoid(c)
    o_ref[0] = jnp.dot(s, w_ref[0], precision=lax.Precision.HIGHEST, preferred_element_type=F32) + b_ref[0]


def _modulation(cvec, w_mod, b_mod):
    depth, d, _ = w_mod.shape
    rows = cvec.shape[0]
    out = pl.pallas_call(
        _mod_kernel,
        grid=(depth, N_MOD),
        in_specs=[
            pl.BlockSpec((rows, d), lambda i, j: (0, 0)),
            pl.BlockSpec((1, d, d), lambda i, j: (i, 0, j)),
            pl.BlockSpec((1, 1, d), lambda i, j: (i, 0, j)),
        ],
        out_specs=pl.BlockSpec((1, rows, d), lambda i, j: (i, 0, j)),
        out_shape=jax.ShapeDtypeStruct((depth, rows, N_MOD * d), F32),
        compiler_params=_cparams(("arbitrary", "arbitrary")),
        name="modulation",
    )(cvec, w_mod, b_mod.reshape(depth, 1, N_MOD * d))
    return out.reshape(depth, rows, N_MOD, d)


def _mod_spec(d, n_lat_tiles):
    return pl.BlockSpec((1, N_MOD, d), lambda b, t: (jnp.where(t >= n_lat_tiles, 0, b + 1), 0, 0))


def _inproj_kernel(h_ref, nw_ref, mod_ref, w_ref, *o_refs, splits):
    mod = mod_ref[0]
    u = _rms(h_ref[0], nw_ref[...]) * (1.0 + mod[1:2]) + mod[0:1]
    z = _dot(u.astype(BF16), w_ref[...])
    for o_ref, (lo, hi) in zip(o_refs, splits):
        o_ref[0] = z[:, lo:hi].astype(o_ref.dtype)


def _inproj(h, norm_w, mods, w, splits, dtypes, n_lat_tiles):
    bsz, t, d = h.shape
    n = w.shape[1]
    nt = t // TOK
    return pl.pallas_call(
        functools.partial(_inproj_kernel, splits=splits),
        grid=(bsz, nt),
        in_specs=[
            pl.BlockSpec((1, TOK, d), lambda b, i: (b, i, 0)),
            _const_spec((1, d)),
            _mod_spec(d, n_lat_tiles),
            _const_spec((d, n)),
        ],
        out_specs=[pl.BlockSpec((1, TOK, hi - lo), lambda b, i: (b, i, 0)) for lo, hi in splits],
        out_shape=[jax.ShapeDtypeStruct((bsz, t, hi - lo), dt) for (lo, hi), dt in zip(splits, dtypes)],
        compiler_params=_cparams(("parallel", "parallel")),
        name="inproj",
    )(h, norm_w.reshape(1, d), mods, w)


def _mlp_kernel(h_ref, ma_ref, mb_ref, mod_ref, woa_ref, wob_ref, nw_ref, w1_ref, w2_ref, fw_ref, o_ref,
                *, ff_chunk, final):
    mod = mod_ref[0]
    y = _dot(ma_ref[0].astype(BF16), woa_ref[...]) + _dot(mb_ref[0].astype(BF16), wob_ref[...])
    h1 = h_ref[0] + mod[2:3] * y
    u = (_rms(h1, nw_ref[...]) * (1.0 + mod[4:5]) + mod[3:4]).astype(BF16)
    d_ff = w1_ref.shape[1]
    acc = jnp.zeros(h1.shape, F32)
    for c in range(d_ff // ff_chunk):
        sl = slice(c * ff_chunk, (c + 1) * ff_chunk)
        a = jnp.maximum(_dot(u, w1_ref[:, sl]), 0.0)
        acc = acc + _dot((a * a).astype(BF16), w2_ref[sl, :])
    h2 = h1 + mod[5:6] * acc
    if final:
        h2 = _rms(h2, fw_ref[...])
    o_ref[0] = h2


def _mlp(h, mix_a, mix_b, mods, wo_a, wo_b, norm_w, w1, w2, final_w, n_lat_tiles, n_tiles, final):
    bsz, _, d = h.shape
    da, db = mix_a.shape[-1], mix_b.shape[-1]
    d_ff = w1.shape[1]
    tok_spec = lambda width: pl.BlockSpec((1, TOK, width), lambda b, i: (b, i, 0))
    return pl.pallas_call(
        functools.partial(_mlp_kernel, ff_chunk=512, final=final),
        grid=(bsz, n_tiles),
        in_specs=[
            tok_spec(d), tok_spec(da), tok_spec(db),
            _mod_spec(d, n_lat_tiles),
            _const_spec((da, d)), _const_spec((db, d)), _const_spec((1, d)),
            _const_spec((d, d_ff)), _const_spec((d_ff, d)), _const_spec((1, d)),
        ],
        out_specs=tok_spec(d),
        out_shape=jax.ShapeDtypeStruct((bsz, n_tiles * TOK, d), F32),
        compiler_params=_cparams(("parallel", "parallel")),
        name="mlp",
    )(h, mix_a, mix_b, mods, wo_a, wo_b, norm_w.reshape(1, d), w1, w2, final_w.reshape(1, d))


NA_QROWS = TOK // GRID_W
NA_WIN = NA_KH + NA_QROWS


def _na_plan(rows):
    n_blk = rows // NA_QROWS
    r = np.arange(rows).reshape(n_blk, NA_QROWS)
    start = np.clip(r - NA_KH // 2, 0, rows - NA_KH)
    u0 = np.clip(r[:, 0] - NA_KH // 2, 0, rows - NA_WIN)
    key = u0[:, None, None] + np.arange(NA_WIN)[None, None, :]
    valid = (key >= start[:, :, None]) & (key < start[:, :, None] + NA_KH)
    d_row = np.where(valid, key - r[:, :, None] + (NA_KH - 1), 0)
    flat = np.concatenate([valid.reshape(n_blk, -1), d_row.reshape(n_blk, -1)], axis=1)
    _, first, pat = np.unique(flat, axis=0, return_index=True, return_inverse=True)
    return u0.astype(np.int32), pat.reshape(-1).astype(np.int32), valid[first], d_row[first]


def _na_bias_table(rpb, valid, d_row):
    n_pat = valid.shape[0]
    heads = rpb.shape[0]
    cols = np.arange(GRID_W)
    col_start = np.clip(cols - NA_KW // 2, 0, GRID_W - NA_KW)
    col_valid = (cols[None, :] >= col_start[:, None]) & (cols[None, :] < col_start[:, None] + NA_KW)
    d_col = np.clip(cols[None, :] - cols[:, None], 1 - NA_KW, NA_KW - 1) + (NA_KW - 1)
    row_sel = (d_row[..., None] == np.arange(2 * NA_KH - 1)).astype(np.float32)
    col_sel = (d_col[None] == np.arange(2 * NA_KW - 1)[:, None, None]).astype(np.float32)
    hp = lax.Precision.HIGHEST
    by_row = jnp.einsum('hrc,piur->phiuc', rpb, row_sel, precision=hp)
    bias = jnp.einsum('phiuc,cqw->phiquw', by_row, col_sel, precision=hp)
    keep = valid[:, None, :, None, :, None] & col_valid[None, None, None, :, None, :]
    return jnp.where(keep, bias, NEG).reshape(n_pat, heads, TOK, NA_WIN * GRID_W)


def _softmax_tiles(tiles):
    m = jnp.max(functools.reduce(jnp.maximum, tiles), axis=-1, keepdims=True)
    p = [jnp.exp(t - m) for t in tiles]
    denom = jnp.sum(functools.reduce(jnp.add, p), axis=-1, keepdims=True)
    return [t.astype(BF16) for t in p], denom


def _lane_tiles(x):
    return [x[:, j:j + LANES] for j in range(0, x.shape[1], LANES)]


def _na_kernel(u0_ref, pat_ref, q_ref, k_ref, v_ref, kc_ref, vc_ref, bias_ref, o_ref, *, n_lat):
    i = pl.program_id(1)
    is_lat = i < n_lat
    row0 = pl.multiple_of(u0_ref[i] * GRID_W, GRID_W)
    win = NA_WIN * GRID_W
    q = q_ref[0] * 0.125
    kw = k_ref[0, pl.ds(row0, win), :]
    vw = v_ref[0, pl.ds(row0, win), :]
    kc = kc_ref[0]
    vc = vc_ref[0]

    def scores(h):
        sl = slice(h * NA_DH, (h + 1) * NA_DH)
        s_loc = jnp.where(is_lat, _dot_nt(q[:, sl], kw[:, sl]) + bias_ref[0, h], NEG)
        return _lane_tiles(s_loc) + _lane_tiles(_dot_nt(q[:, sl], kc[:, sl]))

    s_next = scores(0)
    for h in range(NA_HEADS):
        sl = slice(h * NA_DH, (h + 1) * NA_DH)
        tiles = s_next
        if h + 1 < NA_HEADS:
            s_next = scores(h + 1)
        p, denom = _softmax_tiles(tiles)
        n_loc = win // LANES
        o = (_dot(jnp.concatenate(p[:n_loc], axis=1), vw[:, sl])
             + _dot(jnp.concatenate(p[n_loc:], axis=1), vc[:, sl]))
        o_ref[0, :, sl] = o / denom


def _na(qkv, rpb, s_len, c_len):
    bsz, t, _ = qkv.shape
    width = NA_HEADS * NA_DH
    rows = s_len // GRID_W
    n_lat = s_len // TOK
    n_steps = t // TOK
    assert rows >= NA_WIN and rows % NA_QROWS == 0 and c_len % TOK == 0 and s_len % c_len == 0
    u0, pat, valid, d_row = _na_plan(rows)
    bias = _na_bias_table(rpb, valid, d_row)
    ctx_steps = np.zeros(n_steps - n_lat, np.int32)
    u0 = jnp.asarray(np.concatenate([u0, ctx_steps]))
    pat = jnp.asarray(np.concatenate([pat, ctx_steps]))
    tok_spec = pl.BlockSpec((1, TOK, width), lambda b, i, u0_ref, pat_ref: (b, i, 0))
    ctx_blk = s_len // c_len
    return pl.pallas_call(
        functools.partial(_na_kernel, n_lat=n_lat),
        grid_spec=pltpu.PrefetchScalarGridSpec(
            num_scalar_prefetch=2,
            grid=(bsz, n_steps),
            in_specs=[
                tok_spec,
                pl.BlockSpec((1, s_len, width), lambda b, i, u0_ref, pat_ref: (b, 0, 1),
                             pipeline_mode=pl.Buffered(1)),
                pl.BlockSpec((1, s_len, width), lambda b, i, u0_ref, pat_ref: (b, 0, 2),
                             pipeline_mode=pl.Buffered(1)),
                pl.BlockSpec((1, c_len, width), lambda b, i, u0_ref, pat_ref: (b, ctx_blk, 1)),
                pl.BlockSpec((1, c_len, width), lambda b, i, u0_ref, pat_ref: (b, ctx_blk, 2)),
                pl.BlockSpec((1, NA_HEADS, TOK, NA_WIN * GRID_W),
                             lambda b, i, u0_ref, pat_ref: (pat_ref[i], 0, 0, 0)),
            ],
            out_specs=tok_spec,
        ),
        out_shape=jax.ShapeDtypeStruct((bsz, t, width), F32),
        compiler_params=_cparams(("parallel", "arbitrary")),
        name="na",
    )(u0, pat, qkv, qkv, qkv, qkv, qkv, bias)


def _lru_tile_index(i, n_lat, n_tiles, direction):
    if direction == 0:
        return (i + n_lat) % n_tiles
    return n_tiles - 1 - i


def _lru_kernel(*refs, direction, n_lat, n_tiles, combine):
    if combine:
        (x_ref, prev_ref, next_ref, cw_ref, cb_ref, wg_ref, bg_ref, lam_ref, g_ref, other_ref,
         o_ref, ext_s, a_s, b_s, carry_s) = refs
    else:
        (x_ref, prev_ref, next_ref, cw_ref, cb_ref, wg_ref, bg_ref, lam_ref,
         o_ref, ext_s, a_s, b_s, carry_s) = refs
    i = pl.program_id(1)
    tile = _lru_tile_index(i, n_lat, n_tiles, direction)
    ch = x_ref.shape[-1]
    halo = prev_ref.shape[2]
    seg = TOK // LRU_SEG

    @pl.when(i == 0)
    def _():
        carry_s[...] = jnp.zeros_like(carry_s)

    first = jnp.logical_or(tile == 0, tile == n_lat)
    last = jnp.logical_or(tile == n_lat - 1, tile == n_tiles - 1)
    x = x_ref[0]
    ext_s[0:halo, :] = jnp.where(first, 0.0, prev_ref[0, 0])
    ext_s[halo:halo + TOK, :] = x
    ext_s[halo + TOK:, :] = jnp.where(last, 0.0, next_ref[0, 0])
    cw = cw_ref[...]
    xc = (cw[0:1] * ext_s[halo - 2:halo - 2 + TOK, :] + cw[1:2] * ext_s[halo - 1:halo - 1 + TOK, :]
          + cw[2:3] * x + cw[3:4] * ext_s[halo + 1:halo + 1 + TOK, :] + cb_ref[...])

    pre = _dot(xc.astype(BF16), wg_ref[0]) + bg_ref[0]
    r_gate = jax.nn.sigmoid(pre[:, :ch])
    i_gate = jax.nn.sigmoid(pre[:, ch:])
    log_a = (-LRU_C) * r_gate * _softplus(-lam_ref[0])
    a = jnp.exp(log_a)
    bb = jnp.sqrt(1.0 - a * a) * (i_gate * xc)
    state_all = carry_s[...]
    seg_order = range(LRU_SEG) if direction == 0 else range(LRU_SEG - 1, -1, -1)
    step_order = range(seg) if direction == 0 else range(seg - 1, -1, -1)
    new_state = []
    for cs in range(ch // LANES):
        lanes = slice(cs * LANES, (cs + 1) * LANES)
        a_s[cs] = a[:, lanes]
        b_s[cs] = bb[:, lanes]
        h = jnp.zeros((LRU_SEG, LANES), F32)
        cum = jnp.ones((LRU_SEG, LANES), F32)
        for j in step_order:
            rows_j = pl.ds(j, LRU_SEG, stride=seg)
            a_j = a_s[cs, rows_j, :]
            h = a_j * h + b_s[cs, rows_j, :]
            cum = cum * a_j
            b_s[cs, rows_j, :] = h
            a_s[cs, rows_j, :] = cum
        state = state_all[:, lanes]
        for s in seg_order:
            rows_s = slice(s * seg, (s + 1) * seg)
            y = b_s[cs, rows_s, :] + a_s[cs, rows_s, :] * state
            state = h[s:s + 1, :] + cum[s:s + 1, :] * state
            if combine:
                y = (y + other_ref[0, rows_s, lanes]) * jax.nn.gelu(g_ref[0, rows_s, lanes])
            o_ref[0, rows_s, lanes] = y
        new_state.append(state)
    state = jnp.concatenate(new_state, axis=1)
    carry_s[...] = state


def _lru(xg, conv_w, conv_b, w_gate, b_gate, lam, n_lat, direction, other=None):
    bsz, t, two_ch = xg.shape
    ch = two_ch // 2
    n_tiles = t // TOK
    halo = 8
    combine = other is not None
    tile_of = functools.partial(_lru_tile_index, n_lat=n_lat, n_tiles=n_tiles, direction=direction)
    per_tile = TOK // halo
    xg_halo = xg.reshape(bsz, t // halo, halo, two_ch)
    in_specs = [
        pl.BlockSpec((1, TOK, ch), lambda b, i: (b, tile_of(i), 0)),
        pl.BlockSpec((1, 1, halo, ch), lambda b, i: (b, jnp.maximum(tile_of(i) * per_tile - 1, 0), 0, 0)),
        pl.BlockSpec((1, 1, halo, ch),
                     lambda b, i: (b, jnp.minimum((tile_of(i) + 1) * per_tile, t // halo - 1), 0, 0)),
        _const_spec((4, ch)), _const_spec((1, ch)),
        pl.BlockSpec((1, ch, 2 * ch), lambda b, i: (direction, 0, 0), pipeline_mode=pl.Buffered(1)),
        pl.BlockSpec((1, 1, 2 * ch), lambda b, i: (direction, 0, 0), pipeline_mode=pl.Buffered(1)),
        pl.BlockSpec((1, 1, ch), lambda b, i: (direction, 0, 0), pipeline_mode=pl.Buffered(1)),
    ]
    args = [xg, xg_halo, xg_halo, conv_w, conv_b.reshape(1, ch), w_gate, b_gate, lam.reshape(2, 1, ch)]
    if combine:
        in_specs += [pl.BlockSpec((1, TOK, ch), lambda b, i: (b, tile_of(i), 1)),
                     pl.BlockSpec((1, TOK, ch), lambda b, i: (b, tile_of(i), 0))]
        args += [xg, other]
    return pl.pallas_call(
        functools.partial(_lru_kernel, direction=direction, n_lat=n_lat, n_tiles=n_tiles, combine=combine),
        grid=(bsz, n_tiles),
        in_specs=in_specs,
        out_specs=pl.BlockSpec((1, TOK, ch), lambda b, i: (b, tile_of(i), 0)),
        out_shape=jax.ShapeDtypeStruct((bsz, t, ch), F32),
        scratch_shapes=[pltpu.VMEM((TOK + 2 * halo, ch), F32), pltpu.VMEM((ch // LANES, TOK, LANES), F32),
                        pltpu.VMEM((ch // LANES, TOK, LANES), F32), pltpu.VMEM((1, ch), F32)],
        compiler_params=_cparams(("parallel", "arbitrary")),
        name="lru_fwd" if direction == 0 else "lru_bwd",
    )(*args)


def _block_diag(w):
    heads, n, _ = w.shape
    eye = jnp.eye(heads, dtype=w.dtype)
    return (eye[:, None, :, None] * w[:, :, None, :]).reshape(heads * n, heads * n)


def _gla_kernel(*refs, direction, combine):
    if combine:
        qkvg_ref, lr_ref, wa_ref, ba_ref, tri_ref, other_ref, nw_ref, o_ref, state_s = refs
    else:
        qkvg_ref, lr_ref, wa_ref, ba_ref, tri_ref, o_ref, state_s = refs
    key = GLA_HEADS * GLA_DK
    val = GLA_HEADS * GLA_DV
    nchunk = TOK // GLA_CHUNK

    @pl.when(pl.program_id(1) == 0)
    def _():
        state_s[...] = jnp.zeros_like(state_s)

    z = qkvg_ref[0]
    q = z[:, :key] * (GLA_DK ** -0.5)
    k = z[:, key:2 * key]
    v = z[:, 2 * key:2 * key + val].astype(BF16)
    lr = lr_ref[0][:, direction * GLA_RANK:(direction + 1) * GLA_RANK]
    logit = _dot(lr.astype(BF16), wa_ref[0].astype(BF16)) + ba_ref[0]
    log_a = (jnp.minimum(logit, 0.0) - jnp.log1p(jnp.exp(-jnp.abs(logit)))) / GLA_TAU

    g_hi, g_lo = _split_hi_lo(log_a)
    tri = tri_ref[...]
    b = _dot(tri, g_hi) + _dot(tri, g_lo)
    q_in = (q * jnp.exp(b)).astype(BF16)
    k_in = (k * jnp.exp(-b)).astype(BF16)
    t_idx = lax.broadcasted_iota(jnp.int32, (GLA_CHUNK, GLA_CHUNK), 0)
    s_idx = lax.broadcasted_iota(jnp.int32, (GLA_CHUNK, GLA_CHUNK), 1)
    causal = (s_idx <= t_idx) if direction == 0 else (s_idx >= t_idx)
    ones = jnp.ones((GLA_CHUNK, GLA_DV), BF16)

    chunk_order = range(nchunk) if direction == 0 else range(nchunk - 1, -1, -1)
    for c in chunk_order:
        rows_c = slice(c * GLA_CHUNK, (c + 1) * GLA_CHUNK)
        end = (c + 1) * GLA_CHUNK - 1 if direction == 0 else c * GLA_CHUNK
        b_c = b[rows_c]
        b_end = b[end:end + 1]
        k_out = (k[rows_c] * jnp.exp(b_end - b_c)).astype(BF16)
        for h in range(GLA_HEADS):
            ks = slice(h * GLA_DK, (h + 1) * GLA_DK)
            vs = slice(h * GLA_DV, (h + 1) * GLA_DV)
            v_h = v[rows_c, vs]
            att = jnp.where(causal, _dot_nt(q_in[rows_c, ks], k_in[rows_c, ks]), 0.0)
            state = state_s[h]
            o = _dot(att.astype(BF16), v_h) + _dot(q_in[rows_c, ks], state.astype(BF16))
            dec = _dot_tn(g_hi[rows_c, ks], ones) + _dot_tn(g_lo[rows_c, ks], ones)
            state_s[h] = jnp.exp(dec) * state + _dot_tn(k_out[:, ks], v_h)
            if combine:
                o = o + other_ref[0, rows_c, vs]
                gate = z[rows_c, 2 * key + val + h * GLA_DV:2 * key + val + (h + 1) * GLA_DV]
                o = _rms(o, nw_ref[...]) * (gate * jax.nn.sigmoid(gate))
            o_ref[0, rows_c, vs] = o


def _gla(qkvg, lr, wa2, ba, n_lat, direction, other=None, norm_w=None):
    bsz, t, width = qkvg.shape
    key = GLA_HEADS * GLA_DK
    val = GLA_HEADS * GLA_DV
    n_tiles = t // TOK
    combine = other is not None
    tile_of = functools.partial(_lru_tile_index, n_lat=n_lat, n_tiles=n_tiles, direction=direction)
    pos = np.arange(TOK)
    same = (pos[:, None] // GLA_CHUNK) == (pos[None, :] // GLA_CHUNK)
    tri = same & ((pos[None, :] <= pos[:, None]) if direction == 0 else (pos[None, :] >= pos[:, None]))
    tri = jnp.asarray(tri, BF16)
    in_specs = [
        pl.BlockSpec((1, TOK, width), lambda b, i: (b, tile_of(i), 0)),
        pl.BlockSpec((1, TOK, lr.shape[-1]), lambda b, i: (b, tile_of(i), 0)),
        pl.BlockSpec((1, GLA_RANK, key), lambda b, i: (direction, 0, 0), pipeline_mode=pl.Buffered(1)),
        pl.BlockSpec((1, 1, key), lambda b, i: (direction, 0, 0), pipeline_mode=pl.Buffered(1)),
        _const_spec((TOK, TOK)),
    ]
    args = [qkvg, lr, wa2, ba.reshape(2, 1, key), tri]
    if combine:
        in_specs += [pl.BlockSpec((1, TOK, val), lambda b, i: (b, tile_of(i), 0)), _const_spec((1, GLA_DV))]
        args += [other, norm_w.reshape(1, GLA_DV)]
    return pl.pallas_call(
        functools.partial(_gla_kernel, direction=direction, combine=combine),
        grid=(bsz, n_tiles),
        in_specs=in_specs,
        out_specs=pl.BlockSpec((1, TOK, val), lambda b, i: (b, tile_of(i), 0)),
        out_shape=jax.ShapeDtypeStruct((bsz, t, val), F32),
        scratch_shapes=[pltpu.VMEM((GLA_HEADS, GLA_DK, GLA_DV), F32)],
        compiler_params=_cparams(("parallel", "arbitrary")),
        name="gla_fwd" if direction == 0 else "gla_bwd",
    )(*args)


def _rope_tables(s_len, c_len, reps):
    quarter = GQA_DH // 4
    inv = ROPE_THETA ** (-np.arange(quarter, dtype=np.float32) / quarter)
    t = np.arange(s_len)
    ang_r = (t // GRID_W).astype(np.float32)[:, None] * inv[None, :]
    ang_c = (t % GRID_W).astype(np.float32)[:, None] * inv[None, :]
    cos = np.concatenate([np.cos(ang_r)] * 2 + [np.cos(ang_c)] * 2, axis=1)
    sin = np.concatenate([-np.sin(ang_r), np.sin(ang_r), -np.sin(ang_c), np.sin(ang_c)], axis=1)
    cos = np.concatenate([cos, np.ones((c_len, GQA_DH), np.float32)], axis=0)
    sin = np.concatenate([sin, np.zeros((c_len, GQA_DH), np.float32)], axis=0)
    return (jnp.asarray(np.tile(cos, (1, reps)), F32), jnp.asarray(np.tile(sin, (1, reps)), F32))


def _rope_swap(x):
    lane = lax.broadcasted_iota(jnp.int32, x.shape, 1)
    return jnp.where(lane % 32 < 16, pltpu.roll(x, 128 - 16, axis=1), pltpu.roll(x, 16, axis=1))


def _norm_rope(x, nw, cos, sin, gmean):
    hi, lo = _split_hi_lo(x * x)
    ms = _dot(hi, gmean) + _dot(lo, gmean)
    y = x * lax.rsqrt(ms + EPS) * nw
    slabs = [_rope_swap(y[:, j:j + 128]) for j in range(0, y.shape[1], 128)]
    swapped = slabs[0] if len(slabs) == 1 else jnp.concatenate(slabs, axis=1)
    return y * cos + swapped * sin


def _gqa_prep_kernel(z_ref, qw_ref, kw_ref, cos_ref, sin_ref, gq_ref, gk_ref, q_ref, k_ref, v_ref):
    qd = GQA_HEADS * GQA_DH
    kd = GQA_KV_HEADS * GQA_DH
    z = z_ref[0]
    cos_k = cos_ref[...]
    sin_k = sin_ref[...]
    reps = qd // kd
    cos_q = jnp.concatenate([cos_k] * reps, axis=1)
    sin_q = jnp.concatenate([sin_k] * reps, axis=1)
    q = _norm_rope(z[:, :qd], qw_ref[...], cos_q, sin_q, gq_ref[...]) * (LOG2E * GQA_DH ** -0.5)
    k = _norm_rope(z[:, qd:qd + kd], kw_ref[...], cos_k, sin_k, gk_ref[...])
    v = z[:, qd + kd:qd + 2 * kd]
    for h in range(GQA_HEADS):
        q_ref[0, h] = q[:, h * GQA_DH:(h + 1) * GQA_DH].astype(BF16)
    ones = jnp.ones((z.shape[0], LANES - GQA_DH), BF16)
    for h in range(GQA_KV_HEADS):
        k_ref[0, h] = k[:, h * GQA_DH:(h + 1) * GQA_DH].astype(BF16)
        v_ref[0, h] = jnp.concatenate([v[:, h * GQA_DH:(h + 1) * GQA_DH].astype(BF16), ones], axis=1)


def _group_mean_matrix(n, group):
    idx = np.arange(n) // group
    return jnp.asarray((idx[:, None] == idx[None, :]) / group, BF16)


def _gqa_prep(z, q_norm_w, k_norm_w, s_len, c_len):
    bsz, t, width = z.shape
    qd = GQA_HEADS * GQA_DH
    kd = GQA_KV_HEADS * GQA_DH
    cos, sin = _rope_tables(s_len, c_len, GQA_KV_HEADS)
    head_spec = lambda n, w=GQA_DH: pl.BlockSpec((1, n, TOK, w), lambda b, i: (b, 0, i, 0))
    return pl.pallas_call(
        _gqa_prep_kernel,
        grid=(bsz, t // TOK),
        in_specs=[
            pl.BlockSpec((1, TOK, width), lambda b, i: (b, i, 0)),
            _const_spec((1, qd)), _const_spec((1, kd)),
            pl.BlockSpec((TOK, kd), lambda b, i: (i, 0)),
            pl.BlockSpec((TOK, kd), lambda b, i: (i, 0)),
            _const_spec((qd, qd)), _const_spec((kd, kd)),
        ],
        out_specs=[head_spec(GQA_HEADS), head_spec(GQA_KV_HEADS), head_spec(GQA_KV_HEADS, LANES)],
        out_shape=[jax.ShapeDtypeStruct((bsz, GQA_HEADS, t, GQA_DH), BF16),
                   jax.ShapeDtypeStruct((bsz, GQA_KV_HEADS, t, GQA_DH), BF16),
                   jax.ShapeDtypeStruct((bsz, GQA_KV_HEADS, t, LANES), BF16)],
        compiler_params=_cparams(("parallel", "parallel")),
        name="gqa_prep",
    )(z, jnp.tile(q_norm_w, GQA_HEADS).reshape(1, qd), jnp.tile(k_norm_w, GQA_KV_HEADS).reshape(1, kd),
      cos, sin, _group_mean_matrix(qd, GQA_DH), _group_mean_matrix(kd, GQA_DH))


def _flash_kernel(q_ref, k_ref, v_ref, o_ref, m_s, acc_s, *, group, tk, sub):
    tq = q_ref.shape[2]
    units = [(g, r) for g in range(group) for r in range(0, tq, sub)]
    m_s[...] = jnp.full_like(m_s, -jnp.inf)
    acc_s[...] = jnp.zeros_like(acc_s)

    def kv_step(j, carry):
        rows_k = pl.ds(pl.multiple_of(j * tk, tk), tk)
        k = k_ref[0, 0, rows_k, :]
        v = v_ref[0, 0, rows_k, :]
        scores = lambda u: _dot_nt(q_ref[0, units[u][0], units[u][1]:units[u][1] + sub, :], k)
        s_next = scores(0)
        for u in range(len(units)):
            tiles = _lane_tiles(s_next)
            if u + 1 < len(units):
                s_next = scores(u + 1)
            m_prev = m_s[u]
            m_new = jnp.maximum(m_prev, jnp.max(functools.reduce(jnp.maximum, tiles), axis=-1, keepdims=True))
            p = [jnp.exp2(t - m_new).astype(BF16) for t in tiles]
            acc_s[u] = jnp.exp2(m_prev - m_new) * acc_s[u] + _dot(jnp.concatenate(p, axis=1), v)
            m_s[u] = m_new
        return carry

    lax.fori_loop(0, k_ref.shape[2] // tk, kv_step, 0)
    for u, (g, r) in enumerate(units):
        acc = acc_s[u]
        o_ref[0, r:r + sub, g * GQA_DH:(g + 1) * GQA_DH] = acc[:, :GQA_DH] / acc[:, GQA_DH:GQA_DH + 1]


def _kv_tile(t):
    return max(n * TOK for n in (1, 2, 3) if t % (n * TOK) == 0)


def _flash(q, k, v, s_len, tq, tk):
    bsz, _, t, dh = q.shape
    assert t % tk == 0 and s_len % tq == 0 and tq % TOK == 0
    group = GQA_HEADS // GQA_KV_HEADS
    n_units = group * (tq // TOK)
    return pl.pallas_call(
        functools.partial(_flash_kernel, group=group, tk=tk, sub=TOK),
        grid=(bsz, GQA_KV_HEADS, s_len // tq),
        in_specs=[
            pl.BlockSpec((1, group, tq, dh), lambda b, h, i: (b, h, i, 0)),
            pl.BlockSpec((1, 1, t, dh), lambda b, h, i: (b, h, 0, 0)),
            pl.BlockSpec((1, 1, t, LANES), lambda b, h, i: (b, h, 0, 0)),
        ],
        out_specs=pl.BlockSpec((1, tq, group * dh), lambda b, h, i: (b, i, h)),
        out_shape=jax.ShapeDtypeStruct((bsz, s_len, GQA_HEADS * dh), F32),
        scratch_shapes=[pltpu.VMEM((n_units, TOK, LANES), F32), pltpu.VMEM((n_units, TOK, LANES), F32)],
        compiler_params=_cparams(("parallel", "parallel", "arbitrary")),
        name="gqa_flash",
    )(q, k, v)


def kernel(x, c, ctx, c_ctx, norm1_w, norm2_w, w_mod, b_mod, w_ff1, w_ff2,
           w_in_even, na_rpb, lru_conv_w, lru_conv_b, lru_wa, lru_ba, lru_wx, lru_bx, lru_lambda, w_out_even,
           w_in_odd, gla_wa2, gla_ba, gla_norm_w, gqa_q_norm_w, gqa_k_norm_w, w_out_odd, final_norm_w):
    bsz, s_len, d = x.shape
    c_len = ctx.shape[1]
    depth = w_mod.shape[0]
    assert s_len % TOK == 0 and c_len % TOK == 0 and depth == 2
    n_lat = s_len // TOK
    n_tiles = (s_len + c_len) // TOK

    pad = (-(bsz + 1)) % 8
    cvec = jnp.concatenate([c_ctx[None], c, jnp.zeros((pad, d), F32)], axis=0)
    mods = _modulation(cvec, w_mod, b_mod)
    h = jnp.concatenate([x, ctx], axis=1)

    na_w = NA_HEADS * NA_DH
    ch = lru_lambda.shape[-1]
    qkv, xg = _inproj(h, norm1_w[0], mods[0], w_in_even[0].astype(BF16),
                      ((0, 3 * na_w), (3 * na_w, 3 * na_w + 2 * ch)), (BF16, F32), n_lat)
    na_out = _na(qkv, na_rpb[0], s_len, c_len)
    w_gate = jnp.stack([jnp.concatenate([_block_diag(lru_wa[0, dr]), _block_diag(lru_wx[0, dr])], axis=1)
                        for dr in range(2)]).astype(BF16)
    b_gate = jnp.concatenate([lru_ba[0], lru_bx[0]], axis=-1).reshape(2, 1, 2 * ch)
    lru_args = (xg, lru_conv_w[0], lru_conv_b[0], w_gate, b_gate, lru_lambda[0], n_lat)
    lru_b = _lru(*lru_args, direction=1)
    lru_out = _lru(*lru_args, direction=0, other=lru_b)
    wo = w_out_even[0].astype(BF16)
    h = _mlp(h, na_out, lru_out, mods[0], wo[:na_w], wo[na_w:], norm2_w[0],
             w_ff1[0].astype(BF16), w_ff2[0].astype(BF16), final_norm_w, n_lat, n_tiles, final=False)

    key = GLA_HEADS * GLA_DK
    val = GLA_HEADS * GLA_DV
    gla_w = 2 * key + 2 * val
    gqa_w = (GQA_HEADS + 2 * GQA_KV_HEADS) * GQA_DH
    w_in = w_in_odd[0]
    w_in = jnp.concatenate([w_in[:, :gla_w], w_in[:, gla_w + 2 * GLA_RANK:], w_in[:, gla_w:gla_w + 2 * GLA_RANK]],
                           axis=1).astype(BF16)
    qkvg, gqa_z, lr = _inproj(h, norm1_w[1], mods[1], w_in,
                              ((0, gla_w), (gla_w, gla_w + gqa_w), (gla_w + gqa_w, gla_w + gqa_w + 2 * GLA_RANK)),
                              (F32, F32, F32), n_lat)
    gla_b = _gla(qkvg, lr, gla_wa2[0], gla_ba[0], n_lat, direction=1)
    gla_out = _gla(qkvg, lr, gla_wa2[0], gla_ba[0], n_lat, direction=0, other=gla_b, norm_w=gla_norm_w[0])
    q, k, v = _gqa_prep(gqa_z, gqa_q_norm_w[0], gqa_k_norm_w[0], s_len, c_len)
    gqa_out = _flash(q, k, v, s_len, tq=2 * TOK, tk=_kv_tile(s_len + c_len))
    wo = w_out_odd[0].astype(BF16)
    return _mlp(h, gla_out, gqa_out, mods[1], wo[:val], wo[val:], norm2_w[1],
                w_ff1[1].astype(BF16), w_ff2[1].astype(BF16), final_norm_w, n_lat, n_lat, final=True)
```

```python
import functools

import numpy as np
import jax
import jax.numpy as jnp
from jax import lax
from jax.experimental import pallas as pl
from jax.experimental.pallas import tpu as pltpu

F32 = jnp.float32
BF16 = jnp.bfloat16

EPS = 1e-6
GRID_W = 64
LANES = 128
N_MOD = 6
TOK = 256
DTOK = 512
NEG = -1e30

NA_HEADS = 8
NA_DH = 64
NA_KH = 8
NA_KW = 16
LRU_C = 8.0
LRU_SEG = 8
GLA_HEADS = 4
GLA_DK = 64
GLA_DV = 128
GLA_RANK = 16
GLA_TAU = 16.0
GLA_CHUNK = 64
GQA_HEADS = 8
GQA_KV_HEADS = 2
GQA_DH = 64
ROPE_THETA = 10000.0
LOG2E = 1.4426950408889634

VMEM_LIMIT = 56 * 1024 * 1024


def _cparams(sem):
    return pltpu.CompilerParams(dimension_semantics=sem, vmem_limit_bytes=VMEM_LIMIT)


def _const_spec(shape):
    nd = len(shape)
    return pl.BlockSpec(shape, lambda *_: (0,) * nd, pipeline_mode=pl.Buffered(1))


def _dot(a, b):
    return jnp.dot(a, b, preferred_element_type=F32)


def _dot_nt(a, b):
    return lax.dot_general(a, b, (((1,), (1,)), ((), ())), preferred_element_type=F32)


def _dot_tn(a, b):
    return lax.dot_general(a, b, (((0,), (0,)), ((), ())), preferred_element_type=F32)


def _split_hi_lo(x):
    hi = x.astype(BF16)
    lo = (x - hi.astype(F32)).astype(BF16)
    return hi, lo


def _rms(x, w):
    ms = jnp.mean(x * x, axis=-1, keepdims=True)
    return x * lax.rsqrt(ms + EPS) * w


def _softplus(z):
    return jnp.maximum(z, 0.0) + jnp.log1p(jnp.exp(-jnp.abs(z)))


def _mod_kernel(c_ref, w_ref, b_ref, o_ref):
    c = c_ref[...]
    s = c * jax.nn.sigmoid(c)
    o_ref[0] = jnp.dot(s, w_ref[0], precision=lax.Precision.HIGHEST, preferred_element_type=F32) + b_ref[0]


def _modulation(cvec, w_mod, b_mod):
    depth, d, _ = w_mod.shape
    rows = cvec.shape[0]
    out = pl.pallas_call(
        _mod_kernel,
        grid=(depth, N_MOD),
        in_specs=[
            pl.BlockSpec((rows, d), lambda i, j: (0, 0)),
            pl.BlockSpec((1, d, d), lambda i, j: (i, 0, j)),
            pl.BlockSpec((1, 1, d), lambda i, j: (i, 0, j)),
        ],
        out_specs=pl.BlockSpec((1, rows, d), lambda i, j: (i, 0, j)),
        out_shape=jax.ShapeDtypeStruct((depth, rows, N_MOD * d), F32),
        compiler_params=_cparams(("arbitrary", "arbitrary")),
        name="modulation",
    )(cvec, w_mod, b_mod.reshape(depth, 1, N_MOD * d))
    return out.reshape(depth, rows, N_MOD, d)


def _mod_spec(d, n_lat_tiles):
    return pl.BlockSpec((1, N_MOD, d), lambda b, t: (jnp.where(t >= n_lat_tiles, 0, b + 1), 0, 0))


def _inproj_kernel(h_ref, nw_ref, mod_ref, w_ref, *o_refs, splits):
    mod = mod_ref[0]
    u = _rms(h_ref[0], nw_ref[...]) * (1.0 + mod[1:2]) + mod[0:1]
    z = _dot(u.astype(BF16), w_ref[...])
    for o_ref, (lo, hi) in zip(o_refs, splits):
        o_ref[0] = z[:, lo:hi].astype(o_ref.dtype)


def _inproj(h, norm_w, mods, w, splits, dtypes, s_len):
    bsz, t, d = h.shape
    n = w.shape[1]
    assert s_len % DTOK == 0
    return pl.pallas_call(
        functools.partial(_inproj_kernel, splits=splits),
        grid=(bsz, pl.cdiv(t, DTOK)),
        in_specs=[
            pl.BlockSpec((1, DTOK, d), lambda b, i: (b, i, 0)),
            _const_spec((1, d)),
            _mod_spec(d, s_len // DTOK),
            _const_spec((d, n)),
        ],
        out_specs=[pl.BlockSpec((1, DTOK, hi - lo), lambda b, i: (b, i, 0)) for lo, hi in splits],
        out_shape=[jax.ShapeDtypeStruct((bsz, t, hi - lo), dt) for (lo, hi), dt in zip(splits, dtypes)],
        compiler_params=_cparams(("parallel", "parallel")),
        name="inproj",
    )(h, norm_w.reshape(1, d), mods, w)


def _mlp_kernel(h_ref, ma_ref, mb_ref, mod_ref, woa_ref, wob_ref, nw_ref, w1_ref, w2_ref, fw_ref, o_ref,
                *, ff_chunk, final):
    mod = mod_ref[0]
    y = _dot(ma_ref[0].astype(BF16), woa_ref[...]) + _dot(mb_ref[0].astype(BF16), wob_ref[...])
    h1 = h_ref[0] + mod[2:3] * y
    u = (_rms(h1, nw_ref[...]) * (1.0 + mod[4:5]) + mod[3:4]).astype(BF16)
    d_ff = w1_ref.shape[1]
    acc = jnp.zeros(h1.shape, F32)
    for c in range(d_ff // ff_chunk):
        sl = slice(c * ff_chunk, (c + 1) * ff_chunk)
        a = jnp.maximum(_dot(u, w1_ref[:, sl]), 0.0)
        acc = acc + _dot((a * a).astype(BF16), w2_ref[sl, :])
    h2 = h1 + mod[5:6] * acc
    if final:
        h2 = _rms(h2, fw_ref[...])
    o_ref[0] = h2


def _mlp(h, mix_a, mix_b, mods, wo_a, wo_b, norm_w, w1, w2, final_w, s_len, out_rows, final):
    bsz, _, d = h.shape
    da, db = mix_a.shape[-1], mix_b.shape[-1]
    d_ff = w1.shape[1]
    assert s_len % DTOK == 0
    tok_spec = lambda width: pl.BlockSpec((1, DTOK, width), lambda b, i: (b, i, 0))
    return pl.pallas_call(
        functools.partial(_mlp_kernel, ff_chunk=512, final=final),
        grid=(bsz, pl.cdiv(out_rows, DTOK)),
        in_specs=[
            tok_spec(d), tok_spec(da), tok_spec(db),
            _mod_spec(d, s_len // DTOK),
            _const_spec((da, d)), _const_spec((db, d)), _const_spec((1, d)),
            _const_spec((d, d_ff)), _const_spec((d_ff, d)), _const_spec((1, d)),
        ],
        out_specs=tok_spec(d),
        out_shape=jax.ShapeDtypeStruct((bsz, out_rows, d), F32),
        compiler_params=_cparams(("parallel", "parallel")),
        name="mlp",
    )(h, mix_a, mix_b, mods, wo_a, wo_b, norm_w.reshape(1, d), w1, w2, final_w.reshape(1, d))


NA_QROWS = TOK // GRID_W
NA_WIN = NA_KH + NA_QROWS


def _na_plan(rows):
    n_blk = rows // NA_QROWS
    r = np.arange(rows).reshape(n_blk, NA_QROWS)
    start = np.clip(r - NA_KH // 2, 0, rows - NA_KH)
    u0 = np.clip(r[:, 0] - NA_KH // 2, 0, rows - NA_WIN)
    key = u0[:, None, None] + np.arange(NA_WIN)[None, None, :]
    valid = (key >= start[:, :, None]) & (key < start[:, :, None] + NA_KH)
    d_row = np.where(valid, key - r[:, :, None] + (NA_KH - 1), 0)
    flat = np.concatenate([valid.reshape(n_blk, -1), d_row.reshape(n_blk, -1)], axis=1)
    _, first, pat = np.unique(flat, axis=0, return_index=True, return_inverse=True)
    return u0.astype(np.int32), pat.reshape(-1).astype(np.int32), valid[first], d_row[first]


def _na_bias_table(rpb, valid, d_row):
    n_pat = valid.shape[0]
    heads = rpb.shape[0]
    cols = np.arange(GRID_W)
    col_start = np.clip(cols - NA_KW // 2, 0, GRID_W - NA_KW)
    col_valid = (cols[None, :] >= col_start[:, None]) & (cols[None, :] < col_start[:, None] + NA_KW)
    d_col = np.clip(cols[None, :] - cols[:, None], 1 - NA_KW, NA_KW - 1) + (NA_KW - 1)
    row_sel = (d_row[..., None] == np.arange(2 * NA_KH - 1)).astype(np.float32)
    col_sel = (d_col[None] == np.arange(2 * NA_KW - 1)[:, None, None]).astype(np.float32)
    hp = lax.Precision.HIGHEST
    by_row = jnp.einsum('hrc,piur->phiuc', rpb, row_sel, precision=hp)
    bias = jnp.einsum('phiuc,cqw->phiquw', by_row, col_sel, precision=hp)
    keep = valid[:, None, :, None, :, None] & col_valid[None, None, None, :, None, :]
    return jnp.where(keep, bias, NEG).reshape(n_pat, heads, TOK, NA_WIN * GRID_W)


def _softmax_tiles(tiles):
    m = jnp.max(functools.reduce(jnp.maximum, tiles), axis=-1, keepdims=True)
    p = [jnp.exp(t - m) for t in tiles]
    denom = jnp.sum(functools.reduce(jnp.add, p), axis=-1, keepdims=True)
    return [t.astype(BF16) for t in p], denom


def _lane_tiles(x):
    return [x[:, j:j + LANES] for j in range(0, x.shape[1], LANES)]


def _na_kernel(u0_ref, pat_ref, q_ref, k_ref, v_ref, kc_ref, vc_ref, bias_ref, o_ref, *, n_lat):
    i = pl.program_id(1)
    is_lat = i < n_lat
    row0 = pl.multiple_of(u0_ref[i] * GRID_W, GRID_W)
    win = NA_WIN * GRID_W
    q = q_ref[0] * 0.125
    kw = k_ref[0, pl.ds(row0, win), :]
    vw = v_ref[0, pl.ds(row0, win), :]
    kc = kc_ref[0]
    vc = vc_ref[0]

    def scores(h):
        sl = slice(h * NA_DH, (h + 1) * NA_DH)
        s_loc = jnp.where(is_lat, _dot_nt(q[:, sl], kw[:, sl]) + bias_ref[0, h], NEG)
        return _lane_tiles(s_loc) + _lane_tiles(_dot_nt(q[:, sl], kc[:, sl]))

    s_next = scores(0)
    for h in range(NA_HEADS):
        sl = slice(h * NA_DH, (h + 1) * NA_DH)
        tiles = s_next
        if h + 1 < NA_HEADS:
            s_next = scores(h + 1)
        p, denom = _softmax_tiles(tiles)
        n_loc = win // LANES
        o = (_dot(jnp.concatenate(p[:n_loc], axis=1), vw[:, sl])
             + _dot(jnp.concatenate(p[n_loc:], axis=1), vc[:, sl]))
        o_ref[0, :, sl] = o / denom


def _na(qkv, rpb, s_len, c_len):
    bsz, t, _ = qkv.shape
    width = NA_HEADS * NA_DH
    rows = s_len // GRID_W
    n_lat = s_len // TOK
    n_steps = t // TOK
    assert rows >= NA_WIN and rows % NA_QROWS == 0 and c_len % TOK == 0 and s_len % c_len == 0
    u0, pat, valid, d_row = _na_plan(rows)
    bias = _na_bias_table(rpb, valid, d_row)
    ctx_steps = np.zeros(n_steps - n_lat, np.int32)
    u0 = jnp.asarray(np.concatenate([u0, ctx_steps]))
    pat = jnp.asarray(np.concatenate([pat, ctx_steps]))
    tok_spec = pl.BlockSpec((1, TOK, width), lambda b, i, u0_ref, pat_ref: (b, i, 0))
    ctx_blk = s_len // c_len
    return pl.pallas_call(
        functools.partial(_na_kernel, n_lat=n_lat),
        grid_spec=pltpu.PrefetchScalarGridSpec(
            num_scalar_prefetch=2,
            grid=(bsz, n_steps),
            in_specs=[
                tok_spec,
                pl.BlockSpec((1, s_len, width), lambda b, i, u0_ref, pat_ref: (b, 0, 1),
                             pipeline_mode=pl.Buffered(1)),
                pl.BlockSpec((1, s_len, width), lambda b, i, u0_ref, pat_ref: (b, 0, 2),
                             pipeline_mode=pl.Buffered(1)),
                pl.BlockSpec((1, c_len, width), lambda b, i, u0_ref, pat_ref: (b, ctx_blk, 1)),
                pl.BlockSpec((1, c_len, width), lambda b, i, u0_ref, pat_ref: (b, ctx_blk, 2)),
                pl.BlockSpec((1, NA_HEADS, TOK, NA_WIN * GRID_W),
                             lambda b, i, u0_ref, pat_ref: (pat_ref[i], 0, 0, 0)),
            ],
            out_specs=tok_spec,
        ),
        out_shape=jax.ShapeDtypeStruct((bsz, t, width), F32),
        compiler_params=_cparams(("parallel", "arbitrary")),
        name="na",
    )(u0, pat, qkv, qkv, qkv, qkv, qkv, bias)


def _lru_tile_index(i, n_lat, n_tiles, direction):
    if direction == 0:
        return (i + n_lat) % n_tiles
    return n_tiles - 1 - i


def _lru_kernel(*refs, direction, n_lat, n_tiles, combine):
    if combine:
        (x_ref, prev_ref, next_ref, cw_ref, cb_ref, wg_ref, bg_ref, lam_ref, g_ref, other_ref,
         o_ref, ext_s, a_s, b_s, carry_s) = refs
    else:
        (x_ref, prev_ref, next_ref, cw_ref, cb_ref, wg_ref, bg_ref, lam_ref,
         o_ref, ext_s, a_s, b_s, carry_s) = refs
    i = pl.program_id(1)
    tile = _lru_tile_index(i, n_lat, n_tiles, direction)
    ch = x_ref.shape[-1]
    halo = prev_ref.shape[1]
    seg = TOK // LRU_SEG

    @pl.when(i == 0)
    def _():
        carry_s[...] = jnp.zeros_like(carry_s)

    first = jnp.logical_or(tile == 0, tile == n_lat)
    last = jnp.logical_or(tile == n_lat - 1, tile == n_tiles - 1)
    x = x_ref[0]
    ext_s[0:halo, :] = jnp.where(first, 0.0, prev_ref[0])
    ext_s[halo:halo + TOK, :] = x
    ext_s[halo + TOK:, :] = jnp.where(last, 0.0, next_ref[0])
    cw = cw_ref[...]
    xc = (cw[0:1] * ext_s[halo - 2:halo - 2 + TOK, :] + cw[1:2] * ext_s[halo - 1:halo - 1 + TOK, :]
          + cw[2:3] * x + cw[3:4] * ext_s[halo + 1:halo + 1 + TOK, :] + cb_ref[...])

    pre = _dot(xc.astype(BF16), wg_ref[0]) + bg_ref[0]
    r_gate = jax.nn.sigmoid(pre[:, :ch])
    i_gate = jax.nn.sigmoid(pre[:, ch:])
    log_a = (-LRU_C) * r_gate * _softplus(-lam_ref[0])
    a = jnp.exp(log_a)
    bb = jnp.sqrt(1.0 - a * a) * (i_gate * xc)
    state_all = carry_s[...]
    seg_order = range(LRU_SEG) if direction == 0 else range(LRU_SEG - 1, -1, -1)
    step_order = range(seg) if direction == 0 else range(seg - 1, -1, -1)
    new_state = []
    for cs in range(ch // LANES):
        lanes = slice(cs * LANES, (cs + 1) * LANES)
        a_s[cs] = a[:, lanes]
        b_s[cs] = bb[:, lanes]
        h = jnp.zeros((LRU_SEG, LANES), F32)
        cum = jnp.ones((LRU_SEG, LANES), F32)
        for j in step_order:
            rows_j = pl.ds(j, LRU_SEG, stride=seg)
            a_j = a_s[cs, rows_j, :]
            h = a_j * h + b_s[cs, rows_j, :]
            cum = cum * a_j
            b_s[cs, rows_j, :] = h
            a_s[cs, rows_j, :] = cum
        state = state_all[:, lanes]
        for s in seg_order:
            rows_s = slice(s * seg, (s + 1) * seg)
            y = b_s[cs, rows_s, :] + a_s[cs, rows_s, :] * state
            state = h[s:s + 1, :] + cum[s:s + 1, :] * state
            if combine:
                y = (y + other_ref[0, rows_s, lanes]) * jax.nn.gelu(g_ref[0, rows_s, lanes])
            o_ref[0, rows_s, lanes] = y
        new_state.append(state)
    state = jnp.concatenate(new_state, axis=1)
    carry_s[...] = state


def _lru(xg, conv_w, conv_b, w_gate, b_gate, lam, n_lat, direction, other=None):
    bsz, t, two_ch = xg.shape
    ch = two_ch // 2
    n_tiles = t // TOK
    halo = 8
    combine = other is not None
    tile_of = functools.partial(_lru_tile_index, n_lat=n_lat, n_tiles=n_tiles, direction=direction)
    per_tile = TOK // halo
    in_specs = [
        pl.BlockSpec((1, TOK, ch), lambda b, i: (b, tile_of(i), 0)),
        pl.BlockSpec((1, halo, ch), lambda b, i: (b, jnp.maximum(tile_of(i) * per_tile - 1, 0), 0)),
        pl.BlockSpec((1, halo, ch), lambda b, i: (b, jnp.minimum((tile_of(i) + 1) * per_tile, t // halo - 1), 0)),
        _const_spec((4, ch)), _const_spec((1, ch)),
        pl.BlockSpec((1, ch, 2 * ch), lambda b, i: (direction, 0, 0), pipeline_mode=pl.Buffered(1)),
        pl.BlockSpec((1, 1, 2 * ch), lambda b, i: (direction, 0, 0), pipeline_mode=pl.Buffered(1)),
        pl.BlockSpec((1, 1, ch), lambda b, i: (direction, 0, 0), pipeline_mode=pl.Buffered(1)),
    ]
    args = [xg, xg, xg, conv_w, conv_b.reshape(1, ch), w_gate, b_gate, lam.reshape(2, 1, ch)]
    if combine:
        in_specs += [pl.BlockSpec((1, TOK, ch), lambda b, i: (b, tile_of(i), 1)),
                     pl.BlockSpec((1, TOK, ch), lambda b, i: (b, tile_of(i), 0))]
        args += [xg, other]
    return pl.pallas_call(
        functools.partial(_lru_kernel, direction=direction, n_lat=n_lat, n_tiles=n_tiles, combine=combine),
        grid=(bsz, n_tiles),
        in_specs=in_specs,
        out_specs=pl.BlockSpec((1, TOK, ch), lambda b, i: (b, tile_of(i), 0)),
        out_shape=jax.ShapeDtypeStruct((bsz, t, ch), F32),
        scratch_shapes=[pltpu.VMEM((TOK + 2 * halo, ch), F32), pltpu.VMEM((ch // LANES, TOK, LANES), F32),
                        pltpu.VMEM((ch // LANES, TOK, LANES), F32), pltpu.VMEM((1, ch), F32)],
        compiler_params=_cparams(("parallel", "arbitrary")),
        name="lru_fwd" if direction == 0 else "lru_bwd",
    )(*args)


def _block_diag(w):
    heads, n, _ = w.shape
    eye = jnp.eye(heads, dtype=w.dtype)
    return (eye[:, None, :, None] * w[:, :, None, :]).reshape(heads * n, heads * n)


def _gla_kernel(*refs, direction, combine):
    if combine:
        qkvg_ref, lr_ref, wa_ref, ba_ref, tri_ref, other_ref, nw_ref, o_ref, state_s = refs
    else:
        qkvg_ref, lr_ref, wa_ref, ba_ref, tri_ref, o_ref, state_s = refs
    key = GLA_HEADS * GLA_DK
    val = GLA_HEADS * GLA_DV
    nchunk = TOK // GLA_CHUNK

    @pl.when(pl.program_id(1) == 0)
    def _():
        state_s[...] = jnp.zeros_like(state_s)

    z = qkvg_ref[0]
    q = z[:, :key] * (GLA_DK ** -0.5)
    k = z[:, key:2 * key]
    v = z[:, 2 * key:2 * key + val].astype(BF16)
    lr = lr_ref[0][:, direction * GLA_RANK:(direction + 1) * GLA_RANK]
    logit = _dot(lr.astype(BF16), wa_ref[0].astype(BF16)) + ba_ref[0]
    log_a = (jnp.minimum(logit, 0.0) - jnp.log1p(jnp.exp(-jnp.abs(logit)))) / GLA_TAU

    g_hi, g_lo = _split_hi_lo(log_a)
    tri = tri_ref[...]
    b = _dot(tri, g_hi) + _dot(tri, g_lo)
    q_in = (q * jnp.exp(b)).astype(BF16)
    k_in = (k * jnp.exp(-b)).astype(BF16)
    chunks = [slice(c * GLA_CHUNK, (c + 1) * GLA_CHUNK) for c in range(nchunk)]
    ends = [(c + 1) * GLA_CHUNK - 1 if direction == 0 else c * GLA_CHUNK for c in range(nchunk)]
    b_end = jnp.concatenate([jnp.broadcast_to(b[e:e + 1], (GLA_CHUNK, key)) for e in ends], axis=0)
    k_out = (k * jnp.exp(b_end - b)).astype(BF16)
    causal = tri > 0
    ones = jnp.ones((GLA_CHUNK, GLA_DV), BF16)
    decay = [jnp.exp(_dot_tn(g_hi[rc], ones) + _dot_tn(g_lo[rc], ones)) for rc in chunks]

    chunk_order = range(nchunk) if direction == 0 else range(nchunk - 1, -1, -1)
    for h in range(GLA_HEADS):
        ks = slice(h * GLA_DK, (h + 1) * GLA_DK)
        vs = slice(h * GLA_DV, (h + 1) * GLA_DV)
        v_h = v[:, vs]
        att = jnp.where(causal, _dot_nt(q_in[:, ks], k_in[:, ks]), 0.0)
        o = _dot(att.astype(BF16), v_h)
        state = state_s[h]
        inter = [None] * nchunk
        for c in chunk_order:
            inter[c] = _dot(q_in[chunks[c], ks], state.astype(BF16))
            state = decay[c][ks] * state + _dot_tn(k_out[chunks[c], ks], v_h[chunks[c]])
        state_s[h] = state
        o = o + jnp.concatenate(inter, axis=0)
        if combine:
            o = o + other_ref[0, :, vs]
            gate = z[:, 2 * key + val + h * GLA_DV:2 * key + val + (h + 1) * GLA_DV]
            o = _rms(o, nw_ref[...]) * (gate * jax.nn.sigmoid(gate))
        o_ref[0, :, vs] = o


def _gla(qkvg, lr, wa2, ba, n_lat, direction, other=None, norm_w=None):
    bsz, t, width = qkvg.shape
    key = GLA_HEADS * GLA_DK
    val = GLA_HEADS * GLA_DV
    n_tiles = t // TOK
    combine = other is not None
    tile_of = functools.partial(_lru_tile_index, n_lat=n_lat, n_tiles=n_tiles, direction=direction)
    pos = np.arange(TOK)
    same = (pos[:, None] // GLA_CHUNK) == (pos[None, :] // GLA_CHUNK)
    tri = same & ((pos[None, :] <= pos[:, None]) if direction == 0 else (pos[None, :] >= pos[:, None]))
    tri = jnp.asarray(tri, BF16)
    in_specs = [
        pl.BlockSpec((1, TOK, width), lambda b, i: (b, tile_of(i), 0)),
        pl.BlockSpec((1, TOK, lr.shape[-1]), lambda b, i: (b, tile_of(i), 0)),
        pl.BlockSpec((1, GLA_RANK, key), lambda b, i: (direction, 0, 0), pipeline_mode=pl.Buffered(1)),
        pl.BlockSpec((1, 1, key), lambda b, i: (direction, 0, 0), pipeline_mode=pl.Buffered(1)),
        _const_spec((TOK, TOK)),
    ]
    args = [qkvg, lr, wa2, ba.reshape(2, 1, key), tri]
    if combine:
        in_specs += [pl.BlockSpec((1, TOK, val), lambda b, i: (b, tile_of(i), 0)), _const_spec((1, GLA_DV))]
        args += [other, norm_w.reshape(1, GLA_DV)]
    return pl.pallas_call(
        functools.partial(_gla_kernel, direction=direction, combine=combine),
        grid=(bsz, n_tiles),
        in_specs=in_specs,
        out_specs=pl.BlockSpec((1, TOK, val), lambda b, i: (b, tile_of(i), 0)),
        out_shape=jax.ShapeDtypeStruct((bsz, t, val), F32),
        scratch_shapes=[pltpu.VMEM((GLA_HEADS, GLA_DK, GLA_DV), F32)],
        compiler_params=_cparams(("parallel", "arbitrary")),
        name="gla_fwd" if direction == 0 else "gla_bwd",
    )(*args)


def _rope_tables(s_len, c_len, reps):
    quarter = GQA_DH // 4
    inv = ROPE_THETA ** (-np.arange(quarter, dtype=np.float32) / quarter)
    t = np.arange(s_len)
    ang_r = (t // GRID_W).astype(np.float32)[:, None] * inv[None, :]
    ang_c = (t % GRID_W).astype(np.float32)[:, None] * inv[None, :]
    cos = np.concatenate([np.cos(ang_r)] * 2 + [np.cos(ang_c)] * 2, axis=1)
    sin = np.concatenate([-np.sin(ang_r), np.sin(ang_r), -np.sin(ang_c), np.sin(ang_c)], axis=1)
    cos = np.concatenate([cos, np.ones((c_len, GQA_DH), np.float32)], axis=0)
    sin = np.concatenate([sin, np.zeros((c_len, GQA_DH), np.float32)], axis=0)
    return (jnp.asarray(np.tile(cos, (1, reps)), F32), jnp.asarray(np.tile(sin, (1, reps)), F32))


def _rope_swap(x):
    lane = lax.broadcasted_iota(jnp.int32, x.shape, 1)
    return jnp.where(lane % 32 < 16, pltpu.roll(x, 128 - 16, axis=1), pltpu.roll(x, 16, axis=1))


def _norm_rope(x, nw, cos, sin, gmean):
    hi, lo = _split_hi_lo(x * x)
    ms = _dot(hi, gmean) + _dot(lo, gmean)
    y = x * lax.rsqrt(ms + EPS) * nw
    slabs = [_rope_swap(y[:, j:j + 128]) for j in range(0, y.shape[1], 128)]
    swapped = slabs[0] if len(slabs) == 1 else jnp.concatenate(slabs, axis=1)
    return y * cos + swapped * sin


def _gqa_prep_kernel(z_ref, qw_ref, kw_ref, cos_ref, sin_ref, gq_ref, gk_ref, q_ref, k_ref, v_ref):
    qd = GQA_HEADS * GQA_DH
    kd = GQA_KV_HEADS * GQA_DH
    z = z_ref[0]
    cos_k = cos_ref[...]
    sin_k = sin_ref[...]
    reps = qd // kd
    cos_q = jnp.concatenate([cos_k] * reps, axis=1)
    sin_q = jnp.concatenate([sin_k] * reps, axis=1)
    q = _norm_rope(z[:, :qd], qw_ref[...], cos_q, sin_q, gq_ref[...]) * (LOG2E * GQA_DH ** -0.5)
    k = _norm_rope(z[:, qd:qd + kd], kw_ref[...], cos_k, sin_k, gk_ref[...])
    v = z[:, qd + kd:qd + 2 * kd]
    for h in range(GQA_HEADS):
        q_ref[0, h] = q[:, h * GQA_DH:(h + 1) * GQA_DH].astype(BF16)
    ones = jnp.ones((z.shape[0], LANES - GQA_DH), BF16)
    for h in range(GQA_KV_HEADS):
        k_ref[0, h] = k[:, h * GQA_DH:(h + 1) * GQA_DH].astype(BF16)
        v_ref[0, h] = jnp.concatenate([v[:, h * GQA_DH:(h + 1) * GQA_DH].astype(BF16), ones], axis=1)


def _group_mean_matrix(n, group):
    idx = np.arange(n) // group
    return jnp.asarray((idx[:, None] == idx[None, :]) / group, BF16)


def _gqa_prep(z, q_norm_w, k_norm_w, s_len, c_len):
    bsz, t, width = z.shape
    qd = GQA_HEADS * GQA_DH
    kd = GQA_KV_HEADS * GQA_DH
    cos, sin = _rope_tables(s_len, c_len, GQA_KV_HEADS)
    head_spec = lambda n, w=GQA_DH: pl.BlockSpec((1, n, TOK, w), lambda b, i: (b, 0, i, 0))
    return pl.pallas_call(
        _gqa_prep_kernel,
        grid=(bsz, t // TOK),
        in_specs=[
            pl.BlockSpec((1, TOK, width), lambda b, i: (b, i, 0)),
            _const_spec((1, qd)), _const_spec((1, kd)),
            pl.BlockSpec((TOK, kd), lambda b, i: (i, 0)),
            pl.BlockSpec((TOK, kd), lambda b, i: (i, 0)),
            _const_spec((qd, qd)), _const_spec((kd, kd)),
        ],
        out_specs=[head_spec(GQA_HEADS), head_spec(GQA_KV_HEADS), head_spec(GQA_KV_HEADS, LANES)],
        out_shape=[jax.ShapeDtypeStruct((bsz, GQA_HEADS, t, GQA_DH), BF16),
                   jax.ShapeDtypeStruct((bsz, GQA_KV_HEADS, t, GQA_DH), BF16),
                   jax.ShapeDtypeStruct((bsz, GQA_KV_HEADS, t, LANES), BF16)],
        compiler_params=_cparams(("parallel", "parallel")),
        name="gqa_prep",
    )(z, jnp.tile(q_norm_w, GQA_HEADS).reshape(1, qd), jnp.tile(k_norm_w, GQA_KV_HEADS).reshape(1, kd),
      cos, sin, _group_mean_matrix(qd, GQA_DH), _group_mean_matrix(kd, GQA_DH))


def _flash_kernel(q_ref, k_ref, v_ref, o_ref, m_s, acc_s, *, group, tk, sub):
    tq = q_ref.shape[2]
    units = [(g, r) for g in range(group) for r in range(0, tq, sub)]
    m_s[...] = jnp.full_like(m_s, -jnp.inf)
    acc_s[...] = jnp.zeros_like(acc_s)

    def kv_step(j, carry):
        rows_k = pl.ds(pl.multiple_of(j * tk, tk), tk)
        k = k_ref[0, 0, rows_k, :]
        v = v_ref[0, 0, rows_k, :]
        scores = lambda u: _dot_nt(q_ref[0, units[u][0], units[u][1]:units[u][1] + sub, :], k)
        s_next = scores(0)
        for u in range(len(units)):
            tiles = _lane_tiles(s_next)
            if u + 1 < len(units):
                s_next = scores(u + 1)
            m_prev = m_s[u]
            m_new = jnp.maximum(m_prev, jnp.max(functools.reduce(jnp.maximum, tiles), axis=-1, keepdims=True))
            p = [jnp.exp2(t - m_new).astype(BF16) for t in tiles]
            acc_s[u] = jnp.exp2(m_prev - m_new) * acc_s[u] + _dot(jnp.concatenate(p, axis=1), v)
            m_s[u] = m_new
        return carry

    lax.fori_loop(0, k_ref.shape[2] // tk, kv_step, 0)
    for u, (g, r) in enumerate(units):
        acc = acc_s[u]
        o_ref[0, r:r + sub, g * GQA_DH:(g + 1) * GQA_DH] = acc[:, :GQA_DH] / acc[:, GQA_DH:GQA_DH + 1]


KV_TILE_MAX = 1536


def _kv_tile(t):
    return max(n for n in range(LANES, KV_TILE_MAX + 1, LANES) if t % n == 0)


def _flash(q, k, v, s_len, tq, tk):
    bsz, _, t, dh = q.shape
    assert t % tk == 0 and s_len % tq == 0 and tq % TOK == 0
    group = GQA_HEADS // GQA_KV_HEADS
    n_units = group * (tq // TOK)
    return pl.pallas_call(
        functools.partial(_flash_kernel, group=group, tk=tk, sub=TOK),
        grid=(bsz, GQA_KV_HEADS, s_len // tq),
        in_specs=[
            pl.BlockSpec((1, group, tq, dh), lambda b, h, i: (b, h, i, 0)),
            pl.BlockSpec((1, 1, t, dh), lambda b, h, i: (b, h, 0, 0)),
            pl.BlockSpec((1, 1, t, LANES), lambda b, h, i: (b, h, 0, 0)),
        ],
        out_specs=pl.BlockSpec((1, tq, group * dh), lambda b, h, i: (b, i, h)),
        out_shape=jax.ShapeDtypeStruct((bsz, s_len, GQA_HEADS * dh), F32),
        scratch_shapes=[pltpu.VMEM((n_units, TOK, LANES), F32), pltpu.VMEM((n_units, TOK, LANES), F32)],
        compiler_params=_cparams(("parallel", "parallel", "arbitrary")),
        name="gqa_flash",
    )(q, k, v)


def kernel(x, c, ctx, c_ctx, norm1_w, norm2_w, w_mod, b_mod, w_ff1, w_ff2,
           w_in_even, na_rpb, lru_conv_w, lru_conv_b, lru_wa, lru_ba, lru_wx, lru_bx, lru_lambda, w_out_even,
           w_in_odd, gla_wa2, gla_ba, gla_norm_w, gqa_q_norm_w, gqa_k_norm_w, w_out_odd, final_norm_w):
    bsz, s_len, d = x.shape
    c_len = ctx.shape[1]
    depth = w_mod.shape[0]
    assert s_len % TOK == 0 and c_len % TOK == 0 and depth == 2
    n_lat = s_len // TOK

    pad = (-(bsz + 1)) % 8
    cvec = jnp.concatenate([c_ctx[None], c, jnp.zeros((pad, d), F32)], axis=0)
    mods = _modulation(cvec, w_mod, b_mod)
    h = jnp.concatenate([x, ctx], axis=1)

    na_w = NA_HEADS * NA_DH
    ch = lru_lambda.shape[-1]
    qkv, xg = _inproj(h, norm1_w[0], mods[0], w_in_even[0].astype(BF16),
                      ((0, 3 * na_w), (3 * na_w, 3 * na_w + 2 * ch)), (BF16, F32), s_len)
    na_out = _na(qkv, na_rpb[0], s_len, c_len)
    w_gate = jnp.stack([jnp.concatenate([_block_diag(lru_wa[0, dr]), _block_diag(lru_wx[0, dr])], axis=1)
                        for dr in range(2)]).astype(BF16)
    b_gate = jnp.concatenate([lru_ba[0], lru_bx[0]], axis=-1).reshape(2, 1, 2 * ch)
    lru_args = (xg, lru_conv_w[0], lru_conv_b[0], w_gate, b_gate, lru_lambda[0], n_lat)
    lru_b = _lru(*lru_args, direction=1)
    lru_out = _lru(*lru_args, direction=0, other=lru_b)
    wo = w_out_even[0].astype(BF16)
    h = _mlp(h, na_out, lru_out, mods[0], wo[:na_w], wo[na_w:], norm2_w[0],
             w_ff1[0].astype(BF16), w_ff2[0].astype(BF16), final_norm_w, s_len, s_len + c_len, final=False)

    key = GLA_HEADS * GLA_DK
    val = GLA_HEADS * GLA_DV
    gla_w = 2 * key + 2 * val
    gqa_w = (GQA_HEADS + 2 * GQA_KV_HEADS) * GQA_DH
    w_in = w_in_odd[0]
    w_in = jnp.concatenate([w_in[:, :gla_w], w_in[:, gla_w + 2 * GLA_RANK:], w_in[:, gla_w:gla_w + 2 * GLA_RANK]],
                           axis=1).astype(BF16)
    qkvg, gqa_z, lr = _inproj(h, norm1_w[1], mods[1], w_in,
                              ((0, gla_w), (gla_w, gla_w + gqa_w), (gla_w + gqa_w, gla_w + gqa_w + 2 * GLA_RANK)),
                              (F32, F32, F32), s_len)
    gla_b = _gla(qkvg, lr, gla_wa2[0], gla_ba[0], n_lat, direction=1)
    gla_out = _gla(qkvg, lr, gla_wa2[0], gla_ba[0], n_lat, direction=0, other=gla_b, norm_w=gla_norm_w[0])
    q, k, v = _gqa_prep(gqa_z, gqa_q_norm_w[0], gqa_k_norm_w[0], s_len, c_len)
    gqa_out = _flash(q, k, v, s_len, tq=2 * TOK, tk=_kv_tile(s_len + c_len))
    wo = w_out_odd[0].astype(BF16)
    return _mlp(h, gla_out, gqa_out, mods[1], wo[:val], wo[val:], norm2_w[1],
                w_ff1[1].astype(BF16), w_ff2[1].astype(BF16), final_norm_w, s_len, s_len, final=True)
```

```python
import functools

import numpy as np
import jax
import jax.numpy as jnp
from jax import lax
from jax.experimental import pallas as pl
from jax.experimental.pallas import tpu as pltpu

F32 = jnp.float32
BF16 = jnp.bfloat16

EPS = 1e-6
GRID_W = 64
LANES = 128
N_MOD = 6
TOK = 256
DTOK = 512
NEG = -1e30

NA_HEADS = 8
NA_DH = 64
NA_KH = 8
NA_KW = 16
LRU_C = 8.0
LRU_SEG = 8
GLA_HEADS = 4
GLA_DK = 64
GLA_DV = 128
GLA_RANK = 16
GLA_TAU = 16.0
GLA_CHUNK = 64
GQA_HEADS = 8
GQA_KV_HEADS = 2
GQA_DH = 64
ROPE_THETA = 10000.0
LOG2E = 1.4426950408889634

VMEM_LIMIT = 56 * 1024 * 1024


def _cparams(sem):
    return pltpu.CompilerParams(dimension_semantics=sem, vmem_limit_bytes=VMEM_LIMIT)


def _const_spec(shape):
    nd = len(shape)
    return pl.BlockSpec(shape, lambda *_: (0,) * nd, pipeline_mode=pl.Buffered(1))


def _dot(a, b):
    return jnp.dot(a, b, preferred_element_type=F32)


def _dot_nt(a, b):
    return lax.dot_general(a, b, (((1,), (1,)), ((), ())), preferred_element_type=F32)


def _dot_tn(a, b):
    return lax.dot_general(a, b, (((0,), (0,)), ((), ())), preferred_element_type=F32)


def _split_hi_lo(x):
    hi = x.astype(BF16)
    lo = (x - hi.astype(F32)).astype(BF16)
    return hi, lo


def _rms(x, w):
    ms = jnp.mean(x * x, axis=-1, keepdims=True)
    return x * lax.rsqrt(ms + EPS) * w


def _softplus(z):
    return jnp.maximum(z, 0.0) + jnp.log1p(jnp.exp(-jnp.abs(z)))


def _mod_kernel(c_ref, w_ref, b_ref, o_ref):
    c = c_ref[...]
    s = c * jax.nn.sigmoid(c)
    o_ref[0] = jnp.dot(s, w_ref[0], precision=lax.Precision.HIGHEST, preferred_element_type=F32) + b_ref[0]


def _modulation(cvec, w_mod, b_mod):
    depth, d, _ = w_mod.shape
    rows = cvec.shape[0]
    out = pl.pallas_call(
        _mod_kernel,
        grid=(depth, N_MOD),
        in_specs=[
            pl.BlockSpec((rows, d), lambda i, j: (0, 0)),
            pl.BlockSpec((1, d, d), lambda i, j: (i, 0, j)),
            pl.BlockSpec((1, 1, d), lambda i, j: (i, 0, j)),
        ],
        out_specs=pl.BlockSpec((1, rows, d), lambda i, j: (i, 0, j)),
        out_shape=jax.ShapeDtypeStruct((depth, rows, N_MOD * d), F32),
        compiler_params=_cparams(("arbitrary", "arbitrary")),
        name="modulation",
    )(cvec, w_mod, b_mod.reshape(depth, 1, N_MOD * d))
    return out.reshape(depth, rows, N_MOD, d)


def _mod_spec(d, n_lat_tiles):
    return pl.BlockSpec((1, N_MOD, d), lambda b, t: (jnp.where(t >= n_lat_tiles, 0, b + 1), 0, 0))


def _stream_specs(h, s_len):
    n_lat = s_len // DTOK
    if not isinstance(h, tuple):
        return [h], [pl.BlockSpec((1, DTOK, h.shape[-1]), lambda b, i: (b, i, 0))], 0
    lat, ctx = h
    c_len, d = ctx.shape[1:]
    assert c_len <= DTOK and lat.shape[1] == s_len
    return ([lat, ctx],
            [pl.BlockSpec((1, DTOK, d), lambda b, i: (b, jnp.minimum(i, n_lat - 1), 0)),
             pl.BlockSpec((1, c_len, d), lambda b, i: (b, 0, 0))], c_len)


def _per_stream(body, h_refs, n_lat_tiles, ctx_rows):
    if len(h_refs) == 1:
        body(h_refs[0], DTOK)
        return
    i = pl.program_id(1)
    pl.when(i < n_lat_tiles)(lambda: body(h_refs[0], DTOK))
    pl.when(i >= n_lat_tiles)(lambda: body(h_refs[1], ctx_rows))


def _inproj_kernel(*refs, splits, n_src, n_lat_tiles, ctx_rows):
    h_refs = refs[:n_src]
    nw_ref, mod_ref, w_ref = refs[n_src:n_src + 3]
    o_refs = refs[n_src + 3:]

    def body(h_ref, rows):
        mod = mod_ref[0]
        u = _rms(h_ref[0], nw_ref[...]) * (1.0 + mod[1:2]) + mod[0:1]
        z = _dot(u.astype(BF16), w_ref[...])
        for o_ref, (lo, hi) in zip(o_refs, splits):
            o_ref[0, :rows] = z[:, lo:hi].astype(o_ref.dtype)

    _per_stream(body, h_refs, n_lat_tiles, ctx_rows)


def _inproj(h, norm_w, mods, w, splits, dtypes, s_len, t):
    d, n = w.shape
    assert s_len % DTOK == 0
    srcs, src_specs, ctx_rows = _stream_specs(h, s_len)
    bsz = srcs[0].shape[0]
    return pl.pallas_call(
        functools.partial(_inproj_kernel, splits=splits, n_src=len(srcs), n_lat_tiles=s_len // DTOK,
                          ctx_rows=ctx_rows),
        grid=(bsz, pl.cdiv(t, DTOK)),
        in_specs=src_specs + [
            _const_spec((1, d)),
            _mod_spec(d, s_len // DTOK),
            _const_spec((d, n)),
        ],
        out_specs=[pl.BlockSpec((1, DTOK, hi - lo), lambda b, i: (b, i, 0)) for lo, hi in splits],
        out_shape=[jax.ShapeDtypeStruct((bsz, t, hi - lo), dt) for (lo, hi), dt in zip(splits, dtypes)],
        compiler_params=_cparams(("parallel", "parallel")),
        name="inproj",
    )(*srcs, norm_w.reshape(1, d), mods, w)


def _mlp_kernel(*refs, ff_chunk, final, n_src, n_lat_tiles, ctx_rows):
    h_refs = refs[:n_src]
    ma_ref, mb_ref, mod_ref, woa_ref, wob_ref, nw_ref, w1_ref, w2_ref, fw_ref, o_ref = refs[n_src:]

    def body(h_ref, rows):
        mod = mod_ref[0]
        y = (_dot(ma_ref[0, :rows].astype(BF16), woa_ref[...])
             + _dot(mb_ref[0, :rows].astype(BF16), wob_ref[...]))
        h1 = h_ref[0] + mod[2:3] * y
        u = (_rms(h1, nw_ref[...]) * (1.0 + mod[4:5]) + mod[3:4]).astype(BF16)
        d_ff = w1_ref.shape[1]
        acc = jnp.zeros(h1.shape, F32)
        for c in range(d_ff // ff_chunk):
            sl = slice(c * ff_chunk, (c + 1) * ff_chunk)
            a = jnp.maximum(_dot(u, w1_ref[:, sl]), 0.0)
            acc = acc + _dot((a * a).astype(BF16), w2_ref[sl, :])
        h2 = h1 + mod[5:6] * acc
        if final:
            h2 = _rms(h2, fw_ref[...])
        o_ref[0, :rows] = h2

    _per_stream(body, h_refs, n_lat_tiles, ctx_rows)


def _mlp(h, mix_a, mix_b, mods, wo_a, wo_b, norm_w, w1, w2, final_w, s_len, out_rows, final):
    da, db = mix_a.shape[-1], mix_b.shape[-1]
    d, d_ff = w1.shape
    assert s_len % DTOK == 0
    srcs, src_specs, ctx_rows = _stream_specs(h, s_len)
    bsz = srcs[0].shape[0]
    tok_spec = lambda width: pl.BlockSpec((1, DTOK, width), lambda b, i: (b, i, 0))
    return pl.pallas_call(
        functools.partial(_mlp_kernel, ff_chunk=512, final=final, n_src=len(srcs), n_lat_tiles=s_len // DTOK,
                          ctx_rows=ctx_rows),
        grid=(bsz, pl.cdiv(out_rows, DTOK)),
        in_specs=src_specs + [
            tok_spec(da), tok_spec(db),
            _mod_spec(d, s_len // DTOK),
            _const_spec((da, d)), _const_spec((db, d)), _const_spec((1, d)),
            _const_spec((d, d_ff)), _const_spec((d_ff, d)), _const_spec((1, d)),
        ],
        out_specs=tok_spec(d),
        out_shape=jax.ShapeDtypeStruct((bsz, out_rows, d), F32),
        compiler_params=_cparams(("parallel", "parallel")),
        name="mlp",
    )(*srcs, mix_a, mix_b, mods, wo_a, wo_b, norm_w.reshape(1, d), w1, w2, final_w.reshape(1, d))


NA_QROWS = TOK // GRID_W
NA_WIN = NA_KH + NA_QROWS


def _na_plan(rows):
    n_blk = rows // NA_QROWS
    r = np.arange(rows).reshape(n_blk, NA_QROWS)
    start = np.clip(r - NA_KH // 2, 0, rows - NA_KH)
    u0 = np.clip(r[:, 0] - NA_KH // 2, 0, rows - NA_WIN)
    key = u0[:, None, None] + np.arange(NA_WIN)[None, None, :]
    valid = (key >= start[:, :, None]) & (key < start[:, :, None] + NA_KH)
    d_row = np.where(valid, key - r[:, :, None] + (NA_KH - 1), 0)
    flat = np.concatenate([valid.reshape(n_blk, -1), d_row.reshape(n_blk, -1)], axis=1)
    _, first, pat = np.unique(flat, axis=0, return_index=True, return_inverse=True)
    return u0.astype(np.int32), pat.reshape(-1).astype(np.int32), valid[first], d_row[first]


def _na_bias_table(rpb, valid, d_row):
    n_pat = valid.shape[0]
    heads = rpb.shape[0]
    cols = np.arange(GRID_W)
    col_start = np.clip(cols - NA_KW // 2, 0, GRID_W - NA_KW)
    col_valid = (cols[None, :] >= col_start[:, None]) & (cols[None, :] < col_start[:, None] + NA_KW)
    d_col = np.clip(cols[None, :] - cols[:, None], 1 - NA_KW, NA_KW - 1) + (NA_KW - 1)
    row_sel = (d_row[..., None] == np.arange(2 * NA_KH - 1)).astype(np.float32)
    col_sel = (d_col[None] == np.arange(2 * NA_KW - 1)[:, None, None]).astype(np.float32)
    hp = lax.Precision.HIGHEST
    by_row = jnp.einsum('hrc,piur->phiuc', rpb, row_sel, precision=hp)
    bias = jnp.einsum('phiuc,cqw->phiquw', by_row, col_sel, precision=hp)
    keep = valid[:, None, :, None, :, None] & col_valid[None, None, None, :, None, :]
    return jnp.where(keep, bias, NEG).reshape(n_pat, heads, TOK, NA_WIN * GRID_W)


def _softmax_tiles(tiles):
    m = jnp.max(functools.reduce(jnp.maximum, tiles), axis=-1, keepdims=True)
    p = [jnp.exp(t - m) for t in tiles]
    denom = jnp.sum(functools.reduce(jnp.add, p), axis=-1, keepdims=True)
    return [t.astype(BF16) for t in p], denom


def _lane_tiles(x):
    return [x[:, j:j + LANES] for j in range(0, x.shape[1], LANES)]


def _na_kernel(u0_ref, pat_ref, q_ref, k_ref, v_ref, kc_ref, vc_ref, bias_ref, o_ref, *, n_lat):
    i = pl.program_id(1)
    win = NA_WIN * GRID_W
    n_loc = win // LANES

    def attend(local):
        q = q_ref[0] * 0.125
        kc = kc_ref[0]
        vc = vc_ref[0]
        if local:
            row0 = pl.multiple_of(u0_ref[i] * GRID_W, GRID_W)
            kw = k_ref[0, pl.ds(row0, win), :]
            vw = v_ref[0, pl.ds(row0, win), :]

        def scores(h):
            sl = slice(h * NA_DH, (h + 1) * NA_DH)
            s_ctx = _lane_tiles(_dot_nt(q[:, sl], kc[:, sl]))
            if not local:
                return s_ctx
            return _lane_tiles(_dot_nt(q[:, sl], kw[:, sl]) + bias_ref[0, h]) + s_ctx

        s_next = scores(0)
        for h in range(NA_HEADS):
            sl = slice(h * NA_DH, (h + 1) * NA_DH)
            tiles = s_next
            if h + 1 < NA_HEADS:
                s_next = scores(h + 1)
            p, denom = _softmax_tiles(tiles)
            if local:
                o = (_dot(jnp.concatenate(p[:n_loc], axis=1), vw[:, sl])
                     + _dot(jnp.concatenate(p[n_loc:], axis=1), vc[:, sl]))
            else:
                o = _dot(jnp.concatenate(p, axis=1), vc[:, sl])
            o_ref[0, :, sl] = o / denom

    pl.when(i < n_lat)(functools.partial(attend, True))
    pl.when(i >= n_lat)(functools.partial(attend, False))


def _na(qkv, rpb, s_len, c_len):
    bsz, t, _ = qkv.shape
    width = NA_HEADS * NA_DH
    rows = s_len // GRID_W
    n_lat = s_len // TOK
    n_steps = t // TOK
    assert rows >= NA_WIN and rows % NA_QROWS == 0 and c_len % TOK == 0 and s_len % c_len == 0
    u0, pat, valid, d_row = _na_plan(rows)
    bias = _na_bias_table(rpb, valid, d_row)
    ctx_steps = np.zeros(n_steps - n_lat, np.int32)
    u0 = jnp.asarray(np.concatenate([u0, ctx_steps]))
    pat = jnp.asarray(np.concatenate([pat, ctx_steps]))
    tok_spec = pl.BlockSpec((1, TOK, width), lambda b, i, u0_ref, pat_ref: (b, i, 0))
    ctx_blk = s_len // c_len
    return pl.pallas_call(
        functools.partial(_na_kernel, n_lat=n_lat),
        grid_spec=pltpu.PrefetchScalarGridSpec(
            num_scalar_prefetch=2,
            grid=(bsz, n_steps),
            in_specs=[
                tok_spec,
                pl.BlockSpec((1, s_len, width), lambda b, i, u0_ref, pat_ref: (b, 0, 1),
                             pipeline_mode=pl.Buffered(1)),
                pl.BlockSpec((1, s_len, width), lambda b, i, u0_ref, pat_ref: (b, 0, 2),
                             pipeline_mode=pl.Buffered(1)),
                pl.BlockSpec((1, c_len, width), lambda b, i, u0_ref, pat_ref: (b, ctx_blk, 1)),
                pl.BlockSpec((1, c_len, width), lambda b, i, u0_ref, pat_ref: (b, ctx_blk, 2)),
                pl.BlockSpec((1, NA_HEADS, TOK, NA_WIN * GRID_W),
                             lambda b, i, u0_ref, pat_ref: (pat_ref[i], 0, 0, 0)),
            ],
            out_specs=tok_spec,
        ),
        out_shape=jax.ShapeDtypeStruct((bsz, t, width), F32),
        compiler_params=_cparams(("parallel", "arbitrary")),
        name="na",
    )(u0, pat, qkv, qkv, qkv, qkv, qkv, bias)


def _lru_tile_index(i, n_lat, n_tiles, direction):
    if direction == 0:
        return (i + n_lat) % n_tiles
    return n_tiles - 1 - i


def _lru_row_permutation():
    seg = TOK // LRU_SEG
    p = np.arange(TOK)
    src = (p % LRU_SEG) * seg + p // LRU_SEG
    return (src[:, None] == np.arange(TOK)[None, :]).astype(np.float32)


def _shift_rows(group, boundary_row, down):
    sub = lax.broadcasted_iota(jnp.int32, group.shape, 0)
    if down:
        return jnp.where(sub == 0, boundary_row, pltpu.roll(group, 1, axis=0))
    return jnp.where(sub == LRU_SEG - 1, boundary_row, pltpu.roll(group, LRU_SEG - 1, axis=0))


def _lru_kernel(*refs, direction, n_lat, n_tiles, combine):
    if combine:
        (x_ref, prev_ref, next_ref, cw_ref, cb_ref, wg_ref, bg_ref, lam_ref, perm_ref, g_ref, other_ref,
         unperm_ref, o_ref, carry_s) = refs
    else:
        (x_ref, prev_ref, next_ref, cw_ref, cb_ref, wg_ref, bg_ref, lam_ref, perm_ref,
         o_ref, carry_s) = refs
    i = pl.program_id(1)
    tile = _lru_tile_index(i, n_lat, n_tiles, direction)
    ch = x_ref.shape[-1]
    halo = prev_ref.shape[1]
    seg = TOK // LRU_SEG
    grp = LRU_SEG

    @pl.when(i == 0)
    def _():
        carry_s[...] = jnp.zeros_like(carry_s)

    first = jnp.logical_or(tile == 0, tile == n_lat)
    last = jnp.logical_or(tile == n_lat - 1, tile == n_tiles - 1)
    perm = perm_ref[...]
    x = _dot(perm, x_ref[0])
    prev = jnp.where(first, 0.0, prev_ref[0].astype(F32))
    nxt = jnp.where(last, 0.0, next_ref[0].astype(F32))
    wrap_m1 = _shift_rows(x[(seg - 1) * grp:], prev[halo - 1:halo], down=True)
    wrap_m2 = _shift_rows(x[(seg - 2) * grp:(seg - 1) * grp], prev[halo - 2:halo - 1], down=True)
    wrap_p1 = _shift_rows(x[:grp], nxt[0:1], down=False)
    x_m1 = jnp.concatenate([wrap_m1, x[:-grp]], axis=0)
    x_m2 = jnp.concatenate([wrap_m2, wrap_m1, x[:-2 * grp]], axis=0)
    x_p1 = jnp.concatenate([x[grp:], wrap_p1], axis=0)
    cw = cw_ref[...]
    xc = cw[0:1] * x_m2 + cw[1:2] * x_m1 + cw[2:3] * x + cw[3:4] * x_p1 + cb_ref[...]

    pre = _dot(xc.astype(BF16), wg_ref[0]) + bg_ref[0]
    r_gate = jax.nn.sigmoid(pre[:, :ch])
    i_gate = jax.nn.sigmoid(pre[:, ch:])
    log_a = (-LRU_C) * r_gate * _softplus(-lam_ref[0])
    a = jnp.exp(log_a)
    bb = jnp.sqrt(1.0 - a * a) * (i_gate * xc)

    steps = range(seg) if direction == 0 else range(seg - 1, -1, -1)
    h = jnp.zeros((grp, ch), F32)
    cum = jnp.ones((grp, ch), F32)
    h_loc = [None] * seg
    cum_loc = [None] * seg
    for j in steps:
        a_j = a[j * grp:(j + 1) * grp]
        h = a_j * h + bb[j * grp:(j + 1) * grp]
        cum = cum * a_j
        h_loc[j] = h
        cum_loc[j] = cum

    state = carry_s[...]
    entering = [None] * LRU_SEG
    for s in (range(LRU_SEG) if direction == 0 else range(LRU_SEG - 1, -1, -1)):
        entering[s] = state
        state = h[s:s + 1] + cum[s:s + 1] * state
    carry_s[...] = state
    entering = jnp.concatenate(entering, axis=0)
    y = jnp.concatenate([h_loc[j] + cum_loc[j] * entering for j in range(seg)], axis=0)
    if combine:
        y = (y + other_ref[0]) * jax.nn.gelu(_dot(perm, g_ref[0]))
        o_ref[0] = _dot(unperm_ref[...], y.astype(BF16)).astype(o_ref.dtype)
    else:
        o_ref[0] = y


def _lru(xg, conv_w, conv_b, w_gate, b_gate, lam, n_lat, direction, other=None):
    bsz, t, two_ch = xg.shape
    ch = two_ch // 2
    n_tiles = t // TOK
    halo = 16
    combine = other is not None
    tile_of = functools.partial(_lru_tile_index, n_lat=n_lat, n_tiles=n_tiles, direction=direction)
    per_tile = TOK // halo
    perm = _lru_row_permutation()
    in_specs = [
        pl.BlockSpec((1, TOK, ch), lambda b, i: (b, tile_of(i), 0)),
        pl.BlockSpec((1, halo, ch), lambda b, i: (b, jnp.maximum(tile_of(i) * per_tile - 1, 0), 0)),
        pl.BlockSpec((1, halo, ch), lambda b, i: (b, jnp.minimum((tile_of(i) + 1) * per_tile, t // halo - 1), 0)),
        _const_spec((4, ch)), _const_spec((1, ch)),
        pl.BlockSpec((1, ch, 2 * ch), lambda b, i: (direction, 0, 0), pipeline_mode=pl.Buffered(1)),
        pl.BlockSpec((1, 1, 2 * ch), lambda b, i: (direction, 0, 0), pipeline_mode=pl.Buffered(1)),
        pl.BlockSpec((1, 1, ch), lambda b, i: (direction, 0, 0), pipeline_mode=pl.Buffered(1)),
        _const_spec((TOK, TOK)),
    ]
    args = [xg, xg, xg, conv_w, conv_b.reshape(1, ch), w_gate, b_gate, lam.reshape(2, 1, ch), jnp.asarray(perm, BF16)]
    if combine:
        in_specs += [pl.BlockSpec((1, TOK, ch), lambda b, i: (b, tile_of(i), 1)),
                     pl.BlockSpec((1, TOK, ch), lambda b, i: (b, tile_of(i), 0)),
                     _const_spec((TOK, TOK))]
        args += [xg, other, jnp.asarray(perm.T, BF16)]
    return pl.pallas_call(
        functools.partial(_lru_kernel, direction=direction, n_lat=n_lat, n_tiles=n_tiles, combine=combine),
        grid=(bsz, n_tiles),
        in_specs=in_specs,
        out_specs=pl.BlockSpec((1, TOK, ch), lambda b, i: (b, tile_of(i), 0)),
        out_shape=jax.ShapeDtypeStruct((bsz, t, ch), BF16 if combine else F32),
        scratch_shapes=[pltpu.VMEM((1, ch), F32)],
        compiler_params=_cparams(("parallel", "arbitrary")),
        name="lru_fwd" if direction == 0 else "lru_bwd",
    )(*args)


def _block_diag(w):
    heads, n, _ = w.shape
    eye = jnp.eye(heads, dtype=w.dtype)
    return (eye[:, None, :, None] * w[:, :, None, :]).reshape(heads * n, heads * n)


def _gla_kernel(*refs, direction, combine):
    if combine:
        qkvg_ref, lr_ref, wa_ref, ba_ref, tri_ref, other_ref, nw_ref, o_ref, state_s = refs
    else:
        qkvg_ref, lr_ref, wa_ref, ba_ref, tri_ref, o_ref, state_s = refs
    key = GLA_HEADS * GLA_DK
    val = GLA_HEADS * GLA_DV
    nchunk = TOK // GLA_CHUNK

    @pl.when(pl.program_id(1) == 0)
    def _():
        state_s[...] = jnp.zeros_like(state_s)

    z = qkvg_ref[0]
    q = z[:, :key] * (GLA_DK ** -0.5)
    k = z[:, key:2 * key]
    v = z[:, 2 * key:2 * key + val].astype(BF16)
    lr = lr_ref[0][:, direction * GLA_RANK:(direction + 1) * GLA_RANK]
    logit = _dot(lr.astype(BF16), wa_ref[0].astype(BF16)) + ba_ref[0]
    log_a = (jnp.minimum(logit, 0.0) - jnp.log1p(jnp.exp(-jnp.abs(logit)))) / GLA_TAU

    g_hi, g_lo = _split_hi_lo(log_a)
    tri = tri_ref[...]
    b = _dot(tri, g_hi) + _dot(tri, g_lo)
    q_in = (q * jnp.exp(b)).astype(BF16)
    k_in = (k * jnp.exp(-b)).astype(BF16)
    chunks = [slice(c * GLA_CHUNK, (c + 1) * GLA_CHUNK) for c in range(nchunk)]
    ends = [(c + 1) * GLA_CHUNK - 1 if direction == 0 else c * GLA_CHUNK for c in range(nchunk)]
    b_end = jnp.concatenate([jnp.broadcast_to(b[e:e + 1], (GLA_CHUNK, key)) for e in ends], axis=0)
    k_out = (k * jnp.exp(b_end - b)).astype(BF16)
    causal = tri > 0
    ones = jnp.ones((GLA_CHUNK, GLA_DV), BF16)
    decay = [jnp.exp(_dot_tn(g_hi[rc], ones) + _dot_tn(g_lo[rc], ones)) for rc in chunks]

    chunk_order = range(nchunk) if direction == 0 else range(nchunk - 1, -1, -1)
    for h in range(GLA_HEADS):
        ks = slice(h * GLA_DK, (h + 1) * GLA_DK)
        vs = slice(h * GLA_DV, (h + 1) * GLA_DV)
        v_h = v[:, vs]
        att = jnp.where(causal, _dot_nt(q_in[:, ks], k_in[:, ks]), 0.0)
        o = _dot(att.astype(BF16), v_h)
        state = state_s[h]
        inter = [None] * nchunk
        for c in chunk_order:
            inter[c] = _dot(q_in[chunks[c], ks], state.astype(BF16))
            state = decay[c][ks] * state + _dot_tn(k_out[chunks[c], ks], v_h[chunks[c]])
        state_s[h] = state
        o = o + jnp.concatenate(inter, axis=0)
        if combine:
            o = o + other_ref[0, :, vs]
            gate = z[:, 2 * key + val + h * GLA_DV:2 * key + val + (h + 1) * GLA_DV]
            o = _rms(o, nw_ref[...]) * (gate * jax.nn.sigmoid(gate))
        o_ref[0, :, vs] = o


def _gla(qkvg, lr, wa2, ba, n_lat, direction, other=None, norm_w=None):
    bsz, t, width = qkvg.shape
    key = GLA_HEADS * GLA_DK
    val = GLA_HEADS * GLA_DV
    n_tiles = t // TOK
    combine = other is not None
    tile_of = functools.partial(_lru_tile_index, n_lat=n_lat, n_tiles=n_tiles, direction=direction)
    pos = np.arange(TOK)
    same = (pos[:, None] // GLA_CHUNK) == (pos[None, :] // GLA_CHUNK)
    tri = same & ((pos[None, :] <= pos[:, None]) if direction == 0 else (pos[None, :] >= pos[:, None]))
    tri = jnp.asarray(tri, BF16)
    in_specs = [
        pl.BlockSpec((1, TOK, width), lambda b, i: (b, tile_of(i), 0)),
        pl.BlockSpec((1, TOK, lr.shape[-1]), lambda b, i: (b, tile_of(i), 0)),
        pl.BlockSpec((1, GLA_RANK, key), lambda b, i: (direction, 0, 0), pipeline_mode=pl.Buffered(1)),
        pl.BlockSpec((1, 1, key), lambda b, i: (direction, 0, 0), pipeline_mode=pl.Buffered(1)),
        _const_spec((TOK, TOK)),
    ]
    args = [qkvg, lr, wa2, ba.reshape(2, 1, key), tri]
    if combine:
        in_specs += [pl.BlockSpec((1, TOK, val), lambda b, i: (b, tile_of(i), 0)), _const_spec((1, GLA_DV))]
        args += [other, norm_w.reshape(1, GLA_DV)]
    return pl.pallas_call(
        functools.partial(_gla_kernel, direction=direction, combine=combine),
        grid=(bsz, n_tiles),
        in_specs=in_specs,
        out_specs=pl.BlockSpec((1, TOK, val), lambda b, i: (b, tile_of(i), 0)),
        out_shape=jax.ShapeDtypeStruct((bsz, t, val), F32),
        scratch_shapes=[pltpu.VMEM((GLA_HEADS, GLA_DK, GLA_DV), F32)],
        compiler_params=_cparams(("parallel", "arbitrary")),
        name="gla_fwd" if direction == 0 else "gla_bwd",
    )(*args)


def _rope_tables(s_len, c_len, reps):
    quarter = GQA_DH // 4
    inv = ROPE_THETA ** (-np.arange(quarter, dtype=np.float32) / quarter)
    t = np.arange(s_len)
    ang_r = (t // GRID_W).astype(np.float32)[:, None] * inv[None, :]
    ang_c = (t % GRID_W).astype(np.float32)[:, None] * inv[None, :]
    cos = np.concatenate([np.cos(ang_r)] * 2 + [np.cos(ang_c)] * 2, axis=1)
    sin = np.concatenate([-np.sin(ang_r), np.sin(ang_r), -np.sin(ang_c), np.sin(ang_c)], axis=1)
    cos = np.concatenate([cos, np.ones((c_len, GQA_DH), np.float32)], axis=0)
    sin = np.concatenate([sin, np.zeros((c_len, GQA_DH), np.float32)], axis=0)
    return (jnp.asarray(np.tile(cos, (1, reps)), F32), jnp.asarray(np.tile(sin, (1, reps)), F32))


def _rope_swap(x):
    lane = lax.broadcasted_iota(jnp.int32, x.shape, 1)
    return jnp.where(lane % 32 < 16, pltpu.roll(x, 128 - 16, axis=1), pltpu.roll(x, 16, axis=1))


def _norm_rope(x, nw, cos, sin, gmean):
    hi, lo = _split_hi_lo(x * x)
    ms = _dot(hi, gmean) + _dot(lo, gmean)
    y = x * lax.rsqrt(ms + EPS) * nw
    slabs = [_rope_swap(y[:, j:j + 128]) for j in range(0, y.shape[1], 128)]
    swapped = slabs[0] if len(slabs) == 1 else jnp.concatenate(slabs, axis=1)
    return y * cos + swapped * sin


def _gqa_prep_kernel(z_ref, qw_ref, kw_ref, cos_ref, sin_ref, gq_ref, gk_ref, q_ref, k_ref, v_ref):
    qd = GQA_HEADS * GQA_DH
    kd = GQA_KV_HEADS * GQA_DH
    z = z_ref[0]
    cos_k = cos_ref[...]
    sin_k = sin_ref[...]
    reps = qd // kd
    cos_q = jnp.concatenate([cos_k] * reps, axis=1)
    sin_q = jnp.concatenate([sin_k] * reps, axis=1)
    q = _norm_rope(z[:, :qd], qw_ref[...], cos_q, sin_q, gq_ref[...]) * (LOG2E * GQA_DH ** -0.5)
    k = _norm_rope(z[:, qd:qd + kd], kw_ref[...], cos_k, sin_k, gk_ref[...])
    v = z[:, qd + kd:qd + 2 * kd]
    for h in range(GQA_HEADS):
        q_ref[0, h] = q[:, h * GQA_DH:(h + 1) * GQA_DH].astype(BF16)
    ones = jnp.ones((z.shape[0], LANES - GQA_DH), BF16)
    for h in range(GQA_KV_HEADS):
        k_ref[0, h] = k[:, h * GQA_DH:(h + 1) * GQA_DH].astype(BF16)
        v_ref[0, h] = jnp.concatenate([v[:, h * GQA_DH:(h + 1) * GQA_DH].astype(BF16), ones], axis=1)


def _group_mean_matrix(n, group):
    idx = np.arange(n) // group
    return jnp.asarray((idx[:, None] == idx[None, :]) / group, BF16)


def _gqa_prep(z, q_norm_w, k_norm_w, s_len, c_len):
    bsz, t, width = z.shape
    qd = GQA_HEADS * GQA_DH
    kd = GQA_KV_HEADS * GQA_DH
    cos, sin = _rope_tables(s_len, c_len, GQA_KV_HEADS)
    head_spec = lambda n, w=GQA_DH: pl.BlockSpec((1, n, TOK, w), lambda b, i: (b, 0, i, 0))
    return pl.pallas_call(
        _gqa_prep_kernel,
        grid=(bsz, t // TOK),
        in_specs=[
            pl.BlockSpec((1, TOK, width), lambda b, i: (b, i, 0)),
            _const_spec((1, qd)), _const_spec((1, kd)),
            pl.BlockSpec((TOK, kd), lambda b, i: (i, 0)),
            pl.BlockSpec((TOK, kd), lambda b, i: (i, 0)),
            _const_spec((qd, qd)), _const_spec((kd, kd)),
        ],
        out_specs=[head_spec(GQA_HEADS), head_spec(GQA_KV_HEADS), head_spec(GQA_KV_HEADS, LANES)],
        out_shape=[jax.ShapeDtypeStruct((bsz, GQA_HEADS, t, GQA_DH), BF16),
                   jax.ShapeDtypeStruct((bsz, GQA_KV_HEADS, t, GQA_DH), BF16),
                   jax.ShapeDtypeStruct((bsz, GQA_KV_HEADS, t, LANES), BF16)],
        compiler_params=_cparams(("parallel", "parallel")),
        name="gqa_prep",
    )(z, jnp.tile(q_norm_w, GQA_HEADS).reshape(1, qd), jnp.tile(k_norm_w, GQA_KV_HEADS).reshape(1, kd),
      cos, sin, _group_mean_matrix(qd, GQA_DH), _group_mean_matrix(kd, GQA_DH))


def _flash_kernel(q_ref, k_ref, v_ref, o_ref, m_s, acc_s, *, group, tk, sub):
    tq = q_ref.shape[2]
    n_kv = k_ref.shape[2] // tk
    units = [(g, r) for g in range(group) for r in range(0, tq, sub)]
    m_s[...] = jnp.full_like(m_s, -jnp.inf)
    acc_s[...] = jnp.zeros_like(acc_s)

    def kv_step(j, carry):
        rows_k = pl.ds(pl.multiple_of(j * tk, tk), tk)
        k = k_ref[0, 0, rows_k, :]
        v = v_ref[0, 0, rows_k, :]
        scores = lambda u: _dot_nt(q_ref[0, units[u][0], units[u][1]:units[u][1] + sub, :], k)
        s_next = scores(0)
        for u in range(len(units)):
            tiles = _lane_tiles(s_next)
            if u + 1 < len(units):
                s_next = scores(u + 1)
            m_prev = m_s[u]
            m_new = jnp.maximum(m_prev, jnp.max(functools.reduce(jnp.maximum, tiles), axis=-1, keepdims=True))
            p = [jnp.exp2(t - m_new).astype(BF16) for t in tiles]
            acc_s[u] = jnp.exp2(m_prev - m_new) * acc_s[u] + _dot(jnp.concatenate(p, axis=1), v)
            m_s[u] = m_new
        return carry

    lax.fori_loop(0, n_kv, kv_step, 0)
    for u, (g, r) in enumerate(units):
        acc = acc_s[u]
        o_ref[0, r:r + sub, g * GQA_DH:(g + 1) * GQA_DH] = acc[:, :GQA_DH] / acc[:, GQA_DH:GQA_DH + 1]


KV_TILE_MAX = 1536


def _kv_tile(t):
    return max(n for n in range(LANES, KV_TILE_MAX + 1, LANES) if t % n == 0)


def _flash(q, k, v, s_len, tq, tk):
    bsz, _, t, dh = q.shape
    assert t % tk == 0 and s_len % tq == 0 and tq % TOK == 0
    group = GQA_HEADS // GQA_KV_HEADS
    n_units = group * (tq // TOK)
    return pl.pallas_call(
        functools.partial(_flash_kernel, group=group, tk=tk, sub=TOK),
        grid=(bsz, GQA_KV_HEADS, s_len // tq),
        in_specs=[
            pl.BlockSpec((1, group, tq, dh), lambda b, h, i: (b, h, i, 0)),
            pl.BlockSpec((1, 1, t, dh), lambda b, h, i: (b, h, 0, 0)),
            pl.BlockSpec((1, 1, t, LANES), lambda b, h, i: (b, h, 0, 0)),
        ],
        out_specs=pl.BlockSpec((1, tq, group * dh), lambda b, h, i: (b, i, h)),
        out_shape=jax.ShapeDtypeStruct((bsz, s_len, GQA_HEADS * dh), F32),
        scratch_shapes=[pltpu.VMEM((n_units, TOK, LANES), F32), pltpu.VMEM((n_units, TOK, LANES), F32)],
        compiler_params=_cparams(("parallel", "parallel", "arbitrary")),
        name="gqa_flash",
    )(q, k, v)


def kernel(x, c, ctx, c_ctx, norm1_w, norm2_w, w_mod, b_mod, w_ff1, w_ff2,
           w_in_even, na_rpb, lru_conv_w, lru_conv_b, lru_wa, lru_ba, lru_wx, lru_bx, lru_lambda, w_out_even,
           w_in_odd, gla_wa2, gla_ba, gla_norm_w, gqa_q_norm_w, gqa_k_norm_w, w_out_odd, final_norm_w):
    bsz, s_len, d = x.shape
    c_len = ctx.shape[1]
    depth = w_mod.shape[0]
    assert s_len % TOK == 0 and c_len % TOK == 0 and depth == 2
    n_lat = s_len // TOK

    pad = (-(bsz + 1)) % 8
    cvec = jnp.concatenate([c_ctx[None], c, jnp.zeros((pad, d), F32)], axis=0)
    mods = _modulation(cvec, w_mod, b_mod)
    h = (x, ctx)
    t = s_len + c_len

    na_w = NA_HEADS * NA_DH
    ch = lru_lambda.shape[-1]
    qkv, xg = _inproj(h, norm1_w[0], mods[0], w_in_even[0].astype(BF16),
                      ((0, 3 * na_w), (3 * na_w, 3 * na_w + 2 * ch)), (BF16, BF16), s_len, t)
    na_out = _na(qkv, na_rpb[0], s_len, c_len)
    w_gate = jnp.stack([jnp.concatenate([_block_diag(lru_wa[0, dr]), _block_diag(lru_wx[0, dr])], axis=1)
                        for dr in range(2)]).astype(BF16)
    b_gate = jnp.concatenate([lru_ba[0], lru_bx[0]], axis=-1).reshape(2, 1, 2 * ch)
    lru_args = (xg, lru_conv_w[0], lru_conv_b[0], w_gate, b_gate, lru_lambda[0], n_lat)
    lru_b = _lru(*lru_args, direction=1)
    lru_out = _lru(*lru_args, direction=0, other=lru_b)
    wo = w_out_even[0].astype(BF16)
    h = _mlp(h, na_out, lru_out, mods[0], wo[:na_w], wo[na_w:], norm2_w[0],
             w_ff1[0].astype(BF16), w_ff2[0].astype(BF16), final_norm_w, s_len, s_len + c_len, final=False)

    key = GLA_HEADS * GLA_DK
    val = GLA_HEADS * GLA_DV
    gla_w = 2 * key + 2 * val
    gqa_w = (GQA_HEADS + 2 * GQA_KV_HEADS) * GQA_DH
    w_in = w_in_odd[0]
    w_in = jnp.concatenate([w_in[:, :gla_w], w_in[:, gla_w + 2 * GLA_RANK:], w_in[:, gla_w:gla_w + 2 * GLA_RANK]],
                           axis=1).astype(BF16)
    qkvg, gqa_z, lr = _inproj(h, norm1_w[1], mods[1], w_in,
                              ((0, gla_w), (gla_w, gla_w + gqa_w), (gla_w + gqa_w, gla_w + gqa_w + 2 * GLA_RANK)),
                              (F32, F32, F32), s_len, t)
    gla_b = _gla(qkvg, lr, gla_wa2[0], gla_ba[0], n_lat, direction=1)
    gla_out = _gla(qkvg, lr, gla_wa2[0], gla_ba[0], n_lat, direction=0, other=gla_b, norm_w=gla_norm_w[0])
    q, k, v = _gqa_prep(gqa_z, gqa_q_norm_w[0], gqa_k_norm_w[0], s_len, c_len)
    gqa_out = _flash(q, k, v, s_len, tq=2 * TOK, tk=_kv_tile(s_len + c_len))
    wo = w_out_odd[0].astype(BF16)
    return _mlp(h, gla_out, gqa_out, mods[1], wo[:val], wo[val:], norm2_w[1],
                w_ff1[1].astype(BF16), w_ff2[1].astype(BF16), final_norm_w, s_len, s_len, final=True)
```

```python
import functools

import numpy as np
import jax
import jax.numpy as jnp
from jax import lax
from jax.experimental import pallas as pl
from jax.experimental.pallas import tpu as pltpu

F32 = jnp.float32
BF16 = jnp.bfloat16

EPS = 1e-6
GRID_W = 64
LANES = 128
N_MOD = 6
TOK = 256
DTOK = 512
NEG = -1e30

NA_HEADS = 8
NA_DH = 64
NA_KH = 8
NA_KW = 16
LRU_C = 8.0
LRU_SEG = 8
GLA_HEADS = 4
GLA_DK = 64
GLA_DV = 128
GLA_RANK = 16
GLA_TAU = 16.0
GLA_CHUNK = 64
GQA_HEADS = 8
GQA_KV_HEADS = 2
GQA_DH = 64
ROPE_THETA = 10000.0
LOG2E = 1.4426950408889634

VMEM_LIMIT = 56 * 1024 * 1024


def _cparams(sem):
    return pltpu.CompilerParams(dimension_semantics=sem, vmem_limit_bytes=VMEM_LIMIT)


def _const_spec(shape):
    nd = len(shape)
    return pl.BlockSpec(shape, lambda *_: (0,) * nd, pipeline_mode=pl.Buffered(1))


def _dot(a, b):
    return jnp.dot(a, b, preferred_element_type=F32)


def _dot_nt(a, b):
    return lax.dot_general(a, b, (((1,), (1,)), ((), ())), preferred_element_type=F32)


def _dot_tn(a, b):
    return lax.dot_general(a, b, (((0,), (0,)), ((), ())), preferred_element_type=F32)


def _split_hi_lo(x):
    hi = x.astype(BF16)
    lo = (x - hi.astype(F32)).astype(BF16)
    return hi, lo


def _rms(x, w):
    ms = jnp.mean(x * x, axis=-1, keepdims=True)
    return x * lax.rsqrt(ms + EPS) * w


def _softplus(z):
    return jnp.maximum(z, 0.0) + jnp.log1p(jnp.exp(-jnp.abs(z)))


def _mod_kernel(c_ref, w_ref, b_ref, o_ref):
    c = c_ref[...]
    s = c * jax.nn.sigmoid(c)
    o_ref[0] = jnp.dot(s, w_ref[0], precision=lax.Precision.HIGHEST, preferred_element_type=F32) + b_ref[0]


def _modulation(cvec, w_mod, b_mod):
    depth, d, _ = w_mod.shape
    rows = cvec.shape[0]
    out = pl.pallas_call(
        _mod_kernel,
        grid=(depth, N_MOD),
        in_specs=[
            pl.BlockSpec((rows, d), lambda i, j: (0, 0)),
            pl.BlockSpec((1, d, d), lambda i, j: (i, 0, j)),
            pl.BlockSpec((1, 1, d), lambda i, j: (i, 0, j)),
        ],
        out_specs=pl.BlockSpec((1, rows, d), lambda i, j: (i, 0, j)),
        out_shape=jax.ShapeDtypeStruct((depth, rows, N_MOD * d), F32),
        compiler_params=_cparams(("arbitrary", "arbitrary")),
        name="modulation",
    )(cvec, w_mod, b_mod.reshape(depth, 1, N_MOD * d))
    return out.reshape(depth, rows, N_MOD, d)


def _mod_spec(d, n_lat_tiles):
    return pl.BlockSpec((1, N_MOD, d), lambda b, t: (jnp.where(t >= n_lat_tiles, 0, b + 1), 0, 0))


def _stream_specs(h, s_len):
    n_lat = s_len // DTOK
    if not isinstance(h, tuple):
        return [h], [pl.BlockSpec((1, DTOK, h.shape[-1]), lambda b, i: (b, i, 0))], 0
    lat, ctx = h
    c_len, d = ctx.shape[1:]
    assert c_len <= DTOK and lat.shape[1] == s_len
    return ([lat, ctx],
            [pl.BlockSpec((1, DTOK, d), lambda b, i: (b, jnp.minimum(i, n_lat - 1), 0)),
             pl.BlockSpec((1, c_len, d), lambda b, i: (b, 0, 0))], c_len)


def _per_stream(body, h_refs, n_lat_tiles, ctx_rows):
    if len(h_refs) == 1:
        body(h_refs[0], DTOK)
        return
    i = pl.program_id(1)
    pl.when(i < n_lat_tiles)(lambda: body(h_refs[0], DTOK))
    pl.when(i >= n_lat_tiles)(lambda: body(h_refs[1], ctx_rows))


def _inproj_kernel(*refs, splits, n_src, n_lat_tiles, ctx_rows):
    h_refs = refs[:n_src]
    nw_ref, mod_ref, w_ref = refs[n_src:n_src + 3]
    o_refs = refs[n_src + 3:]

    def body(h_ref, rows):
        mod = mod_ref[0]
        u = _rms(h_ref[0], nw_ref[...]) * (1.0 + mod[1:2]) + mod[0:1]
        z = _dot(u.astype(BF16), w_ref[...])
        for o_ref, (lo, hi) in zip(o_refs, splits):
            o_ref[0, :rows] = z[:, lo:hi].astype(o_ref.dtype)

    _per_stream(body, h_refs, n_lat_tiles, ctx_rows)


def _inproj(h, norm_w, mods, w, splits, dtypes, s_len, t):
    d, n = w.shape
    assert s_len % DTOK == 0
    srcs, src_specs, ctx_rows = _stream_specs(h, s_len)
    bsz = srcs[0].shape[0]
    return pl.pallas_call(
        functools.partial(_inproj_kernel, splits=splits, n_src=len(srcs), n_lat_tiles=s_len // DTOK,
                          ctx_rows=ctx_rows),
        grid=(bsz, pl.cdiv(t, DTOK)),
        in_specs=src_specs + [
            _const_spec((1, d)),
            _mod_spec(d, s_len // DTOK),
            _const_spec((d, n)),
        ],
        out_specs=[pl.BlockSpec((1, DTOK, hi - lo), lambda b, i: (b, i, 0)) for lo, hi in splits],
        out_shape=[jax.ShapeDtypeStruct((bsz, t, hi - lo), dt) for (lo, hi), dt in zip(splits, dtypes)],
        compiler_params=_cparams(("parallel", "parallel")),
        name="inproj",
    )(*srcs, norm_w.reshape(1, d), mods, w)


def _mlp_kernel(*refs, ff_chunk, final, n_src, n_lat_tiles, ctx_rows):
    h_refs = refs[:n_src]
    ma_ref, mb_ref, mod_ref, woa_ref, wob_ref, nw_ref, w1_ref, w2_ref, fw_ref, o_ref = refs[n_src:]

    def body(h_ref, rows):
        mod = mod_ref[0]
        y = (_dot(ma_ref[0, :rows].astype(BF16), woa_ref[...])
             + _dot(mb_ref[0, :rows].astype(BF16), wob_ref[...]))
        h1 = h_ref[0] + mod[2:3] * y
        u = (_rms(h1, nw_ref[...]) * (1.0 + mod[4:5]) + mod[3:4]).astype(BF16)
        d_ff = w1_ref.shape[1]
        acc = jnp.zeros(h1.shape, F32)
        for c in range(d_ff // ff_chunk):
            sl = slice(c * ff_chunk, (c + 1) * ff_chunk)
            a = jnp.maximum(_dot(u, w1_ref[:, sl]), 0.0)
            acc = acc + _dot((a * a).astype(BF16), w2_ref[sl, :])
        h2 = h1 + mod[5:6] * acc
        if final:
            h2 = _rms(h2, fw_ref[...])
        o_ref[0, :rows] = h2

    _per_stream(body, h_refs, n_lat_tiles, ctx_rows)


def _mlp(h, mix_a, mix_b, mods, wo_a, wo_b, norm_w, w1, w2, final_w, s_len, out_rows, final):
    da, db = mix_a.shape[-1], mix_b.shape[-1]
    d, d_ff = w1.shape
    assert s_len % DTOK == 0
    srcs, src_specs, ctx_rows = _stream_specs(h, s_len)
    bsz = srcs[0].shape[0]
    tok_spec = lambda width: pl.BlockSpec((1, DTOK, width), lambda b, i: (b, i, 0))
    return pl.pallas_call(
        functools.partial(_mlp_kernel, ff_chunk=512, final=final, n_src=len(srcs), n_lat_tiles=s_len // DTOK,
                          ctx_rows=ctx_rows),
        grid=(bsz, pl.cdiv(out_rows, DTOK)),
        in_specs=src_specs + [
            tok_spec(da), tok_spec(db),
            _mod_spec(d, s_len // DTOK),
            _const_spec((da, d)), _const_spec((db, d)), _const_spec((1, d)),
            _const_spec((d, d_ff)), _const_spec((d_ff, d)), _const_spec((1, d)),
        ],
        out_specs=tok_spec(d),
        out_shape=jax.ShapeDtypeStruct((bsz, out_rows, d), F32),
        compiler_params=_cparams(("parallel", "parallel")),
        name="mlp",
    )(*srcs, mix_a, mix_b, mods, wo_a, wo_b, norm_w.reshape(1, d), w1, w2, final_w.reshape(1, d))


NA_QROWS = TOK // GRID_W
NA_WIN = NA_KH + NA_QROWS


def _na_plan(rows):
    n_blk = rows // NA_QROWS
    r = np.arange(rows).reshape(n_blk, NA_QROWS)
    start = np.clip(r - NA_KH // 2, 0, rows - NA_KH)
    u0 = np.clip(r[:, 0] - NA_KH // 2, 0, rows - NA_WIN)
    key = u0[:, None, None] + np.arange(NA_WIN)[None, None, :]
    valid = (key >= start[:, :, None]) & (key < start[:, :, None] + NA_KH)
    d_row = np.where(valid, key - r[:, :, None] + (NA_KH - 1), 0)
    flat = np.concatenate([valid.reshape(n_blk, -1), d_row.reshape(n_blk, -1)], axis=1)
    _, first, pat = np.unique(flat, axis=0, return_index=True, return_inverse=True)
    return u0.astype(np.int32), pat.reshape(-1).astype(np.int32), valid[first], d_row[first]


def _na_bias_table(rpb, valid, d_row):
    n_pat = valid.shape[0]
    heads = rpb.shape[0]
    pairs = NA_WIN // 2
    n_dc = 2 * NA_KW - 1
    cols = np.arange(GRID_W)
    col_start = np.clip(cols - NA_KW // 2, 0, GRID_W - NA_KW)
    col_valid = (cols[None, :] >= col_start[:, None]) & (cols[None, :] < col_start[:, None] + NA_KW)
    d_col = np.clip(cols[None, :] - cols[:, None], 1 - NA_KW, NA_KW - 1) + (NA_KW - 1)
    row_sel = (d_row[..., None] == np.arange(2 * NA_KH - 1)).astype(np.float32)
    col_sel = (d_col[None] == np.arange(n_dc)[:, None, None]).astype(np.float32)
    pair_sel = np.zeros((2, n_dc, GRID_W, 2, GRID_W), np.float32)
    for u1 in range(2):
        pair_sel[u1, :, :, u1, :] = col_sel
    pair_sel = pair_sel.reshape(2 * n_dc, GRID_W, 2 * GRID_W)
    hp = lax.Precision.HIGHEST
    by_row = jnp.einsum('hrc,piur->phiuc', rpb, row_sel, precision=hp)
    by_row = by_row.reshape(n_pat, heads, NA_QROWS, pairs, 2 * n_dc)
    bias = jnp.einsum('phiUk,kqm->phUiqm', by_row, pair_sel, precision=hp)
    keep = (valid.reshape(n_pat, 1, NA_QROWS, 1, pairs, 2, 1).transpose(0, 1, 4, 2, 3, 5, 6)
            & col_valid[None, None, None, None, :, None, :])
    keep = keep.reshape(n_pat, 1, pairs, NA_QROWS, GRID_W, 2 * GRID_W)
    return jnp.where(keep, bias, NEG).reshape(n_pat, heads, pairs, TOK, 2 * GRID_W)


def _softmax_tiles(tiles):
    m = jnp.max(functools.reduce(jnp.maximum, tiles), axis=-1, keepdims=True)
    p = [jnp.exp(t - m) for t in tiles]
    denom = jnp.sum(functools.reduce(jnp.add, p), axis=-1, keepdims=True)
    return [t.astype(BF16) for t in p], denom


def _lane_tiles(x):
    return [x[:, j:j + LANES] for j in range(0, x.shape[1], LANES)]


def _na_kernel(u0_ref, pat_ref, q_ref, k_ref, v_ref, kc_ref, vc_ref, bias_ref, o_ref, *, n_lat):
    i = pl.program_id(1)
    win = NA_WIN * GRID_W
    n_loc = win // LANES

    def attend(local):
        q = q_ref[0] * 0.125
        kc = kc_ref[0]
        vc = vc_ref[0]
        if local:
            row0 = pl.multiple_of(u0_ref[i] * GRID_W, GRID_W)
            kw = k_ref[0, pl.ds(row0, win), :]
            vw = v_ref[0, pl.ds(row0, win), :]

        def scores(h):
            sl = slice(h * NA_DH, (h + 1) * NA_DH)
            s_ctx = _lane_tiles(_dot_nt(q[:, sl], kc[:, sl]))
            if not local:
                return s_ctx
            s_loc = _lane_tiles(_dot_nt(q[:, sl], kw[:, sl]))
            return [t + bias_ref[0, h, j] for j, t in enumerate(s_loc)] + s_ctx

        s_next = scores(0)
        for h in range(NA_HEADS):
            sl = slice(h * NA_DH, (h + 1) * NA_DH)
            tiles = s_next
            if h + 1 < NA_HEADS:
                s_next = scores(h + 1)
            p, denom = _softmax_tiles(tiles)
            if local:
                o = (_dot(jnp.concatenate(p[:n_loc], axis=1), vw[:, sl])
                     + _dot(jnp.concatenate(p[n_loc:], axis=1), vc[:, sl]))
            else:
                o = _dot(jnp.concatenate(p, axis=1), vc[:, sl])
            o_ref[0, :, sl] = o / denom

    pl.when(i < n_lat)(functools.partial(attend, True))
    pl.when(i >= n_lat)(functools.partial(attend, False))


def _na(qkv, rpb, s_len, c_len):
    bsz, t, _ = qkv.shape
    width = NA_HEADS * NA_DH
    rows = s_len // GRID_W
    n_lat = s_len // TOK
    n_steps = t // TOK
    assert rows >= NA_WIN and rows % NA_QROWS == 0 and c_len % TOK == 0 and s_len % c_len == 0
    u0, pat, valid, d_row = _na_plan(rows)
    bias = _na_bias_table(rpb, valid, d_row)
    ctx_steps = np.zeros(n_steps - n_lat, np.int32)
    u0 = jnp.asarray(np.concatenate([u0, ctx_steps]))
    pat = jnp.asarray(np.concatenate([pat, ctx_steps]))
    tok_spec = pl.BlockSpec((1, TOK, width), lambda b, i, u0_ref, pat_ref: (b, i, 0))
    ctx_blk = s_len // c_len
    return pl.pallas_call(
        functools.partial(_na_kernel, n_lat=n_lat),
        grid_spec=pltpu.PrefetchScalarGridSpec(
            num_scalar_prefetch=2,
            grid=(bsz, n_steps),
            in_specs=[
                tok_spec,
                pl.BlockSpec((1, s_len, width), lambda b, i, u0_ref, pat_ref: (b, 0, 1),
                             pipeline_mode=pl.Buffered(1)),
                pl.BlockSpec((1, s_len, width), lambda b, i, u0_ref, pat_ref: (b, 0, 2),
                             pipeline_mode=pl.Buffered(1)),
                pl.BlockSpec((1, c_len, width), lambda b, i, u0_ref, pat_ref: (b, ctx_blk, 1)),
                pl.BlockSpec((1, c_len, width), lambda b, i, u0_ref, pat_ref: (b, ctx_blk, 2)),
                pl.BlockSpec((1, NA_HEADS, NA_WIN // 2, TOK, 2 * GRID_W),
                             lambda b, i, u0_ref, pat_ref: (pat_ref[i], 0, 0, 0, 0)),
            ],
            out_specs=tok_spec,
        ),
        out_shape=jax.ShapeDtypeStruct((bsz, t, width), F32),
        compiler_params=_cparams(("parallel", "arbitrary")),
        name="na",
    )(u0, pat, qkv, qkv, qkv, qkv, qkv, bias)


def _lru_tile_index(i, n_lat, n_tiles, direction):
    if direction == 0:
        return (i + n_lat) % n_tiles
    return n_tiles - 1 - i


def _lru_row_permutation():
    seg = TOK // LRU_SEG
    p = np.arange(TOK)
    src = (p % LRU_SEG) * seg + p // LRU_SEG
    return (src[:, None] == np.arange(TOK)[None, :]).astype(np.float32)


def _shift_rows(group, boundary_row, down):
    sub = lax.broadcasted_iota(jnp.int32, group.shape, 0)
    if down:
        return jnp.where(sub == 0, boundary_row, pltpu.roll(group, 1, axis=0))
    return jnp.where(sub == LRU_SEG - 1, boundary_row, pltpu.roll(group, LRU_SEG - 1, axis=0))


def _lru_kernel(*refs, direction, n_lat, n_tiles, combine):
    if combine:
        (x_ref, prev_ref, next_ref, cw_ref, cb_ref, wg_ref, bg_ref, lam_ref, perm_ref, g_ref, other_ref,
         unperm_ref, o_ref, carry_s) = refs
    else:
        (x_ref, prev_ref, next_ref, cw_ref, cb_ref, wg_ref, bg_ref, lam_ref, perm_ref,
         o_ref, carry_s) = refs
    i = pl.program_id(1)
    tile = _lru_tile_index(i, n_lat, n_tiles, direction)
    ch = x_ref.shape[-1]
    halo = prev_ref.shape[1]
    seg = TOK // LRU_SEG
    grp = LRU_SEG

    @pl.when(i == 0)
    def _():
        carry_s[...] = jnp.zeros_like(carry_s)

    first = jnp.logical_or(tile == 0, tile == n_lat)
    last = jnp.logical_or(tile == n_lat - 1, tile == n_tiles - 1)
    perm = perm_ref[...]
    x = _dot(perm, x_ref[0])
    prev = jnp.where(first, 0.0, prev_ref[0].astype(F32))
    nxt = jnp.where(last, 0.0, next_ref[0].astype(F32))
    wrap_m1 = _shift_rows(x[(seg - 1) * grp:], prev[halo - 1:halo], down=True)
    wrap_m2 = _shift_rows(x[(seg - 2) * grp:(seg - 1) * grp], prev[halo - 2:halo - 1], down=True)
    wrap_p1 = _shift_rows(x[:grp], nxt[0:1], down=False)
    x_m1 = jnp.concatenate([wrap_m1, x[:-grp]], axis=0)
    x_m2 = jnp.concatenate([wrap_m2, wrap_m1, x[:-2 * grp]], axis=0)
    x_p1 = jnp.concatenate([x[grp:], wrap_p1], axis=0)
    cw = cw_ref[...]
    xc = cw[0:1] * x_m2 + cw[1:2] * x_m1 + cw[2:3] * x + cw[3:4] * x_p1 + cb_ref[...]

    pre = _dot(xc.astype(BF16), wg_ref[0]) + bg_ref[0]
    r_gate = jax.nn.sigmoid(pre[:, :ch])
    i_gate = jax.nn.sigmoid(pre[:, ch:])
    log_a = (-LRU_C) * r_gate * _softplus(-lam_ref[0])
    a = jnp.exp(log_a)
    bb = jnp.sqrt(1.0 - a * a) * (i_gate * xc)

    steps = range(seg) if direction == 0 else range(seg - 1, -1, -1)
    h = jnp.zeros((grp, ch), F32)
    cum = jnp.ones((grp, ch), F32)
    h_loc = [None] * seg
    cum_loc = [None] * seg
    for j in steps:
        a_j = a[j * grp:(j + 1) * grp]
        h = a_j * h + bb[j * grp:(j + 1) * grp]
        cum = cum * a_j
        h_loc[j] = h
        cum_loc[j] = cum

    state = carry_s[...]
    entering = [None] * LRU_SEG
    for s in (range(LRU_SEG) if direction == 0 else range(LRU_SEG - 1, -1, -1)):
        entering[s] = state
        state = h[s:s + 1] + cum[s:s + 1] * state
    carry_s[...] = state
    entering = jnp.concatenate(entering, axis=0)
    y = jnp.concatenate([h_loc[j] + cum_loc[j] * entering for j in range(seg)], axis=0)
    if combine:
        y = (y + other_ref[0]) * jax.nn.gelu(_dot(perm, g_ref[0]))
        o_ref[0] = _dot(unperm_ref[...], y.astype(BF16)).astype(o_ref.dtype)
    else:
        o_ref[0] = y


def _lru(xg, conv_w, conv_b, w_gate, b_gate, lam, n_lat, direction, other=None):
    bsz, t, two_ch = xg.shape
    ch = two_ch // 2
    n_tiles = t // TOK
    halo = 16
    combine = other is not None
    tile_of = functools.partial(_lru_tile_index, n_lat=n_lat, n_tiles=n_tiles, direction=direction)
    per_tile = TOK // halo
    perm = _lru_row_permutation()
    in_specs = [
        pl.BlockSpec((1, TOK, ch), lambda b, i: (b, tile_of(i), 0)),
        pl.BlockSpec((1, halo, ch), lambda b, i: (b, jnp.maximum(tile_of(i) * per_tile - 1, 0), 0)),
        pl.BlockSpec((1, halo, ch), lambda b, i: (b, jnp.minimum((tile_of(i) + 1) * per_tile, t // halo - 1), 0)),
        _const_spec((4, ch)), _const_spec((1, ch)),
        pl.BlockSpec((1, ch, 2 * ch), lambda b, i: (direction, 0, 0), pipeline_mode=pl.Buffered(1)),
        pl.BlockSpec((1, 1, 2 * ch), lambda b, i: (direction, 0, 0), pipeline_mode=pl.Buffered(1)),
        pl.BlockSpec((1, 1, ch), lambda b, i: (direction, 0, 0), pipeline_mode=pl.Buffered(1)),
        _const_spec((TOK, TOK)),
    ]
    args = [xg, xg, xg, conv_w, conv_b.reshape(1, ch), w_gate, b_gate, lam.reshape(2, 1, ch), jnp.asarray(perm, BF16)]
    if combine:
        in_specs += [pl.BlockSpec((1, TOK, ch), lambda b, i: (b, tile_of(i), 1)),
                     pl.BlockSpec((1, TOK, ch), lambda b, i: (b, tile_of(i), 0)),
                     _const_spec((TOK, TOK))]
        args += [xg, other, jnp.asarray(perm.T, BF16)]
    return pl.pallas_call(
        functools.partial(_lru_kernel, direction=direction, n_lat=n_lat, n_tiles=n_tiles, combine=combine),
        grid=(bsz, n_tiles),
        in_specs=in_specs,
        out_specs=pl.BlockSpec((1, TOK, ch), lambda b, i: (b, tile_of(i), 0)),
        out_shape=jax.ShapeDtypeStruct((bsz, t, ch), BF16 if combine else F32),
        scratch_shapes=[pltpu.VMEM((1, ch), F32)],
        compiler_params=_cparams(("parallel", "arbitrary")),
        name="lru_fwd" if direction == 0 else "lru_bwd",
    )(*args)


def _block_diag(w):
    heads, n, _ = w.shape
    eye = jnp.eye(heads, dtype=w.dtype)
    return (eye[:, None, :, None] * w[:, :, None, :]).reshape(heads * n, heads * n)


def _gla_kernel(*refs, direction, combine):
    if combine:
        qkvg_ref, lr_ref, wa_ref, ba_ref, tri_ref, other_ref, nw_ref, o_ref, state_s = refs
    else:
        qkvg_ref, lr_ref, wa_ref, ba_ref, tri_ref, o_ref, state_s = refs
    key = GLA_HEADS * GLA_DK
    val = GLA_HEADS * GLA_DV
    nchunk = TOK // GLA_CHUNK

    @pl.when(pl.program_id(1) == 0)
    def _():
        state_s[...] = jnp.zeros_like(state_s)

    z = qkvg_ref[0]
    q = z[:, :key] * (GLA_DK ** -0.5)
    k = z[:, key:2 * key]
    v = z[:, 2 * key:2 * key + val].astype(BF16)
    lr = lr_ref[0][:, direction * GLA_RANK:(direction + 1) * GLA_RANK]
    logit = _dot(lr.astype(BF16), wa_ref[0].astype(BF16)) + ba_ref[0]
    log_a = (jnp.minimum(logit, 0.0) - jnp.log1p(jnp.exp(-jnp.abs(logit)))) / GLA_TAU

    g_hi, g_lo = _split_hi_lo(log_a)
    tri = tri_ref[...]
    b = _dot(tri, g_hi) + _dot(tri, g_lo)
    q_in = (q * jnp.exp(b)).astype(BF16)
    k_in = (k * jnp.exp(-b)).astype(BF16)
    chunks = [slice(c * GLA_CHUNK, (c + 1) * GLA_CHUNK) for c in range(nchunk)]
    ends = [(c + 1) * GLA_CHUNK - 1 if direction == 0 else c * GLA_CHUNK for c in range(nchunk)]
    b_end = jnp.concatenate([jnp.broadcast_to(b[e:e + 1], (GLA_CHUNK, key)) for e in ends], axis=0)
    k_out = (k * jnp.exp(b_end - b)).astype(BF16)
    causal = tri > 0
    ones = jnp.ones((GLA_CHUNK, GLA_DV), BF16)
    decay = [jnp.exp(_dot_tn(g_hi[rc], ones) + _dot_tn(g_lo[rc], ones)) for rc in chunks]

    chunk_order = range(nchunk) if direction == 0 else range(nchunk - 1, -1, -1)
    for h in range(GLA_HEADS):
        ks = slice(h * GLA_DK, (h + 1) * GLA_DK)
        vs = slice(h * GLA_DV, (h + 1) * GLA_DV)
        v_h = v[:, vs]
        att = jnp.where(causal, _dot_nt(q_in[:, ks], k_in[:, ks]), 0.0)
        o = _dot(att.astype(BF16), v_h)
        state = state_s[h]
        inter = [None] * nchunk
        for c in chunk_order:
            inter[c] = _dot(q_in[chunks[c], ks], state.astype(BF16))
            state = decay[c][ks] * state + _dot_tn(k_out[chunks[c], ks], v_h[chunks[c]])
        state_s[h] = state
        o = o + jnp.concatenate(inter, axis=0)
        if combine:
            o = o + other_ref[0, :, vs]
            gate = z[:, 2 * key + val + h * GLA_DV:2 * key + val + (h + 1) * GLA_DV]
            o = _rms(o, nw_ref[...]) * (gate * jax.nn.sigmoid(gate))
        o_ref[0, :, vs] = o


def _gla(qkvg, lr, wa2, ba, n_lat, direction, other=None, norm_w=None):
    bsz, t, width = qkvg.shape
    key = GLA_HEADS * GLA_DK
    val = GLA_HEADS * GLA_DV
    n_tiles = t // TOK
    combine = other is not None
    tile_of = functools.partial(_lru_tile_index, n_lat=n_lat, n_tiles=n_tiles, direction=direction)
    pos = np.arange(TOK)
    same = (pos[:, None] // GLA_CHUNK) == (pos[None, :] // GLA_CHUNK)
    tri = same & ((pos[None, :] <= pos[:, None]) if direction == 0 else (pos[None, :] >= pos[:, None]))
    tri = jnp.asarray(tri, BF16)
    in_specs = [
        pl.BlockSpec((1, TOK, width), lambda b, i: (b, tile_of(i), 0)),
        pl.BlockSpec((1, TOK, lr.shape[-1]), lambda b, i: (b, tile_of(i), 0)),
        pl.BlockSpec((1, GLA_RANK, key), lambda b, i: (direction, 0, 0), pipeline_mode=pl.Buffered(1)),
        pl.BlockSpec((1, 1, key), lambda b, i: (direction, 0, 0), pipeline_mode=pl.Buffered(1)),
        _const_spec((TOK, TOK)),
    ]
    args = [qkvg, lr, wa2, ba.reshape(2, 1, key), tri]
    if combine:
        in_specs += [pl.BlockSpec((1, TOK, val), lambda b, i: (b, tile_of(i), 0)), _const_spec((1, GLA_DV))]
        args += [other, norm_w.reshape(1, GLA_DV)]
    return pl.pallas_call(
        functools.partial(_gla_kernel, direction=direction, combine=combine),
        grid=(bsz, n_tiles),
        in_specs=in_specs,
        out_specs=pl.BlockSpec((1, TOK, val), lambda b, i: (b, tile_of(i), 0)),
        out_shape=jax.ShapeDtypeStruct((bsz, t, val), F32),
        scratch_shapes=[pltpu.VMEM((GLA_HEADS, GLA_DK, GLA_DV), F32)],
        compiler_params=_cparams(("parallel", "arbitrary")),
        name="gla_fwd" if direction == 0 else "gla_bwd",
    )(*args)


def _rope_tables(s_len, c_len, reps):
    quarter = GQA_DH // 4
    inv = ROPE_THETA ** (-np.arange(quarter, dtype=np.float32) / quarter)
    t = np.arange(s_len)
    ang_r = (t // GRID_W).astype(np.float32)[:, None] * inv[None, :]
    ang_c = (t % GRID_W).astype(np.float32)[:, None] * inv[None, :]
    cos = np.concatenate([np.cos(ang_r)] * 2 + [np.cos(ang_c)] * 2, axis=1)
    sin = np.concatenate([-np.sin(ang_r), np.sin(ang_r), -np.sin(ang_c), np.sin(ang_c)], axis=1)
    cos = np.concatenate([cos, np.ones((c_len, GQA_DH), np.float32)], axis=0)
    sin = np.concatenate([sin, np.zeros((c_len, GQA_DH), np.float32)], axis=0)
    return (jnp.asarray(np.tile(cos, (1, reps)), F32), jnp.asarray(np.tile(sin, (1, reps)), F32))


def _rope_swap(x):
    lane = lax.broadcasted_iota(jnp.int32, x.shape, 1)
    return jnp.where(lane % 32 < 16, pltpu.roll(x, 128 - 16, axis=1), pltpu.roll(x, 16, axis=1))


def _norm_rope(x, nw, cos, sin, gmean):
    hi, lo = _split_hi_lo(x * x)
    ms = _dot(hi, gmean) + _dot(lo, gmean)
    y = x * lax.rsqrt(ms + EPS) * nw
    slabs = [_rope_swap(y[:, j:j + 128]) for j in range(0, y.shape[1], 128)]
    swapped = slabs[0] if len(slabs) == 1 else jnp.concatenate(slabs, axis=1)
    return y * cos + swapped * sin


def _gqa_prep_kernel(z_ref, qw_ref, kw_ref, cos_ref, sin_ref, gq_ref, gk_ref, q_ref, k_ref, v_ref):
    qd = GQA_HEADS * GQA_DH
    kd = GQA_KV_HEADS * GQA_DH
    z = z_ref[0]
    cos_k = cos_ref[...]
    sin_k = sin_ref[...]
    reps = qd // kd
    cos_q = jnp.concatenate([cos_k] * reps, axis=1)
    sin_q = jnp.concatenate([sin_k] * reps, axis=1)
    q = _norm_rope(z[:, :qd], qw_ref[...], cos_q, sin_q, gq_ref[...]) * (LOG2E * GQA_DH ** -0.5)
    k = _norm_rope(z[:, qd:qd + kd], kw_ref[...], cos_k, sin_k, gk_ref[...])
    v = z[:, qd + kd:qd + 2 * kd]
    for h in range(GQA_HEADS):
        q_ref[0, h] = q[:, h * GQA_DH:(h + 1) * GQA_DH].astype(BF16)
    ones = jnp.ones((z.shape[0], LANES - GQA_DH), BF16)
    for h in range(GQA_KV_HEADS):
        k_ref[0, h] = k[:, h * GQA_DH:(h + 1) * GQA_DH].astype(BF16)
        v_ref[0, h] = jnp.concatenate([v[:, h * GQA_DH:(h + 1) * GQA_DH].astype(BF16), ones], axis=1)


def _group_mean_matrix(n, group):
    idx = np.arange(n) // group
    return jnp.asarray((idx[:, None] == idx[None, :]) / group, BF16)


def _gqa_prep(z, q_norm_w, k_norm_w, s_len, c_len):
    bsz, t, width = z.shape
    qd = GQA_HEADS * GQA_DH
    kd = GQA_KV_HEADS * GQA_DH
    cos, sin = _rope_tables(s_len, c_len, GQA_KV_HEADS)
    head_spec = lambda n, w=GQA_DH: pl.BlockSpec((1, n, TOK, w), lambda b, i: (b, 0, i, 0))
    return pl.pallas_call(
        _gqa_prep_kernel,
        grid=(bsz, t // TOK),
        in_specs=[
            pl.BlockSpec((1, TOK, width), lambda b, i: (b, i, 0)),
            _const_spec((1, qd)), _const_spec((1, kd)),
            pl.BlockSpec((TOK, kd), lambda b, i: (i, 0)),
            pl.BlockSpec((TOK, kd), lambda b, i: (i, 0)),
            _const_spec((qd, qd)), _const_spec((kd, kd)),
        ],
        out_specs=[head_spec(GQA_HEADS), head_spec(GQA_KV_HEADS), head_spec(GQA_KV_HEADS, LANES)],
        out_shape=[jax.ShapeDtypeStruct((bsz, GQA_HEADS, t, GQA_DH), BF16),
                   jax.ShapeDtypeStruct((bsz, GQA_KV_HEADS, t, GQA_DH), BF16),
                   jax.ShapeDtypeStruct((bsz, GQA_KV_HEADS, t, LANES), BF16)],
        compiler_params=_cparams(("parallel", "parallel")),
        name="gqa_prep",
    )(z, jnp.tile(q_norm_w, GQA_HEADS).reshape(1, qd), jnp.tile(k_norm_w, GQA_KV_HEADS).reshape(1, kd),
      cos, sin, _group_mean_matrix(qd, GQA_DH), _group_mean_matrix(kd, GQA_DH))


def _flash_kernel(q_ref, k_ref, v_ref, o_ref, m_s, acc_s, *, group, tk, sub):
    tq = q_ref.shape[2]
    n_kv = k_ref.shape[2] // tk
    units = [(g, r) for g in range(group) for r in range(0, tq, sub)]
    m_s[...] = jnp.full_like(m_s, -jnp.inf)
    acc_s[...] = jnp.zeros_like(acc_s)

    def kv_step(j, carry):
        rows_k = pl.ds(pl.multiple_of(j * tk, tk), tk)
        k = k_ref[0, 0, rows_k, :]
        v = v_ref[0, 0, rows_k, :]
        scores = lambda u: _dot_nt(q_ref[0, units[u][0], units[u][1]:units[u][1] + sub, :], k)
        s_next = scores(0)
        for u in range(len(units)):
            tiles = _lane_tiles(s_next)
            if u + 1 < len(units):
                s_next = scores(u + 1)
            m_prev = m_s[u]
            m_new = jnp.maximum(m_prev, jnp.max(functools.reduce(jnp.maximum, tiles), axis=-1, keepdims=True))
            p = [jnp.exp2(t - m_new).astype(BF16) for t in tiles]
            acc_s[u] = jnp.exp2(m_prev - m_new) * acc_s[u] + _dot(jnp.concatenate(p, axis=1), v)
            m_s[u] = m_new
        return carry

    lax.fori_loop(0, n_kv, kv_step, 0)
    for u, (g, r) in enumerate(units):
        acc = acc_s[u]
        o_ref[0, r:r + sub, g * GQA_DH:(g + 1) * GQA_DH] = acc[:, :GQA_DH] / acc[:, GQA_DH:GQA_DH + 1]


KV_TILE_MAX = 3072
FLASH_SUB = 512


def _kv_tile(t):
    return max(n for n in range(LANES, KV_TILE_MAX + 1, LANES) if t % n == 0)


def _flash(q, k, v, s_len, tq, tk):
    bsz, _, t, dh = q.shape
    sub = FLASH_SUB
    assert t % tk == 0 and s_len % tq == 0 and tq % sub == 0
    group = GQA_HEADS // GQA_KV_HEADS
    n_units = group * (tq // sub)
    return pl.pallas_call(
        functools.partial(_flash_kernel, group=group, tk=tk, sub=sub),
        grid=(bsz, GQA_KV_HEADS, s_len // tq),
        in_specs=[
            pl.BlockSpec((1, group, tq, dh), lambda b, h, i: (b, h, i, 0)),
            pl.BlockSpec((1, 1, t, dh), lambda b, h, i: (b, h, 0, 0)),
            pl.BlockSpec((1, 1, t, LANES), lambda b, h, i: (b, h, 0, 0)),
        ],
        out_specs=pl.BlockSpec((1, tq, group * dh), lambda b, h, i: (b, i, h)),
        out_shape=jax.ShapeDtypeStruct((bsz, s_len, GQA_HEADS * dh), F32),
        scratch_shapes=[pltpu.VMEM((n_units, sub, LANES), F32), pltpu.VMEM((n_units, sub, LANES), F32)],
        compiler_params=_cparams(("parallel", "parallel", "arbitrary")),
        name="gqa_flash",
    )(q, k, v)


def kernel(x, c, ctx, c_ctx, norm1_w, norm2_w, w_mod, b_mod, w_ff1, w_ff2,
           w_in_even, na_rpb, lru_conv_w, lru_conv_b, lru_wa, lru_ba, lru_wx, lru_bx, lru_lambda, w_out_even,
           w_in_odd, gla_wa2, gla_ba, gla_norm_w, gqa_q_norm_w, gqa_k_norm_w, w_out_odd, final_norm_w):
    bsz, s_len, d = x.shape
    c_len = ctx.shape[1]
    depth = w_mod.shape[0]
    assert s_len % TOK == 0 and c_len % TOK == 0 and depth == 2
    n_lat = s_len // TOK

    pad = (-(bsz + 1)) % 8
    cvec = jnp.concatenate([c_ctx[None], c, jnp.zeros((pad, d), F32)], axis=0)
    mods = _modulation(cvec, w_mod, b_mod)
    h = (x, ctx)
    t = s_len + c_len

    na_w = NA_HEADS * NA_DH
    ch = lru_lambda.shape[-1]
    qkv, xg = _inproj(h, norm1_w[0], mods[0], w_in_even[0].astype(BF16),
                      ((0, 3 * na_w), (3 * na_w, 3 * na_w + 2 * ch)), (BF16, BF16), s_len, t)
    na_out = _na(qkv, na_rpb[0], s_len, c_len)
    w_gate = jnp.stack([jnp.concatenate([_block_diag(lru_wa[0, dr]), _block_diag(lru_wx[0, dr])], axis=1)
                        for dr in range(2)]).astype(BF16)
    b_gate = jnp.concatenate([lru_ba[0], lru_bx[0]], axis=-1).reshape(2, 1, 2 * ch)
    lru_args = (xg, lru_conv_w[0], lru_conv_b[0], w_gate, b_gate, lru_lambda[0], n_lat)
    lru_b = _lru(*lru_args, direction=1)
    lru_out = _lru(*lru_args, direction=0, other=lru_b)
    wo = w_out_even[0].astype(BF16)
    h = _mlp(h, na_out, lru_out, mods[0], wo[:na_w], wo[na_w:], norm2_w[0],
             w_ff1[0].astype(BF16), w_ff2[0].astype(BF16), final_norm_w, s_len, s_len + c_len, final=False)

    key = GLA_HEADS * GLA_DK
    val = GLA_HEADS * GLA_DV
    gla_w = 2 * key + 2 * val
    gqa_w = (GQA_HEADS + 2 * GQA_KV_HEADS) * GQA_DH
    w_in = w_in_odd[0]
    w_in = jnp.concatenate([w_in[:, :gla_w], w_in[:, gla_w + 2 * GLA_RANK:], w_in[:, gla_w:gla_w + 2 * GLA_RANK]],
                           axis=1).astype(BF16)
    qkvg, gqa_z, lr = _inproj(h, norm1_w[1], mods[1], w_in,
                              ((0, gla_w), (gla_w, gla_w + gqa_w), (gla_w + gqa_w, gla_w + gqa_w + 2 * GLA_RANK)),
                              (F32, F32, F32), s_len, t)
    gla_b = _gla(qkvg, lr, gla_wa2[0], gla_ba[0], n_lat, direction=1)
    gla_out = _gla(qkvg, lr, gla_wa2[0], gla_ba[0], n_lat, direction=0, other=gla_b, norm_w=gla_norm_w[0])
    q, k, v = _gqa_prep(gqa_z, gqa_q_norm_w[0], gqa_k_norm_w[0], s_len, c_len)
    gqa_out = _flash(q, k, v, s_len, tq=2 * FLASH_SUB, tk=_kv_tile(s_len + c_len))
    wo = w_out_odd[0].astype(BF16)
    return _mlp(h, gla_out, gqa_out, mods[1], wo[:val], wo[val:], norm2_w[1],
                w_ff1[1].astype(BF16), w_ff2[1].astype(BF16), final_norm_w, s_len, s_len, final=True)
```

```python
import functools

import numpy as np
import jax
import jax.numpy as jnp
from jax import lax
from jax.experimental import pallas as pl
from jax.experimental.pallas import tpu as pltpu

F32 = jnp.float32
BF16 = jnp.bfloat16

EPS = 1e-6
GRID_W = 64
LANES = 128
N_MOD = 6
TOK = 256
DTOK = 512
MTOK = 512
NEG = -1e30

NA_HEADS = 8
NA_DH = 64
NA_KH = 8
NA_KW = 16
LRU_C = 8.0
LRU_SEG = 8
GLA_HEADS = 4
GLA_DK = 64
GLA_DV = 128
GLA_RANK = 16
GLA_TAU = 16.0
GLA_CHUNK = 64
GQA_HEADS = 8
GQA_KV_HEADS = 2
GQA_DH = 64
ROPE_THETA = 10000.0
LOG2E = 1.4426950408889634

VMEM_LIMIT = 56 * 1024 * 1024


def _cparams(sem):
    return pltpu.CompilerParams(dimension_semantics=sem, vmem_limit_bytes=VMEM_LIMIT)


def _const_spec(shape):
    nd = len(shape)
    return pl.BlockSpec(shape, lambda *_: (0,) * nd, pipeline_mode=pl.Buffered(1))


def _dot(a, b):
    return jnp.dot(a, b, preferred_element_type=F32)


def _dot_nt(a, b):
    return lax.dot_general(a, b, (((1,), (1,)), ((), ())), preferred_element_type=F32)


def _dot_tn(a, b):
    return lax.dot_general(a, b, (((0,), (0,)), ((), ())), preferred_element_type=F32)


def _split_hi_lo(x):
    hi = x.astype(BF16)
    lo = (x - hi.astype(F32)).astype(BF16)
    return hi, lo


def _rms(x, w):
    ms = jnp.mean(x * x, axis=-1, keepdims=True)
    return x * lax.rsqrt(ms + EPS) * w


def _softplus(z):
    return jnp.maximum(z, 0.0) + jnp.log1p(jnp.exp(-jnp.abs(z)))


def _mod_kernel(c_ref, w_ref, b_ref, o_ref):
    c = c_ref[...]
    s = c * jax.nn.sigmoid(c)
    o_ref[0] = jnp.dot(s, w_ref[0], precision=lax.Precision.HIGHEST, preferred_element_type=F32) + b_ref[0]


def _modulation(cvec, w_mod, b_mod):
    depth, d, _ = w_mod.shape
    rows = cvec.shape[0]
    out = pl.pallas_call(
        _mod_kernel,
        grid=(depth, N_MOD),
        in_specs=[
            pl.BlockSpec((rows, d), lambda i, j: (0, 0)),
            pl.BlockSpec((1, d, d), lambda i, j: (i, 0, j)),
            pl.BlockSpec((1, 1, d), lambda i, j: (i, 0, j)),
        ],
        out_specs=pl.BlockSpec((1, rows, d), lambda i, j: (i, 0, j)),
        out_shape=jax.ShapeDtypeStruct((depth, rows, N_MOD * d), F32),
        compiler_params=_cparams(("arbitrary", "arbitrary")),
        name="modulation",
    )(cvec, w_mod, b_mod.reshape(depth, 1, N_MOD * d))
    return out.reshape(depth, rows, N_MOD, d)


def _mod_spec(d, n_lat_tiles):
    return pl.BlockSpec((1, N_MOD, d), lambda b, t: (jnp.where(t >= n_lat_tiles, 0, b + 1), 0, 0))


def _stream_specs(h, s_len):
    n_lat = s_len // DTOK
    if not isinstance(h, tuple):
        return [h], [pl.BlockSpec((1, DTOK, h.shape[-1]), lambda b, i: (b, i, 0))], 0
    lat, ctx = h
    c_len, d = ctx.shape[1:]
    assert c_len <= DTOK and lat.shape[1] == s_len
    return ([lat, ctx],
            [pl.BlockSpec((1, DTOK, d), lambda b, i: (b, jnp.minimum(i, n_lat - 1), 0)),
             pl.BlockSpec((1, c_len, d), lambda b, i: (b, 0, 0))], c_len)


def _per_stream(body, h_refs, n_lat_tiles, ctx_rows):
    if len(h_refs) == 1:
        body(h_refs[0], DTOK)
        return
    i = pl.program_id(1)
    pl.when(i < n_lat_tiles)(lambda: body(h_refs[0], DTOK))
    pl.when(i >= n_lat_tiles)(lambda: body(h_refs[1], ctx_rows))


def _inproj_kernel(*refs, splits, n_src, n_lat_tiles, ctx_rows):
    h_refs = refs[:n_src]
    nw_ref, mod_ref, w_ref = refs[n_src:n_src + 3]
    o_refs = refs[n_src + 3:]

    def body(h_ref, rows):
        mod = mod_ref[0]
        u = _rms(h_ref[0], nw_ref[...]) * (1.0 + mod[1:2]) + mod[0:1]
        z = _dot(u.astype(BF16), w_ref[...])
        for o_ref, (lo, hi) in zip(o_refs, splits):
            o_ref[0, :rows] = z[:, lo:hi].astype(o_ref.dtype)

    _per_stream(body, h_refs, n_lat_tiles, ctx_rows)


def _inproj(h, norm_w, mods, w, splits, dtypes, s_len, t):
    d, n = w.shape
    assert s_len % DTOK == 0
    srcs, src_specs, ctx_rows = _stream_specs(h, s_len)
    bsz = srcs[0].shape[0]
    return pl.pallas_call(
        functools.partial(_inproj_kernel, splits=splits, n_src=len(srcs), n_lat_tiles=s_len // DTOK,
                          ctx_rows=ctx_rows),
        grid=(bsz, pl.cdiv(t, DTOK)),
        in_specs=src_specs + [
            _const_spec((1, d)),
            _mod_spec(d, s_len // DTOK),
            _const_spec((d, n)),
        ],
        out_specs=[pl.BlockSpec((1, DTOK, hi - lo), lambda b, i: (b, i, 0)) for lo, hi in splits],
        out_shape=[jax.ShapeDtypeStruct((bsz, t, hi - lo), dt) for (lo, hi), dt in zip(splits, dtypes)],
        compiler_params=_cparams(("parallel", "parallel")),
        name="inproj",
    )(*srcs, norm_w.reshape(1, d), mods, w)


def _mlp_kernel(*refs, ff_chunk, final, n_src, n_lat_tiles, ctx_rows):
    h_refs = refs[:n_src]
    ma_ref, mb_ref, mod_ref, wo_ref, nw_ref, w1_ref, w2_ref, fw_ref, o_ref = refs[n_src:]
    da = ma_ref.shape[-1]

    def body(h_ref, rows):
        mod = mod_ref[0]
        y = (_dot(ma_ref[0, :rows].astype(BF16), wo_ref[:da])
             + _dot(mb_ref[0, :rows].astype(BF16), wo_ref[da:]))
        h1 = h_ref[0] + mod[2:3] * y
        u = (_rms(h1, nw_ref[...]) * (1.0 + mod[4:5]) + mod[3:4]).astype(BF16)
        d_ff = w1_ref.shape[1]
        acc = jnp.zeros(h1.shape, F32)
        for c in range(d_ff // ff_chunk):
            sl = slice(c * ff_chunk, (c + 1) * ff_chunk)
            a = jnp.maximum(_dot(u, w1_ref[:, sl]), 0.0)
            acc = acc + _dot((a * a).astype(BF16), w2_ref[sl, :])
        h2 = h1 + mod[5:6] * acc
        if final:
            h2 = _rms(h2, fw_ref[...])
        o_ref[0, :rows] = h2

    _per_stream(body, h_refs, n_lat_tiles, ctx_rows)


def _mlp(h, mix_a, mix_b, mods, wo, norm_w, w1, w2, final_w, s_len, out_rows, final):
    da, db = mix_a.shape[-1], mix_b.shape[-1]
    d, d_ff = w1.shape
    assert s_len % DTOK == 0
    srcs, src_specs, ctx_rows = _stream_specs(h, s_len)
    bsz = srcs[0].shape[0]
    tok_spec = lambda width: pl.BlockSpec((1, DTOK, width), lambda b, i: (b, i, 0))
    return pl.pallas_call(
        functools.partial(_mlp_kernel, ff_chunk=512, final=final, n_src=len(srcs), n_lat_tiles=s_len // DTOK,
                          ctx_rows=ctx_rows),
        grid=(bsz, pl.cdiv(out_rows, DTOK)),
        in_specs=src_specs + [
            tok_spec(da), tok_spec(db),
            _mod_spec(d, s_len // DTOK),
            _const_spec((da + db, d)), _const_spec((1, d)),
            _const_spec((d, d_ff)), _const_spec((d_ff, d)), _const_spec((1, d)),
        ],
        out_specs=tok_spec(d),
        out_shape=jax.ShapeDtypeStruct((bsz, out_rows, d), F32),
        compiler_params=_cparams(("parallel", "parallel")),
        name="mlp",
    )(*srcs, mix_a, mix_b, mods, wo, norm_w.reshape(1, d), w1, w2, final_w.reshape(1, d))


NA_QROWS = TOK // GRID_W
NA_WIN = NA_KH + NA_QROWS


def _na_plan(rows):
    n_blk = rows // NA_QROWS
    r = np.arange(rows).reshape(n_blk, NA_QROWS)
    start = np.clip(r - NA_KH // 2, 0, rows - NA_KH)
    u0 = np.clip(r[:, 0] - NA_KH // 2, 0, rows - NA_WIN)
    key = u0[:, None, None] + np.arange(NA_WIN)[None, None, :]
    valid = (key >= start[:, :, None]) & (key < start[:, :, None] + NA_KH)
    d_row = np.where(valid, key - r[:, :, None] + (NA_KH - 1), 0)
    flat = np.concatenate([valid.reshape(n_blk, -1), d_row.reshape(n_blk, -1)], axis=1)
    _, first, pat = np.unique(flat, axis=0, return_index=True, return_inverse=True)
    return u0.astype(np.int32), pat.reshape(-1).astype(np.int32), valid[first], d_row[first]


def _na_bias_table(rpb, valid, d_row):
    n_pat = valid.shape[0]
    heads = rpb.shape[0]
    pairs = NA_WIN // 2
    n_dc = 2 * NA_KW - 1
    cols = np.arange(GRID_W)
    col_start = np.clip(cols - NA_KW // 2, 0, GRID_W - NA_KW)
    col_valid = (cols[None, :] >= col_start[:, None]) & (cols[None, :] < col_start[:, None] + NA_KW)
    d_col = np.clip(cols[None, :] - cols[:, None], 1 - NA_KW, NA_KW - 1) + (NA_KW - 1)
    row_sel = (d_row[..., None] == np.arange(2 * NA_KH - 1)).astype(np.float32)
    col_sel = (d_col[None] == np.arange(n_dc)[:, None, None]).astype(np.float32)
    pair_sel = np.zeros((2, n_dc, GRID_W, 2, GRID_W), np.float32)
    for u1 in range(2):
        pair_sel[u1, :, :, u1, :] = col_sel
    pair_sel = pair_sel.reshape(2 * n_dc, GRID_W, 2 * GRID_W)
    hp = lax.Precision.HIGHEST
    by_row = jnp.einsum('hrc,piur->phiuc', rpb, row_sel, precision=hp)
    by_row = by_row.reshape(n_pat, heads, NA_QROWS, pairs, 2 * n_dc)
    bias = jnp.einsum('phiUk,kqm->phUiqm', by_row, pair_sel, precision=hp)
    keep = (valid.reshape(n_pat, 1, NA_QROWS, 1, pairs, 2, 1).transpose(0, 1, 4, 2, 3, 5, 6)
            & col_valid[None, None, None, None, :, None, :])
    keep = keep.reshape(n_pat, 1, pairs, NA_QROWS, GRID_W, 2 * GRID_W)
    return jnp.where(keep, bias * LOG2E, NEG).reshape(n_pat, heads, pairs, TOK, 2 * GRID_W)


def _softmax_tiles(tiles):
    m = jnp.max(functools.reduce(jnp.maximum, tiles), axis=-1, keepdims=True)
    p = [jnp.exp2(t - m) for t in tiles]
    denom = jnp.sum(functools.reduce(jnp.add, p), axis=-1, keepdims=True)
    return [t.astype(BF16) for t in p], denom


def _lane_tiles(x):
    return [x[:, j:j + LANES] for j in range(0, x.shape[1], LANES)]


def _na_kernel(u0_ref, pat_ref, q_ref, k_ref, v_ref, kc_ref, vc_ref, bias_ref, o_ref, *, n_lat):
    i = pl.program_id(1)
    win = NA_WIN * GRID_W
    n_loc = win // LANES

    def attend(local):
        q = q_ref[0]
        kc = kc_ref[0]
        vc = vc_ref[0]
        if local:
            row0 = pl.multiple_of(u0_ref[i] * GRID_W, GRID_W)
            kw = k_ref[0, pl.ds(row0, win), :]
            vw = v_ref[0, pl.ds(row0, win), :]

        def scores(h):
            sl = slice(h * NA_DH, (h + 1) * NA_DH)
            s_ctx = _lane_tiles(_dot_nt(q[:, sl], kc[:, sl]))
            if not local:
                return s_ctx
            s_loc = _lane_tiles(_dot_nt(q[:, sl], kw[:, sl]))
            return [t + bias_ref[0, h, j] for j, t in enumerate(s_loc)] + s_ctx

        s_next = scores(0)
        for h in range(NA_HEADS):
            sl = slice(h * NA_DH, (h + 1) * NA_DH)
            tiles = s_next
            if h + 1 < NA_HEADS:
                s_next = scores(h + 1)
            p, denom = _softmax_tiles(tiles)
            if local:
                o = (_dot(jnp.concatenate(p[:n_loc], axis=1), vw[:, sl])
                     + _dot(jnp.concatenate(p[n_loc:], axis=1), vc[:, sl]))
            else:
                o = _dot(jnp.concatenate(p, axis=1), vc[:, sl])
            o_ref[0, :, sl] = o / denom

    pl.when(i < n_lat)(functools.partial(attend, True))
    pl.when(i >= n_lat)(functools.partial(attend, False))


def _na(qkv, rpb, s_len, c_len):
    bsz, t, _ = qkv.shape
    width = NA_HEADS * NA_DH
    rows = s_len // GRID_W
    n_lat = s_len // TOK
    n_steps = t // TOK
    assert rows >= NA_WIN and rows % NA_QROWS == 0 and c_len % TOK == 0 and s_len % c_len == 0
    u0, pat, valid, d_row = _na_plan(rows)
    bias = _na_bias_table(rpb, valid, d_row)
    ctx_steps = np.zeros(n_steps - n_lat, np.int32)
    u0 = jnp.asarray(np.concatenate([u0, ctx_steps]))
    pat = jnp.asarray(np.concatenate([pat, ctx_steps]))
    tok_spec = pl.BlockSpec((1, TOK, width), lambda b, i, u0_ref, pat_ref: (b, i, 0))
    ctx_blk = s_len // c_len
    return pl.pallas_call(
        functools.partial(_na_kernel, n_lat=n_lat),
        grid_spec=pltpu.PrefetchScalarGridSpec(
            num_scalar_prefetch=2,
            grid=(bsz, n_steps),
            in_specs=[
                tok_spec,
                pl.BlockSpec((1, s_len, width), lambda b, i, u0_ref, pat_ref: (b, 0, 1),
                             pipeline_mode=pl.Buffered(1)),
                pl.BlockSpec((1, s_len, width), lambda b, i, u0_ref, pat_ref: (b, 0, 2),
                             pipeline_mode=pl.Buffered(1)),
                pl.BlockSpec((1, c_len, width), lambda b, i, u0_ref, pat_ref: (b, ctx_blk, 1)),
                pl.BlockSpec((1, c_len, width), lambda b, i, u0_ref, pat_ref: (b, ctx_blk, 2)),
                pl.BlockSpec((1, NA_HEADS, NA_WIN // 2, TOK, 2 * GRID_W),
                             lambda b, i, u0_ref, pat_ref: (pat_ref[i], 0, 0, 0, 0)),
            ],
            out_specs=tok_spec,
        ),
        out_shape=jax.ShapeDtypeStruct((bsz, t, width), F32),
        compiler_params=_cparams(("parallel", "arbitrary")),
        name="na",
    )(u0, pat, qkv, qkv, qkv, qkv, qkv, bias)


def _lru_tile_index(i, n_lat, n_tiles, direction):
    if direction == 0:
        return (i + n_lat) % n_tiles
    return n_tiles - 1 - i


def _lru_row_permutation():
    seg = TOK // LRU_SEG
    p = np.arange(TOK)
    src = (p % LRU_SEG) * seg + p // LRU_SEG
    return (src[:, None] == np.arange(TOK)[None, :]).astype(np.float32)


def _shift_rows(group, boundary_row, down):
    sub = lax.broadcasted_iota(jnp.int32, group.shape, 0)
    if down:
        return jnp.where(sub == 0, boundary_row, pltpu.roll(group, 1, axis=0))
    return jnp.where(sub == LRU_SEG - 1, boundary_row, pltpu.roll(group, LRU_SEG - 1, axis=0))


def _lru_kernel(*refs, direction, n_lat, n_tiles, combine, ctx_sub):
    if combine:
        (x_ref, prev_ref, next_ref, cw_ref, cb_ref, wg_ref, bg_ref, lam_ref, perm_ref, g_ref, other_ref,
         unperm_ref, o_ref, carry_s) = refs
    else:
        (x_ref, prev_ref, next_ref, cw_ref, cb_ref, wg_ref, bg_ref, lam_ref, perm_ref,
         o_ref, carry_s) = refs
    i = pl.program_id(1)
    tile = _lru_tile_index(i, n_lat, n_tiles, direction)
    ch = x_ref.shape[-1]
    halo = prev_ref.shape[1]
    seg = TOK // LRU_SEG
    grp = LRU_SEG

    @pl.when(i == 0)
    def _():
        carry_s[...] = jnp.zeros_like(carry_s)

    seq_first = jnp.logical_or(tile == 0, tile == n_lat)
    seq_last = jnp.logical_or(tile == n_lat - 1, tile == n_tiles - 1)
    perm = perm_ref[...]
    cw = cw_ref[...]

    def sub_tile(r, n_sub, state):
        rows = slice(r * TOK, (r + 1) * TOK)
        x = _dot(perm, x_ref[0, rows])
        if r == 0:
            prev = jnp.where(seq_first, 0.0, prev_ref[0].astype(F32))
        else:
            prev = x_ref[0, r * TOK - halo:r * TOK].astype(F32)
        if r == n_sub - 1:
            nxt = jnp.where(seq_last, 0.0, next_ref[0].astype(F32))
        else:
            nxt = x_ref[0, (r + 1) * TOK:(r + 1) * TOK + halo].astype(F32)
        wrap_m1 = _shift_rows(x[(seg - 1) * grp:], prev[halo - 1:halo], down=True)
        wrap_m2 = _shift_rows(x[(seg - 2) * grp:(seg - 1) * grp], prev[halo - 2:halo - 1], down=True)
        wrap_p1 = _shift_rows(x[:grp], nxt[0:1], down=False)
        x_m1 = jnp.concatenate([wrap_m1, x[:-grp]], axis=0)
        x_m2 = jnp.concatenate([wrap_m2, wrap_m1, x[:-2 * grp]], axis=0)
        x_p1 = jnp.concatenate([x[grp:], wrap_p1], axis=0)
        xc = cw[0:1] * x_m2 + cw[1:2] * x_m1 + cw[2:3] * x + cw[3:4] * x_p1 + cb_ref[...]

        pre = _dot(xc.astype(BF16), wg_ref[0]) + bg_ref[0]
        r_gate = jax.nn.sigmoid(pre[:, :ch])
        i_gate = jax.nn.sigmoid(pre[:, ch:])
        log_a = (-LRU_C) * r_gate * _softplus(-lam_ref[0])
        a = jnp.exp(log_a)
        bb = jnp.sqrt(1.0 - a * a) * (i_gate * xc)

        steps = range(seg) if direction == 0 else range(seg - 1, -1, -1)
        h = jnp.zeros((grp, ch), F32)
        cum = jnp.ones((grp, ch), F32)
        h_loc = [None] * seg
        cum_loc = [None] * seg
        for j in steps:
            a_j = a[j * grp:(j + 1) * grp]
            h = a_j * h + bb[j * grp:(j + 1) * grp]
            cum = cum * a_j
            h_loc[j] = h
            cum_loc[j] = cum

        entering = [None] * LRU_SEG
        for s in (range(LRU_SEG) if direction == 0 else range(LRU_SEG - 1, -1, -1)):
            entering[s] = state
            state = h[s:s + 1] + cum[s:s + 1] * state
        entering = jnp.concatenate(entering, axis=0)
        y = jnp.concatenate([h_loc[j] + cum_loc[j] * entering for j in range(seg)], axis=0)
        if combine:
            y = (y + other_ref[0, rows]) * jax.nn.gelu(_dot(perm, g_ref[0, rows]))
            o_ref[0, rows] = _dot(unperm_ref[...], y.astype(BF16)).astype(o_ref.dtype)
        else:
            o_ref[0, rows] = y
        return state

    def run(n_sub):
        state = carry_s[...]
        for r in (range(n_sub) if direction == 0 else range(n_sub - 1, -1, -1)):
            state = sub_tile(r, n_sub, state)
        carry_s[...] = state

    n_blk = x_ref.shape[1] // TOK
    if ctx_sub == n_blk:
        run(n_blk)
    else:
        pl.when(i == 0)(lambda: run(ctx_sub))
        pl.when(i > 0)(lambda: run(n_blk))


def _lru(xg, conv_w, conv_b, w_gate, b_gate, lam, s_len, direction, other=None):
    bsz, t, two_ch = xg.shape
    ch = two_ch // 2
    n_lat, n_tiles, ctx_sub = _scan_blocks(s_len, t - s_len, MTOK)
    halo = 16
    combine = other is not None
    tile_of = functools.partial(_lru_tile_index, n_lat=n_lat, n_tiles=n_tiles, direction=direction)
    per_tile = MTOK // halo
    perm = _lru_row_permutation()
    in_specs = [
        pl.BlockSpec((1, MTOK, ch), lambda b, i: (b, tile_of(i), 0)),
        pl.BlockSpec((1, halo, ch), lambda b, i: (b, jnp.maximum(tile_of(i) * per_tile - 1, 0), 0)),
        pl.BlockSpec((1, halo, ch), lambda b, i: (b, jnp.minimum((tile_of(i) + 1) * per_tile, t // halo - 1), 0)),
        _const_spec((4, ch)), _const_spec((1, ch)),
        pl.BlockSpec((1, ch, 2 * ch), lambda b, i: (direction, 0, 0), pipeline_mode=pl.Buffered(1)),
        pl.BlockSpec((1, 1, 2 * ch), lambda b, i: (direction, 0, 0), pipeline_mode=pl.Buffered(1)),
        pl.BlockSpec((1, 1, ch), lambda b, i: (direction, 0, 0), pipeline_mode=pl.Buffered(1)),
        _const_spec((TOK, TOK)),
    ]
    args = [xg, xg, xg, conv_w, conv_b.reshape(1, ch), w_gate, b_gate, lam.reshape(2, 1, ch), jnp.asarray(perm, BF16)]
    if combine:
        in_specs += [pl.BlockSpec((1, MTOK, ch), lambda b, i: (b, tile_of(i), 1)),
                     pl.BlockSpec((1, MTOK, ch), lambda b, i: (b, tile_of(i), 0)),
                     _const_spec((TOK, TOK))]
        args += [xg, other, jnp.asarray(perm.T, BF16)]
    return pl.pallas_call(
        functools.partial(_lru_kernel, direction=direction, n_lat=n_lat, n_tiles=n_tiles, combine=combine,
                          ctx_sub=ctx_sub),
        grid=(bsz, n_tiles),
        in_specs=in_specs,
        out_specs=pl.BlockSpec((1, MTOK, ch), lambda b, i: (b, tile_of(i), 0)),
        out_shape=jax.ShapeDtypeStruct((bsz, t, ch), BF16 if combine else F32),
        scratch_shapes=[pltpu.VMEM((1, ch), F32)],
        compiler_params=_cparams(("parallel", "arbitrary")),
        name="lru_fwd" if direction == 0 else "lru_bwd",
    )(*args)


def _block_diag(w):
    heads, n, _ = w.shape
    eye = jnp.eye(heads, dtype=w.dtype)
    return (eye[:, None, :, None] * w[:, :, None, :]).reshape(heads * n, heads * n)


def _gla_kernel(*refs, direction, combine, ctx_sub):
    if combine:
        qkvg_ref, lr_ref, wa_ref, ba_ref, tri_ref, other_ref, nw_ref, o_ref, state_s = refs
    else:
        qkvg_ref, lr_ref, wa_ref, ba_ref, tri_ref, o_ref, state_s = refs
    key = GLA_HEADS * GLA_DK
    val = GLA_HEADS * GLA_DV
    nchunk = TOK // GLA_CHUNK
    step = pl.program_id(1)

    @pl.when(step == 0)
    def _():
        state_s[...] = jnp.zeros_like(state_s)

    tri = tri_ref[...]
    causal = tri > 0
    chunks = [slice(c * GLA_CHUNK, (c + 1) * GLA_CHUNK) for c in range(nchunk)]
    ends = [(c + 1) * GLA_CHUNK - 1 if direction == 0 else c * GLA_CHUNK for c in range(nchunk)]
    chunk_order = range(nchunk) if direction == 0 else range(nchunk - 1, -1, -1)
    ones = jnp.ones((GLA_CHUNK, GLA_DV), BF16)

    def sub_tile(rows, states):
        z = qkvg_ref[0, rows]
        q = z[:, :key] * (GLA_DK ** -0.5)
        k = z[:, key:2 * key]
        v = z[:, 2 * key:2 * key + val].astype(BF16)
        lr = lr_ref[0, rows][:, direction * GLA_RANK:(direction + 1) * GLA_RANK]
        logit = _dot(lr.astype(BF16), wa_ref[0].astype(BF16)) + ba_ref[0]
        log_a = (jnp.minimum(logit, 0.0) - jnp.log1p(jnp.exp(-jnp.abs(logit)))) / GLA_TAU

        g_hi, g_lo = _split_hi_lo(log_a)
        b = _dot(tri, g_hi) + _dot(tri, g_lo)
        q_in = (q * jnp.exp(b)).astype(BF16)
        k_in = (k * jnp.exp(-b)).astype(BF16)
        b_end = jnp.concatenate([jnp.broadcast_to(b[e:e + 1], (GLA_CHUNK, key)) for e in ends], axis=0)
        k_out = (k * jnp.exp(b_end - b)).astype(BF16)
        decay = [jnp.exp(_dot_tn(g_hi[rc], ones) + _dot_tn(g_lo[rc], ones)) for rc in chunks]

        new_states = []
        for h in range(GLA_HEADS):
            ks = slice(h * GLA_DK, (h + 1) * GLA_DK)
            vs = slice(h * GLA_DV, (h + 1) * GLA_DV)
            v_h = v[:, vs]
            att = jnp.where(causal, _dot_nt(q_in[:, ks], k_in[:, ks]), 0.0)
            o = _dot(att.astype(BF16), v_h)
            state = states[h]
            inter = [None] * nchunk
            for c in chunk_order:
                inter[c] = _dot(q_in[chunks[c], ks], state.astype(BF16))
                state = decay[c][ks] * state + _dot_tn(k_out[chunks[c], ks], v_h[chunks[c]])
            new_states.append(state)
            o = o + jnp.concatenate(inter, axis=0)
            if combine:
                o = o + other_ref[0, rows, vs]
                gate = z[:, 2 * key + val + h * GLA_DV:2 * key + val + (h + 1) * GLA_DV]
                o = _rms(o, nw_ref[...]) * (gate * jax.nn.sigmoid(gate))
            o_ref[0, rows, vs] = o
        return new_states

    def run(n_sub):
        subs = [slice(r * TOK, (r + 1) * TOK) for r in range(n_sub)]
        states = [state_s[h] for h in range(GLA_HEADS)]
        for rows in (subs if direction == 0 else subs[::-1]):
            states = sub_tile(rows, states)
        for h in range(GLA_HEADS):
            state_s[h] = states[h]

    n_blk = qkvg_ref.shape[1] // TOK
    if ctx_sub == n_blk:
        run(n_blk)
    else:
        pl.when(step == 0)(lambda: run(ctx_sub))
        pl.when(step > 0)(lambda: run(n_blk))


def _scan_blocks(s_len, c_len, blk):
    assert s_len % blk == 0 and c_len % TOK == 0 and c_len <= blk
    return s_len // blk, s_len // blk + 1, c_len // TOK


def _gla(qkvg, lr, wa2, ba, s_len, direction, other=None, norm_w=None):
    bsz, t, width = qkvg.shape
    key = GLA_HEADS * GLA_DK
    val = GLA_HEADS * GLA_DV
    n_lat, n_tiles, ctx_sub = _scan_blocks(s_len, t - s_len, MTOK)
    combine = other is not None
    tile_of = functools.partial(_lru_tile_index, n_lat=n_lat, n_tiles=n_tiles, direction=direction)
    pos = np.arange(TOK)
    same = (pos[:, None] // GLA_CHUNK) == (pos[None, :] // GLA_CHUNK)
    tri = same & ((pos[None, :] <= pos[:, None]) if direction == 0 else (pos[None, :] >= pos[:, None]))
    tri = jnp.asarray(tri, BF16)
    in_specs = [
        pl.BlockSpec((1, MTOK, width), lambda b, i: (b, tile_of(i), 0)),
        pl.BlockSpec((1, MTOK, lr.shape[-1]), lambda b, i: (b, tile_of(i), 0)),
        pl.BlockSpec((1, GLA_RANK, key), lambda b, i: (direction, 0, 0), pipeline_mode=pl.Buffered(1)),
        pl.BlockSpec((1, 1, key), lambda b, i: (direction, 0, 0), pipeline_mode=pl.Buffered(1)),
        _const_spec((TOK, TOK)),
    ]
    args = [qkvg, lr, wa2, ba.reshape(2, 1, key), tri]
    if combine:
        in_specs += [pl.BlockSpec((1, MTOK, val), lambda b, i: (b, tile_of(i), 0)), _const_spec((1, GLA_DV))]
        args += [other, norm_w.reshape(1, GLA_DV)]
    return pl.pallas_call(
        functools.partial(_gla_kernel, direction=direction, combine=combine, ctx_sub=ctx_sub),
        grid=(bsz, n_tiles),
        in_specs=in_specs,
        out_specs=pl.BlockSpec((1, MTOK, val), lambda b, i: (b, tile_of(i), 0)),
        out_shape=jax.ShapeDtypeStruct((bsz, t, val), F32),
        scratch_shapes=[pltpu.VMEM((GLA_HEADS, GLA_DK, GLA_DV), F32)],
        compiler_params=_cparams(("parallel", "arbitrary")),
        name="gla_fwd" if direction == 0 else "gla_bwd",
    )(*args)


def _rope_tables(s_len, c_len, reps):
    quarter = GQA_DH // 4
    inv = ROPE_THETA ** (-np.arange(quarter, dtype=np.float32) / quarter)
    t = np.arange(s_len)
    ang_r = (t // GRID_W).astype(np.float32)[:, None] * inv[None, :]
    ang_c = (t % GRID_W).astype(np.float32)[:, None] * inv[None, :]
    cos = np.concatenate([np.cos(ang_r)] * 2 + [np.cos(ang_c)] * 2, axis=1)
    sin = np.concatenate([-np.sin(ang_r), np.sin(ang_r), -np.sin(ang_c), np.sin(ang_c)], axis=1)
    cos = np.concatenate([cos, np.ones((c_len, GQA_DH), np.float32)], axis=0)
    sin = np.concatenate([sin, np.zeros((c_len, GQA_DH), np.float32)], axis=0)
    return (jnp.asarray(np.tile(cos, (1, reps)), F32), jnp.asarray(np.tile(sin, (1, reps)), F32))


def _rope_swap(x):
    lane = lax.broadcasted_iota(jnp.int32, x.shape, 1)
    return jnp.where(lane % 32 < 16, pltpu.roll(x, 128 - 16, axis=1), pltpu.roll(x, 16, axis=1))


def _norm_rope(x, nw, cos, sin, gmean):
    hi, lo = _split_hi_lo(x * x)
    ms = _dot(hi, gmean) + _dot(lo, gmean)
    y = x * lax.rsqrt(ms + EPS) * nw
    slabs = [_rope_swap(y[:, j:j + 128]) for j in range(0, y.shape[1], 128)]
    swapped = slabs[0] if len(slabs) == 1 else jnp.concatenate(slabs, axis=1)
    return y * cos + swapped * sin


def _gqa_prep_kernel(z_ref, qw_ref, kw_ref, cos_ref, sin_ref, gq_ref, gk_ref, q_ref, k_ref, v_ref):
    qd = GQA_HEADS * GQA_DH
    kd = GQA_KV_HEADS * GQA_DH
    z = z_ref[0]
    cos_k = cos_ref[...]
    sin_k = sin_ref[...]
    reps = qd // kd
    cos_q = jnp.concatenate([cos_k] * reps, axis=1)
    sin_q = jnp.concatenate([sin_k] * reps, axis=1)
    q = _norm_rope(z[:, :qd], qw_ref[...], cos_q, sin_q, gq_ref[...]) * (LOG2E * GQA_DH ** -0.5)
    k = _norm_rope(z[:, qd:qd + kd], kw_ref[...], cos_k, sin_k, gk_ref[...])
    v = z[:, qd + kd:qd + 2 * kd]
    for h in range(GQA_HEADS):
        q_ref[0, h] = q[:, h * GQA_DH:(h + 1) * GQA_DH].astype(BF16)
    ones = jnp.ones((z.shape[0], LANES - GQA_DH), BF16)
    for h in range(GQA_KV_HEADS):
        k_ref[0, h] = k[:, h * GQA_DH:(h + 1) * GQA_DH].astype(BF16)
        v_ref[0, h] = jnp.concatenate([v[:, h * GQA_DH:(h + 1) * GQA_DH].astype(BF16), ones], axis=1)


def _group_mean_matrix(n, group):
    idx = np.arange(n) // group
    return jnp.asarray((idx[:, None] == idx[None, :]) / group, BF16)


def _gqa_prep(z, q_norm_w, k_norm_w, s_len, c_len):
    bsz, t, width = z.shape
    qd = GQA_HEADS * GQA_DH
    kd = GQA_KV_HEADS * GQA_DH
    cos, sin = _rope_tables(s_len, c_len, GQA_KV_HEADS)
    head_spec = lambda n, w=GQA_DH: pl.BlockSpec((1, n, TOK, w), lambda b, i: (b, 0, i, 0))
    return pl.pallas_call(
        _gqa_prep_kernel,
        grid=(bsz, t // TOK),
        in_specs=[
            pl.BlockSpec((1, TOK, width), lambda b, i: (b, i, 0)),
            _const_spec((1, qd)), _const_spec((1, kd)),
            pl.BlockSpec((TOK, kd), lambda b, i: (i, 0)),
            pl.BlockSpec((TOK, kd), lambda b, i: (i, 0)),
            _const_spec((qd, qd)), _const_spec((kd, kd)),
        ],
        out_specs=[head_spec(GQA_HEADS), head_spec(GQA_KV_HEADS), head_spec(GQA_KV_HEADS, LANES)],
        out_shape=[jax.ShapeDtypeStruct((bsz, GQA_HEADS, t, GQA_DH), BF16),
                   jax.ShapeDtypeStruct((bsz, GQA_KV_HEADS, t, GQA_DH), BF16),
                   jax.ShapeDtypeStruct((bsz, GQA_KV_HEADS, t, LANES), BF16)],
        compiler_params=_cparams(("parallel", "parallel")),
        name="gqa_prep",
    )(z, jnp.tile(q_norm_w, GQA_HEADS).reshape(1, qd), jnp.tile(k_norm_w, GQA_KV_HEADS).reshape(1, kd),
      cos, sin, _group_mean_matrix(qd, GQA_DH), _group_mean_matrix(kd, GQA_DH))


def _flash_kernel(q_ref, k_ref, v_ref, o_ref, m_s, acc_s, *, group, tk, sub):
    tq = q_ref.shape[2]
    n_kv = k_ref.shape[2] // tk
    units = [(g, r) for g in range(group) for r in range(0, tq, sub)]
    m_s[...] = jnp.full_like(m_s, -jnp.inf)
    acc_s[...] = jnp.zeros_like(acc_s)

    def kv_step(j, carry):
        rows_k = pl.ds(pl.multiple_of(j * tk, tk), tk)
        k = k_ref[0, 0, rows_k, :]
        v = v_ref[0, 0, rows_k, :]
        scores = lambda u: _dot_nt(q_ref[0, units[u][0], units[u][1]:units[u][1] + sub, :], k)
        s_next = scores(0)
        for u in range(len(units)):
            tiles = _lane_tiles(s_next)
            if u + 1 < len(units):
                s_next = scores(u + 1)
            m_prev = m_s[u]
            m_new = jnp.maximum(m_prev, jnp.max(functools.reduce(jnp.maximum, tiles), axis=-1, keepdims=True))
            p = [jnp.exp2(t - m_new).astype(BF16) for t in tiles]
            acc_s[u] = jnp.exp2(m_prev - m_new) * acc_s[u] + _dot(jnp.concatenate(p, axis=1), v)
            m_s[u] = m_new
        return carry

    lax.fori_loop(0, n_kv, kv_step, 0)
    for u, (g, r) in enumerate(units):
        acc = acc_s[u]
        o_ref[0, r:r + sub, g * GQA_DH:(g + 1) * GQA_DH] = acc[:, :GQA_DH] / acc[:, GQA_DH:GQA_DH + 1]


KV_TILE_MAX = 3072
FLASH_SUB = 512


def _kv_tile(t):
    return max(n for n in range(LANES, KV_TILE_MAX + 1, LANES) if t % n == 0)


def _flash(q, k, v, s_len, tq, tk):
    bsz, _, t, dh = q.shape
    sub = FLASH_SUB
    assert t % tk == 0 and s_len % tq == 0 and tq % sub == 0
    group = GQA_HEADS // GQA_KV_HEADS
    n_units = group * (tq // sub)
    return pl.pallas_call(
        functools.partial(_flash_kernel, group=group, tk=tk, sub=sub),
        grid=(bsz, GQA_KV_HEADS, s_len // tq),
        in_specs=[
            pl.BlockSpec((1, group, tq, dh), lambda b, h, i: (b, h, i, 0)),
            pl.BlockSpec((1, 1, t, dh), lambda b, h, i: (b, h, 0, 0)),
            pl.BlockSpec((1, 1, t, LANES), lambda b, h, i: (b, h, 0, 0)),
        ],
        out_specs=pl.BlockSpec((1, tq, group * dh), lambda b, h, i: (b, i, h)),
        out_shape=jax.ShapeDtypeStruct((bsz, s_len, GQA_HEADS * dh), F32),
        scratch_shapes=[pltpu.VMEM((n_units, sub, LANES), F32), pltpu.VMEM((n_units, sub, LANES), F32)],
        compiler_params=_cparams(("parallel", "parallel", "arbitrary")),
        name="gqa_flash",
    )(q, k, v)


def kernel(x, c, ctx, c_ctx, norm1_w, norm2_w, w_mod, b_mod, w_ff1, w_ff2,
           w_in_even, na_rpb, lru_conv_w, lru_conv_b, lru_wa, lru_ba, lru_wx, lru_bx, lru_lambda, w_out_even,
           w_in_odd, gla_wa2, gla_ba, gla_norm_w, gqa_q_norm_w, gqa_k_norm_w, w_out_odd, final_norm_w):
    bsz, s_len, d = x.shape
    c_len = ctx.shape[1]
    depth = w_mod.shape[0]
    assert s_len % TOK == 0 and c_len % TOK == 0 and depth == 2

    pad = (-(bsz + 1)) % 8
    cvec = jnp.concatenate([c_ctx[None], c, jnp.zeros((pad, d), F32)], axis=0)
    mods = _modulation(cvec, w_mod, b_mod)
    h = (x, ctx)
    t = s_len + c_len

    na_w = NA_HEADS * NA_DH
    ch = lru_lambda.shape[-1]
    col_scale = jnp.where(jnp.arange(w_in_even.shape[-1]) < na_w, LOG2E * NA_DH ** -0.5, 1.0)
    qkv, xg = _inproj(h, norm1_w[0], mods[0], (w_in_even[0] * col_scale).astype(BF16),
                      ((0, 3 * na_w), (3 * na_w, 3 * na_w + 2 * ch)), (BF16, BF16), s_len, t)
    na_out = _na(qkv, na_rpb[0], s_len, c_len)
    w_gate = jnp.stack([jnp.concatenate([_block_diag(lru_wa[0, dr]), _block_diag(lru_wx[0, dr])], axis=1)
                        for dr in range(2)]).astype(BF16)
    b_gate = jnp.concatenate([lru_ba[0], lru_bx[0]], axis=-1).reshape(2, 1, 2 * ch)
    lru_args = (xg, lru_conv_w[0], lru_conv_b[0], w_gate, b_gate, lru_lambda[0], s_len)
    lru_b = _lru(*lru_args, direction=1)
    lru_out = _lru(*lru_args, direction=0, other=lru_b)
    h = _mlp(h, na_out, lru_out, mods[0], w_out_even[0].astype(BF16), norm2_w[0],
             w_ff1[0].astype(BF16), w_ff2[0].astype(BF16), final_norm_w, s_len, s_len + c_len, final=False)

    key = GLA_HEADS * GLA_DK
    val = GLA_HEADS * GLA_DV
    gla_w = 2 * key + 2 * val
    gqa_w = (GQA_HEADS + 2 * GQA_KV_HEADS) * GQA_DH
    w_in = w_in_odd[0]
    w_in = jnp.concatenate([w_in[:, :gla_w], w_in[:, gla_w + 2 * GLA_RANK:], w_in[:, gla_w:gla_w + 2 * GLA_RANK]],
                           axis=1).astype(BF16)
    qkvg, gqa_z, lr = _inproj(h, norm1_w[1], mods[1], w_in,
                              ((0, gla_w), (gla_w, gla_w + gqa_w), (gla_w + gqa_w, gla_w + gqa_w + 2 * GLA_RANK)),
                              (F32, F32, F32), s_len, t)
    gla_b = _gla(qkvg, lr, gla_wa2[0], gla_ba[0], s_len, direction=1)
    gla_out = _gla(qkvg, lr, gla_wa2[0], gla_ba[0], s_len, direction=0, other=gla_b, norm_w=gla_norm_w[0])
    q, k, v = _gqa_prep(gqa_z, gqa_q_norm_w[0], gqa_k_norm_w[0], s_len, c_len)
    gqa_out = _flash(q, k, v, s_len, tq=2 * FLASH_SUB, tk=_kv_tile(s_len + c_len))
    return _mlp(h, gla_out, gqa_out, mods[1], w_out_odd[0].astype(BF16), norm2_w[1],
                w_ff1[1].astype(BF16), w_ff2[1].astype(BF16), final_norm_w, s_len, s_len, final=True)
```

```python
import functools

import numpy as np
import jax
import jax.numpy as jnp
from jax import lax
from jax.experimental import pallas as pl
from jax.experimental.pallas import tpu as pltpu

F32 = jnp.float32
BF16 = jnp.bfloat16

EPS = 1e-6
GRID_W = 64
LANES = 128
N_MOD = 6
TOK = 256
DTOK = 512
MTOK = 512
NEG = -1e30

NA_HEADS = 8
NA_DH = 64
NA_KH = 8
NA_KW = 16
LRU_C = 8.0
LRU_SEG = 8
GLA_HEADS = 4
GLA_DK = 64
GLA_DV = 128
GLA_RANK = 16
GLA_TAU = 16.0
GLA_CHUNK = 64
GQA_HEADS = 8
GQA_KV_HEADS = 2
GQA_DH = 64
ROPE_THETA = 10000.0
LOG2E = 1.4426950408889634

VMEM_LIMIT = 56 * 1024 * 1024


def _cparams(sem):
    return pltpu.CompilerParams(dimension_semantics=sem, vmem_limit_bytes=VMEM_LIMIT)


def _const_spec(shape):
    nd = len(shape)
    return pl.BlockSpec(shape, lambda *_: (0,) * nd, pipeline_mode=pl.Buffered(1))


def _dot(a, b):
    return jnp.dot(a, b, preferred_element_type=F32)


def _dot_nt(a, b):
    return lax.dot_general(a, b, (((1,), (1,)), ((), ())), preferred_element_type=F32)


def _dot_tn(a, b):
    return lax.dot_general(a, b, (((0,), (0,)), ((), ())), preferred_element_type=F32)


def _split_hi_lo(x):
    hi = x.astype(BF16)
    lo = (x - hi.astype(F32)).astype(BF16)
    return hi, lo


def _rms(x, w):
    ms = jnp.mean(x * x, axis=-1, keepdims=True)
    return x * lax.rsqrt(ms + EPS) * w


def _softplus(z):
    return jnp.maximum(z, 0.0) + jnp.log1p(jnp.exp(-jnp.abs(z)))


def _mod_kernel(c_ref, w_ref, b_ref, o_ref):
    c = c_ref[...]
    s = c * jax.nn.sigmoid(c)
    o_ref[0] = jnp.dot(s, w_ref[0], precision=lax.Precision.HIGHEST, preferred_element_type=F32) + b_ref[0]


def _modulation(cvec, w_mod, b_mod):
    depth, d, _ = w_mod.shape
    rows = cvec.shape[0]
    out = pl.pallas_call(
        _mod_kernel,
        grid=(depth, N_MOD),
        in_specs=[
            pl.BlockSpec((rows, d), lambda i, j: (0, 0)),
            pl.BlockSpec((1, d, d), lambda i, j: (i, 0, j)),
            pl.BlockSpec((1, 1, d), lambda i, j: (i, 0, j)),
        ],
        out_specs=pl.BlockSpec((1, rows, d), lambda i, j: (i, 0, j)),
        out_shape=jax.ShapeDtypeStruct((depth, rows, N_MOD * d), F32),
        compiler_params=_cparams(("arbitrary", "arbitrary")),
        name="modulation",
    )(cvec, w_mod, b_mod.reshape(depth, 1, N_MOD * d))
    return out.reshape(depth, rows, N_MOD, d)


def _mod_spec(d, n_lat_tiles):
    return pl.BlockSpec((1, N_MOD, d), lambda b, t: (jnp.where(t >= n_lat_tiles, 0, b + 1), 0, 0))


def _stream_specs(h, s_len):
    n_lat = s_len // DTOK
    if not isinstance(h, tuple):
        return [h], [pl.BlockSpec((1, DTOK, h.shape[-1]), lambda b, i: (b, i, 0))], 0
    lat, ctx = h
    c_len, d = ctx.shape[1:]
    assert c_len <= DTOK and lat.shape[1] == s_len
    return ([lat, ctx],
            [pl.BlockSpec((1, DTOK, d), lambda b, i: (b, jnp.minimum(i, n_lat - 1), 0)),
             pl.BlockSpec((1, c_len, d), lambda b, i: (b, 0, 0))], c_len)


def _per_stream(body, h_refs, n_lat_tiles, ctx_rows):
    if len(h_refs) == 1:
        body(h_refs[0], DTOK)
        return
    i = pl.program_id(1)
    pl.when(i < n_lat_tiles)(lambda: body(h_refs[0], DTOK))
    pl.when(i >= n_lat_tiles)(lambda: body(h_refs[1], ctx_rows))


def _inproj_kernel(*refs, splits, n_src, n_lat_tiles, ctx_rows):
    h_refs = refs[:n_src]
    nw_ref, mod_ref, w_ref = refs[n_src:n_src + 3]
    o_refs = refs[n_src + 3:]

    def body(h_ref, rows):
        mod = mod_ref[0]
        u = _rms(h_ref[0], nw_ref[...]) * (1.0 + mod[1:2]) + mod[0:1]
        z = _dot(u.astype(BF16), w_ref[...])
        for o_ref, (lo, hi) in zip(o_refs, splits):
            o_ref[0, :rows] = z[:, lo:hi].astype(o_ref.dtype)

    _per_stream(body, h_refs, n_lat_tiles, ctx_rows)


def _inproj(h, norm_w, mods, w, splits, dtypes, s_len, t):
    d, n = w.shape
    assert s_len % DTOK == 0
    srcs, src_specs, ctx_rows = _stream_specs(h, s_len)
    bsz = srcs[0].shape[0]
    return pl.pallas_call(
        functools.partial(_inproj_kernel, splits=splits, n_src=len(srcs), n_lat_tiles=s_len // DTOK,
                          ctx_rows=ctx_rows),
        grid=(bsz, pl.cdiv(t, DTOK)),
        in_specs=src_specs + [
            _const_spec((1, d)),
            _mod_spec(d, s_len // DTOK),
            _const_spec((d, n)),
        ],
        out_specs=[pl.BlockSpec((1, DTOK, hi - lo), lambda b, i: (b, i, 0)) for lo, hi in splits],
        out_shape=[jax.ShapeDtypeStruct((bsz, t, hi - lo), dt) for (lo, hi), dt in zip(splits, dtypes)],
        compiler_params=_cparams(("parallel", "parallel")),
        name="inproj",
    )(*srcs, norm_w.reshape(1, d), mods, w)


def _mlp_kernel(*refs, ff_chunk, final, n_src, n_lat_tiles, ctx_rows):
    h_refs = refs[:n_src]
    ma_ref, mb_ref, mod_ref, wo_ref, nw_ref, w1_ref, w2_ref, fw_ref, o_ref = refs[n_src:]
    da = ma_ref.shape[-1]

    def body(h_ref, rows):
        mod = mod_ref[0]
        y = (_dot(ma_ref[0, :rows].astype(BF16), wo_ref[:da])
             + _dot(mb_ref[0, :rows].astype(BF16), wo_ref[da:]))
        h1 = h_ref[0] + mod[2:3] * y
        u = (_rms(h1, nw_ref[...]) * (1.0 + mod[4:5]) + mod[3:4]).astype(BF16)
        d_ff = w1_ref.shape[1]
        acc = jnp.zeros(h1.shape, F32)
        for c in range(d_ff // ff_chunk):
            sl = slice(c * ff_chunk, (c + 1) * ff_chunk)
            a = jnp.maximum(_dot(u, w1_ref[:, sl]), 0.0)
            acc = acc + _dot((a * a).astype(BF16), w2_ref[sl, :])
        h2 = h1 + mod[5:6] * acc
        if final:
            h2 = _rms(h2, fw_ref[...])
        o_ref[0, :rows] = h2

    _per_stream(body, h_refs, n_lat_tiles, ctx_rows)


def _mlp(h, mix_a, mix_b, mods, wo, norm_w, w1, w2, final_w, s_len, out_rows, final):
    da, db = mix_a.shape[-1], mix_b.shape[-1]
    d, d_ff = w1.shape
    assert s_len % DTOK == 0
    srcs, src_specs, ctx_rows = _stream_specs(h, s_len)
    bsz = srcs[0].shape[0]
    tok_spec = lambda width: pl.BlockSpec((1, DTOK, width), lambda b, i: (b, i, 0))
    return pl.pallas_call(
        functools.partial(_mlp_kernel, ff_chunk=512, final=final, n_src=len(srcs), n_lat_tiles=s_len // DTOK,
                          ctx_rows=ctx_rows),
        grid=(bsz, pl.cdiv(out_rows, DTOK)),
        in_specs=src_specs + [
            tok_spec(da), tok_spec(db),
            _mod_spec(d, s_len // DTOK),
            _const_spec((da + db, d)), _const_spec((1, d)),
            _const_spec((d, d_ff)), _const_spec((d_ff, d)), _const_spec((1, d)),
        ],
        out_specs=tok_spec(d),
        out_shape=jax.ShapeDtypeStruct((bsz, out_rows, d), F32),
        compiler_params=_cparams(("parallel", "parallel")),
        name="mlp",
    )(*srcs, mix_a, mix_b, mods, wo, norm_w.reshape(1, d), w1, w2, final_w.reshape(1, d))


NA_QROWS = TOK // GRID_W
NA_WIN = NA_KH + NA_QROWS


def _na_plan(rows):
    n_blk = rows // NA_QROWS
    r = np.arange(rows).reshape(n_blk, NA_QROWS)
    start = np.clip(r - NA_KH // 2, 0, rows - NA_KH)
    u0 = np.clip(r[:, 0] - NA_KH // 2, 0, rows - NA_WIN)
    key = u0[:, None, None] + np.arange(NA_WIN)[None, None, :]
    valid = (key >= start[:, :, None]) & (key < start[:, :, None] + NA_KH)
    d_row = np.where(valid, key - r[:, :, None] + (NA_KH - 1), 0)
    flat = np.concatenate([valid.reshape(n_blk, -1), d_row.reshape(n_blk, -1)], axis=1)
    _, first, pat = np.unique(flat, axis=0, return_index=True, return_inverse=True)
    return u0.astype(np.int32), pat.reshape(-1).astype(np.int32), valid[first], d_row[first]


def _na_bias_table(rpb, valid, d_row):
    n_pat = valid.shape[0]
    heads = rpb.shape[0]
    pairs = NA_WIN // 2
    n_dc = 2 * NA_KW - 1
    cols = np.arange(GRID_W)
    col_start = np.clip(cols - NA_KW // 2, 0, GRID_W - NA_KW)
    col_valid = (cols[None, :] >= col_start[:, None]) & (cols[None, :] < col_start[:, None] + NA_KW)
    d_col = np.clip(cols[None, :] - cols[:, None], 1 - NA_KW, NA_KW - 1) + (NA_KW - 1)
    row_sel = (d_row[..., None] == np.arange(2 * NA_KH - 1)).astype(np.float32)
    col_sel = (d_col[None] == np.arange(n_dc)[:, None, None]).astype(np.float32)
    pair_sel = np.zeros((2, n_dc, GRID_W, 2, GRID_W), np.float32)
    for u1 in range(2):
        pair_sel[u1, :, :, u1, :] = col_sel
    pair_sel = pair_sel.reshape(2 * n_dc, GRID_W, 2 * GRID_W)
    hp = lax.Precision.HIGHEST
    by_row = jnp.einsum('hrc,piur->phiuc', rpb, row_sel, precision=hp)
    by_row = by_row.reshape(n_pat, heads, NA_QROWS, pairs, 2 * n_dc)
    bias = jnp.einsum('phiUk,kqm->phUiqm', by_row, pair_sel, precision=hp)
    keep = (valid.reshape(n_pat, 1, NA_QROWS, 1, pairs, 2, 1).transpose(0, 1, 4, 2, 3, 5, 6)
            & col_valid[None, None, None, None, :, None, :])
    keep = keep.reshape(n_pat, 1, pairs, NA_QROWS, GRID_W, 2 * GRID_W)
    return jnp.where(keep, bias * LOG2E, NEG).reshape(n_pat, heads, pairs, TOK, 2 * GRID_W)


def _softmax_tiles(tiles):
    m = jnp.max(functools.reduce(jnp.maximum, tiles), axis=-1, keepdims=True)
    p = [jnp.exp2(t - m) for t in tiles]
    denom = jnp.sum(functools.reduce(jnp.add, p), axis=-1, keepdims=True)
    return [t.astype(BF16) for t in p], denom


def _lane_tiles(x):
    return [x[:, j:j + LANES] for j in range(0, x.shape[1], LANES)]


def _na_kernel(u0_ref, pat_ref, q_ref, k_ref, v_ref, kc_ref, vc_ref, bias_ref, o_ref, *, n_lat):
    i = pl.program_id(1)
    win = NA_WIN * GRID_W
    n_loc = win // LANES

    def attend(local):
        q = q_ref[0]
        kc = kc_ref[0]
        vc = vc_ref[0]
        if local:
            row0 = pl.multiple_of(u0_ref[i] * GRID_W, GRID_W)
            kw = k_ref[0, pl.ds(row0, win), :]
            vw = v_ref[0, pl.ds(row0, win), :]

        def scores(h):
            sl = slice(h * NA_DH, (h + 1) * NA_DH)
            s_ctx = _lane_tiles(_dot_nt(q[:, sl], kc[:, sl]))
            if not local:
                return s_ctx
            s_loc = _lane_tiles(_dot_nt(q[:, sl], kw[:, sl]))
            return [t + bias_ref[0, h, j] for j, t in enumerate(s_loc)] + s_ctx

        ahead = 1
        pending = [scores(h) for h in range(ahead)]
        for h in range(NA_HEADS):
            sl = slice(h * NA_DH, (h + 1) * NA_DH)
            tiles = pending.pop(0)
            if h + ahead < NA_HEADS:
                pending.append(scores(h + ahead))
            p, denom = _softmax_tiles(tiles)
            if local:
                o = (_dot(jnp.concatenate(p[:n_loc], axis=1), vw[:, sl])
                     + _dot(jnp.concatenate(p[n_loc:], axis=1), vc[:, sl]))
            else:
                o = _dot(jnp.concatenate(p, axis=1), vc[:, sl])
            o_ref[0, :, sl] = o / denom

    pl.when(i < n_lat)(functools.partial(attend, True))
    pl.when(i >= n_lat)(functools.partial(attend, False))


def _na(qkv, rpb, s_len, c_len):
    bsz, t, _ = qkv.shape
    width = NA_HEADS * NA_DH
    rows = s_len // GRID_W
    n_lat = s_len // TOK
    n_steps = t // TOK
    assert rows >= NA_WIN and rows % NA_QROWS == 0 and c_len % TOK == 0 and s_len % c_len == 0
    u0, pat, valid, d_row = _na_plan(rows)
    bias = _na_bias_table(rpb, valid, d_row)
    ctx_steps = np.zeros(n_steps - n_lat, np.int32)
    u0 = jnp.asarray(np.concatenate([u0, ctx_steps]))
    pat = jnp.asarray(np.concatenate([pat, ctx_steps]))
    tok_spec = pl.BlockSpec((1, TOK, width), lambda b, i, u0_ref, pat_ref: (b, i, 0))
    ctx_blk = s_len // c_len
    return pl.pallas_call(
        functools.partial(_na_kernel, n_lat=n_lat),
        grid_spec=pltpu.PrefetchScalarGridSpec(
            num_scalar_prefetch=2,
            grid=(bsz, n_steps),
            in_specs=[
                tok_spec,
                pl.BlockSpec((1, s_len, width), lambda b, i, u0_ref, pat_ref: (b, 0, 1),
                             pipeline_mode=pl.Buffered(1)),
                pl.BlockSpec((1, s_len, width), lambda b, i, u0_ref, pat_ref: (b, 0, 2),
                             pipeline_mode=pl.Buffered(1)),
                pl.BlockSpec((1, c_len, width), lambda b, i, u0_ref, pat_ref: (b, ctx_blk, 1)),
                pl.BlockSpec((1, c_len, width), lambda b, i, u0_ref, pat_ref: (b, ctx_blk, 2)),
                pl.BlockSpec((1, NA_HEADS, NA_WIN // 2, TOK, 2 * GRID_W),
                             lambda b, i, u0_ref, pat_ref: (pat_ref[i], 0, 0, 0, 0)),
            ],
            out_specs=tok_spec,
        ),
        out_shape=jax.ShapeDtypeStruct((bsz, t, width), F32),
        compiler_params=_cparams(("parallel", "arbitrary")),
        name="na",
    )(u0, pat, qkv, qkv, qkv, qkv, qkv, bias)


def _lru_tile_index(i, n_lat, n_tiles, direction):
    if direction == 0:
        return (i + n_lat) % n_tiles
    return n_tiles - 1 - i


def _lru_row_permutation():
    seg = TOK // LRU_SEG
    p = np.arange(TOK)
    src = (p % LRU_SEG) * seg + p // LRU_SEG
    return (src[:, None] == np.arange(TOK)[None, :]).astype(np.float32)


def _shift_rows(group, boundary_row, down):
    sub = lax.broadcasted_iota(jnp.int32, group.shape, 0)
    if down:
        return jnp.where(sub == 0, boundary_row, pltpu.roll(group, 1, axis=0))
    return jnp.where(sub == LRU_SEG - 1, boundary_row, pltpu.roll(group, LRU_SEG - 1, axis=0))


def _lru_kernel(*refs, direction, n_lat, n_tiles, combine, ctx_sub):
    if combine:
        (x_ref, prev_ref, next_ref, cw_ref, cb_ref, wg_ref, bg_ref, lam_ref, perm_ref, g_ref, other_ref,
         unperm_ref, o_ref, carry_s) = refs
    else:
        (x_ref, prev_ref, next_ref, cw_ref, cb_ref, wg_ref, bg_ref, lam_ref, perm_ref,
         o_ref, carry_s) = refs
    i = pl.program_id(1)
    tile = _lru_tile_index(i, n_lat, n_tiles, direction)
    ch = x_ref.shape[-1]
    halo = prev_ref.shape[1]
    seg = TOK // LRU_SEG
    grp = LRU_SEG

    @pl.when(i == 0)
    def _():
        carry_s[...] = jnp.zeros_like(carry_s)

    seq_first = jnp.logical_or(tile == 0, tile == n_lat)
    seq_last = jnp.logical_or(tile == n_lat - 1, tile == n_tiles - 1)
    perm = perm_ref[...]
    cw = cw_ref[...]

    def sub_tile(r, n_sub, state):
        rows = slice(r * TOK, (r + 1) * TOK)
        x = _dot(perm, x_ref[0, rows])
        if r == 0:
            prev = jnp.where(seq_first, 0.0, prev_ref[0].astype(F32))
        else:
            prev = x_ref[0, r * TOK - halo:r * TOK].astype(F32)
        if r == n_sub - 1:
            nxt = jnp.where(seq_last, 0.0, next_ref[0].astype(F32))
        else:
            nxt = x_ref[0, (r + 1) * TOK:(r + 1) * TOK + halo].astype(F32)
        wrap_m1 = _shift_rows(x[(seg - 1) * grp:], prev[halo - 1:halo], down=True)
        wrap_m2 = _shift_rows(x[(seg - 2) * grp:(seg - 1) * grp], prev[halo - 2:halo - 1], down=True)
        wrap_p1 = _shift_rows(x[:grp], nxt[0:1], down=False)
        x_m1 = jnp.concatenate([wrap_m1, x[:-grp]], axis=0)
        x_m2 = jnp.concatenate([wrap_m2, wrap_m1, x[:-2 * grp]], axis=0)
        x_p1 = jnp.concatenate([x[grp:], wrap_p1], axis=0)
        xc = cw[0:1] * x_m2 + cw[1:2] * x_m1 + cw[2:3] * x + cw[3:4] * x_p1 + cb_ref[...]

        pre = _dot(xc.astype(BF16), wg_ref[0]) + bg_ref[0]
        r_gate = jax.nn.sigmoid(pre[:, :ch])
        i_gate = jax.nn.sigmoid(pre[:, ch:])
        log_a = (-LRU_C) * r_gate * _softplus(-lam_ref[0])
        a = jnp.exp(log_a)
        bb = jnp.sqrt(1.0 - a * a) * (i_gate * xc)

        steps = range(seg) if direction == 0 else range(seg - 1, -1, -1)
        h = jnp.zeros((grp, ch), F32)
        cum = jnp.ones((grp, ch), F32)
        h_loc = [None] * seg
        cum_loc = [None] * seg
        for j in steps:
            a_j = a[j * grp:(j + 1) * grp]
            h = a_j * h + bb[j * grp:(j + 1) * grp]
            cum = cum * a_j
            h_loc[j] = h
            cum_loc[j] = cum

        entering = [None] * LRU_SEG
        for s in (range(LRU_SEG) if direction == 0 else range(LRU_SEG - 1, -1, -1)):
            entering[s] = state
            state = h[s:s + 1] + cum[s:s + 1] * state
        entering = jnp.concatenate(entering, axis=0)
        y = jnp.concatenate([h_loc[j] + cum_loc[j] * entering for j in range(seg)], axis=0)
        if combine:
            y = (y + other_ref[0, rows]) * jax.nn.gelu(_dot(perm, g_ref[0, rows]))
            o_ref[0, rows] = _dot(unperm_ref[...], y.astype(BF16)).astype(o_ref.dtype)
        else:
            o_ref[0, rows] = y
        return state

    def run(n_sub):
        state = carry_s[...]
        for r in (range(n_sub) if direction == 0 else range(n_sub - 1, -1, -1)):
            state = sub_tile(r, n_sub, state)
        carry_s[...] = state

    n_blk = x_ref.shape[1] // TOK
    if ctx_sub == n_blk:
        run(n_blk)
    else:
        pl.when(i == 0)(lambda: run(ctx_sub))
        pl.when(i > 0)(lambda: run(n_blk))


def _lru(xg, conv_w, conv_b, w_gate, b_gate, lam, s_len, direction, other=None):
    bsz, t, two_ch = xg.shape
    ch = two_ch // 2
    n_lat, n_tiles, ctx_sub = _scan_blocks(s_len, t - s_len, MTOK)
    halo = 16
    combine = other is not None
    tile_of = functools.partial(_lru_tile_index, n_lat=n_lat, n_tiles=n_tiles, direction=direction)
    per_tile = MTOK // halo
    perm = _lru_row_permutation()
    in_specs = [
        pl.BlockSpec((1, MTOK, ch), lambda b, i: (b, tile_of(i), 0)),
        pl.BlockSpec((1, halo, ch), lambda b, i: (b, jnp.maximum(tile_of(i) * per_tile - 1, 0), 0)),
        pl.BlockSpec((1, halo, ch), lambda b, i: (b, jnp.minimum((tile_of(i) + 1) * per_tile, t // halo - 1), 0)),
        _const_spec((4, ch)), _const_spec((1, ch)),
        pl.BlockSpec((1, ch, 2 * ch), lambda b, i: (direction, 0, 0), pipeline_mode=pl.Buffered(1)),
        pl.BlockSpec((1, 1, 2 * ch), lambda b, i: (direction, 0, 0), pipeline_mode=pl.Buffered(1)),
        pl.BlockSpec((1, 1, ch), lambda b, i: (direction, 0, 0), pipeline_mode=pl.Buffered(1)),
        _const_spec((TOK, TOK)),
    ]
    args = [xg, xg, xg, conv_w, conv_b.reshape(1, ch), w_gate, b_gate, lam.reshape(2, 1, ch), jnp.asarray(perm, BF16)]
    if combine:
        in_specs += [pl.BlockSpec((1, MTOK, ch), lambda b, i: (b, tile_of(i), 1)),
                     pl.BlockSpec((1, MTOK, ch), lambda b, i: (b, tile_of(i), 0)),
                     _const_spec((TOK, TOK))]
        args += [xg, other, jnp.asarray(perm.T, BF16)]
    return pl.pallas_call(
        functools.partial(_lru_kernel, direction=direction, n_lat=n_lat, n_tiles=n_tiles, combine=combine,
                          ctx_sub=ctx_sub),
        grid=(bsz, n_tiles),
        in_specs=in_specs,
        out_specs=pl.BlockSpec((1, MTOK, ch), lambda b, i: (b, tile_of(i), 0)),
        out_shape=jax.ShapeDtypeStruct((bsz, t, ch), BF16 if combine else F32),
        scratch_shapes=[pltpu.VMEM((1, ch), F32)],
        compiler_params=_cparams(("parallel", "arbitrary")),
        name="lru_fwd" if direction == 0 else "lru_bwd",
    )(*args)


def _block_diag(w):
    heads, n, _ = w.shape
    eye = jnp.eye(heads, dtype=w.dtype)
    return (eye[:, None, :, None] * w[:, :, None, :]).reshape(heads * n, heads * n)


def _gla_kernel(*refs, direction, combine, ctx_sub):
    if combine:
        qkvg_ref, lr_ref, wa_ref, ba_ref, tri_ref, other_ref, nw_ref, o_ref, state_s = refs
    else:
        qkvg_ref, lr_ref, wa_ref, ba_ref, tri_ref, o_ref, state_s = refs
    key = GLA_HEADS * GLA_DK
    val = GLA_HEADS * GLA_DV
    nchunk = TOK // GLA_CHUNK
    step = pl.program_id(1)

    @pl.when(step == 0)
    def _():
        state_s[...] = jnp.zeros_like(state_s)

    tri = tri_ref[...]
    causal = tri > 0
    chunks = [slice(c * GLA_CHUNK, (c + 1) * GLA_CHUNK) for c in range(nchunk)]
    ends = [(c + 1) * GLA_CHUNK - 1 if direction == 0 else c * GLA_CHUNK for c in range(nchunk)]
    chunk_order = range(nchunk) if direction == 0 else range(nchunk - 1, -1, -1)
    ones = jnp.ones((GLA_CHUNK, GLA_DV), BF16)

    def sub_tile(rows, states):
        z = qkvg_ref[0, rows]
        q = z[:, :key] * (GLA_DK ** -0.5)
        k = z[:, key:2 * key]
        v = z[:, 2 * key:2 * key + val].astype(BF16)
        lr = lr_ref[0, rows][:, direction * GLA_RANK:(direction + 1) * GLA_RANK]
        logit = _dot(lr.astype(BF16), wa_ref[0].astype(BF16)) + ba_ref[0]
        log_a = (jnp.minimum(logit, 0.0) - jnp.log1p(jnp.exp(-jnp.abs(logit)))) / GLA_TAU

        g_hi, g_lo = _split_hi_lo(log_a)
        b = _dot(tri, g_hi) + _dot(tri, g_lo)
        q_in = (q * jnp.exp(b)).astype(BF16)
        k_in = (k * jnp.exp(-b)).astype(BF16)
        b_end = jnp.concatenate([jnp.broadcast_to(b[e:e + 1], (GLA_CHUNK, key)) for e in ends], axis=0)
        k_out = (k * jnp.exp(b_end - b)).astype(BF16)
        decay = [jnp.exp(_dot_tn(g_hi[rc], ones) + _dot_tn(g_lo[rc], ones)) for rc in chunks]

        new_states = []
        for h in range(GLA_HEADS):
            ks = slice(h * GLA_DK, (h + 1) * GLA_DK)
            vs = slice(h * GLA_DV, (h + 1) * GLA_DV)
            v_h = v[:, vs]
            att = jnp.where(causal, _dot_nt(q_in[:, ks], k_in[:, ks]), 0.0)
            o = _dot(att.astype(BF16), v_h)
            state = states[h]
            inter = [None] * nchunk
            for c in chunk_order:
                inter[c] = _dot(q_in[chunks[c], ks], state.astype(BF16))
                state = decay[c][ks] * state + _dot_tn(k_out[chunks[c], ks], v_h[chunks[c]])
            new_states.append(state)
            o = o + jnp.concatenate(inter, axis=0)
            if combine:
                o = o + other_ref[0, rows, vs]
                gate = z[:, 2 * key + val + h * GLA_DV:2 * key + val + (h + 1) * GLA_DV]
                o = _rms(o, nw_ref[...]) * (gate * jax.nn.sigmoid(gate))
            o_ref[0, rows, vs] = o
        return new_states

    def run(n_sub):
        subs = [slice(r * TOK, (r + 1) * TOK) for r in range(n_sub)]
        states = [state_s[h] for h in range(GLA_HEADS)]
        for rows in (subs if direction == 0 else subs[::-1]):
            states = sub_tile(rows, states)
        for h in range(GLA_HEADS):
            state_s[h] = states[h]

    n_blk = qkvg_ref.shape[1] // TOK
    if ctx_sub == n_blk:
        run(n_blk)
    else:
        pl.when(step == 0)(lambda: run(ctx_sub))
        pl.when(step > 0)(lambda: run(n_blk))


def _scan_blocks(s_len, c_len, blk):
    assert s_len % blk == 0 and c_len % TOK == 0 and c_len <= blk
    return s_len // blk, s_len // blk + 1, c_len // TOK


def _gla(qkvg, lr, wa2, ba, s_len, direction, other=None, norm_w=None):
    bsz, t, width = qkvg.shape
    key = GLA_HEADS * GLA_DK
    val = GLA_HEADS * GLA_DV
    n_lat, n_tiles, ctx_sub = _scan_blocks(s_len, t - s_len, MTOK)
    combine = other is not None
    tile_of = functools.partial(_lru_tile_index, n_lat=n_lat, n_tiles=n_tiles, direction=direction)
    pos = np.arange(TOK)
    same = (pos[:, None] // GLA_CHUNK) == (pos[None, :] // GLA_CHUNK)
    tri = same & ((pos[None, :] <= pos[:, None]) if direction == 0 else (pos[None, :] >= pos[:, None]))
    tri = jnp.asarray(tri, BF16)
    in_specs = [
        pl.BlockSpec((1, MTOK, width), lambda b, i: (b, tile_of(i), 0)),
        pl.BlockSpec((1, MTOK, lr.shape[-1]), lambda b, i: (b, tile_of(i), 0)),
        pl.BlockSpec((1, GLA_RANK, key), lambda b, i: (direction, 0, 0), pipeline_mode=pl.Buffered(1)),
        pl.BlockSpec((1, 1, key), lambda b, i: (direction, 0, 0), pipeline_mode=pl.Buffered(1)),
        _const_spec((TOK, TOK)),
    ]
    args = [qkvg, lr, wa2, ba.reshape(2, 1, key), tri]
    if combine:
        in_specs += [pl.BlockSpec((1, MTOK, val), lambda b, i: (b, tile_of(i), 0)), _const_spec((1, GLA_DV))]
        args += [other, norm_w.reshape(1, GLA_DV)]
    return pl.pallas_call(
        functools.partial(_gla_kernel, direction=direction, combine=combine, ctx_sub=ctx_sub),
        grid=(bsz, n_tiles),
        in_specs=in_specs,
        out_specs=pl.BlockSpec((1, MTOK, val), lambda b, i: (b, tile_of(i), 0)),
        out_shape=jax.ShapeDtypeStruct((bsz, t, val), F32),
        scratch_shapes=[pltpu.VMEM((GLA_HEADS, GLA_DK, GLA_DV), F32)],
        compiler_params=_cparams(("parallel", "arbitrary")),
        name="gla_fwd" if direction == 0 else "gla_bwd",
    )(*args)


def _rope_tables(s_len, c_len, reps):
    quarter = GQA_DH // 4
    inv = ROPE_THETA ** (-np.arange(quarter, dtype=np.float32) / quarter)
    t = np.arange(s_len)
    ang_r = (t // GRID_W).astype(np.float32)[:, None] * inv[None, :]
    ang_c = (t % GRID_W).astype(np.float32)[:, None] * inv[None, :]
    cos = np.concatenate([np.cos(ang_r)] * 2 + [np.cos(ang_c)] * 2, axis=1)
    sin = np.concatenate([-np.sin(ang_r), np.sin(ang_r), -np.sin(ang_c), np.sin(ang_c)], axis=1)
    cos = np.concatenate([cos, np.ones((c_len, GQA_DH), np.float32)], axis=0)
    sin = np.concatenate([sin, np.zeros((c_len, GQA_DH), np.float32)], axis=0)
    return (jnp.asarray(np.tile(cos, (1, reps)), F32), jnp.asarray(np.tile(sin, (1, reps)), F32))


def _rope_swap(x):
    lane = lax.broadcasted_iota(jnp.int32, x.shape, 1)
    return jnp.where(lane % 32 < 16, pltpu.roll(x, 128 - 16, axis=1), pltpu.roll(x, 16, axis=1))


def _norm_rope(x, nw, cos, sin, gmean):
    hi, lo = _split_hi_lo(x * x)
    ms = _dot(hi, gmean) + _dot(lo, gmean)
    y = x * lax.rsqrt(ms + EPS) * nw
    slabs = [_rope_swap(y[:, j:j + 128]) for j in range(0, y.shape[1], 128)]
    swapped = slabs[0] if len(slabs) == 1 else jnp.concatenate(slabs, axis=1)
    return y * cos + swapped * sin


def _group_mean_matrix(n, group):
    idx = np.arange(n) // group
    return jnp.asarray((idx[:, None] == idx[None, :]) / group, BF16)


def _inproj_odd_kernel(h_ref, nw_ref, mod_ref, w_ref, qw_ref, kw_ref, cos_ref, sin_ref, gq_ref, gk_ref,
                       qkvg_ref, lr_ref, q_ref, k_ref, v_ref, *, gla_w):
    qd = GQA_HEADS * GQA_DH
    kd = GQA_KV_HEADS * GQA_DH
    mod = mod_ref[0]
    u = _rms(h_ref[0], nw_ref[...]) * (1.0 + mod[1:2]) + mod[0:1]
    z = _dot(u.astype(BF16), w_ref[...])
    qkvg_ref[0] = z[:, :gla_w]
    lr_ref[0] = z[:, gla_w + qd + 2 * kd:]
    cos_k = cos_ref[...]
    sin_k = sin_ref[...]
    reps = qd // kd
    cos_q = jnp.concatenate([cos_k] * reps, axis=1)
    sin_q = jnp.concatenate([sin_k] * reps, axis=1)
    q = _norm_rope(z[:, gla_w:gla_w + qd], qw_ref[...], cos_q, sin_q, gq_ref[...]) * (LOG2E * GQA_DH ** -0.5)
    k = _norm_rope(z[:, gla_w + qd:gla_w + qd + kd], kw_ref[...], cos_k, sin_k, gk_ref[...])
    v = z[:, gla_w + qd + kd:gla_w + qd + 2 * kd]
    q_ref[0] = q.astype(BF16)
    ones = jnp.ones((z.shape[0], LANES - GQA_DH), BF16)
    for h in range(GQA_KV_HEADS):
        k_ref[0, h] = k[:, h * GQA_DH:(h + 1) * GQA_DH].astype(BF16)
        v_ref[0, h] = jnp.concatenate([v[:, h * GQA_DH:(h + 1) * GQA_DH].astype(BF16), ones], axis=1)


def _inproj_odd(h, norm_w, mods, w, q_norm_w, k_norm_w, s_len, c_len, gla_w):
    bsz, t, d = h.shape
    n = w.shape[1]
    qd = GQA_HEADS * GQA_DH
    kd = GQA_KV_HEADS * GQA_DH
    n_lr = n - gla_w - qd - 2 * kd
    cos, sin = _rope_tables(s_len, c_len, GQA_KV_HEADS)
    tok_spec = lambda width: pl.BlockSpec((1, DTOK, width), lambda b, i: (b, i, 0))
    head_spec = lambda width: pl.BlockSpec((1, GQA_KV_HEADS, DTOK, width), lambda b, i: (b, 0, i, 0))
    return pl.pallas_call(
        functools.partial(_inproj_odd_kernel, gla_w=gla_w),
        grid=(bsz, pl.cdiv(t, DTOK)),
        in_specs=[
            tok_spec(d), _const_spec((1, d)), _mod_spec(d, s_len // DTOK), _const_spec((d, n)),
            _const_spec((1, qd)), _const_spec((1, kd)),
            pl.BlockSpec((DTOK, kd), lambda b, i: (i, 0)),
            pl.BlockSpec((DTOK, kd), lambda b, i: (i, 0)),
            _const_spec((qd, qd)), _const_spec((kd, kd)),
        ],
        out_specs=[tok_spec(gla_w), tok_spec(n_lr), tok_spec(qd), head_spec(GQA_DH), head_spec(LANES)],
        out_shape=[jax.ShapeDtypeStruct((bsz, t, gla_w), F32),
                   jax.ShapeDtypeStruct((bsz, t, n_lr), F32),
                   jax.ShapeDtypeStruct((bsz, t, qd), BF16),
                   jax.ShapeDtypeStruct((bsz, GQA_KV_HEADS, t, GQA_DH), BF16),
                   jax.ShapeDtypeStruct((bsz, GQA_KV_HEADS, t, LANES), BF16)],
        compiler_params=_cparams(("parallel", "parallel")),
        name="inproj_odd",
    )(h, norm_w.reshape(1, d), mods, w,
      jnp.tile(q_norm_w, GQA_HEADS).reshape(1, qd), jnp.tile(k_norm_w, GQA_KV_HEADS).reshape(1, kd),
      cos, sin, _group_mean_matrix(qd, GQA_DH), _group_mean_matrix(kd, GQA_DH))


def _flash_kernel(q_ref, k_ref, v_ref, o_ref, m_s, acc_s, *, group, tk, sub):
    tq = q_ref.shape[1]
    n_kv = k_ref.shape[2] // tk
    units = [(g, r) for g in range(group) for r in range(0, tq, sub)]
    m_s[...] = jnp.full_like(m_s, -jnp.inf)
    acc_s[...] = jnp.zeros_like(acc_s)

    def kv_step(j, carry):
        rows_k = pl.ds(pl.multiple_of(j * tk, tk), tk)
        k = k_ref[0, 0, rows_k, :]
        v = v_ref[0, 0, rows_k, :]

        def scores(u):
            g, r = units[u]
            return _dot_nt(q_ref[0, r:r + sub, g * GQA_DH:(g + 1) * GQA_DH], k)

        s_next = scores(0)
        for u in range(len(units)):
            tiles = _lane_tiles(s_next)
            if u + 1 < len(units):
                s_next = scores(u + 1)
            m_prev = m_s[u]
            m_new = jnp.maximum(m_prev, jnp.max(functools.reduce(jnp.maximum, tiles), axis=-1, keepdims=True))
            p = [jnp.exp2(t - m_new).astype(BF16) for t in tiles]
            acc_s[u] = jnp.exp2(m_prev - m_new) * acc_s[u] + _dot(jnp.concatenate(p, axis=1), v)
            m_s[u] = m_new
        return carry

    lax.fori_loop(0, n_kv, kv_step, 0)
    for u, (g, r) in enumerate(units):
        acc = acc_s[u]
        o_ref[0, r:r + sub, g * GQA_DH:(g + 1) * GQA_DH] = acc[:, :GQA_DH] / acc[:, GQA_DH:GQA_DH + 1]


KV_TILE_MAX = 3072
FLASH_SUB = 512


def _kv_tile(t):
    return max(n for n in range(LANES, KV_TILE_MAX + 1, LANES) if t % n == 0)


def _flash(q, k, v, s_len, tq, tk):
    bsz, _, t, dh = k.shape
    sub = FLASH_SUB
    assert t % tk == 0 and s_len % tq == 0 and tq % sub == 0
    group = GQA_HEADS // GQA_KV_HEADS
    n_units = group * (tq // sub)
    return pl.pallas_call(
        functools.partial(_flash_kernel, group=group, tk=tk, sub=sub),
        grid=(bsz, GQA_KV_HEADS, s_len // tq),
        in_specs=[
            pl.BlockSpec((1, tq, group * dh), lambda b, h, i: (b, i, h)),
            pl.BlockSpec((1, 1, t, dh), lambda b, h, i: (b, h, 0, 0)),
            pl.BlockSpec((1, 1, t, LANES), lambda b, h, i: (b, h, 0, 0)),
        ],
        out_specs=pl.BlockSpec((1, tq, group * dh), lambda b, h, i: (b, i, h)),
        out_shape=jax.ShapeDtypeStruct((bsz, s_len, GQA_HEADS * dh), F32),
        scratch_shapes=[pltpu.VMEM((n_units, sub, LANES), F32), pltpu.VMEM((n_units, sub, LANES), F32)],
        compiler_params=_cparams(("parallel", "parallel", "arbitrary")),
        name="gqa_flash",
    )(q, k, v)


def kernel(x, c, ctx, c_ctx, norm1_w, norm2_w, w_mod, b_mod, w_ff1, w_ff2,
           w_in_even, na_rpb, lru_conv_w, lru_conv_b, lru_wa, lru_ba, lru_wx, lru_bx, lru_lambda, w_out_even,
           w_in_odd, gla_wa2, gla_ba, gla_norm_w, gqa_q_norm_w, gqa_k_norm_w, w_out_odd, final_norm_w):
    bsz, s_len, d = x.shape
    c_len = ctx.shape[1]
    depth = w_mod.shape[0]
    assert s_len % TOK == 0 and c_len % TOK == 0 and depth == 2

    pad = (-(bsz + 1)) % 8
    cvec = jnp.concatenate([c_ctx[None], c, jnp.zeros((pad, d), F32)], axis=0)
    mods = _modulation(cvec, w_mod, b_mod)
    h = (x, ctx)
    t = s_len + c_len

    na_w = NA_HEADS * NA_DH
    ch = lru_lambda.shape[-1]
    col_scale = jnp.where(jnp.arange(w_in_even.shape[-1]) < na_w, LOG2E * NA_DH ** -0.5, 1.0)
    qkv, xg = _inproj(h, norm1_w[0], mods[0], (w_in_even[0] * col_scale).astype(BF16),
                      ((0, 3 * na_w), (3 * na_w, 3 * na_w + 2 * ch)), (BF16, BF16), s_len, t)
    na_out = _na(qkv, na_rpb[0], s_len, c_len)
    w_gate = jnp.stack([jnp.concatenate([_block_diag(lru_wa[0, dr]), _block_diag(lru_wx[0, dr])], axis=1)
                        for dr in range(2)]).astype(BF16)
    b_gate = jnp.concatenate([lru_ba[0], lru_bx[0]], axis=-1).reshape(2, 1, 2 * ch)
    lru_args = (xg, lru_conv_w[0], lru_conv_b[0], w_gate, b_gate, lru_lambda[0], s_len)
    lru_b = _lru(*lru_args, direction=1)
    lru_out = _lru(*lru_args, direction=0, other=lru_b)
    h = _mlp(h, na_out, lru_out, mods[0], w_out_even[0].astype(BF16), norm2_w[0],
             w_ff1[0].astype(BF16), w_ff2[0].astype(BF16), final_norm_w, s_len, s_len + c_len, final=False)

    key = GLA_HEADS * GLA_DK
    val = GLA_HEADS * GLA_DV
    gla_w = 2 * key + 2 * val
    w_in = w_in_odd[0]
    w_in = jnp.concatenate([w_in[:, :gla_w], w_in[:, gla_w + 2 * GLA_RANK:], w_in[:, gla_w:gla_w + 2 * GLA_RANK]],
                           axis=1).astype(BF16)
    qkvg, lr, q, k, v = _inproj_odd(h, norm1_w[1], mods[1], w_in, gqa_q_norm_w[0], gqa_k_norm_w[0],
                                    s_len, c_len, gla_w)
    gla_b = _gla(qkvg, lr, gla_wa2[0], gla_ba[0], s_len, direction=1)
    gla_out = _gla(qkvg, lr, gla_wa2[0], gla_ba[0], s_len, direction=0, other=gla_b, norm_w=gla_norm_w[0])
    gqa_out = _flash(q, k, v, s_len, tq=2 * FLASH_SUB, tk=_kv_tile(t))
    return _mlp(h, gla_out, gqa_out, mods[1], w_out_odd[0].astype(BF16), norm2_w[1],
                w_ff1[1].astype(BF16), w_ff2[1].astype(BF16), final_norm_w, s_len, s_len, final=True)
```

```python
import functools

import numpy as np
import jax
import jax.numpy as jnp
from jax import lax
from jax.experimental import pallas as pl
from jax.experimental.pallas import tpu as pltpu

F32 = jnp.float32
BF16 = jnp.bfloat16

EPS = 1e-6
GRID_W = 64
LANES = 128
N_MOD = 6
TOK = 256
DTOK = 512
MTOK = 512
NEG = -1e30

NA_HEADS = 8
NA_DH = 64
NA_KH = 8
NA_KW = 16
LRU_C = 8.0
LRU_SEG = 8
GLA_HEADS = 4
GLA_DK = 64
GLA_DV = 128
GLA_RANK = 16
GLA_TAU = 16.0
GLA_CHUNK = 64
GQA_HEADS = 8
GQA_KV_HEADS = 2
GQA_DH = 64
ROPE_THETA = 10000.0
LOG2E = 1.4426950408889634

VMEM_LIMIT = 56 * 1024 * 1024


def _cparams(sem):
    return pltpu.CompilerParams(dimension_semantics=sem, vmem_limit_bytes=VMEM_LIMIT)


def _const_spec(shape):
    nd = len(shape)
    return pl.BlockSpec(shape, lambda *_: (0,) * nd, pipeline_mode=pl.Buffered(1))


def _dot(a, b):
    return jnp.dot(a, b, preferred_element_type=F32)


def _dot_nt(a, b):
    return lax.dot_general(a, b, (((1,), (1,)), ((), ())), preferred_element_type=F32)


def _dot_tn(a, b):
    return lax.dot_general(a, b, (((0,), (0,)), ((), ())), preferred_element_type=F32)


def _split_hi_lo(x):
    hi = x.astype(BF16)
    lo = (x - hi.astype(F32)).astype(BF16)
    return hi, lo


def _rms(x, w):
    ms = jnp.mean(x * x, axis=-1, keepdims=True)
    return x * lax.rsqrt(ms + EPS) * w


def _softplus(z):
    return jnp.maximum(z, 0.0) + jnp.log1p(jnp.exp(-jnp.abs(z)))


def _mod_kernel(c_ref, w_ref, b_ref, o_ref):
    c = c_ref[...]
    s = c * jax.nn.sigmoid(c)
    o_ref[0] = jnp.dot(s, w_ref[0], precision=lax.Precision.HIGHEST, preferred_element_type=F32) + b_ref[0]


def _modulation(cvec, w_mod, b_mod):
    depth, d, _ = w_mod.shape
    rows = cvec.shape[0]
    out = pl.pallas_call(
        _mod_kernel,
        grid=(depth, N_MOD),
        in_specs=[
            pl.BlockSpec((rows, d), lambda i, j: (0, 0)),
            pl.BlockSpec((1, d, d), lambda i, j: (i, 0, j)),
            pl.BlockSpec((1, 1, d), lambda i, j: (i, 0, j)),
        ],
        out_specs=pl.BlockSpec((1, rows, d), lambda i, j: (i, 0, j)),
        out_shape=jax.ShapeDtypeStruct((depth, rows, N_MOD * d), F32),
        compiler_params=_cparams(("arbitrary", "arbitrary")),
        name="modulation",
    )(cvec, w_mod, b_mod.reshape(depth, 1, N_MOD * d))
    return out.reshape(depth, rows, N_MOD, d)


def _mod_spec(d, n_lat_tiles):
    return pl.BlockSpec((1, N_MOD, d), lambda b, t: (jnp.where(t >= n_lat_tiles, 0, b + 1), 0, 0))


def _stream_specs(h, s_len):
    n_lat = s_len // DTOK
    if not isinstance(h, tuple):
        return [h], [pl.BlockSpec((1, DTOK, h.shape[-1]), lambda b, i: (b, i, 0))], 0
    lat, ctx = h
    c_len, d = ctx.shape[1:]
    assert c_len <= DTOK and lat.shape[1] == s_len
    return ([lat, ctx],
            [pl.BlockSpec((1, DTOK, d), lambda b, i: (b, jnp.minimum(i, n_lat - 1), 0)),
             pl.BlockSpec((1, c_len, d), lambda b, i: (b, 0, 0))], c_len)


def _per_stream(body, h_refs, n_lat_tiles, ctx_rows):
    if len(h_refs) == 1:
        body(h_refs[0], DTOK)
        return
    i = pl.program_id(1)
    pl.when(i < n_lat_tiles)(lambda: body(h_refs[0], DTOK))
    pl.when(i >= n_lat_tiles)(lambda: body(h_refs[1], ctx_rows))


def _inproj_kernel(*refs, splits, n_src, n_lat_tiles, ctx_rows):
    h_refs = refs[:n_src]
    nw_ref, mod_ref, w_ref = refs[n_src:n_src + 3]
    o_refs = refs[n_src + 3:]

    def body(h_ref, rows):
        mod = mod_ref[0]
        u = _rms(h_ref[0], nw_ref[...]) * (1.0 + mod[1:2]) + mod[0:1]
        z = _dot(u.astype(BF16), w_ref[...])
        for o_ref, (lo, hi) in zip(o_refs, splits):
            o_ref[0, :rows] = z[:, lo:hi].astype(o_ref.dtype)

    _per_stream(body, h_refs, n_lat_tiles, ctx_rows)


def _inproj(h, norm_w, mods, w, splits, dtypes, s_len, t):
    d, n = w.shape
    assert s_len % DTOK == 0
    srcs, src_specs, ctx_rows = _stream_specs(h, s_len)
    bsz = srcs[0].shape[0]
    return pl.pallas_call(
        functools.partial(_inproj_kernel, splits=splits, n_src=len(srcs), n_lat_tiles=s_len // DTOK,
                          ctx_rows=ctx_rows),
        grid=(bsz, pl.cdiv(t, DTOK)),
        in_specs=src_specs + [
            _const_spec((1, d)),
            _mod_spec(d, s_len // DTOK),
            _const_spec((d, n)),
        ],
        out_specs=[pl.BlockSpec((1, DTOK, hi - lo), lambda b, i: (b, i, 0)) for lo, hi in splits],
        out_shape=[jax.ShapeDtypeStruct((bsz, t, hi - lo), dt) for (lo, hi), dt in zip(splits, dtypes)],
        compiler_params=_cparams(("parallel", "parallel")),
        name="inproj",
    )(*srcs, norm_w.reshape(1, d), mods, w)


class _Mix:
    def __init__(self, arrays, specs, width, load):
        self.arrays, self.specs, self.width, self.load = arrays, specs, width, load


def _tok_spec(width, col_block=0):
    return pl.BlockSpec((1, DTOK, width), lambda b, i: (b, i, col_block))


def _mix_plain(a):
    return _Mix([a], [_tok_spec(a.shape[-1])], a.shape[-1], lambda refs, rows: refs[0][0, :rows].astype(BF16))


def _mix_gla(o_fwd, o_bwd, qkvg, norm_w):
    val = GLA_HEADS * GLA_DV

    def load(refs, rows):
        of_ref, ob_ref, gate_ref, nw_ref = refs
        o = of_ref[0, :rows] + ob_ref[0, :rows]
        gate = gate_ref[0, :rows]
        heads = [_rms(o[:, sl], nw_ref[...]) for sl in (slice(h * GLA_DV, (h + 1) * GLA_DV) for h in range(GLA_HEADS))]
        return (jnp.concatenate(heads, axis=1) * (gate * jax.nn.sigmoid(gate))).astype(BF16)

    return _Mix([o_fwd, o_bwd, qkvg, norm_w.reshape(1, GLA_DV)],
                [_tok_spec(val), _tok_spec(val), _tok_spec(val, col_block=2), _const_spec((1, GLA_DV))], val, load)


def _mix_lru(y_fwd, y_bwd, xg):
    ch = y_fwd.shape[-1]
    perm = _lru_row_permutation()

    def load(refs, rows):
        yf_ref, yb_ref, g_ref, perm_ref, unperm_ref = refs
        out = []
        for r in range(0, rows, TOK):
            y = yf_ref[0, r:r + TOK] + yb_ref[0, r:r + TOK]
            y = y * jax.nn.gelu(_dot(perm_ref[...], g_ref[0, r:r + TOK]))
            out.append(_dot(unperm_ref[...], y.astype(BF16)).astype(BF16))
        return out[0] if len(out) == 1 else jnp.concatenate(out, axis=0)

    return _Mix([y_fwd, y_bwd, xg, jnp.asarray(perm, BF16), jnp.asarray(perm.T, BF16)],
                [_tok_spec(ch), _tok_spec(ch), _tok_spec(ch, col_block=1), _const_spec((TOK, TOK)),
                 _const_spec((TOK, TOK))], ch, load)


def _mlp_kernel(*refs, ff_chunk, final, n_src, n_lat_tiles, ctx_rows, mixes):
    h_refs = refs[:n_src]
    pos = n_src
    mix_refs = []
    for n_arrays, _, _ in mixes:
        mix_refs.append(refs[pos:pos + n_arrays])
        pos += n_arrays
    mod_ref, wo_ref, nw_ref, w1_ref, w2_ref, fw_ref, o_ref = refs[pos:]

    def body(h_ref, rows):
        mod = mod_ref[0]
        y = None
        row0 = 0
        for (_, width, load), m_refs in zip(mixes, mix_refs):
            part = _dot(load(m_refs, rows), wo_ref[row0:row0 + width])
            y = part if y is None else y + part
            row0 += width
        h1 = h_ref[0] + mod[2:3] * y
        u = (_rms(h1, nw_ref[...]) * (1.0 + mod[4:5]) + mod[3:4]).astype(BF16)
        d_ff = w1_ref.shape[1]
        acc = jnp.zeros(h1.shape, F32)
        for c in range(d_ff // ff_chunk):
            sl = slice(c * ff_chunk, (c + 1) * ff_chunk)
            a = jnp.maximum(_dot(u, w1_ref[:, sl]), 0.0)
            acc = acc + _dot((a * a).astype(BF16), w2_ref[sl, :])
        h2 = h1 + mod[5:6] * acc
        if final:
            h2 = _rms(h2, fw_ref[...])
        o_ref[0, :rows] = h2

    _per_stream(body, h_refs, n_lat_tiles, ctx_rows)


def _mlp(h, mixes, mods, wo, norm_w, w1, w2, final_w, s_len, out_rows, final):
    d, d_ff = w1.shape
    assert s_len % DTOK == 0 and wo.shape[0] == sum(m.width for m in mixes)
    srcs, src_specs, ctx_rows = _stream_specs(h, s_len)
    bsz = srcs[0].shape[0]
    return pl.pallas_call(
        functools.partial(_mlp_kernel, ff_chunk=512, final=final, n_src=len(srcs), n_lat_tiles=s_len // DTOK,
                          ctx_rows=ctx_rows, mixes=tuple((len(m.arrays), m.width, m.load) for m in mixes)),
        grid=(bsz, pl.cdiv(out_rows, DTOK)),
        in_specs=src_specs + [s for m in mixes for s in m.specs] + [
            _mod_spec(d, s_len // DTOK),
            _const_spec(wo.shape), _const_spec((1, d)),
            _const_spec((d, d_ff)), _const_spec((d_ff, d)), _const_spec((1, d)),
        ],
        out_specs=_tok_spec(d),
        out_shape=jax.ShapeDtypeStruct((bsz, out_rows, d), F32),
        compiler_params=_cparams(("parallel", "parallel")),
        name="mlp",
    )(*srcs, *[a for m in mixes for a in m.arrays], mods, wo, norm_w.reshape(1, d), w1, w2, final_w.reshape(1, d))


NA_QROWS = TOK // GRID_W
NA_WIN = NA_KH + NA_QROWS


def _na_plan(rows):
    n_blk = rows // NA_QROWS
    r = np.arange(rows).reshape(n_blk, NA_QROWS)
    start = np.clip(r - NA_KH // 2, 0, rows - NA_KH)
    u0 = np.clip(r[:, 0] - NA_KH // 2, 0, rows - NA_WIN)
    key = u0[:, None, None] + np.arange(NA_WIN)[None, None, :]
    valid = (key >= start[:, :, None]) & (key < start[:, :, None] + NA_KH)
    d_row = np.where(valid, key - r[:, :, None] + (NA_KH - 1), 0)
    flat = np.concatenate([valid.reshape(n_blk, -1), d_row.reshape(n_blk, -1)], axis=1)
    _, first, pat = np.unique(flat, axis=0, return_index=True, return_inverse=True)
    return u0.astype(np.int32), pat.reshape(-1).astype(np.int32), valid[first], d_row[first]


def _na_bias_table(rpb, valid, d_row):
    n_pat = valid.shape[0]
    heads = rpb.shape[0]
    pairs = NA_WIN // 2
    n_dc = 2 * NA_KW - 1
    cols = np.arange(GRID_W)
    col_start = np.clip(cols - NA_KW // 2, 0, GRID_W - NA_KW)
    col_valid = (cols[None, :] >= col_start[:, None]) & (cols[None, :] < col_start[:, None] + NA_KW)
    d_col = np.clip(cols[None, :] - cols[:, None], 1 - NA_KW, NA_KW - 1) + (NA_KW - 1)
    row_sel = (d_row[..., None] == np.arange(2 * NA_KH - 1)).astype(np.float32)
    col_sel = (d_col[None] == np.arange(n_dc)[:, None, None]).astype(np.float32)
    pair_sel = np.zeros((2, n_dc, GRID_W, 2, GRID_W), np.float32)
    for u1 in range(2):
        pair_sel[u1, :, :, u1, :] = col_sel
    pair_sel = pair_sel.reshape(2 * n_dc, GRID_W, 2 * GRID_W)
    hp = lax.Precision.HIGHEST
    by_row = jnp.einsum('hrc,piur->phiuc', rpb, row_sel, precision=hp)
    by_row = by_row.reshape(n_pat, heads, NA_QROWS, pairs, 2 * n_dc)
    bias = jnp.einsum('phiUk,kqm->phUiqm', by_row, pair_sel, precision=hp)
    keep = (valid.reshape(n_pat, 1, NA_QROWS, 1, pairs, 2, 1).transpose(0, 1, 4, 2, 3, 5, 6)
            & col_valid[None, None, None, None, :, None, :])
    keep = keep.reshape(n_pat, 1, pairs, NA_QROWS, GRID_W, 2 * GRID_W)
    return jnp.where(keep, bias * LOG2E, NEG).reshape(n_pat, heads, pairs, TOK, 2 * GRID_W)


def _softmax_tiles(tiles):
    m = jnp.max(functools.reduce(jnp.maximum, tiles), axis=-1, keepdims=True)
    p = [jnp.exp2(t - m) for t in tiles]
    denom = jnp.sum(functools.reduce(jnp.add, p), axis=-1, keepdims=True)
    return [t.astype(BF16) for t in p], denom


def _lane_tiles(x):
    return [x[:, j:j + LANES] for j in range(0, x.shape[1], LANES)]


def _na_kernel(u0_ref, pat_ref, q_ref, k_ref, v_ref, kc_ref, vc_ref, bias_ref, o_ref, *, n_lat):
    i = pl.program_id(1)
    win = NA_WIN * GRID_W
    n_loc = win // LANES

    def attend(local):
        q = q_ref[0]
        kc = kc_ref[0]
        vc = vc_ref[0]
        if local:
            row0 = pl.multiple_of(u0_ref[i] * GRID_W, GRID_W)
            kw = k_ref[0, pl.ds(row0, win), :]
            vw = v_ref[0, pl.ds(row0, win), :]

        def scores(h):
            sl = slice(h * NA_DH, (h + 1) * NA_DH)
            s_ctx = _lane_tiles(_dot_nt(q[:, sl], kc[:, sl]))
            if not local:
                return s_ctx
            s_loc = _lane_tiles(_dot_nt(q[:, sl], kw[:, sl]))
            return [t + bias_ref[0, h, j] for j, t in enumerate(s_loc)] + s_ctx

        ahead = 1
        pending = [scores(h) for h in range(ahead)]
        for h in range(NA_HEADS):
            sl = slice(h * NA_DH, (h + 1) * NA_DH)
            tiles = pending.pop(0)
            if h + ahead < NA_HEADS:
                pending.append(scores(h + ahead))
            p, denom = _softmax_tiles(tiles)
            v_all = jnp.concatenate([vw[:, sl], vc[:, sl]], axis=0) if local else vc[:, sl]
            o = _dot(jnp.concatenate(p, axis=1), v_all)
            o_ref[0, :, sl] = o / denom

    pl.when(i < n_lat)(functools.partial(attend, True))
    pl.when(i >= n_lat)(functools.partial(attend, False))


def _na(qkv, rpb, s_len, c_len):
    bsz, t, _ = qkv.shape
    width = NA_HEADS * NA_DH
    rows = s_len // GRID_W
    n_lat = s_len // TOK
    n_steps = t // TOK
    assert rows >= NA_WIN and rows % NA_QROWS == 0 and c_len % TOK == 0 and s_len % c_len == 0
    u0, pat, valid, d_row = _na_plan(rows)
    bias = _na_bias_table(rpb, valid, d_row)
    ctx_steps = np.zeros(n_steps - n_lat, np.int32)
    u0 = jnp.asarray(np.concatenate([u0, ctx_steps]))
    pat = jnp.asarray(np.concatenate([pat, ctx_steps]))
    tok_spec = pl.BlockSpec((1, TOK, width), lambda b, i, u0_ref, pat_ref: (b, i, 0))
    ctx_blk = s_len // c_len
    return pl.pallas_call(
        functools.partial(_na_kernel, n_lat=n_lat),
        grid_spec=pltpu.PrefetchScalarGridSpec(
            num_scalar_prefetch=2,
            grid=(bsz, n_steps),
            in_specs=[
                tok_spec,
                pl.BlockSpec((1, s_len, width), lambda b, i, u0_ref, pat_ref: (b, 0, 1),
                             pipeline_mode=pl.Buffered(1)),
                pl.BlockSpec((1, s_len, width), lambda b, i, u0_ref, pat_ref: (b, 0, 2),
                             pipeline_mode=pl.Buffered(1)),
                pl.BlockSpec((1, c_len, width), lambda b, i, u0_ref, pat_ref: (b, ctx_blk, 1)),
                pl.BlockSpec((1, c_len, width), lambda b, i, u0_ref, pat_ref: (b, ctx_blk, 2)),
                pl.BlockSpec((1, NA_HEADS, NA_WIN // 2, TOK, 2 * GRID_W),
                             lambda b, i, u0_ref, pat_ref: (pat_ref[i], 0, 0, 0, 0)),
            ],
            out_specs=tok_spec,
        ),
        out_shape=jax.ShapeDtypeStruct((bsz, t, width), F32),
        compiler_params=_cparams(("parallel", "arbitrary")),
        name="na",
    )(u0, pat, qkv, qkv, qkv, qkv, qkv, bias)


def _lru_tile_index(i, n_lat, n_tiles, direction):
    if direction == 0:
        return (i + n_lat) % n_tiles
    return n_tiles - 1 - i


def _lru_row_permutation():
    seg = TOK // LRU_SEG
    p = np.arange(TOK)
    src = (p % LRU_SEG) * seg + p // LRU_SEG
    return (src[:, None] == np.arange(TOK)[None, :]).astype(np.float32)


def _shift_rows(group, boundary_row, down):
    sub = lax.broadcasted_iota(jnp.int32, group.shape, 0)
    if down:
        return jnp.where(sub == 0, boundary_row, pltpu.roll(group, 1, axis=0))
    return jnp.where(sub == LRU_SEG - 1, boundary_row, pltpu.roll(group, LRU_SEG - 1, axis=0))


def _lru_kernel(xf_ref, prevf_ref, nextf_ref, xb_ref, prevb_ref, nextb_ref, cw_ref, cb_ref, wg_ref, bg_ref, lam_ref,
                perm_ref, of_ref, ob_ref, carry_s, *, n_lat, n_tiles, ctx_sub):
    i = pl.program_id(1)
    ch = xf_ref.shape[-1]
    halo = prevf_ref.shape[1]
    seg = TOK // LRU_SEG
    grp = LRU_SEG

    @pl.when(i == 0)
    def _():
        carry_s[...] = jnp.zeros_like(carry_s)

    perm = perm_ref[...]
    cw = cw_ref[...]

    def sub_tile(direction, r, n_sub, state):
        x_ref, prev_ref, next_ref, o_ref = ((xf_ref, prevf_ref, nextf_ref, of_ref) if direction == 0 else
                                            (xb_ref, prevb_ref, nextb_ref, ob_ref))
        tile = _lru_tile_index(i, n_lat, n_tiles, direction)
        seq_first = jnp.logical_or(tile == 0, tile == n_lat)
        seq_last = jnp.logical_or(tile == n_lat - 1, tile == n_tiles - 1)
        rows = slice(r * TOK, (r + 1) * TOK)
        x = _dot(perm, x_ref[0, rows])
        if r == 0:
            prev = jnp.where(seq_first, 0.0, prev_ref[0].astype(F32))
        else:
            prev = x_ref[0, r * TOK - halo:r * TOK].astype(F32)
        if r == n_sub - 1:
            nxt = jnp.where(seq_last, 0.0, next_ref[0].astype(F32))
        else:
            nxt = x_ref[0, (r + 1) * TOK:(r + 1) * TOK + halo].astype(F32)
        wrap_m1 = _shift_rows(x[(seg - 1) * grp:], prev[halo - 1:halo], down=True)
        wrap_m2 = _shift_rows(x[(seg - 2) * grp:(seg - 1) * grp], prev[halo - 2:halo - 1], down=True)
        wrap_p1 = _shift_rows(x[:grp], nxt[0:1], down=False)
        x_m1 = jnp.concatenate([wrap_m1, x[:-grp]], axis=0)
        x_m2 = jnp.concatenate([wrap_m2, wrap_m1, x[:-2 * grp]], axis=0)
        x_p1 = jnp.concatenate([x[grp:], wrap_p1], axis=0)
        xc = cw[0:1] * x_m2 + cw[1:2] * x_m1 + cw[2:3] * x + cw[3:4] * x_p1 + cb_ref[...]

        pre = _dot(xc.astype(BF16), wg_ref[direction]) + bg_ref[direction]
        r_gate = jax.nn.sigmoid(pre[:, :ch])
        i_gate = jax.nn.sigmoid(pre[:, ch:])
        log_a = (-LRU_C) * r_gate * _softplus(-lam_ref[direction])
        a = jnp.exp(log_a)
        bb = jnp.sqrt(1.0 - a * a) * (i_gate * xc)

        steps = range(seg) if direction == 0 else range(seg - 1, -1, -1)
        h = jnp.zeros((grp, ch), F32)
        cum = jnp.ones((grp, ch), F32)
        h_loc = [None] * seg
        cum_loc = [None] * seg
        for j in steps:
            a_j = a[j * grp:(j + 1) * grp]
            h = a_j * h + bb[j * grp:(j + 1) * grp]
            cum = cum * a_j
            h_loc[j] = h
            cum_loc[j] = cum

        entering = [None] * LRU_SEG
        for s in (range(LRU_SEG) if direction == 0 else range(LRU_SEG - 1, -1, -1)):
            entering[s] = state
            state = h[s:s + 1] + cum[s:s + 1] * state
        entering = jnp.concatenate(entering, axis=0)
        o_ref[0, rows] = jnp.concatenate([h_loc[j] + cum_loc[j] * entering for j in range(seg)], axis=0)
        return state

    def run(n_sub):
        states = [carry_s[0], carry_s[1]]
        for r in range(n_sub):
            states[0] = sub_tile(0, r, n_sub, states[0])
            states[1] = sub_tile(1, n_sub - 1 - r, n_sub, states[1])
        carry_s[0] = states[0]
        carry_s[1] = states[1]

    n_blk = xf_ref.shape[1] // TOK
    if ctx_sub == n_blk:
        run(n_blk)
    else:
        pl.when(i == 0)(lambda: run(ctx_sub))
        pl.when(i > 0)(lambda: run(n_blk))


def _lru(xg, conv_w, conv_b, w_gate, b_gate, lam, s_len):
    bsz, t, two_ch = xg.shape
    ch = two_ch // 2
    n_lat, n_tiles, ctx_sub = _scan_blocks(s_len, t - s_len, MTOK)
    halo = 16
    per_tile = MTOK // halo
    tiles = [functools.partial(_lru_tile_index, n_lat=n_lat, n_tiles=n_tiles, direction=d) for d in range(2)]

    def stream_specs(tile_of):
        return [pl.BlockSpec((1, MTOK, ch), lambda b, i: (b, tile_of(i), 0)),
                pl.BlockSpec((1, halo, ch), lambda b, i: (b, jnp.maximum(tile_of(i) * per_tile - 1, 0), 0)),
                pl.BlockSpec((1, halo, ch),
                             lambda b, i: (b, jnp.minimum((tile_of(i) + 1) * per_tile, t // halo - 1), 0))]

    out_spec = lambda tile_of: pl.BlockSpec((1, MTOK, ch), lambda b, i: (b, tile_of(i), 0))
    return pl.pallas_call(
        functools.partial(_lru_kernel, n_lat=n_lat, n_tiles=n_tiles, ctx_sub=ctx_sub),
        grid=(bsz, n_tiles),
        in_specs=stream_specs(tiles[0]) + stream_specs(tiles[1]) + [
            _const_spec((4, ch)), _const_spec((1, ch)),
            _const_spec((2, ch, 2 * ch)), _const_spec((2, 1, 2 * ch)), _const_spec((2, 1, ch)),
            _const_spec((TOK, TOK)),
        ],
        out_specs=[out_spec(tiles[0]), out_spec(tiles[1])],
        out_shape=[jax.ShapeDtypeStruct((bsz, t, ch), F32)] * 2,
        scratch_shapes=[pltpu.VMEM((2, 1, ch), F32)],
        compiler_params=_cparams(("parallel", "arbitrary")),
        name="lru",
    )(xg, xg, xg, xg, xg, xg, conv_w, conv_b.reshape(1, ch), w_gate, b_gate, lam.reshape(2, 1, ch),
      jnp.asarray(_lru_row_permutation(), BF16))


def _block_diag(w):
    heads, n, _ = w.shape
    eye = jnp.eye(heads, dtype=w.dtype)
    return (eye[:, None, :, None] * w[:, :, None, :]).reshape(heads * n, heads * n)


def _gla_kernel(qf_ref, lrf_ref, qb_ref, lrb_ref, wa_ref, ba_ref, trif_ref, trib_ref, of_ref, ob_ref, state_s,
                *, ctx_sub):
    key = GLA_HEADS * GLA_DK
    val = GLA_HEADS * GLA_DV
    nchunk = TOK // GLA_CHUNK
    step = pl.program_id(1)

    @pl.when(step == 0)
    def _():
        state_s[...] = jnp.zeros_like(state_s)

    chunks = [slice(c * GLA_CHUNK, (c + 1) * GLA_CHUNK) for c in range(nchunk)]
    ones = jnp.ones((GLA_CHUNK, GLA_DV), BF16)

    def sub_tile(direction, rows, states):
        qkvg_ref, lr_ref, tri_ref, o_ref = ((qf_ref, lrf_ref, trif_ref, of_ref) if direction == 0 else
                                            (qb_ref, lrb_ref, trib_ref, ob_ref))
        tri = tri_ref[...]
        causal = tri > 0
        ends = [(c + 1) * GLA_CHUNK - 1 if direction == 0 else c * GLA_CHUNK for c in range(nchunk)]
        chunk_order = range(nchunk) if direction == 0 else range(nchunk - 1, -1, -1)
        z = qkvg_ref[0, rows]
        q = z[:, :key] * (GLA_DK ** -0.5)
        k = z[:, key:2 * key]
        v = z[:, 2 * key:2 * key + val].astype(BF16)
        lr = lr_ref[0, rows][:, direction * GLA_RANK:(direction + 1) * GLA_RANK]
        logit = _dot(lr.astype(BF16), wa_ref[direction].astype(BF16)) + ba_ref[direction]
        log_a = (jnp.minimum(logit, 0.0) - jnp.log1p(jnp.exp(-jnp.abs(logit)))) / GLA_TAU

        g_hi, g_lo = _split_hi_lo(log_a)
        b = _dot(tri, g_hi) + _dot(tri, g_lo)
        q_in = (q * jnp.exp(b)).astype(BF16)
        k_in = (k * jnp.exp(-b)).astype(BF16)
        b_end = jnp.concatenate([jnp.broadcast_to(b[e:e + 1], (GLA_CHUNK, key)) for e in ends], axis=0)
        k_out = (k * jnp.exp(b_end - b)).astype(BF16)
        decay = [jnp.exp(_dot_tn(g_hi[rc], ones) + _dot_tn(g_lo[rc], ones)) for rc in chunks]

        new_states = []
        for h in range(GLA_HEADS):
            ks = slice(h * GLA_DK, (h + 1) * GLA_DK)
            vs = slice(h * GLA_DV, (h + 1) * GLA_DV)
            v_h = v[:, vs]
            att = jnp.where(causal, _dot_nt(q_in[:, ks], k_in[:, ks]), 0.0)
            o = _dot(att.astype(BF16), v_h)
            state = states[h]
            inter = [None] * nchunk
            for c in chunk_order:
                inter[c] = _dot(q_in[chunks[c], ks], state.astype(BF16))
                state = decay[c][ks] * state + _dot_tn(k_out[chunks[c], ks], v_h[chunks[c]])
            new_states.append(state)
            o_ref[0, rows, vs] = o + jnp.concatenate(inter, axis=0)
        return new_states

    def run(n_sub):
        subs = [slice(r * TOK, (r + 1) * TOK) for r in range(n_sub)]
        states = [[state_s[d, h] for h in range(GLA_HEADS)] for d in range(2)]
        for r in range(n_sub):
            states[0] = sub_tile(0, subs[r], states[0])
            states[1] = sub_tile(1, subs[n_sub - 1 - r], states[1])
        for d in range(2):
            for h in range(GLA_HEADS):
                state_s[d, h] = states[d][h]

    n_blk = qf_ref.shape[1] // TOK
    if ctx_sub == n_blk:
        run(n_blk)
    else:
        pl.when(step == 0)(lambda: run(ctx_sub))
        pl.when(step > 0)(lambda: run(n_blk))


def _scan_blocks(s_len, c_len, blk):
    assert s_len % blk == 0 and c_len % TOK == 0 and c_len <= blk
    return s_len // blk, s_len // blk + 1, c_len // TOK


def _gla(qkvg, lr, wa2, ba, s_len):
    bsz, t, width = qkvg.shape
    key = GLA_HEADS * GLA_DK
    val = GLA_HEADS * GLA_DV
    n_lat, n_tiles, ctx_sub = _scan_blocks(s_len, t - s_len, MTOK)
    tiles = [functools.partial(_lru_tile_index, n_lat=n_lat, n_tiles=n_tiles, direction=d) for d in range(2)]
    pos = np.arange(TOK)
    same = (pos[:, None] // GLA_CHUNK) == (pos[None, :] // GLA_CHUNK)
    tri = [jnp.asarray(same & m, BF16) for m in (pos[None, :] <= pos[:, None], pos[None, :] >= pos[:, None])]
    blk = lambda width, tile_of: pl.BlockSpec((1, MTOK, width), lambda b, i: (b, tile_of(i), 0))
    return pl.pallas_call(
        functools.partial(_gla_kernel, ctx_sub=ctx_sub),
        grid=(bsz, n_tiles),
        in_specs=[
            blk(width, tiles[0]), blk(lr.shape[-1], tiles[0]), blk(width, tiles[1]), blk(lr.shape[-1], tiles[1]),
            _const_spec((2, GLA_RANK, key)), _const_spec((2, 1, key)),
            _const_spec((TOK, TOK)), _const_spec((TOK, TOK)),
        ],
        out_specs=[blk(val, tiles[0]), blk(val, tiles[1])],
        out_shape=[jax.ShapeDtypeStruct((bsz, t, val), F32)] * 2,
        scratch_shapes=[pltpu.VMEM((2, GLA_HEADS, GLA_DK, GLA_DV), F32)],
        compiler_params=_cparams(("parallel", "arbitrary")),
        name="gla",
    )(qkvg, lr, qkvg, lr, wa2, ba.reshape(2, 1, key), tri[0], tri[1])


def _rope_tables(s_len, c_len, reps):
    quarter = GQA_DH // 4
    inv = ROPE_THETA ** (-np.arange(quarter, dtype=np.float32) / quarter)
    t = np.arange(s_len)
    ang_r = (t // GRID_W).astype(np.float32)[:, None] * inv[None, :]
    ang_c = (t % GRID_W).astype(np.float32)[:, None] * inv[None, :]
    cos = np.concatenate([np.cos(ang_r)] * 2 + [np.cos(ang_c)] * 2, axis=1)
    sin = np.concatenate([-np.sin(ang_r), np.sin(ang_r), -np.sin(ang_c), np.sin(ang_c)], axis=1)
    cos = np.concatenate([cos, np.ones((c_len, GQA_DH), np.float32)], axis=0)
    sin = np.concatenate([sin, np.zeros((c_len, GQA_DH), np.float32)], axis=0)
    return (jnp.asarray(np.tile(cos, (1, reps)), F32), jnp.asarray(np.tile(sin, (1, reps)), F32))


def _rope_swap(x):
    lane = lax.broadcasted_iota(jnp.int32, x.shape, 1)
    return jnp.where(lane % 32 < 16, pltpu.roll(x, 128 - 16, axis=1), pltpu.roll(x, 16, axis=1))


def _norm_rope(x, nw, cos, sin, gmean):
    hi, lo = _split_hi_lo(x * x)
    ms = _dot(hi, gmean) + _dot(lo, gmean)
    y = x * lax.rsqrt(ms + EPS) * nw
    slabs = [_rope_swap(y[:, j:j + 128]) for j in range(0, y.shape[1], 128)]
    swapped = slabs[0] if len(slabs) == 1 else jnp.concatenate(slabs, axis=1)
    return y * cos + swapped * sin


def _group_mean_matrix(n, group):
    idx = np.arange(n) // group
    return jnp.asarray((idx[:, None] == idx[None, :]) / group, BF16)


def _inproj_odd_kernel(h_ref, nw_ref, mod_ref, w_ref, qw_ref, kw_ref, cos_ref, sin_ref, gq_ref, gk_ref,
                       qkvg_ref, lr_ref, q_ref, k_ref, v_ref, *, gla_w):
    qd = GQA_HEADS * GQA_DH
    kd = GQA_KV_HEADS * GQA_DH
    mod = mod_ref[0]
    u = _rms(h_ref[0], nw_ref[...]) * (1.0 + mod[1:2]) + mod[0:1]
    z = _dot(u.astype(BF16), w_ref[...])
    qkvg_ref[0] = z[:, :gla_w]
    lr_ref[0] = z[:, gla_w + qd + 2 * kd:]
    cos_k = cos_ref[...]
    sin_k = sin_ref[...]
    reps = qd // kd
    cos_q = jnp.concatenate([cos_k] * reps, axis=1)
    sin_q = jnp.concatenate([sin_k] * reps, axis=1)
    q = _norm_rope(z[:, gla_w:gla_w + qd], qw_ref[...], cos_q, sin_q, gq_ref[...]) * (LOG2E * GQA_DH ** -0.5)
    k = _norm_rope(z[:, gla_w + qd:gla_w + qd + kd], kw_ref[...], cos_k, sin_k, gk_ref[...])
    v = z[:, gla_w + qd + kd:gla_w + qd + 2 * kd]
    q_ref[0] = q.astype(BF16)
    ones = jnp.ones((z.shape[0], LANES - GQA_DH), BF16)
    for h in range(GQA_KV_HEADS):
        k_ref[0, h] = k[:, h * GQA_DH:(h + 1) * GQA_DH].astype(BF16)
        v_ref[0, h] = jnp.concatenate([v[:, h * GQA_DH:(h + 1) * GQA_DH].astype(BF16), ones], axis=1)


def _inproj_odd(h, norm_w, mods, w, q_norm_w, k_norm_w, s_len, c_len, gla_w):
    bsz, t, d = h.shape
    n = w.shape[1]
    qd = GQA_HEADS * GQA_DH
    kd = GQA_KV_HEADS * GQA_DH
    n_lr = n - gla_w - qd - 2 * kd
    cos, sin = _rope_tables(s_len, c_len, GQA_KV_HEADS)
    tok_spec = lambda width: pl.BlockSpec((1, DTOK, width), lambda b, i: (b, i, 0))
    head_spec = lambda width: pl.BlockSpec((1, GQA_KV_HEADS, DTOK, width), lambda b, i: (b, 0, i, 0))
    return pl.pallas_call(
        functools.partial(_inproj_odd_kernel, gla_w=gla_w),
        grid=(bsz, pl.cdiv(t, DTOK)),
        in_specs=[
            tok_spec(d), _const_spec((1, d)), _mod_spec(d, s_len // DTOK), _const_spec((d, n)),
            _const_spec((1, qd)), _const_spec((1, kd)),
            pl.BlockSpec((DTOK, kd), lambda b, i: (i, 0)),
            pl.BlockSpec((DTOK, kd), lambda b, i: (i, 0)),
            _const_spec((qd, qd)), _const_spec((kd, kd)),
        ],
        out_specs=[tok_spec(gla_w), tok_spec(n_lr), tok_spec(qd), head_spec(GQA_DH), head_spec(LANES)],
        out_shape=[jax.ShapeDtypeStruct((bsz, t, gla_w), F32),
                   jax.ShapeDtypeStruct((bsz, t, n_lr), F32),
                   jax.ShapeDtypeStruct((bsz, t, qd), BF16),
                   jax.ShapeDtypeStruct((bsz, GQA_KV_HEADS, t, GQA_DH), BF16),
                   jax.ShapeDtypeStruct((bsz, GQA_KV_HEADS, t, LANES), BF16)],
        compiler_params=_cparams(("parallel", "parallel")),
        name="inproj_odd",
    )(h, norm_w.reshape(1, d), mods, w,
      jnp.tile(q_norm_w, GQA_HEADS).reshape(1, qd), jnp.tile(k_norm_w, GQA_KV_HEADS).reshape(1, kd),
      cos, sin, _group_mean_matrix(qd, GQA_DH), _group_mean_matrix(kd, GQA_DH))


def _flash_kernel(q_ref, k_ref, v_ref, o_ref, m_s, acc_s, *, group, tk, sub):
    tq = q_ref.shape[1]
    n_kv = k_ref.shape[2] // tk
    units = [(g, r) for g in range(group) for r in range(0, tq, sub)]
    m_s[...] = jnp.full_like(m_s, -jnp.inf)
    acc_s[...] = jnp.zeros_like(acc_s)

    def kv_step(j, carry):
        rows_k = pl.ds(pl.multiple_of(j * tk, tk), tk)
        k = k_ref[0, 0, rows_k, :]
        v = v_ref[0, 0, rows_k, :]

        def scores(u):
            g, r = units[u]
            return _dot_nt(q_ref[0, r:r + sub, g * GQA_DH:(g + 1) * GQA_DH], k)

        s_next = scores(0)
        for u in range(len(units)):
            tiles = _lane_tiles(s_next)
            if u + 1 < len(units):
                s_next = scores(u + 1)
            m_prev = m_s[u]
            m_new = jnp.maximum(m_prev, jnp.max(functools.reduce(jnp.maximum, tiles), axis=-1, keepdims=True))
            p = [jnp.exp2(t - m_new).astype(BF16) for t in tiles]
            acc_s[u] = jnp.exp2(m_prev - m_new) * acc_s[u] + _dot(jnp.concatenate(p, axis=1), v)
            m_s[u] = m_new
        return carry

    lax.fori_loop(0, n_kv, kv_step, 0)
    for u, (g, r) in enumerate(units):
        acc = acc_s[u]
        o_ref[0, r:r + sub, g * GQA_DH:(g + 1) * GQA_DH] = acc[:, :GQA_DH] / acc[:, GQA_DH:GQA_DH + 1]


KV_TILE_MAX = 3072
FLASH_SUB = 512


def _kv_tile(t):
    return max(n for n in range(LANES, KV_TILE_MAX + 1, LANES) if t % n == 0)


def _flash(q, k, v, s_len, tq, tk):
    bsz, _, t, dh = k.shape
    sub = FLASH_SUB
    assert t % tk == 0 and s_len % tq == 0 and tq % sub == 0
    group = GQA_HEADS // GQA_KV_HEADS
    n_units = group * (tq // sub)
    return pl.pallas_call(
        functools.partial(_flash_kernel, group=group, tk=tk, sub=sub),
        grid=(bsz, GQA_KV_HEADS, s_len // tq),
        in_specs=[
            pl.BlockSpec((1, tq, group * dh), lambda b, h, i: (b, i, h)),
            pl.BlockSpec((1, 1, t, dh), lambda b, h, i: (b, h, 0, 0)),
            pl.BlockSpec((1, 1, t, LANES), lambda b, h, i: (b, h, 0, 0)),
        ],
        out_specs=pl.BlockSpec((1, tq, group * dh), lambda b, h, i: (b, i, h)),
        out_shape=jax.ShapeDtypeStruct((bsz, s_len, GQA_HEADS * dh), F32),
        scratch_shapes=[pltpu.VMEM((n_units, sub, LANES), F32), pltpu.VMEM((n_units, sub, LANES), F32)],
        compiler_params=_cparams(("parallel", "parallel", "arbitrary")),
        name="gqa_flash",
    )(q, k, v)


def kernel(x, c, ctx, c_ctx, norm1_w, norm2_w, w_mod, b_mod, w_ff1, w_ff2,
           w_in_even, na_rpb, lru_conv_w, lru_conv_b, lru_wa, lru_ba, lru_wx, lru_bx, lru_lambda, w_out_even,
           w_in_odd, gla_wa2, gla_ba, gla_norm_w, gqa_q_norm_w, gqa_k_norm_w, w_out_odd, final_norm_w):
    bsz, s_len, d = x.shape
    c_len = ctx.shape[1]
    depth = w_mod.shape[0]
    assert s_len % TOK == 0 and c_len % TOK == 0 and depth == 2

    pad = (-(bsz + 1)) % 8
    cvec = jnp.concatenate([c_ctx[None], c, jnp.zeros((pad, d), F32)], axis=0)
    mods = _modulation(cvec, w_mod, b_mod)
    h = (x, ctx)
    t = s_len + c_len

    na_w = NA_HEADS * NA_DH
    ch = lru_lambda.shape[-1]
    col_scale = jnp.where(jnp.arange(w_in_even.shape[-1]) < na_w, LOG2E * NA_DH ** -0.5, 1.0)
    qkv, xg = _inproj(h, norm1_w[0], mods[0], (w_in_even[0] * col_scale).astype(BF16),
                      ((0, 3 * na_w), (3 * na_w, 3 * na_w + 2 * ch)), (BF16, BF16), s_len, t)
    na_out = _na(qkv, na_rpb[0], s_len, c_len)
    w_gate = jnp.stack([jnp.concatenate([_block_diag(lru_wa[0, dr]), _block_diag(lru_wx[0, dr])], axis=1)
                        for dr in range(2)]).astype(BF16)
    b_gate = jnp.concatenate([lru_ba[0], lru_bx[0]], axis=-1).reshape(2, 1, 2 * ch)
    lru_f, lru_b = _lru(xg, lru_conv_w[0], lru_conv_b[0], w_gate, b_gate, lru_lambda[0], s_len)
    mixes = (_mix_plain(na_out), _mix_lru(lru_f, lru_b, xg))
    h = _mlp(h, mixes, mods[0], w_out_even[0].astype(BF16), norm2_w[0],
             w_ff1[0].astype(BF16), w_ff2[0].astype(BF16), final_norm_w, s_len, s_len + c_len, final=False)

    key = GLA_HEADS * GLA_DK
    val = GLA_HEADS * GLA_DV
    gla_w = 2 * key + 2 * val
    w_in = w_in_odd[0]
    w_in = jnp.concatenate([w_in[:, :gla_w], w_in[:, gla_w + 2 * GLA_RANK:], w_in[:, gla_w:gla_w + 2 * GLA_RANK]],
                           axis=1).astype(BF16)
    qkvg, lr, q, k, v = _inproj_odd(h, norm1_w[1], mods[1], w_in, gqa_q_norm_w[0], gqa_k_norm_w[0],
                                    s_len, c_len, gla_w)
    gla_f, gla_b = _gla(qkvg, lr, gla_wa2[0], gla_ba[0], s_len)
    gqa_out = _flash(q, k, v, s_len, tq=2 * FLASH_SUB, tk=_kv_tile(t))
    mixes = (_mix_gla(gla_f, gla_b, qkvg, gla_norm_w[0]), _mix_plain(gqa_out))
    return _mlp(h, mixes, mods[1], w_out_odd[0].astype(BF16), norm2_w[1],
                w_ff1[1].astype(BF16), w_ff2[1].astype(BF16), final_norm_w, s_len, s_len, final=True)
```

```python
import functools

import numpy as np
import jax
import jax.numpy as jnp
from jax import lax
from jax.experimental import pallas as pl
from jax.experimental.pallas import tpu as pltpu

F32 = jnp.float32
BF16 = jnp.bfloat16

EPS = 1e-6
GRID_W = 64
LANES = 128
N_MOD = 6
TOK = 256
DTOK = 512
MTOK = 512
NEG = -1e30

NA_HEADS = 8
NA_DH = 64
NA_KH = 8
NA_KW = 16
LRU_C = 8.0
LRU_SEG = 8
GLA_HEADS = 4
GLA_DK = 64
GLA_DV = 128
GLA_RANK = 16
GLA_TAU = 16.0
GLA_CHUNK = 64
GQA_HEADS = 8
GQA_KV_HEADS = 2
GQA_DH = 64
ROPE_THETA = 10000.0
LOG2E = 1.4426950408889634

VMEM_LIMIT = 56 * 1024 * 1024


def _cparams(sem):
    return pltpu.CompilerParams(dimension_semantics=sem, vmem_limit_bytes=VMEM_LIMIT)


def _const_spec(shape):
    nd = len(shape)
    return pl.BlockSpec(shape, lambda *_: (0,) * nd, pipeline_mode=pl.Buffered(1))


def _dot(a, b):
    return jnp.dot(a, b, preferred_element_type=F32)


def _dot_nt(a, b):
    return lax.dot_general(a, b, (((1,), (1,)), ((), ())), preferred_element_type=F32)


def _dot_tn(a, b):
    return lax.dot_general(a, b, (((0,), (0,)), ((), ())), preferred_element_type=F32)


def _split_hi_lo(x):
    hi = x.astype(BF16)
    lo = (x - hi.astype(F32)).astype(BF16)
    return hi, lo


def _rms(x, w):
    ms = jnp.mean(x * x, axis=-1, keepdims=True)
    return x * lax.rsqrt(ms + EPS) * w


def _softplus(z):
    return jnp.maximum(z, 0.0) + jnp.log1p(jnp.exp(-jnp.abs(z)))


def _mod_kernel(c_ref, w_ref, b_ref, o_ref):
    c = c_ref[...]
    s = c * jax.nn.sigmoid(c)
    o_ref[0] = jnp.dot(s, w_ref[0], precision=lax.Precision.HIGHEST, preferred_element_type=F32) + b_ref[0]


def _modulation(cvec, w_mod, b_mod):
    depth, d, _ = w_mod.shape
    rows = cvec.shape[0]
    out = pl.pallas_call(
        _mod_kernel,
        grid=(depth, N_MOD),
        in_specs=[
            pl.BlockSpec((rows, d), lambda i, j: (0, 0)),
            pl.BlockSpec((1, d, d), lambda i, j: (i, 0, j)),
            pl.BlockSpec((1, 1, d), lambda i, j: (i, 0, j)),
        ],
        out_specs=pl.BlockSpec((1, rows, d), lambda i, j: (i, 0, j)),
        out_shape=jax.ShapeDtypeStruct((depth, rows, N_MOD * d), F32),
        compiler_params=_cparams(("arbitrary", "arbitrary")),
        name="modulation",
    )(cvec, w_mod, b_mod.reshape(depth, 1, N_MOD * d))
    return out.reshape(depth, rows, N_MOD, d)


def _mod_spec(d, n_lat_tiles):
    return pl.BlockSpec((1, N_MOD, d), lambda b, t: (jnp.where(t >= n_lat_tiles, 0, b + 1), 0, 0))


def _stream_specs(h, s_len):
    n_lat = s_len // DTOK
    if not isinstance(h, tuple):
        return [h], [pl.BlockSpec((1, DTOK, h.shape[-1]), lambda b, i: (b, i, 0))], 0
    lat, ctx = h
    c_len, d = ctx.shape[1:]
    assert c_len <= DTOK and lat.shape[1] == s_len
    return ([lat, ctx],
            [pl.BlockSpec((1, DTOK, d), lambda b, i: (b, jnp.minimum(i, n_lat - 1), 0)),
             pl.BlockSpec((1, c_len, d), lambda b, i: (b, 0, 0))], c_len)


def _per_stream(body, h_refs, n_lat_tiles, ctx_rows):
    if len(h_refs) == 1:
        body(h_refs[0], DTOK)
        return
    i = pl.program_id(1)
    pl.when(i < n_lat_tiles)(lambda: body(h_refs[0], DTOK))
    pl.when(i >= n_lat_tiles)(lambda: body(h_refs[1], ctx_rows))


def _inproj_kernel(*refs, splits, n_src, n_lat_tiles, ctx_rows):
    h_refs = refs[:n_src]
    nw_ref, mod_ref, w_ref = refs[n_src:n_src + 3]
    o_refs = refs[n_src + 3:]

    def body(h_ref, rows):
        mod = mod_ref[0]
        u = _rms(h_ref[0], nw_ref[...]) * (1.0 + mod[1:2]) + mod[0:1]
        z = _dot(u.astype(BF16), w_ref[...])
        for o_ref, (lo, hi) in zip(o_refs, splits):
            o_ref[0, :rows] = z[:, lo:hi].astype(o_ref.dtype)

    _per_stream(body, h_refs, n_lat_tiles, ctx_rows)


def _inproj(h, norm_w, mods, w, splits, dtypes, s_len, t):
    d, n = w.shape
    assert s_len % DTOK == 0
    srcs, src_specs, ctx_rows = _stream_specs(h, s_len)
    bsz = srcs[0].shape[0]
    return pl.pallas_call(
        functools.partial(_inproj_kernel, splits=splits, n_src=len(srcs), n_lat_tiles=s_len // DTOK,
                          ctx_rows=ctx_rows),
        grid=(bsz, pl.cdiv(t, DTOK)),
        in_specs=src_specs + [
            _const_spec((1, d)),
            _mod_spec(d, s_len // DTOK),
            _const_spec((d, n)),
        ],
        out_specs=[pl.BlockSpec((1, DTOK, hi - lo), lambda b, i: (b, i, 0)) for lo, hi in splits],
        out_shape=[jax.ShapeDtypeStruct((bsz, t, hi - lo), dt) for (lo, hi), dt in zip(splits, dtypes)],
        compiler_params=_cparams(("parallel", "parallel")),
        name="inproj",
    )(*srcs, norm_w.reshape(1, d), mods, w)


class _Mix:
    def __init__(self, arrays, specs, width, load):
        self.arrays, self.specs, self.width, self.load = arrays, specs, width, load


def _tok_spec(width, col_block=0):
    return pl.BlockSpec((1, DTOK, width), lambda b, i: (b, i, col_block))


def _mix_plain(a):
    return _Mix([a], [_tok_spec(a.shape[-1])], a.shape[-1], lambda refs, rows: refs[0][0, :rows].astype(BF16))


def _mix_gla(o_fwd, o_bwd, qkvg, norm_w):
    val = GLA_HEADS * GLA_DV

    def load(refs, rows):
        of_ref, ob_ref, gate_ref, nw_ref = refs
        o = of_ref[0, :rows] + ob_ref[0, :rows]
        gate = gate_ref[0, :rows]
        heads = [_rms(o[:, sl], nw_ref[...]) for sl in (slice(h * GLA_DV, (h + 1) * GLA_DV) for h in range(GLA_HEADS))]
        return (jnp.concatenate(heads, axis=1) * (gate * jax.nn.sigmoid(gate))).astype(BF16)

    return _Mix([o_fwd, o_bwd, qkvg, norm_w.reshape(1, GLA_DV)],
                [_tok_spec(val), _tok_spec(val), _tok_spec(val, col_block=2), _const_spec((1, GLA_DV))], val, load)


def _mix_lru(y_fwd, y_bwd, xg):
    ch = y_fwd.shape[-1]
    perm = _lru_row_permutation()

    def load(refs, rows):
        yf_ref, yb_ref, g_ref, perm_ref, unperm_ref = refs
        out = []
        for r in range(0, rows, TOK):
            y = yf_ref[0, r:r + TOK] + yb_ref[0, r:r + TOK]
            y = y * jax.nn.gelu(_dot(perm_ref[...], g_ref[0, r:r + TOK]))
            out.append(_dot(unperm_ref[...], y.astype(BF16)).astype(BF16))
        return out[0] if len(out) == 1 else jnp.concatenate(out, axis=0)

    return _Mix([y_fwd, y_bwd, xg, jnp.asarray(perm, BF16), jnp.asarray(perm.T, BF16)],
                [_tok_spec(ch), _tok_spec(ch), _tok_spec(ch, col_block=1), _const_spec((TOK, TOK)),
                 _const_spec((TOK, TOK))], ch, load)


def _mlp_kernel(*refs, ff_chunk, final, n_src, n_lat_tiles, ctx_rows, mixes):
    h_refs = refs[:n_src]
    pos = n_src
    mix_refs = []
    for n_arrays, _, _ in mixes:
        mix_refs.append(refs[pos:pos + n_arrays])
        pos += n_arrays
    mod_ref, wo_ref, nw_ref, w1_ref, w2_ref, fw_ref, o_ref = refs[pos:]

    def body(h_ref, rows):
        mod = mod_ref[0]
        y = None
        row0 = 0
        for (_, width, load), m_refs in zip(mixes, mix_refs):
            part = _dot(load(m_refs, rows), wo_ref[row0:row0 + width])
            y = part if y is None else y + part
            row0 += width
        h1 = h_ref[0] + mod[2:3] * y
        u = (_rms(h1, nw_ref[...]) * (1.0 + mod[4:5]) + mod[3:4]).astype(BF16)
        d_ff = w1_ref.shape[1]
        acc = jnp.zeros(h1.shape, F32)
        for c in range(d_ff // ff_chunk):
            sl = slice(c * ff_chunk, (c + 1) * ff_chunk)
            a = jnp.maximum(_dot(u, w1_ref[:, sl]), 0.0)
            acc = acc + _dot((a * a).astype(BF16), w2_ref[sl, :])
        h2 = h1 + mod[5:6] * acc
        if final:
            h2 = _rms(h2, fw_ref[...])
        o_ref[0, :rows] = h2

    _per_stream(body, h_refs, n_lat_tiles, ctx_rows)


def _mlp(h, mixes, mods, wo, norm_w, w1, w2, final_w, s_len, out_rows, final):
    d, d_ff = w1.shape
    assert s_len % DTOK == 0 and wo.shape[0] == sum(m.width for m in mixes)
    srcs, src_specs, ctx_rows = _stream_specs(h, s_len)
    bsz = srcs[0].shape[0]
    return pl.pallas_call(
        functools.partial(_mlp_kernel, ff_chunk=512, final=final, n_src=len(srcs), n_lat_tiles=s_len // DTOK,
                          ctx_rows=ctx_rows, mixes=tuple((len(m.arrays), m.width, m.load) for m in mixes)),
        grid=(bsz, pl.cdiv(out_rows, DTOK)),
        in_specs=src_specs + [s for m in mixes for s in m.specs] + [
            _mod_spec(d, s_len // DTOK),
            _const_spec(wo.shape), _const_spec((1, d)),
            _const_spec((d, d_ff)), _const_spec((d_ff, d)), _const_spec((1, d)),
        ],
        out_specs=_tok_spec(d),
        out_shape=jax.ShapeDtypeStruct((bsz, out_rows, d), F32),
        compiler_params=_cparams(("parallel", "parallel")),
        name="mlp",
    )(*srcs, *[a for m in mixes for a in m.arrays], mods, wo, norm_w.reshape(1, d), w1, w2, final_w.reshape(1, d))


NA_QROWS = TOK // GRID_W
NA_WIN = NA_KH + NA_QROWS


def _na_plan(rows):
    n_blk = rows // NA_QROWS
    r = np.arange(rows).reshape(n_blk, NA_QROWS)
    start = np.clip(r - NA_KH // 2, 0, rows - NA_KH)
    u0 = np.clip(r[:, 0] - NA_KH // 2, 0, rows - NA_WIN)
    key = u0[:, None, None] + np.arange(NA_WIN)[None, None, :]
    valid = (key >= start[:, :, None]) & (key < start[:, :, None] + NA_KH)
    d_row = np.where(valid, key - r[:, :, None] + (NA_KH - 1), 0)
    flat = np.concatenate([valid.reshape(n_blk, -1), d_row.reshape(n_blk, -1)], axis=1)
    _, first, pat = np.unique(flat, axis=0, return_index=True, return_inverse=True)
    return u0.astype(np.int32), pat.reshape(-1).astype(np.int32), valid[first], d_row[first]


def _na_bias_table(rpb, valid, d_row):
    n_pat = valid.shape[0]
    heads = rpb.shape[0]
    pairs = NA_WIN // 2
    n_dc = 2 * NA_KW - 1
    cols = np.arange(GRID_W)
    col_start = np.clip(cols - NA_KW // 2, 0, GRID_W - NA_KW)
    col_valid = (cols[None, :] >= col_start[:, None]) & (cols[None, :] < col_start[:, None] + NA_KW)
    d_col = np.clip(cols[None, :] - cols[:, None], 1 - NA_KW, NA_KW - 1) + (NA_KW - 1)
    row_sel = (d_row[..., None] == np.arange(2 * NA_KH - 1)).astype(np.float32)
    col_sel = (d_col[None] == np.arange(n_dc)[:, None, None]).astype(np.float32)
    n_k = n_dc + 2
    pair_sel = np.zeros((2, n_k, GRID_W, 2, GRID_W), np.float32)
    for u1 in range(2):
        pair_sel[u1, :n_dc, :, u1, :] = col_sel
        pair_sel[u1, n_dc, :, u1, :] = ~col_valid
        pair_sel[u1, n_dc + 1, :, u1, :] = 1.0
    pair_sel = pair_sel.reshape(2 * n_k, GRID_W, 2 * GRID_W)
    hp = lax.Precision.HIGHEST
    by_row = jnp.einsum('hrc,piur->phiuc', rpb, row_sel, precision=hp) * LOG2E
    col_mask = jnp.full((n_pat, heads, NA_QROWS, NA_WIN, 1), NEG, F32)
    row_mask = jnp.broadcast_to(jnp.where(valid, 0.0, NEG)[:, None, :, :, None], col_mask.shape)
    by_row = jnp.concatenate([by_row, col_mask, row_mask], axis=-1)
    by_row = by_row.reshape(n_pat, heads, NA_QROWS, pairs, 2 * n_k).transpose(0, 1, 3, 2, 4)
    bias = lax.dot_general(by_row, pair_sel, (((4,), (0,)), ((), ())), precision=hp)
    return bias.reshape(n_pat, heads, pairs, TOK, 2 * GRID_W)


def _softmax_tiles(tiles):
    m = jnp.max(functools.reduce(jnp.maximum, tiles), axis=-1, keepdims=True)
    p = [jnp.exp2(t - m) for t in tiles]
    denom = jnp.sum(functools.reduce(jnp.add, p), axis=-1, keepdims=True)
    return [t.astype(BF16) for t in p], denom


def _lane_tiles(x):
    return [x[:, j:j + LANES] for j in range(0, x.shape[1], LANES)]


def _na_kernel(u0_ref, pat_ref, q_ref, k_ref, v_ref, kc_ref, vc_ref, bias_ref, o_ref, *, n_lat):
    i = pl.program_id(1)
    win = NA_WIN * GRID_W
    n_loc = win // LANES

    def attend(local):
        q = q_ref[0]
        kc = kc_ref[0]
        vc = vc_ref[0]
        if local:
            row0 = pl.multiple_of(u0_ref[i] * GRID_W, GRID_W)
            kw = k_ref[0, pl.ds(row0, win), :]
            vw = v_ref[0, pl.ds(row0, win), :]

        def scores(h):
            sl = slice(h * NA_DH, (h + 1) * NA_DH)
            s_ctx = _lane_tiles(_dot_nt(q[:, sl], kc[:, sl]))
            if not local:
                return s_ctx
            s_loc = _lane_tiles(_dot_nt(q[:, sl], kw[:, sl]))
            return [t + bias_ref[0, h, j] for j, t in enumerate(s_loc)] + s_ctx

        ahead = 1
        pending = [scores(h) for h in range(ahead)]
        for h in range(NA_HEADS):
            sl = slice(h * NA_DH, (h + 1) * NA_DH)
            tiles = pending.pop(0)
            if h + ahead < NA_HEADS:
                pending.append(scores(h + ahead))
            p, denom = _softmax_tiles(tiles)
            v_all = jnp.concatenate([vw[:, sl], vc[:, sl]], axis=0) if local else vc[:, sl]
            o = _dot(jnp.concatenate(p, axis=1), v_all)
            o_ref[0, :, sl] = o / denom

    pl.when(i < n_lat)(functools.partial(attend, True))
    pl.when(i >= n_lat)(functools.partial(attend, False))


def _na(qkv, rpb, s_len, c_len):
    bsz, t, _ = qkv.shape
    width = NA_HEADS * NA_DH
    rows = s_len // GRID_W
    n_lat = s_len // TOK
    n_steps = t // TOK
    assert rows >= NA_WIN and rows % NA_QROWS == 0 and c_len % TOK == 0 and s_len % c_len == 0
    u0, pat, valid, d_row = _na_plan(rows)
    bias = _na_bias_table(rpb, valid, d_row)
    ctx_steps = np.zeros(n_steps - n_lat, np.int32)
    u0 = jnp.asarray(np.concatenate([u0, ctx_steps]))
    pat = jnp.asarray(np.concatenate([pat, ctx_steps]))
    tok_spec = pl.BlockSpec((1, TOK, width), lambda b, i, u0_ref, pat_ref: (b, i, 0))
    ctx_blk = s_len // c_len
    return pl.pallas_call(
        functools.partial(_na_kernel, n_lat=n_lat),
        grid_spec=pltpu.PrefetchScalarGridSpec(
            num_scalar_prefetch=2,
            grid=(bsz, n_steps),
            in_specs=[
                tok_spec,
                pl.BlockSpec((1, s_len, width), lambda b, i, u0_ref, pat_ref: (b, 0, 1),
                             pipeline_mode=pl.Buffered(1)),
                pl.BlockSpec((1, s_len, width), lambda b, i, u0_ref, pat_ref: (b, 0, 2),
                             pipeline_mode=pl.Buffered(1)),
                pl.BlockSpec((1, c_len, width), lambda b, i, u0_ref, pat_ref: (b, ctx_blk, 1)),
                pl.BlockSpec((1, c_len, width), lambda b, i, u0_ref, pat_ref: (b, ctx_blk, 2)),
                pl.BlockSpec((1, NA_HEADS, NA_WIN // 2, TOK, 2 * GRID_W),
                             lambda b, i, u0_ref, pat_ref: (pat_ref[i], 0, 0, 0, 0)),
            ],
            out_specs=tok_spec,
        ),
        out_shape=jax.ShapeDtypeStruct((bsz, t, width), F32),
        compiler_params=_cparams(("parallel", "arbitrary")),
        name="na",
    )(u0, pat, qkv, qkv, qkv, qkv, qkv, bias)


def _lru_tile_index(i, n_lat, n_tiles, direction):
    if direction == 0:
        return (i + n_lat) % n_tiles
    return n_tiles - 1 - i


def _lru_row_permutation():
    seg = TOK // LRU_SEG
    p = np.arange(TOK)
    src = (p % LRU_SEG) * seg + p // LRU_SEG
    return (src[:, None] == np.arange(TOK)[None, :]).astype(np.float32)


def _shift_rows(group, boundary_row, down):
    sub = lax.broadcasted_iota(jnp.int32, group.shape, 0)
    if down:
        return jnp.where(sub == 0, boundary_row, pltpu.roll(group, 1, axis=0))
    return jnp.where(sub == LRU_SEG - 1, boundary_row, pltpu.roll(group, LRU_SEG - 1, axis=0))


def _lru_kernel(xf_ref, prevf_ref, nextf_ref, xb_ref, prevb_ref, nextb_ref, cw_ref, cb_ref, wg_ref, bg_ref, lam_ref,
                perm_ref, of_ref, ob_ref, carry_s, *, n_lat, n_tiles, ctx_sub):
    i = pl.program_id(1)
    ch = xf_ref.shape[-1]
    halo = prevf_ref.shape[1]
    seg = TOK // LRU_SEG
    grp = LRU_SEG

    @pl.when(i == 0)
    def _():
        carry_s[...] = jnp.zeros_like(carry_s)

    perm = perm_ref[...]
    cw = cw_ref[...]

    def sub_tile(direction, r, n_sub, state):
        x_ref, prev_ref, next_ref, o_ref = ((xf_ref, prevf_ref, nextf_ref, of_ref) if direction == 0 else
                                            (xb_ref, prevb_ref, nextb_ref, ob_ref))
        tile = _lru_tile_index(i, n_lat, n_tiles, direction)
        seq_first = jnp.logical_or(tile == 0, tile == n_lat)
        seq_last = jnp.logical_or(tile == n_lat - 1, tile == n_tiles - 1)
        rows = slice(r * TOK, (r + 1) * TOK)
        x = _dot(perm, x_ref[0, rows])
        if r == 0:
            prev = jnp.where(seq_first, 0.0, prev_ref[0].astype(F32))
        else:
            prev = x_ref[0, r * TOK - halo:r * TOK].astype(F32)
        if r == n_sub - 1:
            nxt = jnp.where(seq_last, 0.0, next_ref[0].astype(F32))
        else:
            nxt = x_ref[0, (r + 1) * TOK:(r + 1) * TOK + halo].astype(F32)
        wrap_m1 = _shift_rows(x[(seg - 1) * grp:], prev[halo - 1:halo], down=True)
        wrap_m2 = _shift_rows(x[(seg - 2) * grp:(seg - 1) * grp], prev[halo - 2:halo - 1], down=True)
        wrap_p1 = _shift_rows(x[:grp], nxt[0:1], down=False)
        x_m1 = jnp.concatenate([wrap_m1, x[:-grp]], axis=0)
        x_m2 = jnp.concatenate([wrap_m2, wrap_m1, x[:-2 * grp]], axis=0)
        x_p1 = jnp.concatenate([x[grp:], wrap_p1], axis=0)
        xc = cw[0:1] * x_m2 + cw[1:2] * x_m1 + cw[2:3] * x + cw[3:4] * x_p1 + cb_ref[...]

        pre = _dot(xc.astype(BF16), wg_ref[direction]) + bg_ref[direction]
        gates = 0.5 * jnp.tanh(0.5 * pre) + 0.5
        r_gate = gates[:, :ch]
        i_gate = gates[:, ch:]
        log_a = (-LRU_C) * r_gate * _softplus(-lam_ref[direction])
        a = jnp.exp(log_a)
        bb = jnp.sqrt(1.0 - a * a) * (i_gate * xc)

        steps = range(seg) if direction == 0 else range(seg - 1, -1, -1)
        h = jnp.zeros((grp, ch), F32)
        cum = jnp.ones((grp, ch), F32)
        h_loc = [None] * seg
        cum_loc = [None] * seg
        for j in steps:
            a_j = a[j * grp:(j + 1) * grp]
            h = a_j * h + bb[j * grp:(j + 1) * grp]
            cum = cum * a_j
            h_loc[j] = h
            cum_loc[j] = cum

        entering = [None] * LRU_SEG
        for s in (range(LRU_SEG) if direction == 0 else range(LRU_SEG - 1, -1, -1)):
            entering[s] = state
            state = h[s:s + 1] + cum[s:s + 1] * state
        entering = jnp.concatenate(entering, axis=0)
        o_ref[0, rows] = jnp.concatenate([h_loc[j] + cum_loc[j] * entering for j in range(seg)], axis=0)
        return state

    def run(n_sub):
        states = [carry_s[0], carry_s[1]]
        for r in range(n_sub):
            states[0] = sub_tile(0, r, n_sub, states[0])
            states[1] = sub_tile(1, n_sub - 1 - r, n_sub, states[1])
        carry_s[0] = states[0]
        carry_s[1] = states[1]

    n_blk = xf_ref.shape[1] // TOK
    if ctx_sub == n_blk:
        run(n_blk)
    else:
        pl.when(i == 0)(lambda: run(ctx_sub))
        pl.when(i > 0)(lambda: run(n_blk))


def _lru(xg, conv_w, conv_b, w_gate, b_gate, lam, s_len):
    bsz, t, two_ch = xg.shape
    ch = two_ch // 2
    n_lat, n_tiles, ctx_sub = _scan_blocks(s_len, t - s_len, MTOK)
    halo = 16
    per_tile = MTOK // halo
    tiles = [functools.partial(_lru_tile_index, n_lat=n_lat, n_tiles=n_tiles, direction=d) for d in range(2)]

    def stream_specs(tile_of):
        return [pl.BlockSpec((1, MTOK, ch), lambda b, i: (b, tile_of(i), 0)),
                pl.BlockSpec((1, halo, ch), lambda b, i: (b, jnp.maximum(tile_of(i) * per_tile - 1, 0), 0)),
                pl.BlockSpec((1, halo, ch),
                             lambda b, i: (b, jnp.minimum((tile_of(i) + 1) * per_tile, t // halo - 1), 0))]

    out_spec = lambda tile_of: pl.BlockSpec((1, MTOK, ch), lambda b, i: (b, tile_of(i), 0))
    return pl.pallas_call(
        functools.partial(_lru_kernel, n_lat=n_lat, n_tiles=n_tiles, ctx_sub=ctx_sub),
        grid=(bsz, n_tiles),
        in_specs=stream_specs(tiles[0]) + stream_specs(tiles[1]) + [
            _const_spec((4, ch)), _const_spec((1, ch)),
            _const_spec((2, ch, 2 * ch)), _const_spec((2, 1, 2 * ch)), _const_spec((2, 1, ch)),
            _const_spec((TOK, TOK)),
        ],
        out_specs=[out_spec(tiles[0]), out_spec(tiles[1])],
        out_shape=[jax.ShapeDtypeStruct((bsz, t, ch), F32)] * 2,
        scratch_shapes=[pltpu.VMEM((2, 1, ch), F32)],
        compiler_params=_cparams(("parallel", "arbitrary")),
        name="lru",
    )(xg, xg, xg, xg, xg, xg, conv_w, conv_b.reshape(1, ch), w_gate, b_gate, lam.reshape(2, 1, ch),
      jnp.asarray(_lru_row_permutation(), BF16))


def _block_diag(w):
    heads, n, _ = w.shape
    eye = jnp.eye(heads, dtype=w.dtype)
    return (eye[:, None, :, None] * w[:, :, None, :]).reshape(heads * n, heads * n)


def _gla_kernel(qf_ref, lrf_ref, qb_ref, lrb_ref, wa_ref, ba_ref, trif_ref, trib_ref, csum_ref, of_ref, ob_ref,
                state_s, *, ctx_sub):
    key = GLA_HEADS * GLA_DK
    val = GLA_HEADS * GLA_DV
    nchunk = TOK // GLA_CHUNK
    step = pl.program_id(1)

    @pl.when(step == 0)
    def _():
        state_s[...] = jnp.zeros_like(state_s)

    chunks = [slice(c * GLA_CHUNK, (c + 1) * GLA_CHUNK) for c in range(nchunk)]

    def prepare(direction, rows):
        qkvg_ref, lr_ref, tri_ref = (qf_ref, lrf_ref, trif_ref) if direction == 0 else (qb_ref, lrb_ref, trib_ref)
        tri = tri_ref[...]
        ends = [(c + 1) * GLA_CHUNK - 1 if direction == 0 else c * GLA_CHUNK for c in range(nchunk)]
        z = qkvg_ref[0, rows]
        q = z[:, :key] * (GLA_DK ** -0.5)
        k = z[:, key:2 * key]
        v = z[:, 2 * key:2 * key + val].astype(BF16)
        lr = lr_ref[0, rows][:, direction * GLA_RANK:(direction + 1) * GLA_RANK]
        logit = _dot(lr.astype(BF16), wa_ref[direction].astype(BF16)) + ba_ref[direction]
        log_a = (jnp.minimum(logit, 0.0) - jnp.log1p(jnp.exp(-jnp.abs(logit)))) / GLA_TAU

        g_hi, g_lo = _split_hi_lo(log_a)
        b = _dot(tri, g_hi) + _dot(tri, g_lo)
        q_in = (q * jnp.exp(b)).astype(BF16)
        k_in = (k * jnp.exp(-b)).astype(BF16)
        b_end = jnp.concatenate([jnp.broadcast_to(b[e:e + 1], (GLA_CHUNK, key)) for e in ends], axis=0)
        k_out_t = (k * jnp.exp(b_end - b)).T.astype(BF16)
        gt_hi, gt_lo = _split_hi_lo(log_a.T)
        decay = jnp.exp(_dot(gt_hi, csum_ref[...]) + _dot(gt_lo, csum_ref[...]))
        return tri > 0, q_in, k_in, k_out_t, v, decay

    def attend(direction, rows, prepared, states):
        o_ref = of_ref if direction == 0 else ob_ref
        causal, q_in, k_in, k_out_t, v, decay = prepared
        chunk_order = range(nchunk) if direction == 0 else range(nchunk - 1, -1, -1)
        new_states = []
        for h in range(GLA_HEADS):
            ks = slice(h * GLA_DK, (h + 1) * GLA_DK)
            vs = slice(h * GLA_DV, (h + 1) * GLA_DV)
            v_h = v[:, vs]
            att = jnp.where(causal, _dot_nt(q_in[:, ks], k_in[:, ks]), 0.0)
            o = _dot(att.astype(BF16), v_h)
            state = states[h]
            inter = [None] * nchunk
            for c in chunk_order:
                inter[c] = _dot(q_in[chunks[c], ks], state.astype(BF16))
                state = (decay[ks, c * GLA_DV:(c + 1) * GLA_DV] * state
                         + _dot(k_out_t[ks, chunks[c]], v_h[chunks[c]]))
            new_states.append(state)
            o_ref[0, rows, vs] = o + jnp.concatenate(inter, axis=0)
        return new_states

    def run(n_sub):
        subs = [slice(r * TOK, (r + 1) * TOK) for r in range(n_sub)]
        states = [[state_s[d, h] for h in range(GLA_HEADS)] for d in range(2)]
        work = [(d, subs[r] if d == 0 else subs[n_sub - 1 - r]) for r in range(n_sub) for d in range(2)]
        prepared = [prepare(d, rows) for d, rows in work]
        for (d, rows), prep in zip(work, prepared):
            states[d] = attend(d, rows, prep, states[d])
        for d in range(2):
            for h in range(GLA_HEADS):
                state_s[d, h] = states[d][h]

    n_blk = qf_ref.shape[1] // TOK
    if ctx_sub == n_blk:
        run(n_blk)
    else:
        pl.when(step == 0)(lambda: run(ctx_sub))
        pl.when(step > 0)(lambda: run(n_blk))


def _scan_blocks(s_len, c_len, blk):
    assert s_len % blk == 0 and c_len % TOK == 0 and c_len <= blk
    return s_len // blk, s_len // blk + 1, c_len // TOK


def _gla(qkvg, lr, wa2, ba, s_len):
    bsz, t, width = qkvg.shape
    key = GLA_HEADS * GLA_DK
    val = GLA_HEADS * GLA_DV
    n_lat, n_tiles, ctx_sub = _scan_blocks(s_len, t - s_len, MTOK)
    tiles = [functools.partial(_lru_tile_index, n_lat=n_lat, n_tiles=n_tiles, direction=d) for d in range(2)]
    pos = np.arange(TOK)
    same = (pos[:, None] // GLA_CHUNK) == (pos[None, :] // GLA_CHUNK)
    tri = [jnp.asarray(same & m, BF16) for m in (pos[None, :] <= pos[:, None], pos[None, :] >= pos[:, None])]
    chunk_sum = jnp.asarray((pos[:, None] // GLA_CHUNK) == (np.arange(TOK // GLA_CHUNK * GLA_DV)[None, :] // GLA_DV), BF16)
    blk = lambda width, tile_of: pl.BlockSpec((1, MTOK, width), lambda b, i: (b, tile_of(i), 0))
    return pl.pallas_call(
        functools.partial(_gla_kernel, ctx_sub=ctx_sub),
        grid=(bsz, n_tiles),
        in_specs=[
            blk(width, tiles[0]), blk(lr.shape[-1], tiles[0]), blk(width, tiles[1]), blk(lr.shape[-1], tiles[1]),
            _const_spec((2, GLA_RANK, key)), _const_spec((2, 1, key)),
            _const_spec((TOK, TOK)), _const_spec((TOK, TOK)), _const_spec(chunk_sum.shape),
        ],
        out_specs=[blk(val, tiles[0]), blk(val, tiles[1])],
        out_shape=[jax.ShapeDtypeStruct((bsz, t, val), F32)] * 2,
        scratch_shapes=[pltpu.VMEM((2, GLA_HEADS, GLA_DK, GLA_DV), F32)],
        compiler_params=_cparams(("parallel", "arbitrary")),
        name="gla",
    )(qkvg, lr, qkvg, lr, wa2, ba.reshape(2, 1, key), tri[0], tri[1], chunk_sum)


def _rope_tables(s_len, c_len, reps):
    quarter = GQA_DH // 4
    inv = ROPE_THETA ** (-np.arange(quarter, dtype=np.float32) / quarter)
    t = np.arange(s_len)
    ang_r = (t // GRID_W).astype(np.float32)[:, None] * inv[None, :]
    ang_c = (t % GRID_W).astype(np.float32)[:, None] * inv[None, :]
    cos = np.concatenate([np.cos(ang_r)] * 2 + [np.cos(ang_c)] * 2, axis=1)
    sin = np.concatenate([-np.sin(ang_r), np.sin(ang_r), -np.sin(ang_c), np.sin(ang_c)], axis=1)
    cos = np.concatenate([cos, np.ones((c_len, GQA_DH), np.float32)], axis=0)
    sin = np.concatenate([sin, np.zeros((c_len, GQA_DH), np.float32)], axis=0)
    return (jnp.asarray(np.tile(cos, (1, reps)), F32), jnp.asarray(np.tile(sin, (1, reps)), F32))


def _rope_swap(x):
    lane = lax.broadcasted_iota(jnp.int32, x.shape, 1)
    return jnp.where(lane % 32 < 16, pltpu.roll(x, 128 - 16, axis=1), pltpu.roll(x, 16, axis=1))


def _norm_rope(x, nw, cos, sin, gmean):
    hi, lo = _split_hi_lo(x * x)
    ms = _dot(hi, gmean) + _dot(lo, gmean)
    y = x * lax.rsqrt(ms + EPS) * nw
    slabs = [_rope_swap(y[:, j:j + 128]) for j in range(0, y.shape[1], 128)]
    swapped = slabs[0] if len(slabs) == 1 else jnp.concatenate(slabs, axis=1)
    return y * cos + swapped * sin


def _group_mean_matrix(n, group):
    idx = np.arange(n) // group
    return jnp.asarray((idx[:, None] == idx[None, :]) / group, BF16)


def _inproj_odd_kernel(h_ref, nw_ref, mod_ref, w_ref, qw_ref, kw_ref, cos_ref, sin_ref, gq_ref, gk_ref,
                       qkvg_ref, lr_ref, q_ref, k_ref, v_ref, *, gla_w):
    qd = GQA_HEADS * GQA_DH
    kd = GQA_KV_HEADS * GQA_DH
    mod = mod_ref[0]
    u = _rms(h_ref[0], nw_ref[...]) * (1.0 + mod[1:2]) + mod[0:1]
    z = _dot(u.astype(BF16), w_ref[...])
    qkvg_ref[0] = z[:, :gla_w]
    lr_ref[0] = z[:, gla_w + qd + 2 * kd:]
    cos_k = cos_ref[...]
    sin_k = sin_ref[...]
    reps = qd // kd
    cos_q = jnp.concatenate([cos_k] * reps, axis=1)
    sin_q = jnp.concatenate([sin_k] * reps, axis=1)
    q = _norm_rope(z[:, gla_w:gla_w + qd], qw_ref[...], cos_q, sin_q, gq_ref[...]) * (LOG2E * GQA_DH ** -0.5)
    k = _norm_rope(z[:, gla_w + qd:gla_w + qd + kd], kw_ref[...], cos_k, sin_k, gk_ref[...])
    v = z[:, gla_w + qd + kd:gla_w + qd + 2 * kd]
    q_ref[0] = q.astype(BF16)
    ones = jnp.ones((z.shape[0], LANES - GQA_DH), BF16)
    for h in range(GQA_KV_HEADS):
        k_ref[0, h] = k[:, h * GQA_DH:(h + 1) * GQA_DH].astype(BF16)
        v_ref[0, h] = jnp.concatenate([v[:, h * GQA_DH:(h + 1) * GQA_DH].astype(BF16), ones], axis=1)


def _inproj_odd(h, norm_w, mods, w, q_norm_w, k_norm_w, s_len, c_len, gla_w):
    bsz, t, d = h.shape
    n = w.shape[1]
    qd = GQA_HEADS * GQA_DH
    kd = GQA_KV_HEADS * GQA_DH
    n_lr = n - gla_w - qd - 2 * kd
    cos, sin = _rope_tables(s_len, c_len, GQA_KV_HEADS)
    tok_spec = lambda width: pl.BlockSpec((1, DTOK, width), lambda b, i: (b, i, 0))
    head_spec = lambda width: pl.BlockSpec((1, GQA_KV_HEADS, DTOK, width), lambda b, i: (b, 0, i, 0))
    return pl.pallas_call(
        functools.partial(_inproj_odd_kernel, gla_w=gla_w),
        grid=(bsz, pl.cdiv(t, DTOK)),
        in_specs=[
            tok_spec(d), _const_spec((1, d)), _mod_spec(d, s_len // DTOK), _const_spec((d, n)),
            _const_spec((1, qd)), _const_spec((1, kd)),
            pl.BlockSpec((DTOK, kd), lambda b, i: (i, 0)),
            pl.BlockSpec((DTOK, kd), lambda b, i: (i, 0)),
            _const_spec((qd, qd)), _const_spec((kd, kd)),
        ],
        out_specs=[tok_spec(gla_w), tok_spec(n_lr), tok_spec(qd), head_spec(GQA_DH), head_spec(LANES)],
        out_shape=[jax.ShapeDtypeStruct((bsz, t, gla_w), F32),
                   jax.ShapeDtypeStruct((bsz, t, n_lr), F32),
                   jax.ShapeDtypeStruct((bsz, t, qd), BF16),
                   jax.ShapeDtypeStruct((bsz, GQA_KV_HEADS, t, GQA_DH), BF16),
                   jax.ShapeDtypeStruct((bsz, GQA_KV_HEADS, t, LANES), BF16)],
        compiler_params=_cparams(("parallel", "parallel")),
        name="inproj_odd",
    )(h, norm_w.reshape(1, d), mods, w,
      jnp.tile(q_norm_w, GQA_HEADS).reshape(1, qd), jnp.tile(k_norm_w, GQA_KV_HEADS).reshape(1, kd),
      cos, sin, _group_mean_matrix(qd, GQA_DH), _group_mean_matrix(kd, GQA_DH))


def _flash_kernel(q_ref, k_ref, v_ref, o_ref, m_s, acc_s, *, group, tk, sub):
    tq = q_ref.shape[1]
    n_kv = k_ref.shape[2] // tk
    units = [(g, r) for g in range(group) for r in range(0, tq, sub)]
    m_s[...] = jnp.full_like(m_s, -jnp.inf)
    acc_s[...] = jnp.zeros_like(acc_s)

    def kv_step(j, carry):
        rows_k = pl.ds(pl.multiple_of(j * tk, tk), tk)
        k = k_ref[0, 0, rows_k, :]
        v = v_ref[0, 0, rows_k, :]

        def scores(u):
            g, r = units[u]
            return _dot_nt(q_ref[0, r:r + sub, g * GQA_DH:(g + 1) * GQA_DH], k)

        s_next = scores(0)
        for u in range(len(units)):
            tiles = _lane_tiles(s_next)
            if u + 1 < len(units):
                s_next = scores(u + 1)
            m_prev = m_s[u]
            m_new = jnp.maximum(m_prev, jnp.max(functools.reduce(jnp.maximum, tiles), axis=-1, keepdims=True))
            p = [jnp.exp2(t - m_new).astype(BF16) for t in tiles]
            acc_s[u] = jnp.exp2(m_prev - m_new) * acc_s[u] + _dot(jnp.concatenate(p, axis=1), v)
            m_s[u] = m_new
        return carry

    lax.fori_loop(0, n_kv, kv_step, 0)
    for u, (g, r) in enumerate(units):
        acc = acc_s[u]
        o_ref[0, r:r + sub, g * GQA_DH:(g + 1) * GQA_DH] = acc[:, :GQA_DH] / acc[:, GQA_DH:GQA_DH + 1]


KV_TILE_MAX = 3072
FLASH_SUB = 512


def _kv_tile(t):
    return max(n for n in range(LANES, KV_TILE_MAX + 1, LANES) if t % n == 0)


def _flash(q, k, v, s_len, tq, tk):
    bsz, _, t, dh = k.shape
    sub = FLASH_SUB
    assert t % tk == 0 and s_len % tq == 0 and tq % sub == 0
    group = GQA_HEADS // GQA_KV_HEADS
    n_units = group * (tq // sub)
    return pl.pallas_call(
        functools.partial(_flash_kernel, group=group, tk=tk, sub=sub),
        grid=(bsz, GQA_KV_HEADS, s_len // tq),
        in_specs=[
            pl.BlockSpec((1, tq, group * dh), lambda b, h, i: (b, i, h)),
            pl.BlockSpec((1, 1, t, dh), lambda b, h, i: (b, h, 0, 0)),
            pl.BlockSpec((1, 1, t, LANES), lambda b, h, i: (b, h, 0, 0)),
        ],
        out_specs=pl.BlockSpec((1, tq, group * dh), lambda b, h, i: (b, i, h)),
        out_shape=jax.ShapeDtypeStruct((bsz, s_len, GQA_HEADS * dh), F32),
        scratch_shapes=[pltpu.VMEM((n_units, sub, LANES), F32), pltpu.VMEM((n_units, sub, LANES), F32)],
        compiler_params=_cparams(("parallel", "parallel", "arbitrary")),
        name="gqa_flash",
    )(q, k, v)


def kernel(x, c, ctx, c_ctx, norm1_w, norm2_w, w_mod, b_mod, w_ff1, w_ff2,
           w_in_even, na_rpb, lru_conv_w, lru_conv_b, lru_wa, lru_ba, lru_wx, lru_bx, lru_lambda, w_out_even,
           w_in_odd, gla_wa2, gla_ba, gla_norm_w, gqa_q_norm_w, gqa_k_norm_w, w_out_odd, final_norm_w):
    bsz, s_len, d = x.shape
    c_len = ctx.shape[1]
    depth = w_mod.shape[0]
    assert s_len % TOK == 0 and c_len % TOK == 0 and depth == 2

    pad = (-(bsz + 1)) % 8
    cvec = jnp.concatenate([c_ctx[None], c, jnp.zeros((pad, d), F32)], axis=0)
    mods = _modulation(cvec, w_mod, b_mod)
    h = (x, ctx)
    t = s_len + c_len

    na_w = NA_HEADS * NA_DH
    ch = lru_lambda.shape[-1]
    col_scale = jnp.where(jnp.arange(w_in_even.shape[-1]) < na_w, LOG2E * NA_DH ** -0.5, 1.0)
    qkv, xg = _inproj(h, norm1_w[0], mods[0], (w_in_even[0] * col_scale).astype(BF16),
                      ((0, 3 * na_w), (3 * na_w, 3 * na_w + 2 * ch)), (BF16, BF16), s_len, t)
    na_out = _na(qkv, na_rpb[0], s_len, c_len)
    w_gate = jnp.stack([jnp.concatenate([_block_diag(lru_wa[0, dr]), _block_diag(lru_wx[0, dr])], axis=1)
                        for dr in range(2)]).astype(BF16)
    b_gate = jnp.concatenate([lru_ba[0], lru_bx[0]], axis=-1).reshape(2, 1, 2 * ch)
    lru_f, lru_b = _lru(xg, lru_conv_w[0], lru_conv_b[0], w_gate, b_gate, lru_lambda[0], s_len)
    mixes = (_mix_plain(na_out), _mix_lru(lru_f, lru_b, xg))
    h = _mlp(h, mixes, mods[0], w_out_even[0].astype(BF16), norm2_w[0],
             w_ff1[0].astype(BF16), w_ff2[0].astype(BF16), final_norm_w, s_len, s_len + c_len, final=False)

    key = GLA_HEADS * GLA_DK
    val = GLA_HEADS * GLA_DV
    gla_w = 2 * key + 2 * val
    w_in = w_in_odd[0]
    w_in = jnp.concatenate([w_in[:, :gla_w], w_in[:, gla_w + 2 * GLA_RANK:], w_in[:, gla_w:gla_w + 2 * GLA_RANK]],
                           axis=1).astype(BF16)
    qkvg, lr, q, k, v = _inproj_odd(h, norm1_w[1], mods[1], w_in, gqa_q_norm_w[0], gqa_k_norm_w[0],
                                    s_len, c_len, gla_w)
    gla_f, gla_b = _gla(qkvg, lr, gla_wa2[0], gla_ba[0], s_len)
    gqa_out = _flash(q, k, v, s_len, tq=2 * FLASH_SUB, tk=_kv_tile(t))
    mixes = (_mix_gla(gla_f, gla_b, qkvg, gla_norm_w[0]), _mix_plain(gqa_out))
    return _mlp(h, mixes, mods[1], w_out_odd[0].astype(BF16), norm2_w[1],
                w_ff1[1].astype(BF16), w_ff2[1].astype(BF16), final_norm_w, s_len, s_len, final=True)
```

```python
import functools

import numpy as np
import jax
import jax.numpy as jnp
from jax import lax
from jax.experimental import pallas as pl
from jax.experimental.pallas import tpu as pltpu

F32 = jnp.float32
BF16 = jnp.bfloat16

EPS = 1e-6
GRID_W = 64
LANES = 128
N_MOD = 6
TOK = 256
DTOK = 512
MTOK = 512
NEG = -1e30

NA_HEADS = 8
NA_DH = 64
NA_KH = 8
NA_KW = 16
LRU_C = 8.0
LRU_SEG = 8
GLA_HEADS = 4
GLA_DK = 64
GLA_DV = 128
GLA_RANK = 16
GLA_TAU = 16.0
GLA_CHUNK = 64
GQA_HEADS = 8
GQA_KV_HEADS = 2
GQA_DH = 64
ROPE_THETA = 10000.0
LOG2E = 1.4426950408889634

VMEM_LIMIT = 56 * 1024 * 1024


def _cparams(sem):
    return pltpu.CompilerParams(dimension_semantics=sem, vmem_limit_bytes=VMEM_LIMIT)


def _const_spec(shape):
    nd = len(shape)
    return pl.BlockSpec(shape, lambda *_: (0,) * nd, pipeline_mode=pl.Buffered(1))


def _layer_spec(shape, layer):
    nd = len(shape)
    return pl.BlockSpec((None,) + tuple(shape), lambda *_: (layer,) + (0,) * nd, pipeline_mode=pl.Buffered(1))


def _cast_kernel(x_ref, o_ref):
    o_ref[...] = x_ref[...].astype(o_ref.dtype)


def _to_bf16(w, row_blocks=2):
    layers, rows, cols = w.shape
    blk = rows // row_blocks
    spec = pl.BlockSpec((1, blk, cols), lambda l, i: (l, i, 0))
    return pl.pallas_call(
        _cast_kernel,
        grid=(layers, row_blocks),
        in_specs=[spec],
        out_specs=spec,
        out_shape=jax.ShapeDtypeStruct(w.shape, BF16),
        compiler_params=_cparams(("parallel", "parallel")),
        name="cast_bf16",
    )(w)


def _dot(a, b):
    return jnp.dot(a, b, preferred_element_type=F32)


def _dot_nt(a, b):
    return lax.dot_general(a, b, (((1,), (1,)), ((), ())), preferred_element_type=F32)


def _dot_tn(a, b):
    return lax.dot_general(a, b, (((0,), (0,)), ((), ())), preferred_element_type=F32)


def _split_hi_lo(x):
    hi = x.astype(BF16)
    lo = (x - hi.astype(F32)).astype(BF16)
    return hi, lo


def _rms(x, w):
    ms = jnp.mean(x * x, axis=-1, keepdims=True)
    return x * lax.rsqrt(ms + EPS) * w


def _softplus(z):
    return jnp.maximum(z, 0.0) + jnp.log1p(jnp.exp(-jnp.abs(z)))


def _mod_kernel(c_ref, w_ref, b_ref, o_ref):
    c = c_ref[...]
    s = c * jax.nn.sigmoid(c)
    o_ref[0] = jnp.dot(s, w_ref[0], precision=lax.Precision.HIGHEST, preferred_element_type=F32) + b_ref[0]


def _modulation(cvec, w_mod, b_mod):
    depth, d, _ = w_mod.shape
    rows = cvec.shape[0]
    out = pl.pallas_call(
        _mod_kernel,
        grid=(depth, N_MOD),
        in_specs=[
            pl.BlockSpec((rows, d), lambda i, j: (0, 0)),
            pl.BlockSpec((1, d, d), lambda i, j: (i, 0, j)),
            pl.BlockSpec((1, 1, d), lambda i, j: (i, 0, j)),
        ],
        out_specs=pl.BlockSpec((1, rows, d), lambda i, j: (i, 0, j)),
        out_shape=jax.ShapeDtypeStruct((depth, rows, N_MOD * d), F32),
        compiler_params=_cparams(("arbitrary", "arbitrary")),
        name="modulation",
    )(cvec, w_mod, b_mod.reshape(depth, 1, N_MOD * d))
    return out.reshape(depth, rows, N_MOD, d)


def _mod_spec(d, n_lat_tiles):
    return pl.BlockSpec((1, N_MOD, d), lambda b, t: (jnp.where(t >= n_lat_tiles, 0, b + 1), 0, 0))


def _stream_specs(h, s_len):
    n_lat = s_len // DTOK
    if not isinstance(h, tuple):
        return [h], [pl.BlockSpec((1, DTOK, h.shape[-1]), lambda b, i: (b, i, 0))], 0
    lat, ctx = h
    c_len, d = ctx.shape[1:]
    assert c_len <= DTOK and lat.shape[1] == s_len
    return ([lat, ctx],
            [pl.BlockSpec((1, DTOK, d), lambda b, i: (b, jnp.minimum(i, n_lat - 1), 0)),
             pl.BlockSpec((1, c_len, d), lambda b, i: (b, 0, 0))], c_len)


def _per_stream(body, h_refs, n_lat_tiles, ctx_rows):
    if len(h_refs) == 1:
        body(h_refs[0], DTOK)
        return
    i = pl.program_id(1)
    pl.when(i < n_lat_tiles)(lambda: body(h_refs[0], DTOK))
    pl.when(i >= n_lat_tiles)(lambda: body(h_refs[1], ctx_rows))


def _inproj_kernel(*refs, splits, n_src, n_lat_tiles, ctx_rows):
    h_refs = refs[:n_src]
    nw_ref, mod_ref, w_ref = refs[n_src:n_src + 3]
    o_refs = refs[n_src + 3:]

    def body(h_ref, rows):
        mod = mod_ref[0]
        u = _rms(h_ref[0], nw_ref[...]) * (1.0 + mod[1:2]) + mod[0:1]
        z = _dot(u.astype(BF16), w_ref[...])
        for o_ref, (lo, hi) in zip(o_refs, splits):
            o_ref[0, :rows] = z[:, lo:hi].astype(o_ref.dtype)

    _per_stream(body, h_refs, n_lat_tiles, ctx_rows)


def _inproj(h, norm_w, mods, w, splits, dtypes, s_len, t):
    d, n = w.shape
    assert s_len % DTOK == 0
    srcs, src_specs, ctx_rows = _stream_specs(h, s_len)
    bsz = srcs[0].shape[0]
    return pl.pallas_call(
        functools.partial(_inproj_kernel, splits=splits, n_src=len(srcs), n_lat_tiles=s_len // DTOK,
                          ctx_rows=ctx_rows),
        grid=(bsz, pl.cdiv(t, DTOK)),
        in_specs=src_specs + [
            _const_spec((1, d)),
            _mod_spec(d, s_len // DTOK),
            _const_spec((d, n)),
        ],
        out_specs=[pl.BlockSpec((1, DTOK, hi - lo), lambda b, i: (b, i, 0)) for lo, hi in splits],
        out_shape=[jax.ShapeDtypeStruct((bsz, t, hi - lo), dt) for (lo, hi), dt in zip(splits, dtypes)],
        compiler_params=_cparams(("parallel", "parallel")),
        name="inproj",
    )(*srcs, norm_w.reshape(1, d), mods, w)


class _Mix:
    def __init__(self, arrays, specs, width, load):
        self.arrays, self.specs, self.width, self.load = arrays, specs, width, load


def _tok_spec(width, col_block=0):
    return pl.BlockSpec((1, DTOK, width), lambda b, i: (b, i, col_block))


def _mix_plain(a):
    return _Mix([a], [_tok_spec(a.shape[-1])], a.shape[-1], lambda refs, rows: refs[0][0, :rows].astype(BF16))


def _mix_gla(o_fwd, o_bwd, qkvg, norm_w):
    val = GLA_HEADS * GLA_DV

    def load(refs, rows):
        of_ref, ob_ref, gate_ref, nw_ref = refs
        o = of_ref[0, :rows] + ob_ref[0, :rows]
        gate = gate_ref[0, :rows]
        heads = [_rms(o[:, sl], nw_ref[...]) for sl in (slice(h * GLA_DV, (h + 1) * GLA_DV) for h in range(GLA_HEADS))]
        return (jnp.concatenate(heads, axis=1) * (gate * jax.nn.sigmoid(gate))).astype(BF16)

    return _Mix([o_fwd, o_bwd, qkvg, norm_w.reshape(1, GLA_DV)],
                [_tok_spec(val), _tok_spec(val), _tok_spec(val, col_block=2), _const_spec((1, GLA_DV))], val, load)


def _mix_lru(y_fwd, y_bwd, xg):
    ch = y_fwd.shape[-1]
    perm = _lru_row_permutation()

    def load(refs, rows):
        yf_ref, yb_ref, g_ref, perm_ref, unperm_ref = refs
        out = []
        for r in range(0, rows, TOK):
            y = yf_ref[0, r:r + TOK] + yb_ref[0, r:r + TOK]
            y = y * jax.nn.gelu(_dot(perm_ref[...], g_ref[0, r:r + TOK]))
            out.append(_dot(unperm_ref[...], y.astype(BF16)).astype(BF16))
        return out[0] if len(out) == 1 else jnp.concatenate(out, axis=0)

    return _Mix([y_fwd, y_bwd, xg, jnp.asarray(perm, BF16), jnp.asarray(perm.T, BF16)],
                [_tok_spec(ch), _tok_spec(ch), _tok_spec(ch, col_block=1), _const_spec((TOK, TOK)),
                 _const_spec((TOK, TOK))], ch, load)


def _mlp_kernel(*refs, ff_chunk, final, n_src, n_lat_tiles, ctx_rows, mixes):
    h_refs = refs[:n_src]
    pos = n_src
    mix_refs = []
    for n_arrays, _, _ in mixes:
        mix_refs.append(refs[pos:pos + n_arrays])
        pos += n_arrays
    mod_ref, wo_ref, nw_ref, w1_ref, w2_ref, fw_ref, o_ref = refs[pos:]

    def body(h_ref, rows):
        mod = mod_ref[0]
        y = None
        row0 = 0
        for (_, width, load), m_refs in zip(mixes, mix_refs):
            part = _dot(load(m_refs, rows), wo_ref[row0:row0 + width])
            y = part if y is None else y + part
            row0 += width
        h1 = h_ref[0] + mod[2:3] * y
        u = (_rms(h1, nw_ref[...]) * (1.0 + mod[4:5]) + mod[3:4]).astype(BF16)
        d_ff = w1_ref.shape[1]
        acc = jnp.zeros(h1.shape, F32)
        for c in range(d_ff // ff_chunk):
            sl = slice(c * ff_chunk, (c + 1) * ff_chunk)
            a = jnp.maximum(_dot(u, w1_ref[:, sl]), 0.0)
            acc = acc + _dot((a * a).astype(BF16), w2_ref[sl, :])
        h2 = h1 + mod[5:6] * acc
        if final:
            h2 = _rms(h2, fw_ref[...])
        o_ref[0, :rows] = h2

    _per_stream(body, h_refs, n_lat_tiles, ctx_rows)


def _mlp(h, mixes, mods, wo, norm_w, w1, w2, layer, final_w, s_len, out_rows, final):
    _, d, d_ff = w1.shape
    assert s_len % DTOK == 0 and wo.shape[0] == sum(m.width for m in mixes)
    srcs, src_specs, ctx_rows = _stream_specs(h, s_len)
    bsz = srcs[0].shape[0]
    return pl.pallas_call(
        functools.partial(_mlp_kernel, ff_chunk=512, final=final, n_src=len(srcs), n_lat_tiles=s_len // DTOK,
                          ctx_rows=ctx_rows, mixes=tuple((len(m.arrays), m.width, m.load) for m in mixes)),
        grid=(bsz, pl.cdiv(out_rows, DTOK)),
        in_specs=src_specs + [s for m in mixes for s in m.specs] + [
            _mod_spec(d, s_len // DTOK),
            _const_spec(wo.shape), _const_spec((1, d)),
            _layer_spec((d, d_ff), layer), _layer_spec((d_ff, d), layer), _const_spec((1, d)),
        ],
        out_specs=_tok_spec(d),
        out_shape=jax.ShapeDtypeStruct((bsz, out_rows, d), F32),
        compiler_params=_cparams(("parallel", "parallel")),
        name="mlp",
    )(*srcs, *[a for m in mixes for a in m.arrays], mods, wo, norm_w.reshape(1, d), w1, w2, final_w.reshape(1, d))


NA_QROWS = TOK // GRID_W
NA_WIN = NA_KH + NA_QROWS


def _na_plan(rows):
    n_blk = rows // NA_QROWS
    r = np.arange(rows).reshape(n_blk, NA_QROWS)
    start = np.clip(r - NA_KH // 2, 0, rows - NA_KH)
    u0 = np.clip(r[:, 0] - NA_KH // 2, 0, rows - NA_WIN)
    key = u0[:, None, None] + np.arange(NA_WIN)[None, None, :]
    valid = (key >= start[:, :, None]) & (key < start[:, :, None] + NA_KH)
    d_row = np.where(valid, key - r[:, :, None] + (NA_KH - 1), 0)
    flat = np.concatenate([valid.reshape(n_blk, -1), d_row.reshape(n_blk, -1)], axis=1)
    _, first, pat = np.unique(flat, axis=0, return_index=True, return_inverse=True)
    return u0.astype(np.int32), pat.reshape(-1).astype(np.int32), valid[first], d_row[first]


def _na_bias_table(rpb, valid, d_row):
    n_pat = valid.shape[0]
    heads = rpb.shape[0]
    pairs = NA_WIN // 2
    n_dc = 2 * NA_KW - 1
    cols = np.arange(GRID_W)
    col_start = np.clip(cols - NA_KW // 2, 0, GRID_W - NA_KW)
    col_valid = (cols[None, :] >= col_start[:, None]) & (cols[None, :] < col_start[:, None] + NA_KW)
    d_col = np.clip(cols[None, :] - cols[:, None], 1 - NA_KW, NA_KW - 1) + (NA_KW - 1)
    row_sel = (d_row[..., None] == np.arange(2 * NA_KH - 1)).astype(np.float32)
    col_sel = (d_col[None] == np.arange(n_dc)[:, None, None]).astype(np.float32)
    n_k = n_dc + 2
    pair_sel = np.zeros((2, n_k, GRID_W, 2, GRID_W), np.float32)
    for u1 in range(2):
        pair_sel[u1, :n_dc, :, u1, :] = col_sel
        pair_sel[u1, n_dc, :, u1, :] = ~col_valid
        pair_sel[u1, n_dc + 1, :, u1, :] = 1.0
    pair_sel = pair_sel.reshape(2 * n_k, GRID_W, 2 * GRID_W)
    hp = lax.Precision.HIGHEST
    by_row = jnp.einsum('hrc,piur->phiuc', rpb, row_sel, precision=hp) * LOG2E
    col_mask = jnp.full((n_pat, heads, NA_QROWS, NA_WIN, 1), NEG, F32)
    row_mask = jnp.broadcast_to(jnp.where(valid, 0.0, NEG)[:, None, :, :, None], col_mask.shape)
    by_row = jnp.concatenate([by_row, col_mask, row_mask], axis=-1)
    by_row = by_row.reshape(n_pat, heads, NA_QROWS, pairs, 2 * n_k).transpose(0, 1, 3, 2, 4)
    bias = lax.dot_general(by_row, pair_sel, (((4,), (0,)), ((), ())), precision=hp)
    return bias.reshape(n_pat, heads, pairs, TOK, 2 * GRID_W)


def _softmax_tiles(tiles):
    m = jnp.max(functools.reduce(jnp.maximum, tiles), axis=-1, keepdims=True)
    p = [jnp.exp2(t - m) for t in tiles]
    denom = jnp.sum(functools.reduce(jnp.add, p), axis=-1, keepdims=True)
    return [t.astype(BF16) for t in p], denom


def _lane_tiles(x):
    return [x[:, j:j + LANES] for j in range(0, x.shape[1], LANES)]


def _na_kernel(u0_ref, pat_ref, q_ref, k_ref, v_ref, kc_ref, vc_ref, bias_ref, o_ref, *, n_lat):
    i = pl.program_id(1)
    win = NA_WIN * GRID_W
    n_loc = win // LANES

    def attend(local):
        q = q_ref[0]
        kc = kc_ref[0]
        vc = vc_ref[0]
        if local:
            row0 = pl.multiple_of(u0_ref[i] * GRID_W, GRID_W)
            kw = k_ref[0, pl.ds(row0, win), :]
            vw = v_ref[0, pl.ds(row0, win), :]

        def scores(h):
            sl = slice(h * NA_DH, (h + 1) * NA_DH)
            s_ctx = _lane_tiles(_dot_nt(q[:, sl], kc[:, sl]))
            if not local:
                return s_ctx
            s_loc = _lane_tiles(_dot_nt(q[:, sl], kw[:, sl]))
            return [t + bias_ref[0, h, j] for j, t in enumerate(s_loc)] + s_ctx

        ahead = 1
        pending = [scores(h) for h in range(ahead)]
        for h in range(NA_HEADS):
            sl = slice(h * NA_DH, (h + 1) * NA_DH)
            tiles = pending.pop(0)
            if h + ahead < NA_HEADS:
                pending.append(scores(h + ahead))
            p, denom = _softmax_tiles(tiles)
            v_all = jnp.concatenate([vw[:, sl], vc[:, sl]], axis=0) if local else vc[:, sl]
            o = _dot(jnp.concatenate(p, axis=1), v_all)
            o_ref[0, :, sl] = o / denom

    pl.when(i < n_lat)(functools.partial(attend, True))
    pl.when(i >= n_lat)(functools.partial(attend, False))


def _na(qkv, rpb, s_len, c_len):
    bsz, t, _ = qkv.shape
    width = NA_HEADS * NA_DH
    rows = s_len // GRID_W
    n_lat = s_len // TOK
    n_steps = t // TOK
    assert rows >= NA_WIN and rows % NA_QROWS == 0 and c_len % TOK == 0 and s_len % c_len == 0
    u0, pat, valid, d_row = _na_plan(rows)
    bias = _na_bias_table(rpb, valid, d_row)
    ctx_steps = np.zeros(n_steps - n_lat, np.int32)
    u0 = jnp.asarray(np.concatenate([u0, ctx_steps]))
    pat = jnp.asarray(np.concatenate([pat, ctx_steps]))
    tok_spec = pl.BlockSpec((1, TOK, width), lambda b, i, u0_ref, pat_ref: (b, i, 0))
    ctx_blk = s_len // c_len
    return pl.pallas_call(
        functools.partial(_na_kernel, n_lat=n_lat),
        grid_spec=pltpu.PrefetchScalarGridSpec(
            num_scalar_prefetch=2,
            grid=(bsz, n_steps),
            in_specs=[
                tok_spec,
                pl.BlockSpec((1, s_len, width), lambda b, i, u0_ref, pat_ref: (b, 0, 1),
                             pipeline_mode=pl.Buffered(1)),
                pl.BlockSpec((1, s_len, width), lambda b, i, u0_ref, pat_ref: (b, 0, 2),
                             pipeline_mode=pl.Buffered(1)),
                pl.BlockSpec((1, c_len, width), lambda b, i, u0_ref, pat_ref: (b, ctx_blk, 1)),
                pl.BlockSpec((1, c_len, width), lambda b, i, u0_ref, pat_ref: (b, ctx_blk, 2)),
                pl.BlockSpec((1, NA_HEADS, NA_WIN // 2, TOK, 2 * GRID_W),
                             lambda b, i, u0_ref, pat_ref: (pat_ref[i], 0, 0, 0, 0)),
            ],
            out_specs=tok_spec,
        ),
        out_shape=jax.ShapeDtypeStruct((bsz, t, width), F32),
        compiler_params=_cparams(("parallel", "arbitrary")),
        name="na",
    )(u0, pat, qkv, qkv, qkv, qkv, qkv, bias)


def _lru_tile_index(i, n_lat, n_tiles, direction):
    if direction == 0:
        return (i + n_lat) % n_tiles
    return n_tiles - 1 - i


def _lru_row_permutation():
    seg = TOK // LRU_SEG
    p = np.arange(TOK)
    src = (p % LRU_SEG) * seg + p // LRU_SEG
    return (src[:, None] == np.arange(TOK)[None, :]).astype(np.float32)


def _shift_rows(group, boundary_row, down):
    sub = lax.broadcasted_iota(jnp.int32, group.shape, 0)
    if down:
        return jnp.where(sub == 0, boundary_row, pltpu.roll(group, 1, axis=0))
    return jnp.where(sub == LRU_SEG - 1, boundary_row, pltpu.roll(group, LRU_SEG - 1, axis=0))


def _lru_kernel(xf_ref, prevf_ref, nextf_ref, xb_ref, prevb_ref, nextb_ref, cw_ref, cb_ref, wg_ref, bg_ref, lam_ref,
                perm_ref, of_ref, ob_ref, carry_s, *, n_lat, n_tiles, ctx_sub):
    i = pl.program_id(1)
    ch = xf_ref.shape[-1]
    halo = prevf_ref.shape[1]
    seg = TOK // LRU_SEG
    grp = LRU_SEG

    @pl.when(i == 0)
    def _():
        carry_s[...] = jnp.zeros_like(carry_s)

    perm = perm_ref[...]
    cw = cw_ref[...]

    def sub_tile(direction, r, n_sub, state):
        x_ref, prev_ref, next_ref, o_ref = ((xf_ref, prevf_ref, nextf_ref, of_ref) if direction == 0 else
                                            (xb_ref, prevb_ref, nextb_ref, ob_ref))
        tile = _lru_tile_index(i, n_lat, n_tiles, direction)
        seq_first = jnp.logical_or(tile == 0, tile == n_lat)
        seq_last = jnp.logical_or(tile == n_lat - 1, tile == n_tiles - 1)
        rows = slice(r * TOK, (r + 1) * TOK)
        x = _dot(perm, x_ref[0, rows])
        if r == 0:
            prev = jnp.where(seq_first, 0.0, prev_ref[0].astype(F32))
        else:
            prev = x_ref[0, r * TOK - halo:r * TOK].astype(F32)
        if r == n_sub - 1:
            nxt = jnp.where(seq_last, 0.0, next_ref[0].astype(F32))
        else:
            nxt = x_ref[0, (r + 1) * TOK:(r + 1) * TOK + halo].astype(F32)
        wrap_m1 = _shift_rows(x[(seg - 1) * grp:], prev[halo - 1:halo], down=True)
        wrap_m2 = _shift_rows(x[(seg - 2) * grp:(seg - 1) * grp], prev[halo - 2:halo - 1], down=True)
        wrap_p1 = _shift_rows(x[:grp], nxt[0:1], down=False)
        x_m1 = jnp.concatenate([wrap_m1, x[:-grp]], axis=0)
        x_m2 = jnp.concatenate([wrap_m2, wrap_m1, x[:-2 * grp]], axis=0)
        x_p1 = jnp.concatenate([x[grp:], wrap_p1], axis=0)
        xc = cw[0:1] * x_m2 + cw[1:2] * x_m1 + cw[2:3] * x + cw[3:4] * x_p1 + cb_ref[...]

        pre = _dot(xc.astype(BF16), wg_ref[direction]) + bg_ref[direction]
        gates = 0.5 * jnp.tanh(0.5 * pre) + 0.5
        r_gate = gates[:, :ch]
        i_gate = gates[:, ch:]
        log_a = (-LRU_C) * r_gate * _softplus(-lam_ref[direction])
        a = jnp.exp(log_a)
        bb = jnp.sqrt(1.0 - a * a) * (i_gate * xc)

        steps = range(seg) if direction == 0 else range(seg - 1, -1, -1)
        h = jnp.zeros((grp, ch), F32)
        cum = jnp.ones((grp, ch), F32)
        h_loc = [None] * seg
        cum_loc = [None] * seg
        for j in steps:
            a_j = a[j * grp:(j + 1) * grp]
            h = a_j * h + bb[j * grp:(j + 1) * grp]
            cum = cum * a_j
            h_loc[j] = h
            cum_loc[j] = cum

        entering = [None] * LRU_SEG
        for s in (range(LRU_SEG) if direction == 0 else range(LRU_SEG - 1, -1, -1)):
            entering[s] = state
            state = h[s:s + 1] + cum[s:s + 1] * state
        entering = jnp.concatenate(entering, axis=0)
        o_ref[0, rows] = jnp.concatenate([h_loc[j] + cum_loc[j] * entering for j in range(seg)], axis=0)
        return state

    def run(n_sub):
        states = [carry_s[0], carry_s[1]]
        for r in range(n_sub):
            states[0] = sub_tile(0, r, n_sub, states[0])
            states[1] = sub_tile(1, n_sub - 1 - r, n_sub, states[1])
        carry_s[0] = states[0]
        carry_s[1] = states[1]

    n_blk = xf_ref.shape[1] // TOK
    if ctx_sub == n_blk:
        run(n_blk)
    else:
        pl.when(i == 0)(lambda: run(ctx_sub))
        pl.when(i > 0)(lambda: run(n_blk))


def _lru(xg, conv_w, conv_b, w_gate, b_gate, lam, s_len):
    bsz, t, two_ch = xg.shape
    ch = two_ch // 2
    n_lat, n_tiles, ctx_sub = _scan_blocks(s_len, t - s_len, MTOK)
    halo = 16
    per_tile = MTOK // halo
    tiles = [functools.partial(_lru_tile_index, n_lat=n_lat, n_tiles=n_tiles, direction=d) for d in range(2)]

    def stream_specs(tile_of):
        return [pl.BlockSpec((1, MTOK, ch), lambda b, i: (b, tile_of(i), 0)),
                pl.BlockSpec((1, halo, ch), lambda b, i: (b, jnp.maximum(tile_of(i) * per_tile - 1, 0), 0)),
                pl.BlockSpec((1, halo, ch),
                             lambda b, i: (b, jnp.minimum((tile_of(i) + 1) * per_tile, t // halo - 1), 0))]

    out_spec = lambda tile_of: pl.BlockSpec((1, MTOK, ch), lambda b, i: (b, tile_of(i), 0))
    return pl.pallas_call(
        functools.partial(_lru_kernel, n_lat=n_lat, n_tiles=n_tiles, ctx_sub=ctx_sub),
        grid=(bsz, n_tiles),
        in_specs=stream_specs(tiles[0]) + stream_specs(tiles[1]) + [
            _const_spec((4, ch)), _const_spec((1, ch)),
            _const_spec((2, ch, 2 * ch)), _const_spec((2, 1, 2 * ch)), _const_spec((2, 1, ch)),
            _const_spec((TOK, TOK)),
        ],
        out_specs=[out_spec(tiles[0]), out_spec(tiles[1])],
        out_shape=[jax.ShapeDtypeStruct((bsz, t, ch), F32)] * 2,
        scratch_shapes=[pltpu.VMEM((2, 1, ch), F32)],
        compiler_params=_cparams(("parallel", "arbitrary")),
        name="lru",
    )(xg, xg, xg, xg, xg, xg, conv_w, conv_b.reshape(1, ch), w_gate, b_gate, lam.reshape(2, 1, ch),
      jnp.asarray(_lru_row_permutation(), BF16))


def _block_diag(w):
    heads, n, _ = w.shape
    eye = jnp.eye(heads, dtype=w.dtype)
    return (eye[:, None, :, None] * w[:, :, None, :]).reshape(heads * n, heads * n)


def _gla_kernel(qf_ref, lrf_ref, qb_ref, lrb_ref, wa_ref, ba_ref, trif_ref, trib_ref, csum_ref, of_ref, ob_ref,
                state_s, *, ctx_sub):
    key = GLA_HEADS * GLA_DK
    val = GLA_HEADS * GLA_DV
    nchunk = TOK // GLA_CHUNK
    step = pl.program_id(1)

    @pl.when(step == 0)
    def _():
        state_s[...] = jnp.zeros_like(state_s)

    chunks = [slice(c * GLA_CHUNK, (c + 1) * GLA_CHUNK) for c in range(nchunk)]

    def prepare(direction, rows):
        qkvg_ref, lr_ref, tri_ref = (qf_ref, lrf_ref, trif_ref) if direction == 0 else (qb_ref, lrb_ref, trib_ref)
        tri = tri_ref[...]
        ends = [(c + 1) * GLA_CHUNK - 1 if direction == 0 else c * GLA_CHUNK for c in range(nchunk)]
        z = qkvg_ref[0, rows]
        q = z[:, :key] * (GLA_DK ** -0.5)
        k = z[:, key:2 * key]
        v = z[:, 2 * key:2 * key + val].astype(BF16)
        lr = lr_ref[0, rows][:, direction * GLA_RANK:(direction + 1) * GLA_RANK]
        logit = _dot(lr.astype(BF16), wa_ref[direction].astype(BF16)) + ba_ref[direction]
        log_a = (jnp.minimum(logit, 0.0) - jnp.log1p(jnp.exp(-jnp.abs(logit)))) / GLA_TAU

        g_hi, g_lo = _split_hi_lo(log_a)
        b = _dot(tri, g_hi) + _dot(tri, g_lo)
        q_in = (q * jnp.exp(b)).astype(BF16)
        k_in = (k * jnp.exp(-b)).astype(BF16)
        b_end = jnp.concatenate([jnp.broadcast_to(b[e:e + 1], (GLA_CHUNK, key)) for e in ends], axis=0)
        k_out_t = (k * jnp.exp(b_end - b)).T.astype(BF16)
        gt_hi, gt_lo = _split_hi_lo(log_a.T)
        decay = jnp.exp(_dot(gt_hi, csum_ref[...]) + _dot(gt_lo, csum_ref[...]))
        return tri > 0, q_in, k_in, k_out_t, v, decay

    def attend(direction, rows, prepared, states):
        o_ref = of_ref if direction == 0 else ob_ref
        causal, q_in, k_in, k_out_t, v, decay = prepared
        chunk_order = range(nchunk) if direction == 0 else range(nchunk - 1, -1, -1)
        new_states = []
        for h in range(GLA_HEADS):
            ks = slice(h * GLA_DK, (h + 1) * GLA_DK)
            vs = slice(h * GLA_DV, (h + 1) * GLA_DV)
            v_h = v[:, vs]
            att = jnp.where(causal, _dot_nt(q_in[:, ks], k_in[:, ks]), 0.0)
            o = _dot(att.astype(BF16), v_h)
            state = states[h]
            inter = [None] * nchunk
            for c in chunk_order:
                inter[c] = _dot(q_in[chunks[c], ks], state.astype(BF16))
                state = (decay[ks, c * GLA_DV:(c + 1) * GLA_DV] * state
                         + _dot(k_out_t[ks, chunks[c]], v_h[chunks[c]]))
            new_states.append(state)
            o_ref[0, rows, vs] = o + jnp.concatenate(inter, axis=0)
        return new_states

    def run(n_sub):
        subs = [slice(r * TOK, (r + 1) * TOK) for r in range(n_sub)]
        states = [[state_s[d, h] for h in range(GLA_HEADS)] for d in range(2)]
        work = [(d, subs[r] if d == 0 else subs[n_sub - 1 - r]) for r in range(n_sub) for d in range(2)]
        prepared = [prepare(d, rows) for d, rows in work]
        for (d, rows), prep in zip(work, prepared):
            states[d] = attend(d, rows, prep, states[d])
        for d in range(2):
            for h in range(GLA_HEADS):
                state_s[d, h] = states[d][h]

    n_blk = qf_ref.shape[1] // TOK
    if ctx_sub == n_blk:
        run(n_blk)
    else:
        pl.when(step == 0)(lambda: run(ctx_sub))
        pl.when(step > 0)(lambda: run(n_blk))


def _scan_blocks(s_len, c_len, blk):
    assert s_len % blk == 0 and c_len % TOK == 0 and c_len <= blk
    return s_len // blk, s_len // blk + 1, c_len // TOK


def _gla(qkvg, lr, wa2, ba, s_len):
    bsz, t, width = qkvg.shape
    key = GLA_HEADS * GLA_DK
    val = GLA_HEADS * GLA_DV
    n_lat, n_tiles, ctx_sub = _scan_blocks(s_len, t - s_len, MTOK)
    tiles = [functools.partial(_lru_tile_index, n_lat=n_lat, n_tiles=n_tiles, direction=d) for d in range(2)]
    pos = np.arange(TOK)
    same = (pos[:, None] // GLA_CHUNK) == (pos[None, :] // GLA_CHUNK)
    tri = [jnp.asarray(same & m, BF16) for m in (pos[None, :] <= pos[:, None], pos[None, :] >= pos[:, None])]
    chunk_sum = jnp.asarray((pos[:, None] // GLA_CHUNK) == (np.arange(TOK // GLA_CHUNK * GLA_DV)[None, :] // GLA_DV), BF16)
    blk = lambda width, tile_of: pl.BlockSpec((1, MTOK, width), lambda b, i: (b, tile_of(i), 0))
    return pl.pallas_call(
        functools.partial(_gla_kernel, ctx_sub=ctx_sub),
        grid=(bsz, n_tiles),
        in_specs=[
            blk(width, tiles[0]), blk(lr.shape[-1], tiles[0]), blk(width, tiles[1]), blk(lr.shape[-1], tiles[1]),
            _const_spec((2, GLA_RANK, key)), _const_spec((2, 1, key)),
            _const_spec((TOK, TOK)), _const_spec((TOK, TOK)), _const_spec(chunk_sum.shape),
        ],
        out_specs=[blk(val, tiles[0]), blk(val, tiles[1])],
        out_shape=[jax.ShapeDtypeStruct((bsz, t, val), F32)] * 2,
        scratch_shapes=[pltpu.VMEM((2, GLA_HEADS, GLA_DK, GLA_DV), F32)],
        compiler_params=_cparams(("parallel", "arbitrary")),
        name="gla",
    )(qkvg, lr, qkvg, lr, wa2, ba.reshape(2, 1, key), tri[0], tri[1], chunk_sum)


def _rope_tables(s_len, c_len, reps):
    quarter = GQA_DH // 4
    inv = ROPE_THETA ** (-np.arange(quarter, dtype=np.float32) / quarter)
    t = np.arange(s_len)
    ang_r = (t // GRID_W).astype(np.float32)[:, None] * inv[None, :]
    ang_c = (t % GRID_W).astype(np.float32)[:, None] * inv[None, :]
    cos = np.concatenate([np.cos(ang_r)] * 2 + [np.cos(ang_c)] * 2, axis=1)
    sin = np.concatenate([-np.sin(ang_r), np.sin(ang_r), -np.sin(ang_c), np.sin(ang_c)], axis=1)
    cos = np.concatenate([cos, np.ones((c_len, GQA_DH), np.float32)], axis=0)
    sin = np.concatenate([sin, np.zeros((c_len, GQA_DH), np.float32)], axis=0)
    return (jnp.asarray(np.tile(cos, (1, reps)), F32), jnp.asarray(np.tile(sin, (1, reps)), F32))


def _rope_swap(x):
    lane = lax.broadcasted_iota(jnp.int32, x.shape, 1)
    return jnp.where(lane % 32 < 16, pltpu.roll(x, 128 - 16, axis=1), pltpu.roll(x, 16, axis=1))


def _norm_rope(x, nw, cos, sin, gmean):
    hi, lo = _split_hi_lo(x * x)
    ms = _dot(hi, gmean) + _dot(lo, gmean)
    y = x * lax.rsqrt(ms + EPS) * nw
    slabs = [_rope_swap(y[:, j:j + 128]) for j in range(0, y.shape[1], 128)]
    swapped = slabs[0] if len(slabs) == 1 else jnp.concatenate(slabs, axis=1)
    return y * cos + swapped * sin


def _group_mean_matrix(n, group):
    idx = np.arange(n) // group
    return jnp.asarray((idx[:, None] == idx[None, :]) / group, BF16)


def _inproj_odd_kernel(h_ref, nw_ref, mod_ref, w_ref, qw_ref, kw_ref, cos_ref, sin_ref, gq_ref, gk_ref,
                       qkvg_ref, lr_ref, q_ref, k_ref, v_ref, *, gla_w):
    qd = GQA_HEADS * GQA_DH
    kd = GQA_KV_HEADS * GQA_DH
    mod = mod_ref[0]
    u = _rms(h_ref[0], nw_ref[...]) * (1.0 + mod[1:2]) + mod[0:1]
    z = _dot(u.astype(BF16), w_ref[...])
    qkvg_ref[0] = z[:, :gla_w]
    lr_ref[0] = z[:, gla_w + qd + 2 * kd:]
    cos_k = cos_ref[...]
    sin_k = sin_ref[...]
    reps = qd // kd
    cos_q = jnp.concatenate([cos_k] * reps, axis=1)
    sin_q = jnp.concatenate([sin_k] * reps, axis=1)
    q = _norm_rope(z[:, gla_w:gla_w + qd], qw_ref[...], cos_q, sin_q, gq_ref[...]) * (LOG2E * GQA_DH ** -0.5)
    k = _norm_rope(z[:, gla_w + qd:gla_w + qd + kd], kw_ref[...], cos_k, sin_k, gk_ref[...])
    v = z[:, gla_w + qd + kd:gla_w + qd + 2 * kd]
    q_ref[0] = q.astype(BF16)
    ones = jnp.ones((z.shape[0], LANES - GQA_DH), BF16)
    for h in range(GQA_KV_HEADS):
        k_ref[0, h] = k[:, h * GQA_DH:(h + 1) * GQA_DH].astype(BF16)
        v_ref[0, h] = jnp.concatenate([v[:, h * GQA_DH:(h + 1) * GQA_DH].astype(BF16), ones], axis=1)


def _inproj_odd(h, norm_w, mods, w, q_norm_w, k_norm_w, s_len, c_len, gla_w):
    bsz, t, d = h.shape
    n = w.shape[1]
    qd = GQA_HEADS * GQA_DH
    kd = GQA_KV_HEADS * GQA_DH
    n_lr = n - gla_w - qd - 2 * kd
    cos, sin = _rope_tables(s_len, c_len, GQA_KV_HEADS)
    tok_spec = lambda width: pl.BlockSpec((1, DTOK, width), lambda b, i: (b, i, 0))
    head_spec = lambda width: pl.BlockSpec((1, GQA_KV_HEADS, DTOK, width), lambda b, i: (b, 0, i, 0))
    return pl.pallas_call(
        functools.partial(_inproj_odd_kernel, gla_w=gla_w),
        grid=(bsz, pl.cdiv(t, DTOK)),
        in_specs=[
            tok_spec(d), _const_spec((1, d)), _mod_spec(d, s_len // DTOK), _const_spec((d, n)),
            _const_spec((1, qd)), _const_spec((1, kd)),
            pl.BlockSpec((DTOK, kd), lambda b, i: (i, 0)),
            pl.BlockSpec((DTOK, kd), lambda b, i: (i, 0)),
            _const_spec((qd, qd)), _const_spec((kd, kd)),
        ],
        out_specs=[tok_spec(gla_w), tok_spec(n_lr), tok_spec(qd), head_spec(GQA_DH), head_spec(LANES)],
        out_shape=[jax.ShapeDtypeStruct((bsz, t, gla_w), F32),
                   jax.ShapeDtypeStruct((bsz, t, n_lr), F32),
                   jax.ShapeDtypeStruct((bsz, t, qd), BF16),
                   jax.ShapeDtypeStruct((bsz, GQA_KV_HEADS, t, GQA_DH), BF16),
                   jax.ShapeDtypeStruct((bsz, GQA_KV_HEADS, t, LANES), BF16)],
        compiler_params=_cparams(("parallel", "parallel")),
        name="inproj_odd",
    )(h, norm_w.reshape(1, d), mods, w,
      jnp.tile(q_norm_w, GQA_HEADS).reshape(1, qd), jnp.tile(k_norm_w, GQA_KV_HEADS).reshape(1, kd),
      cos, sin, _group_mean_matrix(qd, GQA_DH), _group_mean_matrix(kd, GQA_DH))


def _flash_kernel(q_ref, k_ref, v_ref, o_ref, m_s, acc_s, *, group, tk, sub):
    tq = q_ref.shape[1]
    n_kv = k_ref.shape[2] // tk
    units = [(g, r) for g in range(group) for r in range(0, tq, sub)]
    m_s[...] = jnp.full_like(m_s, -jnp.inf)
    acc_s[...] = jnp.zeros_like(acc_s)

    def kv_step(j, carry):
        rows_k = pl.ds(pl.multiple_of(j * tk, tk), tk)
        k = k_ref[0, 0, rows_k, :]
        v = v_ref[0, 0, rows_k, :]

        def scores(u):
            g, r = units[u]
            return _dot_nt(q_ref[0, r:r + sub, g * GQA_DH:(g + 1) * GQA_DH], k)

        s_next = scores(0)
        for u in range(len(units)):
            tiles = _lane_tiles(s_next)
            if u + 1 < len(units):
                s_next = scores(u + 1)
            m_prev = m_s[u]
            m_new = jnp.maximum(m_prev, jnp.max(functools.reduce(jnp.maximum, tiles), axis=-1, keepdims=True))
            p = [jnp.exp2(t - m_new).astype(BF16) for t in tiles]
            acc_s[u] = jnp.exp2(m_prev - m_new) * acc_s[u] + _dot(jnp.concatenate(p, axis=1), v)
            m_s[u] = m_new
        return carry

    lax.fori_loop(0, n_kv, kv_step, 0)
    for u, (g, r) in enumerate(units):
        acc = acc_s[u]
        o_ref[0, r:r + sub, g * GQA_DH:(g + 1) * GQA_DH] = acc[:, :GQA_DH] / acc[:, GQA_DH:GQA_DH + 1]


KV_TILE_MAX = 3072
FLASH_SUB = 512


def _kv_tile(t):
    return max(n for n in range(LANES, KV_TILE_MAX + 1, LANES) if t % n == 0)


def _flash(q, k, v, s_len, tq, tk):
    bsz, _, t, dh = k.shape
    sub = FLASH_SUB
    assert t % tk == 0 and s_len % tq == 0 and tq % sub == 0
    group = GQA_HEADS // GQA_KV_HEADS
    n_units = group * (tq // sub)
    return pl.pallas_call(
        functools.partial(_flash_kernel, group=group, tk=tk, sub=sub),
        grid=(bsz, GQA_KV_HEADS, s_len // tq),
        in_specs=[
            pl.BlockSpec((1, tq, group * dh), lambda b, h, i: (b, i, h)),
            pl.BlockSpec((1, 1, t, dh), lambda b, h, i: (b, h, 0, 0)),
            pl.BlockSpec((1, 1, t, LANES), lambda b, h, i: (b, h, 0, 0)),
        ],
        out_specs=pl.BlockSpec((1, tq, group * dh), lambda b, h, i: (b, i, h)),
        out_shape=jax.ShapeDtypeStruct((bsz, s_len, GQA_HEADS * dh), F32),
        scratch_shapes=[pltpu.VMEM((n_units, sub, LANES), F32), pltpu.VMEM((n_units, sub, LANES), F32)],
        compiler_params=_cparams(("parallel", "parallel", "arbitrary")),
        name="gqa_flash",
    )(q, k, v)


def kernel(x, c, ctx, c_ctx, norm1_w, norm2_w, w_mod, b_mod, w_ff1, w_ff2,
           w_in_even, na_rpb, lru_conv_w, lru_conv_b, lru_wa, lru_ba, lru_wx, lru_bx, lru_lambda, w_out_even,
           w_in_odd, gla_wa2, gla_ba, gla_norm_w, gqa_q_norm_w, gqa_k_norm_w, w_out_odd, final_norm_w):
    bsz, s_len, d = x.shape
    c_len = ctx.shape[1]
    depth = w_mod.shape[0]
    assert s_len % TOK == 0 and c_len % TOK == 0 and depth == 2

    pad = (-(bsz + 1)) % 8
    cvec = jnp.concatenate([c_ctx[None], c, jnp.zeros((pad, d), F32)], axis=0)
    mods = _modulation(cvec, w_mod, b_mod)
    h = (x, ctx)
    t = s_len + c_len

    na_w = NA_HEADS * NA_DH
    ch = lru_lambda.shape[-1]
    col_scale = jnp.where(jnp.arange(w_in_even.shape[-1]) < na_w, LOG2E * NA_DH ** -0.5, 1.0)
    qkv, xg = _inproj(h, norm1_w[0], mods[0], (w_in_even[0] * col_scale).astype(BF16),
                      ((0, 3 * na_w), (3 * na_w, 3 * na_w + 2 * ch)), (BF16, BF16), s_len, t)
    na_out = _na(qkv, na_rpb[0], s_len, c_len)
    w_gate = jnp.stack([jnp.concatenate([_block_diag(lru_wa[0, dr]), _block_diag(lru_wx[0, dr])], axis=1)
                        for dr in range(2)]).astype(BF16)
    b_gate = jnp.concatenate([lru_ba[0], lru_bx[0]], axis=-1).reshape(2, 1, 2 * ch)
    lru_f, lru_b = _lru(xg, lru_conv_w[0], lru_conv_b[0], w_gate, b_gate, lru_lambda[0], s_len)
    mixes = (_mix_plain(na_out), _mix_lru(lru_f, lru_b, xg))
    w_ff1_bf, w_ff2_bf = _to_bf16(w_ff1), _to_bf16(w_ff2)
    h = _mlp(h, mixes, mods[0], w_out_even[0].astype(BF16), norm2_w[0],
             w_ff1_bf, w_ff2_bf, 0, final_norm_w, s_len, s_len + c_len, final=False)

    key = GLA_HEADS * GLA_DK
    val = GLA_HEADS * GLA_DV
    gla_w = 2 * key + 2 * val
    w_in = w_in_odd[0]
    w_in = jnp.concatenate([w_in[:, :gla_w], w_in[:, gla_w + 2 * GLA_RANK:], w_in[:, gla_w:gla_w + 2 * GLA_RANK]],
                           axis=1).astype(BF16)
    qkvg, lr, q, k, v = _inproj_odd(h, norm1_w[1], mods[1], w_in, gqa_q_norm_w[0], gqa_k_norm_w[0],
                                    s_len, c_len, gla_w)
    gla_f, gla_b = _gla(qkvg, lr, gla_wa2[0], gla_ba[0], s_len)
    gqa_out = _flash(q, k, v, s_len, tq=2 * FLASH_SUB, tk=_kv_tile(t))
    mixes = (_mix_gla(gla_f, gla_b, qkvg, gla_norm_w[0]), _mix_plain(gqa_out))
    return _mlp(h, mixes, mods[1], w_out_odd[0].astype(BF16), norm2_w[1],
                w_ff1_bf, w_ff2_bf, 1, final_norm_w, s_len, s_len, final=True)
```

```python
import functools

import numpy as np
import jax
import jax.numpy as jnp
from jax import lax
from jax.experimental import pallas as pl
from jax.experimental.pallas import tpu as pltpu

F32 = jnp.float32
BF16 = jnp.bfloat16

EPS = 1e-6
GRID_W = 64
LANES = 128
N_MOD = 6
TOK = 256
DTOK = 512
MTOK = 512
NEG = -1e30

NA_HEADS = 8
NA_DH = 64
NA_KH = 8
NA_KW = 16
LRU_C = 8.0
LRU_SEG = 8
GLA_HEADS = 4
GLA_DK = 64
GLA_DV = 128
GLA_RANK = 16
GLA_TAU = 16.0
GLA_CHUNK = 64
GQA_HEADS = 8
GQA_KV_HEADS = 2
GQA_DH = 64
ROPE_THETA = 10000.0
LOG2E = 1.4426950408889634

VMEM_LIMIT = 56 * 1024 * 1024


def _cparams(sem):
    return pltpu.CompilerParams(dimension_semantics=sem, vmem_limit_bytes=VMEM_LIMIT)


def _const_spec(shape):
    nd = len(shape)
    return pl.BlockSpec(shape, lambda *_: (0,) * nd, pipeline_mode=pl.Buffered(1))


def _layer_spec(shape, layer):
    nd = len(shape)
    return pl.BlockSpec((None,) + tuple(shape), lambda *_: (layer,) + (0,) * nd, pipeline_mode=pl.Buffered(1))


def _cast_kernel(x_ref, o_ref):
    o_ref[...] = x_ref[...].astype(o_ref.dtype)


def _to_bf16(w, row_blocks=2):
    layers, rows, cols = w.shape
    blk = rows // row_blocks
    spec = pl.BlockSpec((1, blk, cols), lambda l, i: (l, i, 0))
    return pl.pallas_call(
        _cast_kernel,
        grid=(layers, row_blocks),
        in_specs=[spec],
        out_specs=spec,
        out_shape=jax.ShapeDtypeStruct(w.shape, BF16),
        compiler_params=_cparams(("parallel", "parallel")),
        name="cast_bf16",
    )(w)


def _dot(a, b):
    return jnp.dot(a, b, preferred_element_type=F32)


def _dot_nt(a, b):
    return lax.dot_general(a, b, (((1,), (1,)), ((), ())), preferred_element_type=F32)


def _dot_tn(a, b):
    return lax.dot_general(a, b, (((0,), (0,)), ((), ())), preferred_element_type=F32)


def _split_hi_lo(x):
    hi = x.astype(BF16)
    lo = (x - hi.astype(F32)).astype(BF16)
    return hi, lo


def _rms(x, w):
    ms = jnp.mean(x * x, axis=-1, keepdims=True)
    return x * lax.rsqrt(ms + EPS) * w


def _softplus(z):
    return jnp.maximum(z, 0.0) + jnp.log1p(jnp.exp(-jnp.abs(z)))


def _mod_kernel(c_ref, w_ref, b_ref, o_ref):
    c = c_ref[...]
    s = c * jax.nn.sigmoid(c)
    o_ref[0] = jnp.dot(s, w_ref[0], precision=lax.Precision.HIGHEST, preferred_element_type=F32) + b_ref[0]


def _modulation(cvec, w_mod, b_mod):
    depth, d, _ = w_mod.shape
    rows = cvec.shape[0]
    out = pl.pallas_call(
        _mod_kernel,
        grid=(depth, N_MOD),
        in_specs=[
            pl.BlockSpec((rows, d), lambda i, j: (0, 0)),
            pl.BlockSpec((1, d, d), lambda i, j: (i, 0, j)),
            pl.BlockSpec((1, 1, d), lambda i, j: (i, 0, j)),
        ],
        out_specs=pl.BlockSpec((1, rows, d), lambda i, j: (i, 0, j)),
        out_shape=jax.ShapeDtypeStruct((depth, rows, N_MOD * d), F32),
        compiler_params=_cparams(("arbitrary", "arbitrary")),
        name="modulation",
    )(cvec, w_mod, b_mod.reshape(depth, 1, N_MOD * d))
    return out.reshape(depth, rows, N_MOD, d)


def _mod_spec(d, n_lat_tiles):
    return pl.BlockSpec((1, N_MOD, d), lambda b, t: (jnp.where(t >= n_lat_tiles, 0, b + 1), 0, 0))


def _stream_specs(h, s_len):
    n_lat = s_len // DTOK
    if not isinstance(h, tuple):
        return [h], [pl.BlockSpec((1, DTOK, h.shape[-1]), lambda b, i: (b, i, 0))], 0
    lat, ctx = h
    c_len, d = ctx.shape[1:]
    assert c_len <= DTOK and lat.shape[1] == s_len
    return ([lat, ctx],
            [pl.BlockSpec((1, DTOK, d), lambda b, i: (b, jnp.minimum(i, n_lat - 1), 0)),
             pl.BlockSpec((1, c_len, d), lambda b, i: (b, 0, 0))], c_len)


def _per_stream(body, h_refs, n_lat_tiles, ctx_rows):
    if len(h_refs) == 1:
        body(h_refs[0], DTOK)
        return
    i = pl.program_id(1)
    pl.when(i < n_lat_tiles)(lambda: body(h_refs[0], DTOK))
    pl.when(i >= n_lat_tiles)(lambda: body(h_refs[1], ctx_rows))


def _inproj_kernel(*refs, splits, n_src, n_lat_tiles, ctx_rows):
    h_refs = refs[:n_src]
    nw_ref, mod_ref, w_ref = refs[n_src:n_src + 3]
    o_refs = refs[n_src + 3:]

    def body(h_ref, rows):
        mod = mod_ref[0]
        u = _rms(h_ref[0], nw_ref[...]) * (1.0 + mod[1:2]) + mod[0:1]
        z = _dot(u.astype(BF16), w_ref[...])
        for o_ref, (lo, hi) in zip(o_refs, splits):
            o_ref[0, :rows] = z[:, lo:hi].astype(o_ref.dtype)

    _per_stream(body, h_refs, n_lat_tiles, ctx_rows)


def _inproj(h, norm_w, mods, w, splits, dtypes, s_len, t):
    d, n = w.shape
    assert s_len % DTOK == 0
    srcs, src_specs, ctx_rows = _stream_specs(h, s_len)
    bsz = srcs[0].shape[0]
    return pl.pallas_call(
        functools.partial(_inproj_kernel, splits=splits, n_src=len(srcs), n_lat_tiles=s_len // DTOK,
                          ctx_rows=ctx_rows),
        grid=(bsz, pl.cdiv(t, DTOK)),
        in_specs=src_specs + [
            _const_spec((1, d)),
            _mod_spec(d, s_len // DTOK),
            _const_spec((d, n)),
        ],
        out_specs=[pl.BlockSpec((1, DTOK, hi - lo), lambda b, i: (b, i, 0)) for lo, hi in splits],
        out_shape=[jax.ShapeDtypeStruct((bsz, t, hi - lo), dt) for (lo, hi), dt in zip(splits, dtypes)],
        compiler_params=_cparams(("parallel", "parallel")),
        name="inproj",
    )(*srcs, norm_w.reshape(1, d), mods, w)


class _Mix:
    def __init__(self, arrays, specs, width, load):
        self.arrays, self.specs, self.width, self.load = arrays, specs, width, load


def _tok_spec(width, col_block=0):
    return pl.BlockSpec((1, DTOK, width), lambda b, i: (b, i, col_block))


def _mix_plain(a):
    return _Mix([a], [_tok_spec(a.shape[-1])], a.shape[-1], lambda refs, rows: refs[0][0, :rows].astype(BF16))


def _mix_gla(o_fwd, o_bwd, qkvg, norm_w):
    val = GLA_HEADS * GLA_DV

    def load(refs, rows):
        of_ref, ob_ref, gate_ref, nw_ref = refs
        o = of_ref[0, :rows] + ob_ref[0, :rows]
        gate = gate_ref[0, :rows]
        heads = [_rms(o[:, sl], nw_ref[...]) for sl in (slice(h * GLA_DV, (h + 1) * GLA_DV) for h in range(GLA_HEADS))]
        return (jnp.concatenate(heads, axis=1) * (gate * jax.nn.sigmoid(gate))).astype(BF16)

    return _Mix([o_fwd, o_bwd, qkvg, norm_w.reshape(1, GLA_DV)],
                [_tok_spec(val), _tok_spec(val), _tok_spec(val, col_block=2), _const_spec((1, GLA_DV))], val, load)


def _mix_lru(y_fwd, y_bwd, xg):
    ch = y_fwd.shape[-1]
    perm = _lru_row_permutation()

    def load(refs, rows):
        yf_ref, yb_ref, g_ref, perm_ref, unperm_ref = refs
        out = []
        for r in range(0, rows, TOK):
            y = yf_ref[0, r:r + TOK] + yb_ref[0, r:r + TOK]
            y = y * jax.nn.gelu(_dot(perm_ref[...], g_ref[0, r:r + TOK]))
            out.append(_dot(unperm_ref[...], y.astype(BF16)).astype(BF16))
        return out[0] if len(out) == 1 else jnp.concatenate(out, axis=0)

    return _Mix([y_fwd, y_bwd, xg, jnp.asarray(perm, BF16), jnp.asarray(perm.T, BF16)],
                [_tok_spec(ch), _tok_spec(ch), _tok_spec(ch, col_block=1), _const_spec((TOK, TOK)),
                 _const_spec((TOK, TOK))], ch, load)


def _mlp_kernel(*refs, ff_chunk, final, n_src, n_lat_tiles, ctx_rows, mixes):
    h_refs = refs[:n_src]
    pos = n_src
    mix_refs = []
    for n_arrays, _, _ in mixes:
        mix_refs.append(refs[pos:pos + n_arrays])
        pos += n_arrays
    mod_ref, wo_ref, nw_ref, w1_ref, w2_ref, fw_ref, o_ref = refs[pos:]

    def body(h_ref, rows):
        mod = mod_ref[0]
        y = None
        row0 = 0
        for (_, width, load), m_refs in zip(mixes, mix_refs):
            part = _dot(load(m_refs, rows), wo_ref[row0:row0 + width])
            y = part if y is None else y + part
            row0 += width
        h1 = h_ref[0] + mod[2:3] * y
        u = (_rms(h1, nw_ref[...]) * (1.0 + mod[4:5]) + mod[3:4]).astype(BF16)
        d_ff = w1_ref.shape[1]
        acc = jnp.zeros(h1.shape, F32)
        for c in range(d_ff // ff_chunk):
            sl = slice(c * ff_chunk, (c + 1) * ff_chunk)
            a = jnp.maximum(_dot(u, w1_ref[:, sl]), 0.0)
            acc = acc + _dot((a * a).astype(BF16), w2_ref[sl, :])
        h2 = h1 + mod[5:6] * acc
        if final:
            h2 = _rms(h2, fw_ref[...])
        o_ref[0, :rows] = h2

    _per_stream(body, h_refs, n_lat_tiles, ctx_rows)


def _mlp(h, mixes, mods, wo, norm_w, w1, w2, layer, final_w, s_len, out_rows, final):
    _, d, d_ff = w1.shape
    assert s_len % DTOK == 0 and wo.shape[0] == sum(m.width for m in mixes)
    srcs, src_specs, ctx_rows = _stream_specs(h, s_len)
    bsz = srcs[0].shape[0]
    return pl.pallas_call(
        functools.partial(_mlp_kernel, ff_chunk=512, final=final, n_src=len(srcs), n_lat_tiles=s_len // DTOK,
                          ctx_rows=ctx_rows, mixes=tuple((len(m.arrays), m.width, m.load) for m in mixes)),
        grid=(bsz, pl.cdiv(out_rows, DTOK)),
        in_specs=src_specs + [s for m in mixes for s in m.specs] + [
            _mod_spec(d, s_len // DTOK),
            _const_spec(wo.shape), _const_spec((1, d)),
            _layer_spec((d, d_ff), layer), _layer_spec((d_ff, d), layer), _const_spec((1, d)),
        ],
        out_specs=_tok_spec(d),
        out_shape=jax.ShapeDtypeStruct((bsz, out_rows, d), F32),
        compiler_params=_cparams(("parallel", "parallel")),
        name="mlp",
    )(*srcs, *[a for m in mixes for a in m.arrays], mods, wo, norm_w.reshape(1, d), w1, w2, final_w.reshape(1, d))


NA_QROWS = TOK // GRID_W
NA_WIN = NA_KH + NA_QROWS


def _na_plan(rows):
    n_blk = rows // NA_QROWS
    r = np.arange(rows).reshape(n_blk, NA_QROWS)
    start = np.clip(r - NA_KH // 2, 0, rows - NA_KH)
    u0 = np.clip(r[:, 0] - NA_KH // 2, 0, rows - NA_WIN)
    key = u0[:, None, None] + np.arange(NA_WIN)[None, None, :]
    valid = (key >= start[:, :, None]) & (key < start[:, :, None] + NA_KH)
    d_row = np.where(valid, key - r[:, :, None] + (NA_KH - 1), 0)
    flat = np.concatenate([valid.reshape(n_blk, -1), d_row.reshape(n_blk, -1)], axis=1)
    _, first, pat = np.unique(flat, axis=0, return_index=True, return_inverse=True)
    return u0.astype(np.int32), pat.reshape(-1).astype(np.int32), valid[first], d_row[first]


def _na_bias_table(rpb, valid, d_row):
    n_pat = valid.shape[0]
    heads = rpb.shape[0]
    pairs = NA_WIN // 2
    n_dc = 2 * NA_KW - 1
    cols = np.arange(GRID_W)
    col_start = np.clip(cols - NA_KW // 2, 0, GRID_W - NA_KW)
    col_valid = (cols[None, :] >= col_start[:, None]) & (cols[None, :] < col_start[:, None] + NA_KW)
    d_col = np.clip(cols[None, :] - cols[:, None], 1 - NA_KW, NA_KW - 1) + (NA_KW - 1)
    row_sel = (d_row[..., None] == np.arange(2 * NA_KH - 1)).astype(np.float32)
    col_sel = (d_col[None] == np.arange(n_dc)[:, None, None]).astype(np.float32)
    n_k = n_dc + 2
    pair_sel = np.zeros((2, n_k, GRID_W, 2, GRID_W), np.float32)
    for u1 in range(2):
        pair_sel[u1, :n_dc, :, u1, :] = col_sel
        pair_sel[u1, n_dc, :, u1, :] = ~col_valid
        pair_sel[u1, n_dc + 1, :, u1, :] = 1.0
    pair_sel = pair_sel.reshape(2 * n_k, GRID_W, 2 * GRID_W)
    hp = lax.Precision.HIGHEST
    by_row = jnp.einsum('hrc,piur->phiuc', rpb, row_sel, precision=hp) * LOG2E
    col_mask = jnp.full((n_pat, heads, NA_QROWS, NA_WIN, 1), NEG, F32)
    row_mask = jnp.broadcast_to(jnp.where(valid, 0.0, NEG)[:, None, :, :, None], col_mask.shape)
    by_row = jnp.concatenate([by_row, col_mask, row_mask], axis=-1)
    by_row = by_row.reshape(n_pat, heads, NA_QROWS, pairs, 2 * n_k).transpose(0, 1, 3, 2, 4)
    bias = lax.dot_general(by_row, pair_sel, (((4,), (0,)), ((), ())), precision=hp)
    return bias


def _softmax_tiles(tiles):
    m = jnp.max(functools.reduce(jnp.maximum, tiles), axis=-1, keepdims=True)
    p = [jnp.exp2(t - m) for t in tiles]
    denom = jnp.sum(functools.reduce(jnp.add, p), axis=-1, keepdims=True)
    return [t.astype(BF16) for t in p], denom


def _lane_tiles(x):
    return [x[:, j:j + LANES] for j in range(0, x.shape[1], LANES)]


def _na_kernel(u0_ref, pat_ref, q_ref, k_ref, v_ref, kc_ref, vc_ref, bias_ref, o_ref, *, n_lat):
    i = pl.program_id(1)
    win = NA_WIN * GRID_W
    n_loc = win // LANES

    def attend(local):
        q = q_ref[0]
        kc = kc_ref[0]
        vc = vc_ref[0]
        if local:
            row0 = pl.multiple_of(u0_ref[i] * GRID_W, GRID_W)
            kw = k_ref[0, pl.ds(row0, win), :]
            vw = v_ref[0, pl.ds(row0, win), :]

        def scores(h):
            sl = slice(h * NA_DH, (h + 1) * NA_DH)
            s_ctx = _lane_tiles(_dot_nt(q[:, sl], kc[:, sl]))
            if not local:
                return s_ctx
            s_loc = _lane_tiles(_dot_nt(q[:, sl], kw[:, sl]))
            return [t + bias_ref[0, h, j].reshape(TOK, LANES) for j, t in enumerate(s_loc)] + s_ctx

        ahead = 1
        pending = [scores(h) for h in range(ahead)]
        for h in range(NA_HEADS):
            sl = slice(h * NA_DH, (h + 1) * NA_DH)
            tiles = pending.pop(0)
            if h + ahead < NA_HEADS:
                pending.append(scores(h + ahead))
            p, denom = _softmax_tiles(tiles)
            v_all = jnp.concatenate([vw[:, sl], vc[:, sl]], axis=0) if local else vc[:, sl]
            o = _dot(jnp.concatenate(p, axis=1), v_all)
            o_ref[0, :, sl] = o / denom

    pl.when(i < n_lat)(functools.partial(attend, True))
    pl.when(i >= n_lat)(functools.partial(attend, False))


def _na(qkv, rpb, s_len, c_len):
    bsz, t, _ = qkv.shape
    width = NA_HEADS * NA_DH
    rows = s_len // GRID_W
    n_lat = s_len // TOK
    n_steps = t // TOK
    assert rows >= NA_WIN and rows % NA_QROWS == 0 and c_len % TOK == 0 and s_len % c_len == 0
    u0, pat, valid, d_row = _na_plan(rows)
    bias = _na_bias_table(rpb, valid, d_row)
    ctx_steps = np.zeros(n_steps - n_lat, np.int32)
    u0 = jnp.asarray(np.concatenate([u0, ctx_steps]))
    pat = jnp.asarray(np.concatenate([pat, ctx_steps]))
    tok_spec = pl.BlockSpec((1, TOK, width), lambda b, i, u0_ref, pat_ref: (b, i, 0))
    ctx_blk = s_len // c_len
    return pl.pallas_call(
        functools.partial(_na_kernel, n_lat=n_lat),
        grid_spec=pltpu.PrefetchScalarGridSpec(
            num_scalar_prefetch=2,
            grid=(bsz, n_steps),
            in_specs=[
                tok_spec,
                pl.BlockSpec((1, s_len, width), lambda b, i, u0_ref, pat_ref: (b, 0, 1),
                             pipeline_mode=pl.Buffered(1)),
                pl.BlockSpec((1, s_len, width), lambda b, i, u0_ref, pat_ref: (b, 0, 2),
                             pipeline_mode=pl.Buffered(1)),
                pl.BlockSpec((1, c_len, width), lambda b, i, u0_ref, pat_ref: (b, ctx_blk, 1)),
                pl.BlockSpec((1, c_len, width), lambda b, i, u0_ref, pat_ref: (b, ctx_blk, 2)),
                pl.BlockSpec((1, NA_HEADS, NA_WIN // 2, NA_QROWS, GRID_W, 2 * GRID_W),
                             lambda b, i, u0_ref, pat_ref: (pat_ref[i], 0, 0, 0, 0, 0)),
            ],
            out_specs=tok_spec,
        ),
        out_shape=jax.ShapeDtypeStruct((bsz, t, width), F32),
        compiler_params=_cparams(("parallel", "arbitrary")),
        name="na",
    )(u0, pat, qkv, qkv, qkv, qkv, qkv, bias)


def _lru_tile_index(i, n_lat, n_tiles, direction):
    if direction == 0:
        return (i + n_lat) % n_tiles
    return n_tiles - 1 - i


def _lru_row_permutation():
    seg = TOK // LRU_SEG
    p = np.arange(TOK)
    src = (p % LRU_SEG) * seg + p // LRU_SEG
    return (src[:, None] == np.arange(TOK)[None, :]).astype(np.float32)


def _shift_rows(group, boundary_row, down):
    sub = lax.broadcasted_iota(jnp.int32, group.shape, 0)
    if down:
        return jnp.where(sub == 0, boundary_row, pltpu.roll(group, 1, axis=0))
    return jnp.where(sub == LRU_SEG - 1, boundary_row, pltpu.roll(group, LRU_SEG - 1, axis=0))


def _lru_kernel(xf_ref, prevf_ref, nextf_ref, xb_ref, prevb_ref, nextb_ref, cw_ref, cb_ref, wg_ref, bg_ref, lam_ref,
                perm_ref, of_ref, ob_ref, carry_s, *, n_lat, n_tiles, ctx_sub):
    i = pl.program_id(1)
    ch = xf_ref.shape[-1]
    halo = prevf_ref.shape[1]
    seg = TOK // LRU_SEG
    grp = LRU_SEG

    @pl.when(i == 0)
    def _():
        carry_s[...] = jnp.zeros_like(carry_s)

    perm = perm_ref[...]
    cw = cw_ref[...]

    def sub_tile(direction, r, n_sub, state):
        x_ref, prev_ref, next_ref, o_ref = ((xf_ref, prevf_ref, nextf_ref, of_ref) if direction == 0 else
                                            (xb_ref, prevb_ref, nextb_ref, ob_ref))
        tile = _lru_tile_index(i, n_lat, n_tiles, direction)
        seq_first = jnp.logical_or(tile == 0, tile == n_lat)
        seq_last = jnp.logical_or(tile == n_lat - 1, tile == n_tiles - 1)
        rows = slice(r * TOK, (r + 1) * TOK)
        x = _dot(perm, x_ref[0, rows])
        if r == 0:
            prev = jnp.where(seq_first, 0.0, prev_ref[0].astype(F32))
        else:
            prev = x_ref[0, r * TOK - halo:r * TOK].astype(F32)
        if r == n_sub - 1:
            nxt = jnp.where(seq_last, 0.0, next_ref[0].astype(F32))
        else:
            nxt = x_ref[0, (r + 1) * TOK:(r + 1) * TOK + halo].astype(F32)
        wrap_m1 = _shift_rows(x[(seg - 1) * grp:], prev[halo - 1:halo], down=True)
        wrap_m2 = _shift_rows(x[(seg - 2) * grp:(seg - 1) * grp], prev[halo - 2:halo - 1], down=True)
        wrap_p1 = _shift_rows(x[:grp], nxt[0:1], down=False)
        x_m1 = jnp.concatenate([wrap_m1, x[:-grp]], axis=0)
        x_m2 = jnp.concatenate([wrap_m2, wrap_m1, x[:-2 * grp]], axis=0)
        x_p1 = jnp.concatenate([x[grp:], wrap_p1], axis=0)
        xc = cw[0:1] * x_m2 + cw[1:2] * x_m1 + cw[2:3] * x + cw[3:4] * x_p1 + cb_ref[...]

        pre = _dot(xc.astype(BF16), wg_ref[direction]) + bg_ref[direction]
        gates = 0.5 * jnp.tanh(0.5 * pre) + 0.5
        r_gate = gates[:, :ch]
        i_gate = gates[:, ch:]
        log_a = (-LRU_C) * r_gate * _softplus(-lam_ref[direction])
        a = jnp.exp(log_a)
        bb = jnp.sqrt(1.0 - a * a) * (i_gate * xc)

        steps = range(seg) if direction == 0 else range(seg - 1, -1, -1)
        h = jnp.zeros((grp, ch), F32)
        cum = jnp.ones((grp, ch), F32)
        h_loc = [None] * seg
        cum_loc = [None] * seg
        for j in steps:
            a_j = a[j * grp:(j + 1) * grp]
            h = a_j * h + bb[j * grp:(j + 1) * grp]
            cum = cum * a_j
            h_loc[j] = h
            cum_loc[j] = cum

        entering = [None] * LRU_SEG
        for s in (range(LRU_SEG) if direction == 0 else range(LRU_SEG - 1, -1, -1)):
            entering[s] = state
            state = h[s:s + 1] + cum[s:s + 1] * state
        entering = jnp.concatenate(entering, axis=0)
        o_ref[0, rows] = jnp.concatenate([h_loc[j] + cum_loc[j] * entering for j in range(seg)], axis=0)
        return state

    def run(n_sub):
        states = [carry_s[0], carry_s[1]]
        for r in range(n_sub):
            states[0] = sub_tile(0, r, n_sub, states[0])
            states[1] = sub_tile(1, n_sub - 1 - r, n_sub, states[1])
        carry_s[0] = states[0]
        carry_s[1] = states[1]

    n_blk = xf_ref.shape[1] // TOK
    if ctx_sub == n_blk:
        run(n_blk)
    else:
        pl.when(i == 0)(lambda: run(ctx_sub))
        pl.when(i > 0)(lambda: run(n_blk))


def _lru(xg, conv_w, conv_b, w_gate, b_gate, lam, s_len):
    bsz, t, two_ch = xg.shape
    ch = two_ch // 2
    n_lat, n_tiles, ctx_sub = _scan_blocks(s_len, t - s_len, MTOK)
    halo = 16
    per_tile = MTOK // halo
    tiles = [functools.partial(_lru_tile_index, n_lat=n_lat, n_tiles=n_tiles, direction=d) for d in range(2)]

    def stream_specs(tile_of):
        return [pl.BlockSpec((1, MTOK, ch), lambda b, i: (b, tile_of(i), 0)),
                pl.BlockSpec((1, halo, ch), lambda b, i: (b, jnp.maximum(tile_of(i) * per_tile - 1, 0), 0)),
                pl.BlockSpec((1, halo, ch),
                             lambda b, i: (b, jnp.minimum((tile_of(i) + 1) * per_tile, t // halo - 1), 0))]

    out_spec = lambda tile_of: pl.BlockSpec((1, MTOK, ch), lambda b, i: (b, tile_of(i), 0))
    return pl.pallas_call(
        functools.partial(_lru_kernel, n_lat=n_lat, n_tiles=n_tiles, ctx_sub=ctx_sub),
        grid=(bsz, n_tiles),
        in_specs=stream_specs(tiles[0]) + stream_specs(tiles[1]) + [
            _const_spec((4, ch)), _const_spec((1, ch)),
            _const_spec((2, ch, 2 * ch)), _const_spec((2, 1, 2 * ch)), _const_spec((2, 1, ch)),
            _const_spec((TOK, TOK)),
        ],
        out_specs=[out_spec(tiles[0]), out_spec(tiles[1])],
        out_shape=[jax.ShapeDtypeStruct((bsz, t, ch), F32)] * 2,
        scratch_shapes=[pltpu.VMEM((2, 1, ch), F32)],
        compiler_params=_cparams(("parallel", "arbitrary")),
        name="lru",
    )(xg, xg, xg, xg, xg, xg, conv_w, conv_b.reshape(1, ch), w_gate, b_gate, lam.reshape(2, 1, ch),
      jnp.asarray(_lru_row_permutation(), BF16))


def _block_diag(w):
    heads, n, _ = w.shape
    eye = jnp.eye(heads, dtype=w.dtype)
    return (eye[:, None, :, None] * w[:, :, None, :]).reshape(heads * n, heads * n)


def _gla_kernel(qf_ref, lrf_ref, qb_ref, lrb_ref, wa_ref, ba_ref, trif_ref, trib_ref, csum_ref, of_ref, ob_ref,
                state_s, *, ctx_sub):
    key = GLA_HEADS * GLA_DK
    val = GLA_HEADS * GLA_DV
    nchunk = TOK // GLA_CHUNK
    step = pl.program_id(1)

    @pl.when(step == 0)
    def _():
        state_s[...] = jnp.zeros_like(state_s)

    chunks = [slice(c * GLA_CHUNK, (c + 1) * GLA_CHUNK) for c in range(nchunk)]

    def prepare(direction, rows):
        qkvg_ref, lr_ref, tri_ref = (qf_ref, lrf_ref, trif_ref) if direction == 0 else (qb_ref, lrb_ref, trib_ref)
        tri = tri_ref[...]
        ends = [(c + 1) * GLA_CHUNK - 1 if direction == 0 else c * GLA_CHUNK for c in range(nchunk)]
        z = qkvg_ref[0, rows]
        q = z[:, :key] * (GLA_DK ** -0.5)
        k = z[:, key:2 * key]
        v = z[:, 2 * key:2 * key + val].astype(BF16)
        lr = lr_ref[0, rows][:, direction * GLA_RANK:(direction + 1) * GLA_RANK]
        logit = _dot(lr.astype(BF16), wa_ref[direction].astype(BF16)) + ba_ref[direction]
        log_a = (jnp.minimum(logit, 0.0) - jnp.log1p(jnp.exp(-jnp.abs(logit)))) / GLA_TAU

        g_hi, g_lo = _split_hi_lo(log_a)
        b = _dot(tri, g_hi) + _dot(tri, g_lo)
        q_in = (q * jnp.exp(b)).astype(BF16)
        k_in = (k * jnp.exp(-b)).astype(BF16)
        b_end = jnp.concatenate([jnp.broadcast_to(b[e:e + 1], (GLA_CHUNK, key)) for e in ends], axis=0)
        k_out_t = (k * jnp.exp(b_end - b)).T.astype(BF16)
        gt_hi, gt_lo = _split_hi_lo(log_a.T)
        decay = jnp.exp(_dot(gt_hi, csum_ref[...]) + _dot(gt_lo, csum_ref[...]))
        causal = tri > 0
        intra, update = [], []
        for h in range(GLA_HEADS):
            ks = slice(h * GLA_DK, (h + 1) * GLA_DK)
            v_h = v[:, h * GLA_DV:(h + 1) * GLA_DV]
            att = jnp.where(causal, _dot_nt(q_in[:, ks], k_in[:, ks]), 0.0)
            intra.append(_dot(att.astype(BF16), v_h))
            update.append([_dot(k_out_t[ks, rc], v_h[rc]) for rc in chunks])
        return q_in, decay, intra, update

    def recur(direction, rows, prepared, states):
        o_ref = of_ref if direction == 0 else ob_ref
        q_in, decay, intra, update = prepared
        chunk_order = range(nchunk) if direction == 0 else range(nchunk - 1, -1, -1)
        new_states = []
        for h in range(GLA_HEADS):
            ks = slice(h * GLA_DK, (h + 1) * GLA_DK)
            state = states[h]
            inter = [None] * nchunk
            for c in chunk_order:
                inter[c] = _dot(q_in[chunks[c], ks], state.astype(BF16))
                state = decay[ks, c * GLA_DV:(c + 1) * GLA_DV] * state + update[h][c]
            new_states.append(state)
            o_ref[0, rows, h * GLA_DV:(h + 1) * GLA_DV] = intra[h] + jnp.concatenate(inter, axis=0)
        return new_states

    def run(n_sub):
        subs = [slice(r * TOK, (r + 1) * TOK) for r in range(n_sub)]
        states = [[state_s[d, h] for h in range(GLA_HEADS)] for d in range(2)]
        work = [(d, subs[r] if d == 0 else subs[n_sub - 1 - r]) for r in range(n_sub) for d in range(2)]
        prepared = [prepare(d, rows) for d, rows in work]
        for (d, rows), prep in zip(work, prepared):
            states[d] = recur(d, rows, prep, states[d])
        for d in range(2):
            for h in range(GLA_HEADS):
                state_s[d, h] = states[d][h]

    n_blk = qf_ref.shape[1] // TOK
    if ctx_sub == n_blk:
        run(n_blk)
    else:
        pl.when(step == 0)(lambda: run(ctx_sub))
        pl.when(step > 0)(lambda: run(n_blk))


def _scan_blocks(s_len, c_len, blk):
    assert s_len % blk == 0 and c_len % TOK == 0 and c_len <= blk
    return s_len // blk, s_len // blk + 1, c_len // TOK


def _gla(qkvg, lr, wa2, ba, s_len):
    bsz, t, width = qkvg.shape
    key = GLA_HEADS * GLA_DK
    val = GLA_HEADS * GLA_DV
    n_lat, n_tiles, ctx_sub = _scan_blocks(s_len, t - s_len, MTOK)
    tiles = [functools.partial(_lru_tile_index, n_lat=n_lat, n_tiles=n_tiles, direction=d) for d in range(2)]
    pos = np.arange(TOK)
    same = (pos[:, None] // GLA_CHUNK) == (pos[None, :] // GLA_CHUNK)
    tri = [jnp.asarray(same & m, BF16) for m in (pos[None, :] <= pos[:, None], pos[None, :] >= pos[:, None])]
    chunk_sum = jnp.asarray((pos[:, None] // GLA_CHUNK) == (np.arange(TOK // GLA_CHUNK * GLA_DV)[None, :] // GLA_DV), BF16)
    blk = lambda width, tile_of: pl.BlockSpec((1, MTOK, width), lambda b, i: (b, tile_of(i), 0))
    return pl.pallas_call(
        functools.partial(_gla_kernel, ctx_sub=ctx_sub),
        grid=(bsz, n_tiles),
        in_specs=[
            blk(width, tiles[0]), blk(lr.shape[-1], tiles[0]), blk(width, tiles[1]), blk(lr.shape[-1], tiles[1]),
            _const_spec((2, GLA_RANK, key)), _const_spec((2, 1, key)),
            _const_spec((TOK, TOK)), _const_spec((TOK, TOK)), _const_spec(chunk_sum.shape),
        ],
        out_specs=[blk(val, tiles[0]), blk(val, tiles[1])],
        out_shape=[jax.ShapeDtypeStruct((bsz, t, val), F32)] * 2,
        scratch_shapes=[pltpu.VMEM((2, GLA_HEADS, GLA_DK, GLA_DV), F32)],
        compiler_params=_cparams(("parallel", "arbitrary")),
        name="gla",
    )(qkvg, lr, qkvg, lr, wa2, ba.reshape(2, 1, key), tri[0], tri[1], chunk_sum)


def _rope_tables(s_len, c_len, reps):
    quarter = GQA_DH // 4
    inv = ROPE_THETA ** (-np.arange(quarter, dtype=np.float32) / quarter)
    t = np.arange(s_len)
    ang_r = (t // GRID_W).astype(np.float32)[:, None] * inv[None, :]
    ang_c = (t % GRID_W).astype(np.float32)[:, None] * inv[None, :]
    cos = np.concatenate([np.cos(ang_r)] * 2 + [np.cos(ang_c)] * 2, axis=1)
    sin = np.concatenate([-np.sin(ang_r), np.sin(ang_r), -np.sin(ang_c), np.sin(ang_c)], axis=1)
    cos = np.concatenate([cos, np.ones((c_len, GQA_DH), np.float32)], axis=0)
    sin = np.concatenate([sin, np.zeros((c_len, GQA_DH), np.float32)], axis=0)
    return (jnp.asarray(np.tile(cos, (1, reps)), F32), jnp.asarray(np.tile(sin, (1, reps)), F32))


def _rope_swap(x):
    lane = lax.broadcasted_iota(jnp.int32, x.shape, 1)
    return jnp.where(lane % 32 < 16, pltpu.roll(x, 128 - 16, axis=1), pltpu.roll(x, 16, axis=1))


def _norm_rope(x, nw, cos, sin, gmean):
    hi, lo = _split_hi_lo(x * x)
    ms = _dot(hi, gmean) + _dot(lo, gmean)
    y = x * lax.rsqrt(ms + EPS) * nw
    slabs = [_rope_swap(y[:, j:j + 128]) for j in range(0, y.shape[1], 128)]
    swapped = slabs[0] if len(slabs) == 1 else jnp.concatenate(slabs, axis=1)
    return y * cos + swapped * sin


def _group_mean_matrix(n, group):
    idx = np.arange(n) // group
    return jnp.asarray((idx[:, None] == idx[None, :]) / group, BF16)


def _inproj_odd_kernel(h_ref, nw_ref, mod_ref, w_ref, qw_ref, kw_ref, cos_ref, sin_ref, gq_ref, gk_ref,
                       qkvg_ref, lr_ref, q_ref, k_ref, v_ref, *, gla_w):
    qd = GQA_HEADS * GQA_DH
    kd = GQA_KV_HEADS * GQA_DH
    mod = mod_ref[0]
    u = _rms(h_ref[0], nw_ref[...]) * (1.0 + mod[1:2]) + mod[0:1]
    z = _dot(u.astype(BF16), w_ref[...])
    qkvg_ref[0] = z[:, :gla_w]
    lr_ref[0] = z[:, gla_w + qd + 2 * kd:]
    cos_k = cos_ref[...]
    sin_k = sin_ref[...]
    reps = qd // kd
    cos_q = jnp.concatenate([cos_k] * reps, axis=1)
    sin_q = jnp.concatenate([sin_k] * reps, axis=1)
    q = _norm_rope(z[:, gla_w:gla_w + qd], qw_ref[...], cos_q, sin_q, gq_ref[...]) * (LOG2E * GQA_DH ** -0.5)
    k = _norm_rope(z[:, gla_w + qd:gla_w + qd + kd], kw_ref[...], cos_k, sin_k, gk_ref[...])
    v = z[:, gla_w + qd + kd:gla_w + qd + 2 * kd]
    q_ref[0] = q.astype(BF16)
    ones = jnp.ones((z.shape[0], LANES - GQA_DH), BF16)
    for h in range(GQA_KV_HEADS):
        k_ref[0, h] = k[:, h * GQA_DH:(h + 1) * GQA_DH].astype(BF16)
        v_ref[0, h] = jnp.concatenate([v[:, h * GQA_DH:(h + 1) * GQA_DH].astype(BF16), ones], axis=1)


def _inproj_odd(h, norm_w, mods, w, q_norm_w, k_norm_w, s_len, c_len, gla_w):
    bsz, t, d = h.shape
    n = w.shape[1]
    qd = GQA_HEADS * GQA_DH
    kd = GQA_KV_HEADS * GQA_DH
    n_lr = n - gla_w - qd - 2 * kd
    cos, sin = _rope_tables(s_len, c_len, GQA_KV_HEADS)
    tok_spec = lambda width: pl.BlockSpec((1, DTOK, width), lambda b, i: (b, i, 0))
    head_spec = lambda width: pl.BlockSpec((1, GQA_KV_HEADS, DTOK, width), lambda b, i: (b, 0, i, 0))
    return pl.pallas_call(
        functools.partial(_inproj_odd_kernel, gla_w=gla_w),
        grid=(bsz, pl.cdiv(t, DTOK)),
        in_specs=[
            tok_spec(d), _const_spec((1, d)), _mod_spec(d, s_len // DTOK), _const_spec((d, n)),
            _const_spec((1, qd)), _const_spec((1, kd)),
            pl.BlockSpec((DTOK, kd), lambda b, i: (i, 0)),
            pl.BlockSpec((DTOK, kd), lambda b, i: (i, 0)),
            _const_spec((qd, qd)), _const_spec((kd, kd)),
        ],
        out_specs=[tok_spec(gla_w), tok_spec(n_lr), tok_spec(qd), head_spec(GQA_DH), head_spec(LANES)],
        out_shape=[jax.ShapeDtypeStruct((bsz, t, gla_w), F32),
                   jax.ShapeDtypeStruct((bsz, t, n_lr), F32),
                   jax.ShapeDtypeStruct((bsz, t, qd), BF16),
                   jax.ShapeDtypeStruct((bsz, GQA_KV_HEADS, t, GQA_DH), BF16),
                   jax.ShapeDtypeStruct((bsz, GQA_KV_HEADS, t, LANES), BF16)],
        compiler_params=_cparams(("parallel", "parallel")),
        name="inproj_odd",
    )(h, norm_w.reshape(1, d), mods, w,
      jnp.tile(q_norm_w, GQA_HEADS).reshape(1, qd), jnp.tile(k_norm_w, GQA_KV_HEADS).reshape(1, kd),
      cos, sin, _group_mean_matrix(qd, GQA_DH), _group_mean_matrix(kd, GQA_DH))


def _flash_kernel(q_ref, k_ref, v_ref, o_ref, m_s, acc_s, *, group, tk, sub):
    tq = q_ref.shape[1]
    n_kv = k_ref.shape[2] // tk
    units = [(g, r) for g in range(group) for r in range(0, tq, sub)]
    m_s[...] = jnp.full_like(m_s, -jnp.inf)
    acc_s[...] = jnp.zeros_like(acc_s)

    def kv_step(j, carry):
        rows_k = pl.ds(pl.multiple_of(j * tk, tk), tk)
        k = k_ref[0, 0, rows_k, :]
        v = v_ref[0, 0, rows_k, :]

        def scores(u):
            g, r = units[u]
            return _dot_nt(q_ref[0, r:r + sub, g * GQA_DH:(g + 1) * GQA_DH], k)

        s_next = scores(0)
        for u in range(len(units)):
            tiles = _lane_tiles(s_next)
            if u + 1 < len(units):
                s_next = scores(u + 1)
            m_prev = m_s[u]
            m_new = jnp.maximum(m_prev, jnp.max(functools.reduce(jnp.maximum, tiles), axis=-1, keepdims=True))
            p = [jnp.exp2(t - m_new).astype(BF16) for t in tiles]
            acc_s[u] = jnp.exp2(m_prev - m_new) * acc_s[u] + _dot(jnp.concatenate(p, axis=1), v)
            m_s[u] = m_new
        return carry

    lax.fori_loop(0, n_kv, kv_step, 0)
    for u, (g, r) in enumerate(units):
        acc = acc_s[u]
        o_ref[0, r:r + sub, g * GQA_DH:(g + 1) * GQA_DH] = acc[:, :GQA_DH] / acc[:, GQA_DH:GQA_DH + 1]


KV_TILE_MAX = 3072
FLASH_SUB = 512


def _kv_tile(t):
    return max(n for n in range(LANES, KV_TILE_MAX + 1, LANES) if t % n == 0)


def _flash(q, k, v, s_len, tq, tk):
    bsz, _, t, dh = k.shape
    sub = FLASH_SUB
    assert t % tk == 0 and s_len % tq == 0 and tq % sub == 0
    group = GQA_HEADS // GQA_KV_HEADS
    n_units = group * (tq // sub)
    return pl.pallas_call(
        functools.partial(_flash_kernel, group=group, tk=tk, sub=sub),
        grid=(bsz, GQA_KV_HEADS, s_len // tq),
        in_specs=[
            pl.BlockSpec((1, tq, group * dh), lambda b, h, i: (b, i, h)),
            pl.BlockSpec((1, 1, t, dh), lambda b, h, i: (b, h, 0, 0)),
            pl.BlockSpec((1, 1, t, LANES), lambda b, h, i: (b, h, 0, 0)),
        ],
        out_specs=pl.BlockSpec((1, tq, group * dh), lambda b, h, i: (b, i, h)),
        out_shape=jax.ShapeDtypeStruct((bsz, s_len, GQA_HEADS * dh), F32),
        scratch_shapes=[pltpu.VMEM((n_units, sub, LANES), F32), pltpu.VMEM((n_units, sub, LANES), F32)],
        compiler_params=_cparams(("parallel", "parallel", "arbitrary")),
        name="gqa_flash",
    )(q, k, v)


def kernel(x, c, ctx, c_ctx, norm1_w, norm2_w, w_mod, b_mod, w_ff1, w_ff2,
           w_in_even, na_rpb, lru_conv_w, lru_conv_b, lru_wa, lru_ba, lru_wx, lru_bx, lru_lambda, w_out_even,
           w_in_odd, gla_wa2, gla_ba, gla_norm_w, gqa_q_norm_w, gqa_k_norm_w, w_out_odd, final_norm_w):
    bsz, s_len, d = x.shape
    c_len = ctx.shape[1]
    depth = w_mod.shape[0]
    assert s_len % TOK == 0 and c_len % TOK == 0 and depth == 2

    pad = (-(bsz + 1)) % 8
    cvec = jnp.concatenate([c_ctx[None], c, jnp.zeros((pad, d), F32)], axis=0)
    mods = _modulation(cvec, w_mod, b_mod)
    h = (x, ctx)
    t = s_len + c_len

    na_w = NA_HEADS * NA_DH
    ch = lru_lambda.shape[-1]
    col_scale = jnp.where(jnp.arange(w_in_even.shape[-1]) < na_w, LOG2E * NA_DH ** -0.5, 1.0)
    qkv, xg = _inproj(h, norm1_w[0], mods[0], (w_in_even[0] * col_scale).astype(BF16),
                      ((0, 3 * na_w), (3 * na_w, 3 * na_w + 2 * ch)), (BF16, BF16), s_len, t)
    na_out = _na(qkv, na_rpb[0], s_len, c_len)
    w_gate = jnp.stack([jnp.concatenate([_block_diag(lru_wa[0, dr]), _block_diag(lru_wx[0, dr])], axis=1)
                        for dr in range(2)]).astype(BF16)
    b_gate = jnp.concatenate([lru_ba[0], lru_bx[0]], axis=-1).reshape(2, 1, 2 * ch)
    lru_f, lru_b = _lru(xg, lru_conv_w[0], lru_conv_b[0], w_gate, b_gate, lru_lambda[0], s_len)
    mixes = (_mix_plain(na_out), _mix_lru(lru_f, lru_b, xg))
    w_ff1_bf, w_ff2_bf = _to_bf16(w_ff1), _to_bf16(w_ff2)
    h = _mlp(h, mixes, mods[0], w_out_even[0].astype(BF16), norm2_w[0],
             w_ff1_bf, w_ff2_bf, 0, final_norm_w, s_len, s_len + c_len, final=False)

    key = GLA_HEADS * GLA_DK
    val = GLA_HEADS * GLA_DV
    gla_w = 2 * key + 2 * val
    w_in = w_in_odd[0]
    w_in = jnp.concatenate([w_in[:, :gla_w], w_in[:, gla_w + 2 * GLA_RANK:], w_in[:, gla_w:gla_w + 2 * GLA_RANK]],
                           axis=1).astype(BF16)
    qkvg, lr, q, k, v = _inproj_odd(h, norm1_w[1], mods[1], w_in, gqa_q_norm_w[0], gqa_k_norm_w[0],
                                    s_len, c_len, gla_w)
    gla_f, gla_b = _gla(qkvg, lr, gla_wa2[0], gla_ba[0], s_len)
    gqa_out = _flash(q, k, v, s_len, tq=2 * FLASH_SUB, tk=_kv_tile(t))
    mixes = (_mix_gla(gla_f, gla_b, qkvg, gla_norm_w[0]), _mix_plain(gqa_out))
    return _mlp(h, mixes, mods[1], w_out_odd[0].astype(BF16), norm2_w[1],
                w_ff1_bf, w_ff2_bf, 1, final_norm_w, s_len, s_len, final=True)
```

```python
import functools

import numpy as np
import jax
import jax.numpy as jnp
from jax import lax
from jax.experimental import pallas as pl
from jax.experimental.pallas import tpu as pltpu

F32 = jnp.float32
BF16 = jnp.bfloat16

EPS = 1e-6
GRID_W = 64
LANES = 128
N_MOD = 6
TOK = 256
DTOK = 512
MTOK = 512
NEG = -1e30

NA_HEADS = 8
NA_DH = 64
NA_KH = 8
NA_KW = 16
LRU_C = 8.0
LRU_SEG = 8
GLA_HEADS = 4
GLA_DK = 64
GLA_DV = 128
GLA_RANK = 16
GLA_TAU = 16.0
GLA_CHUNK = 64
GQA_HEADS = 8
GQA_KV_HEADS = 2
GQA_DH = 64
ROPE_THETA = 10000.0
LOG2E = 1.4426950408889634

VMEM_LIMIT = 56 * 1024 * 1024


def _cparams(sem):
    return pltpu.CompilerParams(dimension_semantics=sem, vmem_limit_bytes=VMEM_LIMIT)


def _const_spec(shape):
    nd = len(shape)
    return pl.BlockSpec(shape, lambda *_: (0,) * nd, pipeline_mode=pl.Buffered(1))


def _layer_spec(shape, layer):
    nd = len(shape)
    return pl.BlockSpec((None,) + tuple(shape), lambda *_: (layer,) + (0,) * nd, pipeline_mode=pl.Buffered(1))


def _cast_kernel(x_ref, o_ref):
    o_ref[...] = x_ref[...].astype(o_ref.dtype)


def _to_bf16(w, row_blocks=2):
    layers, rows, cols = w.shape
    blk = rows // row_blocks
    spec = pl.BlockSpec((1, blk, cols), lambda l, i: (l, i, 0))
    return pl.pallas_call(
        _cast_kernel,
        grid=(layers, row_blocks),
        in_specs=[spec],
        out_specs=spec,
        out_shape=jax.ShapeDtypeStruct(w.shape, BF16),
        compiler_params=_cparams(("parallel", "parallel")),
        name="cast_bf16",
    )(w)


def _dot(a, b):
    return jnp.dot(a, b, preferred_element_type=F32)


def _dot_nt(a, b):
    return lax.dot_general(a, b, (((1,), (1,)), ((), ())), preferred_element_type=F32)


def _dot_tn(a, b):
    return lax.dot_general(a, b, (((0,), (0,)), ((), ())), preferred_element_type=F32)


def _split_hi_lo(x):
    hi = x.astype(BF16)
    lo = (x - hi.astype(F32)).astype(BF16)
    return hi, lo


def _rms(x, w):
    ms = jnp.mean(x * x, axis=-1, keepdims=True)
    return x * lax.rsqrt(ms + EPS) * w


def _softplus(z):
    return jnp.maximum(z, 0.0) + jnp.log1p(jnp.exp(-jnp.abs(z)))


def _mod_kernel(c_ref, w_ref, b_ref, o_ref):
    c = c_ref[...]
    s = c * jax.nn.sigmoid(c)
    s_hi, s_lo = _split_hi_lo(s)
    w_hi, w_lo = _split_hi_lo(w_ref[0])
    o_ref[0] = _dot(s_hi, w_hi) + (_dot(s_hi, w_lo) + _dot(s_lo, w_hi)) + b_ref[0]


def _modulation(cvec, w_mod, b_mod):
    depth, d, _ = w_mod.shape
    rows = cvec.shape[0]
    out = pl.pallas_call(
        _mod_kernel,
        grid=(depth, N_MOD),
        in_specs=[
            pl.BlockSpec((rows, d), lambda i, j: (0, 0)),
            pl.BlockSpec((1, d, d), lambda i, j: (i, 0, j)),
            pl.BlockSpec((1, 1, d), lambda i, j: (i, 0, j)),
        ],
        out_specs=pl.BlockSpec((1, rows, d), lambda i, j: (i, 0, j)),
        out_shape=jax.ShapeDtypeStruct((depth, rows, N_MOD * d), F32),
        compiler_params=_cparams(("arbitrary", "arbitrary")),
        name="modulation",
    )(cvec, w_mod, b_mod.reshape(depth, 1, N_MOD * d))
    return out.reshape(depth, rows, N_MOD, d)


def _mod_spec(d, n_lat_tiles):
    return pl.BlockSpec((1, N_MOD, d), lambda b, t: (jnp.where(t >= n_lat_tiles, 0, b + 1), 0, 0))


def _stream_specs(h, s_len):
    n_lat = s_len // DTOK
    if not isinstance(h, tuple):
        return [h], [pl.BlockSpec((1, DTOK, h.shape[-1]), lambda b, i: (b, i, 0))], 0
    lat, ctx = h
    c_len, d = ctx.shape[1:]
    assert c_len <= DTOK and lat.shape[1] == s_len
    return ([lat, ctx],
            [pl.BlockSpec((1, DTOK, d), lambda b, i: (b, jnp.minimum(i, n_lat - 1), 0)),
             pl.BlockSpec((1, c_len, d), lambda b, i: (b, 0, 0))], c_len)


def _per_stream(body, h_refs, n_lat_tiles, ctx_rows):
    if len(h_refs) == 1:
        body(h_refs[0], DTOK)
        return
    i = pl.program_id(1)
    pl.when(i < n_lat_tiles)(lambda: body(h_refs[0], DTOK))
    pl.when(i >= n_lat_tiles)(lambda: body(h_refs[1], ctx_rows))


def _inproj_kernel(*refs, splits, n_src, n_lat_tiles, ctx_rows):
    h_refs = refs[:n_src]
    nw_ref, mod_ref, w_ref = refs[n_src:n_src + 3]
    o_refs = refs[n_src + 3:]

    def body(h_ref, rows):
        mod = mod_ref[0]
        u = _rms(h_ref[0], nw_ref[...]) * (1.0 + mod[1:2]) + mod[0:1]
        z = _dot(u.astype(BF16), w_ref[...])
        for o_ref, (lo, hi) in zip(o_refs, splits):
            o_ref[0, :rows] = z[:, lo:hi].astype(o_ref.dtype)

    _per_stream(body, h_refs, n_lat_tiles, ctx_rows)


def _inproj(h, norm_w, mods, w, splits, dtypes, s_len, t):
    d, n = w.shape
    assert s_len % DTOK == 0
    srcs, src_specs, ctx_rows = _stream_specs(h, s_len)
    bsz = srcs[0].shape[0]
    return pl.pallas_call(
        functools.partial(_inproj_kernel, splits=splits, n_src=len(srcs), n_lat_tiles=s_len // DTOK,
                          ctx_rows=ctx_rows),
        grid=(bsz, pl.cdiv(t, DTOK)),
        in_specs=src_specs + [
            _const_spec((1, d)),
            _mod_spec(d, s_len // DTOK),
            _const_spec((d, n)),
        ],
        out_specs=[pl.BlockSpec((1, DTOK, hi - lo), lambda b, i: (b, i, 0)) for lo, hi in splits],
        out_shape=[jax.ShapeDtypeStruct((bsz, t, hi - lo), dt) for (lo, hi), dt in zip(splits, dtypes)],
        compiler_params=_cparams(("parallel", "parallel")),
        name="inproj",
    )(*srcs, norm_w.reshape(1, d), mods, w)


class _Mix:
    def __init__(self, arrays, specs, width, load):
        self.arrays, self.specs, self.width, self.load = arrays, specs, width, load


def _tok_spec(width, col_block=0):
    return pl.BlockSpec((1, DTOK, width), lambda b, i: (b, i, col_block))


def _mix_plain(a):
    return _Mix([a], [_tok_spec(a.shape[-1])], a.shape[-1], lambda refs, rows: refs[0][0, :rows].astype(BF16))


def _mix_gla(o_fwd, o_bwd, qkvg, norm_w):
    val = GLA_HEADS * GLA_DV

    def load(refs, rows):
        of_ref, ob_ref, gate_ref, nw_ref = refs
        o = of_ref[0, :rows] + ob_ref[0, :rows]
        gate = gate_ref[0, :rows]
        heads = [_rms(o[:, sl], nw_ref[...]) for sl in (slice(h * GLA_DV, (h + 1) * GLA_DV) for h in range(GLA_HEADS))]
        return (jnp.concatenate(heads, axis=1) * (gate * jax.nn.sigmoid(gate))).astype(BF16)

    return _Mix([o_fwd, o_bwd, qkvg, norm_w.reshape(1, GLA_DV)],
                [_tok_spec(val), _tok_spec(val), _tok_spec(val, col_block=2), _const_spec((1, GLA_DV))], val, load)


def _mix_lru(y_fwd, y_bwd, xg):
    ch = y_fwd.shape[-1]
    perm = _lru_row_permutation()

    def load(refs, rows):
        yf_ref, yb_ref, g_ref, perm_ref, unperm_ref = refs
        out = []
        for r in range(0, rows, TOK):
            y = yf_ref[0, r:r + TOK] + yb_ref[0, r:r + TOK]
            y = y * jax.nn.gelu(_dot(perm_ref[...], g_ref[0, r:r + TOK]))
            out.append(_dot(unperm_ref[...], y.astype(BF16)).astype(BF16))
        return out[0] if len(out) == 1 else jnp.concatenate(out, axis=0)

    return _Mix([y_fwd, y_bwd, xg, jnp.asarray(perm, BF16), jnp.asarray(perm.T, BF16)],
                [_tok_spec(ch), _tok_spec(ch), _tok_spec(ch, col_block=1), _const_spec((TOK, TOK)),
                 _const_spec((TOK, TOK))], ch, load)


def _mlp_kernel(*refs, ff_chunk, final, n_src, n_lat_tiles, ctx_rows, mixes):
    h_refs = refs[:n_src]
    pos = n_src
    mix_refs = []
    for n_arrays, _, _ in mixes:
        mix_refs.append(refs[pos:pos + n_arrays])
        pos += n_arrays
    mod_ref, wo_ref, nw_ref, w1_ref, w2_ref, fw_ref, o_ref = refs[pos:]

    def body(h_ref, rows):
        mod = mod_ref[0]
        y = None
        row0 = 0
        for (_, width, load), m_refs in zip(mixes, mix_refs):
            part = _dot(load(m_refs, rows), wo_ref[row0:row0 + width])
            y = part if y is None else y + part
            row0 += width
        h1 = h_ref[0] + mod[2:3] * y
        u = (_rms(h1, nw_ref[...]) * (1.0 + mod[4:5]) + mod[3:4]).astype(BF16)
        d_ff = w1_ref.shape[1]
        acc = jnp.zeros(h1.shape, F32)
        for c in range(d_ff // ff_chunk):
            sl = slice(c * ff_chunk, (c + 1) * ff_chunk)
            a = jnp.maximum(_dot(u, w1_ref[:, sl]), 0.0)
            acc = acc + _dot((a * a).astype(BF16), w2_ref[sl, :])
        h2 = h1 + mod[5:6] * acc
        if final:
            h2 = _rms(h2, fw_ref[...])
        o_ref[0, :rows] = h2

    _per_stream(body, h_refs, n_lat_tiles, ctx_rows)


def _mlp(h, mixes, mods, wo, norm_w, w1, w2, layer, final_w, s_len, out_rows, final):
    _, d, d_ff = w1.shape
    assert s_len % DTOK == 0 and wo.shape[0] == sum(m.width for m in mixes)
    srcs, src_specs, ctx_rows = _stream_specs(h, s_len)
    bsz = srcs[0].shape[0]
    return pl.pallas_call(
        functools.partial(_mlp_kernel, ff_chunk=512, final=final, n_src=len(srcs), n_lat_tiles=s_len // DTOK,
                          ctx_rows=ctx_rows, mixes=tuple((len(m.arrays), m.width, m.load) for m in mixes)),
        grid=(bsz, pl.cdiv(out_rows, DTOK)),
        in_specs=src_specs + [s for m in mixes for s in m.specs] + [
            _mod_spec(d, s_len // DTOK),
            _const_spec(wo.shape), _const_spec((1, d)),
            _layer_spec((d, d_ff), layer), _layer_spec((d_ff, d), layer), _const_spec((1, d)),
        ],
        out_specs=_tok_spec(d),
        out_shape=jax.ShapeDtypeStruct((bsz, out_rows, d), F32),
        compiler_params=_cparams(("parallel", "parallel")),
        name="mlp",
    )(*srcs, *[a for m in mixes for a in m.arrays], mods, wo, norm_w.reshape(1, d), w1, w2, final_w.reshape(1, d))


NA_QROWS = TOK // GRID_W
NA_WIN = NA_KH + NA_QROWS


def _na_plan(rows):
    n_blk = rows // NA_QROWS
    r = np.arange(rows).reshape(n_blk, NA_QROWS)
    start = np.clip(r - NA_KH // 2, 0, rows - NA_KH)
    u0 = np.clip(r[:, 0] - NA_KH // 2, 0, rows - NA_WIN)
    key = u0[:, None, None] + np.arange(NA_WIN)[None, None, :]
    valid = (key >= start[:, :, None]) & (key < start[:, :, None] + NA_KH)
    d_row = np.where(valid, key - r[:, :, None] + (NA_KH - 1), 0)
    flat = np.concatenate([valid.reshape(n_blk, -1), d_row.reshape(n_blk, -1)], axis=1)
    _, first, pat = np.unique(flat, axis=0, return_index=True, return_inverse=True)
    return u0.astype(np.int32), pat.reshape(-1).astype(np.int32), valid[first], d_row[first]


def _na_bias_table(rpb, valid, d_row):
    n_pat = valid.shape[0]
    heads = rpb.shape[0]
    pairs = NA_WIN // 2
    n_dc = 2 * NA_KW - 1
    cols = np.arange(GRID_W)
    col_start = np.clip(cols - NA_KW // 2, 0, GRID_W - NA_KW)
    col_valid = (cols[None, :] >= col_start[:, None]) & (cols[None, :] < col_start[:, None] + NA_KW)
    d_col = np.clip(cols[None, :] - cols[:, None], 1 - NA_KW, NA_KW - 1) + (NA_KW - 1)
    row_sel = (d_row[..., None] == np.arange(2 * NA_KH - 1)).astype(np.float32)
    col_sel = (d_col[None] == np.arange(n_dc)[:, None, None]).astype(np.float32)
    n_k = n_dc + 2
    pair_sel = np.zeros((2, n_k, GRID_W, 2, GRID_W), np.float32)
    for u1 in range(2):
        pair_sel[u1, :n_dc, :, u1, :] = col_sel
        pair_sel[u1, n_dc, :, u1, :] = ~col_valid
        pair_sel[u1, n_dc + 1, :, u1, :] = 1.0
    pair_sel = pair_sel.reshape(2 * n_k, GRID_W, 2 * GRID_W)
    hp = lax.Precision.HIGHEST
    by_row = jnp.einsum('hrc,piur->phiuc', rpb, row_sel, precision=hp) * LOG2E
    col_mask = jnp.full((n_pat, heads, NA_QROWS, NA_WIN, 1), NEG, F32)
    row_mask = jnp.broadcast_to(jnp.where(valid, 0.0, NEG)[:, None, :, :, None], col_mask.shape)
    by_row = jnp.concatenate([by_row, col_mask, row_mask], axis=-1)
    by_row = by_row.reshape(n_pat, heads, NA_QROWS, pairs, 2 * n_k).transpose(0, 1, 3, 2, 4)
    bias = lax.dot_general(by_row, pair_sel, (((4,), (0,)), ((), ())), precision=hp)
    return bias


def _softmax_tiles(tiles):
    m = jnp.max(functools.reduce(jnp.maximum, tiles), axis=-1, keepdims=True)
    p = [jnp.exp2(t - m) for t in tiles]
    denom = jnp.sum(functools.reduce(jnp.add, p), axis=-1, keepdims=True)
    return [t.astype(BF16) for t in p], denom


def _lane_tiles(x):
    return [x[:, j:j + LANES] for j in range(0, x.shape[1], LANES)]


def _na_kernel(u0_ref, pat_ref, q_ref, k_ref, v_ref, kc_ref, vc_ref, bias_ref, o_ref, *, n_lat):
    i = pl.program_id(1)
    win = NA_WIN * GRID_W
    n_loc = win // LANES

    def attend(local):
        q = q_ref[0]
        kc = kc_ref[0]
        vc = vc_ref[0]
        if local:
            row0 = pl.multiple_of(u0_ref[i] * GRID_W, GRID_W)
            kw = k_ref[0, pl.ds(row0, win), :]
            vw = v_ref[0, pl.ds(row0, win), :]

        def scores(h):
            sl = slice(h * NA_DH, (h + 1) * NA_DH)
            s_ctx = _lane_tiles(_dot_nt(q[:, sl], kc[:, sl]))
            if not local:
                return s_ctx
            s_loc = _lane_tiles(_dot_nt(q[:, sl], kw[:, sl]))
            return [t + bias_ref[0, h, j].reshape(TOK, LANES) for j, t in enumerate(s_loc)] + s_ctx

        ahead = 1
        pending = [scores(h) for h in range(ahead)]
        for h in range(NA_HEADS):
            sl = slice(h * NA_DH, (h + 1) * NA_DH)
            tiles = pending.pop(0)
            if h + ahead < NA_HEADS:
                pending.append(scores(h + ahead))
            p, denom = _softmax_tiles(tiles)
            v_all = jnp.concatenate([vw[:, sl], vc[:, sl]], axis=0) if local else vc[:, sl]
            o = _dot(jnp.concatenate(p, axis=1), v_all)
            o_ref[0, :, sl] = (o / denom).astype(o_ref.dtype)

    pl.when(i < n_lat)(functools.partial(attend, True))
    pl.when(i >= n_lat)(functools.partial(attend, False))


def _na(qkv, rpb, s_len, c_len):
    bsz, t, _ = qkv.shape
    width = NA_HEADS * NA_DH
    rows = s_len // GRID_W
    n_lat = s_len // TOK
    n_steps = t // TOK
    assert rows >= NA_WIN and rows % NA_QROWS == 0 and c_len % TOK == 0 and s_len % c_len == 0
    u0, pat, valid, d_row = _na_plan(rows)
    bias = _na_bias_table(rpb, valid, d_row)
    ctx_steps = np.zeros(n_steps - n_lat, np.int32)
    u0 = jnp.asarray(np.concatenate([u0, ctx_steps]))
    pat = jnp.asarray(np.concatenate([pat, ctx_steps]))
    tok_spec = pl.BlockSpec((1, TOK, width), lambda b, i, u0_ref, pat_ref: (b, i, 0))
    ctx_blk = s_len // c_len
    return pl.pallas_call(
        functools.partial(_na_kernel, n_lat=n_lat),
        grid_spec=pltpu.PrefetchScalarGridSpec(
            num_scalar_prefetch=2,
            grid=(bsz, n_steps),
            in_specs=[
                tok_spec,
                pl.BlockSpec((1, s_len, width), lambda b, i, u0_ref, pat_ref: (b, 0, 1),
                             pipeline_mode=pl.Buffered(1)),
                pl.BlockSpec((1, s_len, width), lambda b, i, u0_ref, pat_ref: (b, 0, 2),
                             pipeline_mode=pl.Buffered(1)),
                pl.BlockSpec((1, c_len, width), lambda b, i, u0_ref, pat_ref: (b, ctx_blk, 1)),
                pl.BlockSpec((1, c_len, width), lambda b, i, u0_ref, pat_ref: (b, ctx_blk, 2)),
                pl.BlockSpec((1, NA_HEADS, NA_WIN // 2, NA_QROWS, GRID_W, 2 * GRID_W),
                             lambda b, i, u0_ref, pat_ref: (pat_ref[i], 0, 0, 0, 0, 0)),
            ],
            out_specs=tok_spec,
        ),
        out_shape=jax.ShapeDtypeStruct((bsz, t, width), BF16),
        compiler_params=_cparams(("parallel", "arbitrary")),
        name="na",
    )(u0, pat, qkv, qkv, qkv, qkv, qkv, bias)


def _lru_tile_index(i, n_lat, n_tiles, direction):
    if direction == 0:
        return (i + n_lat) % n_tiles
    return n_tiles - 1 - i


def _lru_row_permutation():
    seg = TOK // LRU_SEG
    p = np.arange(TOK)
    src = (p % LRU_SEG) * seg + p // LRU_SEG
    return (src[:, None] == np.arange(TOK)[None, :]).astype(np.float32)


def _shift_rows(group, boundary_row, down):
    sub = lax.broadcasted_iota(jnp.int32, group.shape, 0)
    if down:
        return jnp.where(sub == 0, boundary_row, pltpu.roll(group, 1, axis=0))
    return jnp.where(sub == LRU_SEG - 1, boundary_row, pltpu.roll(group, LRU_SEG - 1, axis=0))


def _lru_kernel(xf_ref, prevf_ref, nextf_ref, xb_ref, prevb_ref, nextb_ref, cw_ref, cb_ref, wg_ref, bg_ref, lam_ref,
                perm_ref, of_ref, ob_ref, carry_s, *, n_lat, n_tiles, ctx_sub):
    i = pl.program_id(1)
    ch = xf_ref.shape[-1]
    halo = prevf_ref.shape[1]
    seg = TOK // LRU_SEG
    grp = LRU_SEG

    @pl.when(i == 0)
    def _():
        carry_s[...] = jnp.zeros_like(carry_s)

    perm = perm_ref[...]
    cw = cw_ref[...]

    def sub_tile(direction, r, n_sub, state):
        x_ref, prev_ref, next_ref, o_ref = ((xf_ref, prevf_ref, nextf_ref, of_ref) if direction == 0 else
                                            (xb_ref, prevb_ref, nextb_ref, ob_ref))
        tile = _lru_tile_index(i, n_lat, n_tiles, direction)
        seq_first = jnp.logical_or(tile == 0, tile == n_lat)
        seq_last = jnp.logical_or(tile == n_lat - 1, tile == n_tiles - 1)
        rows = slice(r * TOK, (r + 1) * TOK)
        x = _dot(perm, x_ref[0, rows])
        if r == 0:
            prev = jnp.where(seq_first, 0.0, prev_ref[0].astype(F32))
        else:
            prev = x_ref[0, r * TOK - halo:r * TOK].astype(F32)
        if r == n_sub - 1:
            nxt = jnp.where(seq_last, 0.0, next_ref[0].astype(F32))
        else:
            nxt = x_ref[0, (r + 1) * TOK:(r + 1) * TOK + halo].astype(F32)
        wrap_m1 = _shift_rows(x[(seg - 1) * grp:], prev[halo - 1:halo], down=True)
        wrap_m2 = _shift_rows(x[(seg - 2) * grp:(seg - 1) * grp], prev[halo - 2:halo - 1], down=True)
        wrap_p1 = _shift_rows(x[:grp], nxt[0:1], down=False)
        x_m1 = jnp.concatenate([wrap_m1, x[:-grp]], axis=0)
        x_m2 = jnp.concatenate([wrap_m2, wrap_m1, x[:-2 * grp]], axis=0)
        x_p1 = jnp.concatenate([x[grp:], wrap_p1], axis=0)
        xc = cw[0:1] * x_m2 + cw[1:2] * x_m1 + cw[2:3] * x + cw[3:4] * x_p1 + cb_ref[...]

        pre = _dot(xc.astype(BF16), wg_ref[direction]) + bg_ref[direction]
        gates = 0.5 * jnp.tanh(0.5 * pre) + 0.5
        r_gate = gates[:, :ch]
        i_gate = gates[:, ch:]
        log_a = (-LRU_C) * r_gate * _softplus(-lam_ref[direction])
        a = jnp.exp(log_a)
        bb = jnp.sqrt(1.0 - a * a) * (i_gate * xc)

        steps = range(seg) if direction == 0 else range(seg - 1, -1, -1)
        h = jnp.zeros((grp, ch), F32)
        cum = jnp.ones((grp, ch), F32)
        h_loc = [None] * seg
        cum_loc = [None] * seg
        for j in steps:
            a_j = a[j * grp:(j + 1) * grp]
            h = a_j * h + bb[j * grp:(j + 1) * grp]
            cum = cum * a_j
            h_loc[j] = h
            cum_loc[j] = cum

        entering = [None] * LRU_SEG
        for s in (range(LRU_SEG) if direction == 0 else range(LRU_SEG - 1, -1, -1)):
            entering[s] = state
            state = h[s:s + 1] + cum[s:s + 1] * state
        entering = jnp.concatenate(entering, axis=0)
        o_ref[0, rows] = jnp.concatenate([h_loc[j] + cum_loc[j] * entering for j in range(seg)], axis=0)
        return state

    def run(n_sub):
        states = [carry_s[0], carry_s[1]]
        for r in range(n_sub):
            states[0] = sub_tile(0, r, n_sub, states[0])
            states[1] = sub_tile(1, n_sub - 1 - r, n_sub, states[1])
        carry_s[0] = states[0]
        carry_s[1] = states[1]

    n_blk = xf_ref.shape[1] // TOK
    if ctx_sub == n_blk:
        run(n_blk)
    else:
        pl.when(i == 0)(lambda: run(ctx_sub))
        pl.when(i > 0)(lambda: run(n_blk))


def _lru(xg, conv_w, conv_b, w_gate, b_gate, lam, s_len):
    bsz, t, two_ch = xg.shape
    ch = two_ch // 2
    n_lat, n_tiles, ctx_sub = _scan_blocks(s_len, t - s_len, MTOK)
    halo = 16
    per_tile = MTOK // halo
    tiles = [functools.partial(_lru_tile_index, n_lat=n_lat, n_tiles=n_tiles, direction=d) for d in range(2)]

    def stream_specs(tile_of):
        return [pl.BlockSpec((1, MTOK, ch), lambda b, i: (b, tile_of(i), 0)),
                pl.BlockSpec((1, halo, ch), lambda b, i: (b, jnp.maximum(tile_of(i) * per_tile - 1, 0), 0)),
                pl.BlockSpec((1, halo, ch),
                             lambda b, i: (b, jnp.minimum((tile_of(i) + 1) * per_tile, t // halo - 1), 0))]

    out_spec = lambda tile_of: pl.BlockSpec((1, MTOK, ch), lambda b, i: (b, tile_of(i), 0))
    return pl.pallas_call(
        functools.partial(_lru_kernel, n_lat=n_lat, n_tiles=n_tiles, ctx_sub=ctx_sub),
        grid=(bsz, n_tiles),
        in_specs=stream_specs(tiles[0]) + stream_specs(tiles[1]) + [
            _const_spec((4, ch)), _const_spec((1, ch)),
            _const_spec((2, ch, 2 * ch)), _const_spec((2, 1, 2 * ch)), _const_spec((2, 1, ch)),
            _const_spec((TOK, TOK)),
        ],
        out_specs=[out_spec(tiles[0]), out_spec(tiles[1])],
        out_shape=[jax.ShapeDtypeStruct((bsz, t, ch), F32)] * 2,
        scratch_shapes=[pltpu.VMEM((2, 1, ch), F32)],
        compiler_params=_cparams(("parallel", "arbitrary")),
        name="lru",
    )(xg, xg, xg, xg, xg, xg, conv_w, conv_b.reshape(1, ch), w_gate, b_gate, lam.reshape(2, 1, ch),
      jnp.asarray(_lru_row_permutation(), BF16))


def _block_diag(w):
    heads, n, _ = w.shape
    eye = jnp.eye(heads, dtype=w.dtype)
    return (eye[:, None, :, None] * w[:, :, None, :]).reshape(heads * n, heads * n)


def _gla_kernel(qf_ref, lrf_ref, qb_ref, lrb_ref, wa_ref, ba_ref, trif_ref, trib_ref, csum_ref, of_ref, ob_ref,
                state_s, *, ctx_sub):
    key = GLA_HEADS * GLA_DK
    val = GLA_HEADS * GLA_DV
    nchunk = TOK // GLA_CHUNK
    step = pl.program_id(1)

    @pl.when(step == 0)
    def _():
        state_s[...] = jnp.zeros_like(state_s)

    chunks = [slice(c * GLA_CHUNK, (c + 1) * GLA_CHUNK) for c in range(nchunk)]

    def prepare(direction, rows):
        qkvg_ref, lr_ref, tri_ref = (qf_ref, lrf_ref, trif_ref) if direction == 0 else (qb_ref, lrb_ref, trib_ref)
        tri = tri_ref[...]
        ends = [(c + 1) * GLA_CHUNK - 1 if direction == 0 else c * GLA_CHUNK for c in range(nchunk)]
        z = qkvg_ref[0, rows]
        q = z[:, :key] * (GLA_DK ** -0.5)
        k = z[:, key:2 * key]
        v = z[:, 2 * key:2 * key + val].astype(BF16)
        lr = lr_ref[0, rows][:, direction * GLA_RANK:(direction + 1) * GLA_RANK]
        logit = _dot(lr.astype(BF16), wa_ref[direction].astype(BF16)) + ba_ref[direction]
        log_a = (jnp.minimum(logit, 0.0) - jnp.log1p(jnp.exp(-jnp.abs(logit)))) / GLA_TAU

        g_hi, g_lo = _split_hi_lo(log_a)
        b = _dot(tri, g_hi) + _dot(tri, g_lo)
        q_in = (q * jnp.exp(b)).astype(BF16)
        k_in = (k * jnp.exp(-b)).astype(BF16)
        b_end = jnp.concatenate([jnp.broadcast_to(b[e:e + 1], (GLA_CHUNK, key)) for e in ends], axis=0)
        k_out_t = (k * jnp.exp(b_end - b)).T.astype(BF16)
        gt_hi, gt_lo = _split_hi_lo(log_a.T)
        decay = jnp.exp(_dot(gt_hi, csum_ref[...]) + _dot(gt_lo, csum_ref[...]))
        causal = tri > 0
        intra, update = [], []
        for h in range(GLA_HEADS):
            ks = slice(h * GLA_DK, (h + 1) * GLA_DK)
            v_h = v[:, h * GLA_DV:(h + 1) * GLA_DV]
            att = jnp.where(causal, _dot_nt(q_in[:, ks], k_in[:, ks]), 0.0)
            intra.append(_dot(att.astype(BF16), v_h))
            update.append([_dot(k_out_t[ks, rc], v_h[rc]) for rc in chunks])
        return q_in, decay, intra, update

    def recur(direction, rows, prepared, states):
        o_ref = of_ref if direction == 0 else ob_ref
        q_in, decay, intra, update = prepared
        chunk_order = range(nchunk) if direction == 0 else range(nchunk - 1, -1, -1)
        new_states = []
        for h in range(GLA_HEADS):
            ks = slice(h * GLA_DK, (h + 1) * GLA_DK)
            state = states[h]
            inter = [None] * nchunk
            for c in chunk_order:
                inter[c] = _dot(q_in[chunks[c], ks], state.astype(BF16))
                state = decay[ks, c * GLA_DV:(c + 1) * GLA_DV] * state + update[h][c]
            new_states.append(state)
            o_ref[0, rows, h * GLA_DV:(h + 1) * GLA_DV] = intra[h] + jnp.concatenate(inter, axis=0)
        return new_states

    def run(n_sub):
        subs = [slice(r * TOK, (r + 1) * TOK) for r in range(n_sub)]
        states = [[state_s[d, h] for h in range(GLA_HEADS)] for d in range(2)]
        work = [(d, subs[r] if d == 0 else subs[n_sub - 1 - r]) for r in range(n_sub) for d in range(2)]
        prepared = [prepare(d, rows) for d, rows in work]
        for (d, rows), prep in zip(work, prepared):
            states[d] = recur(d, rows, prep, states[d])
        for d in range(2):
            for h in range(GLA_HEADS):
                state_s[d, h] = states[d][h]

    n_blk = qf_ref.shape[1] // TOK
    if ctx_sub == n_blk:
        run(n_blk)
    else:
        pl.when(step == 0)(lambda: run(ctx_sub))
        pl.when(step > 0)(lambda: run(n_blk))


def _scan_blocks(s_len, c_len, blk):
    assert s_len % blk == 0 and c_len % TOK == 0 and c_len <= blk
    return s_len // blk, s_len // blk + 1, c_len // TOK


def _gla(qkvg, lr, wa2, ba, s_len):
    bsz, t, width = qkvg.shape
    key = GLA_HEADS * GLA_DK
    val = GLA_HEADS * GLA_DV
    n_lat, n_tiles, ctx_sub = _scan_blocks(s_len, t - s_len, MTOK)
    tiles = [functools.partial(_lru_tile_index, n_lat=n_lat, n_tiles=n_tiles, direction=d) for d in range(2)]
    pos = np.arange(TOK)
    same = (pos[:, None] // GLA_CHUNK) == (pos[None, :] // GLA_CHUNK)
    tri = [jnp.asarray(same & m, BF16) for m in (pos[None, :] <= pos[:, None], pos[None, :] >= pos[:, None])]
    chunk_sum = jnp.asarray((pos[:, None] // GLA_CHUNK) == (np.arange(TOK // GLA_CHUNK * GLA_DV)[None, :] // GLA_DV), BF16)
    blk = lambda width, tile_of: pl.BlockSpec((1, MTOK, width), lambda b, i: (b, tile_of(i), 0))
    return pl.pallas_call(
        functools.partial(_gla_kernel, ctx_sub=ctx_sub),
        grid=(bsz, n_tiles),
        in_specs=[
            blk(width, tiles[0]), blk(lr.shape[-1], tiles[0]), blk(width, tiles[1]), blk(lr.shape[-1], tiles[1]),
            _const_spec((2, GLA_RANK, key)), _const_spec((2, 1, key)),
            _const_spec((TOK, TOK)), _const_spec((TOK, TOK)), _const_spec(chunk_sum.shape),
        ],
        out_specs=[blk(val, tiles[0]), blk(val, tiles[1])],
        out_shape=[jax.ShapeDtypeStruct((bsz, t, val), F32)] * 2,
        scratch_shapes=[pltpu.VMEM((2, GLA_HEADS, GLA_DK, GLA_DV), F32)],
        compiler_params=_cparams(("parallel", "arbitrary")),
        name="gla",
    )(qkvg, lr, qkvg, lr, wa2, ba.reshape(2, 1, key), tri[0], tri[1], chunk_sum)


def _rope_tables(s_len, c_len, reps):
    quarter = GQA_DH // 4
    inv = ROPE_THETA ** (-np.arange(quarter, dtype=np.float32) / quarter)
    t = np.arange(s_len)
    ang_r = (t // GRID_W).astype(np.float32)[:, None] * inv[None, :]
    ang_c = (t % GRID_W).astype(np.float32)[:, None] * inv[None, :]
    cos = np.concatenate([np.cos(ang_r)] * 2 + [np.cos(ang_c)] * 2, axis=1)
    sin = np.concatenate([-np.sin(ang_r), np.sin(ang_r), -np.sin(ang_c), np.sin(ang_c)], axis=1)
    cos = np.concatenate([cos, np.ones((c_len, GQA_DH), np.float32)], axis=0)
    sin = np.concatenate([sin, np.zeros((c_len, GQA_DH), np.float32)], axis=0)
    return (jnp.asarray(np.tile(cos, (1, reps)), F32), jnp.asarray(np.tile(sin, (1, reps)), F32))


def _rope_swap(x):
    lane = lax.broadcasted_iota(jnp.int32, x.shape, 1)
    return jnp.where(lane % 32 < 16, pltpu.roll(x, 128 - 16, axis=1), pltpu.roll(x, 16, axis=1))


def _norm_rope(x, nw, cos, sin, gmean):
    hi, lo = _split_hi_lo(x * x)
    ms = _dot(hi, gmean) + _dot(lo, gmean)
    y = x * lax.rsqrt(ms + EPS) * nw
    slabs = [_rope_swap(y[:, j:j + 128]) for j in range(0, y.shape[1], 128)]
    swapped = slabs[0] if len(slabs) == 1 else jnp.concatenate(slabs, axis=1)
    return y * cos + swapped * sin


def _group_mean_matrix(n, group):
    idx = np.arange(n) // group
    return jnp.asarray((idx[:, None] == idx[None, :]) / group, BF16)


def _inproj_odd_kernel(h_ref, nw_ref, mod_ref, w_ref, qw_ref, kw_ref, cos_ref, sin_ref, gq_ref, gk_ref,
                       qkvg_ref, lr_ref, q_ref, k_ref, v_ref, *, gla_w):
    qd = GQA_HEADS * GQA_DH
    kd = GQA_KV_HEADS * GQA_DH
    mod = mod_ref[0]
    u = _rms(h_ref[0], nw_ref[...]) * (1.0 + mod[1:2]) + mod[0:1]
    z = _dot(u.astype(BF16), w_ref[...])
    qkvg_ref[0] = z[:, :gla_w]
    lr_ref[0] = z[:, gla_w + qd + 2 * kd:]
    cos_k = cos_ref[...]
    sin_k = sin_ref[...]
    reps = qd // kd
    cos_q = jnp.concatenate([cos_k] * reps, axis=1)
    sin_q = jnp.concatenate([sin_k] * reps, axis=1)
    q = _norm_rope(z[:, gla_w:gla_w + qd], qw_ref[...], cos_q, sin_q, gq_ref[...]) * (LOG2E * GQA_DH ** -0.5)
    k = _norm_rope(z[:, gla_w + qd:gla_w + qd + kd], kw_ref[...], cos_k, sin_k, gk_ref[...])
    v = z[:, gla_w + qd + kd:gla_w + qd + 2 * kd]
    q_ref[0] = q.astype(BF16)
    ones = jnp.ones((z.shape[0], LANES - GQA_DH), BF16)
    for h in range(GQA_KV_HEADS):
        k_ref[0, h] = k[:, h * GQA_DH:(h + 1) * GQA_DH].astype(BF16)
        v_ref[0, h] = jnp.concatenate([v[:, h * GQA_DH:(h + 1) * GQA_DH].astype(BF16), ones], axis=1)


def _inproj_odd(h, norm_w, mods, w, q_norm_w, k_norm_w, s_len, c_len, gla_w):
    bsz, t, d = h.shape
    n = w.shape[1]
    qd = GQA_HEADS * GQA_DH
    kd = GQA_KV_HEADS * GQA_DH
    n_lr = n - gla_w - qd - 2 * kd
    cos, sin = _rope_tables(s_len, c_len, GQA_KV_HEADS)
    tok_spec = lambda width: pl.BlockSpec((1, DTOK, width), lambda b, i: (b, i, 0))
    head_spec = lambda width: pl.BlockSpec((1, GQA_KV_HEADS, DTOK, width), lambda b, i: (b, 0, i, 0))
    return pl.pallas_call(
        functools.partial(_inproj_odd_kernel, gla_w=gla_w),
        grid=(bsz, pl.cdiv(t, DTOK)),
        in_specs=[
            tok_spec(d), _const_spec((1, d)), _mod_spec(d, s_len // DTOK), _const_spec((d, n)),
            _const_spec((1, qd)), _const_spec((1, kd)),
            pl.BlockSpec((DTOK, kd), lambda b, i: (i, 0)),
            pl.BlockSpec((DTOK, kd), lambda b, i: (i, 0)),
            _const_spec((qd, qd)), _const_spec((kd, kd)),
        ],
        out_specs=[tok_spec(gla_w), tok_spec(n_lr), tok_spec(qd), head_spec(GQA_DH), head_spec(LANES)],
        out_shape=[jax.ShapeDtypeStruct((bsz, t, gla_w), F32),
                   jax.ShapeDtypeStruct((bsz, t, n_lr), F32),
                   jax.ShapeDtypeStruct((bsz, t, qd), BF16),
                   jax.ShapeDtypeStruct((bsz, GQA_KV_HEADS, t, GQA_DH), BF16),
                   jax.ShapeDtypeStruct((bsz, GQA_KV_HEADS, t, LANES), BF16)],
        compiler_params=_cparams(("parallel", "parallel")),
        name="inproj_odd",
    )(h, norm_w.reshape(1, d), mods, w,
      jnp.tile(q_norm_w, GQA_HEADS).reshape(1, qd), jnp.tile(k_norm_w, GQA_KV_HEADS).reshape(1, kd),
      cos, sin, _group_mean_matrix(qd, GQA_DH), _group_mean_matrix(kd, GQA_DH))


def _flash_kernel(q_ref, k_ref, v_ref, o_ref, m_s, acc_s, *, group, tk, sub):
    tq = q_ref.shape[1]
    n_kv = k_ref.shape[2] // tk
    units = [(g, r) for g in range(group) for r in range(0, tq, sub)]
    m_s[...] = jnp.full_like(m_s, -jnp.inf)
    acc_s[...] = jnp.zeros_like(acc_s)

    def kv_step(j, carry):
        rows_k = pl.ds(pl.multiple_of(j * tk, tk), tk)
        k = k_ref[0, 0, rows_k, :]
        v = v_ref[0, 0, rows_k, :]

        def scores(u):
            g, r = units[u]
            return _dot_nt(q_ref[0, r:r + sub, g * GQA_DH:(g + 1) * GQA_DH], k)

        s_next = scores(0)
        for u in range(len(units)):
            tiles = _lane_tiles(s_next)
            if u + 1 < len(units):
                s_next = scores(u + 1)
            m_prev = m_s[u]
            m_new = jnp.maximum(m_prev, jnp.max(functools.reduce(jnp.maximum, tiles), axis=-1, keepdims=True))
            p = [jnp.exp2(t - m_new).astype(BF16) for t in tiles]
            acc_s[u] = jnp.exp2(m_prev - m_new) * acc_s[u] + _dot(jnp.concatenate(p, axis=1), v)
            m_s[u] = m_new
        return carry

    lax.fori_loop(0, n_kv, kv_step, 0)
    for u, (g, r) in enumerate(units):
        acc = acc_s[u]
        out = acc[:, :GQA_DH] / acc[:, GQA_DH:GQA_DH + 1]
        o_ref[0, r:r + sub, g * GQA_DH:(g + 1) * GQA_DH] = out.astype(o_ref.dtype)


KV_TILE_MAX = 3072
FLASH_SUB = 512


def _kv_tile(t):
    return max(n for n in range(LANES, KV_TILE_MAX + 1, LANES) if t % n == 0)


def _flash(q, k, v, s_len, tq, tk):
    bsz, _, t, dh = k.shape
    sub = FLASH_SUB
    assert t % tk == 0 and s_len % tq == 0 and tq % sub == 0
    group = GQA_HEADS // GQA_KV_HEADS
    n_units = group * (tq // sub)
    return pl.pallas_call(
        functools.partial(_flash_kernel, group=group, tk=tk, sub=sub),
        grid=(bsz, GQA_KV_HEADS, s_len // tq),
        in_specs=[
            pl.BlockSpec((1, tq, group * dh), lambda b, h, i: (b, i, h)),
            pl.BlockSpec((1, 1, t, dh), lambda b, h, i: (b, h, 0, 0)),
            pl.BlockSpec((1, 1, t, LANES), lambda b, h, i: (b, h, 0, 0)),
        ],
        out_specs=pl.BlockSpec((1, tq, group * dh), lambda b, h, i: (b, i, h)),
        out_shape=jax.ShapeDtypeStruct((bsz, s_len, GQA_HEADS * dh), BF16),
        scratch_shapes=[pltpu.VMEM((n_units, sub, LANES), F32), pltpu.VMEM((n_units, sub, LANES), F32)],
        compiler_params=_cparams(("parallel", "parallel", "arbitrary")),
        name="gqa_flash",
    )(q, k, v)


def kernel(x, c, ctx, c_ctx, norm1_w, norm2_w, w_mod, b_mod, w_ff1, w_ff2,
           w_in_even, na_rpb, lru_conv_w, lru_conv_b, lru_wa, lru_ba, lru_wx, lru_bx, lru_lambda, w_out_even,
           w_in_odd, gla_wa2, gla_ba, gla_norm_w, gqa_q_norm_w, gqa_k_norm_w, w_out_odd, final_norm_w):
    bsz, s_len, d = x.shape
    c_len = ctx.shape[1]
    depth = w_mod.shape[0]
    assert s_len % TOK == 0 and c_len % TOK == 0 and depth == 2

    pad = (-(bsz + 1)) % 8
    cvec = jnp.concatenate([c_ctx[None], c, jnp.zeros((pad, d), F32)], axis=0)
    mods = _modulation(cvec, w_mod, b_mod)
    h = (x, ctx)
    t = s_len + c_len

    na_w = NA_HEADS * NA_DH
    ch = lru_lambda.shape[-1]
    col_scale = jnp.where(jnp.arange(w_in_even.shape[-1]) < na_w, LOG2E * NA_DH ** -0.5, 1.0)
    qkv, xg = _inproj(h, norm1_w[0], mods[0], (w_in_even[0] * col_scale).astype(BF16),
                      ((0, 3 * na_w), (3 * na_w, 3 * na_w + 2 * ch)), (BF16, BF16), s_len, t)
    na_out = _na(qkv, na_rpb[0], s_len, c_len)
    w_gate = jnp.stack([jnp.concatenate([_block_diag(lru_wa[0, dr]), _block_diag(lru_wx[0, dr])], axis=1)
                        for dr in range(2)]).astype(BF16)
    b_gate = jnp.concatenate([lru_ba[0], lru_bx[0]], axis=-1).reshape(2, 1, 2 * ch)
    lru_f, lru_b = _lru(xg, lru_conv_w[0], lru_conv_b[0], w_gate, b_gate, lru_lambda[0], s_len)
    mixes = (_mix_plain(na_out), _mix_lru(lru_f, lru_b, xg))
    w_ff1_bf, w_ff2_bf = _to_bf16(w_ff1), _to_bf16(w_ff2)
    h = _mlp(h, mixes, mods[0], w_out_even[0].astype(BF16), norm2_w[0],
             w_ff1_bf, w_ff2_bf, 0, final_norm_w, s_len, s_len + c_len, final=False)

    key = GLA_HEADS * GLA_DK
    val = GLA_HEADS * GLA_DV
    gla_w = 2 * key + 2 * val
    w_in = w_in_odd[0]
    w_in = jnp.concatenate([w_in[:, :gla_w], w_in[:, gla_w + 2 * GLA_RANK:], w_in[:, gla_w:gla_w + 2 * GLA_RANK]],
                           axis=1).astype(BF16)
    qkvg, lr, q, k, v = _inproj_odd(h, norm1_w[1], mods[1], w_in, gqa_q_norm_w[0], gqa_k_norm_w[0],
                                    s_len, c_len, gla_w)
    gla_f, gla_b = _gla(qkvg, lr, gla_wa2[0], gla_ba[0], s_len)
    gqa_out = _flash(q, k, v, s_len, tq=2 * FLASH_SUB, tk=_kv_tile(t))
    mixes = (_mix_gla(gla_f, gla_b, qkvg, gla_norm_w[0]), _mix_plain(gqa_out))
    return _mlp(h, mixes, mods[1], w_out_odd[0].astype(BF16), norm2_w[1],
                w_ff1_bf, w_ff2_bf, 1, final_norm_w, s_len, s_len, final=True)
```

```python
import functools

import numpy as np
import jax
import jax.numpy as jnp
from jax import lax
from jax.experimental import pallas as pl
from jax.experimental.pallas import tpu as pltpu

F32 = jnp.float32
BF16 = jnp.bfloat16

EPS = 1e-6
GRID_W = 64
LANES = 128
N_MOD = 6
TOK = 256
DTOK = 512
MTOK = 512
FF_CHUNK = 512
NEG = -1e30

NA_HEADS = 8
NA_DH = 64
NA_KH = 8
NA_KW = 16
LRU_C = 8.0
LRU_SEG = 8
GLA_HEADS = 4
GLA_DK = 64
GLA_DV = 128
GLA_RANK = 16
GLA_TAU = 16.0
GLA_CHUNK = 64
GQA_HEADS = 8
GQA_KV_HEADS = 2
GQA_DH = 64
ROPE_THETA = 10000.0
LOG2E = 1.4426950408889634

VMEM_LIMIT = 56 * 1024 * 1024


def _cparams(sem):
    return pltpu.CompilerParams(dimension_semantics=sem, vmem_limit_bytes=VMEM_LIMIT)


def _const_spec(shape):
    nd = len(shape)
    return pl.BlockSpec(shape, lambda *_: (0,) * nd, pipeline_mode=pl.Buffered(1))


def _layer_spec(shape, layer):
    nd = len(shape)
    return pl.BlockSpec((None,) + tuple(shape), lambda *_: (layer,) + (0,) * nd, pipeline_mode=pl.Buffered(1))


def _cast_kernel(x_ref, o_ref):
    o_ref[...] = x_ref[...].astype(o_ref.dtype)


def _to_bf16(w, row_blocks=2):
    layers, rows, cols = w.shape
    blk = rows // row_blocks
    spec = pl.BlockSpec((1, blk, cols), lambda l, i: (l, i, 0))
    return pl.pallas_call(
        _cast_kernel,
        grid=(layers, row_blocks),
        in_specs=[spec],
        out_specs=spec,
        out_shape=jax.ShapeDtypeStruct(w.shape, BF16),
        compiler_params=_cparams(("parallel", "parallel")),
        name="cast_bf16",
    )(w)


def _dot(a, b):
    return jnp.dot(a, b, preferred_element_type=F32)


def _dot_nt(a, b):
    return lax.dot_general(a, b, (((1,), (1,)), ((), ())), preferred_element_type=F32)


def _split_hi_lo(x):
    hi = x.astype(BF16)
    lo = (x - hi.astype(F32)).astype(BF16)
    return hi, lo


def _rms(x, w):
    ms = jnp.mean(x * x, axis=-1, keepdims=True)
    return x * lax.rsqrt(ms + EPS) * w


def _softplus(z):
    return jnp.maximum(z, 0.0) + jnp.log1p(jnp.exp(-jnp.abs(z)))


def _mod_kernel(c_ref, w_ref, b_ref, o_ref):
    c = c_ref[...]
    s = c * jax.nn.sigmoid(c)
    s_hi, s_lo = _split_hi_lo(s)
    w_hi, w_lo = _split_hi_lo(w_ref[0])
    o_ref[0] = _dot(s_hi, w_hi) + (_dot(s_hi, w_lo) + _dot(s_lo, w_hi)) + b_ref[0]


def _modulation(cvec, w_mod, b_mod):
    depth, d, _ = w_mod.shape
    rows = cvec.shape[0]
    out = pl.pallas_call(
        _mod_kernel,
        grid=(depth, N_MOD),
        in_specs=[
            pl.BlockSpec((rows, d), lambda i, j: (0, 0)),
            pl.BlockSpec((1, d, d), lambda i, j: (i, 0, j)),
            pl.BlockSpec((1, 1, d), lambda i, j: (i, 0, j)),
        ],
        out_specs=pl.BlockSpec((1, rows, d), lambda i, j: (i, 0, j)),
        out_shape=jax.ShapeDtypeStruct((depth, rows, N_MOD * d), F32),
        compiler_params=_cparams(("arbitrary", "arbitrary")),
        name="modulation",
    )(cvec, w_mod, b_mod.reshape(depth, 1, N_MOD * d))
    return out.reshape(depth, rows, N_MOD, d)


def _mod_spec(d, n_lat_tiles):
    return pl.BlockSpec((1, N_MOD, d), lambda b, t: (jnp.where(t >= n_lat_tiles, 0, b + 1), 0, 0))


def _stream_specs(h, s_len):
    n_lat = s_len // DTOK
    if not isinstance(h, tuple):
        return [h], [pl.BlockSpec((1, DTOK, h.shape[-1]), lambda b, i: (b, i, 0))], 0
    lat, ctx = h
    c_len, d = ctx.shape[1:]
    assert c_len <= DTOK and lat.shape[1] == s_len
    return ([lat, ctx],
            [pl.BlockSpec((1, DTOK, d), lambda b, i: (b, jnp.minimum(i, n_lat - 1), 0)),
             pl.BlockSpec((1, c_len, d), lambda b, i: (b, 0, 0))], c_len)


def _per_stream(body, h_refs, n_lat_tiles, ctx_rows):
    if len(h_refs) == 1:
        body(h_refs[0], DTOK)
        return
    i = pl.program_id(1)
    pl.when(i < n_lat_tiles)(lambda: body(h_refs[0], DTOK))
    pl.when(i >= n_lat_tiles)(lambda: body(h_refs[1], ctx_rows))


def _inproj_kernel(*refs, splits, n_src, n_lat_tiles, ctx_rows):
    h_refs = refs[:n_src]
    nw_ref, mod_ref, w_ref = refs[n_src:n_src + 3]
    o_refs = refs[n_src + 3:]

    def body(h_ref, rows):
        mod = mod_ref[0]
        u = _rms(h_ref[0], nw_ref[...]) * (1.0 + mod[1:2]) + mod[0:1]
        z = _dot(u.astype(BF16), w_ref[...])
        for o_ref, (lo, hi) in zip(o_refs, splits):
            o_ref[0, :rows] = z[:, lo:hi].astype(o_ref.dtype)

    _per_stream(body, h_refs, n_lat_tiles, ctx_rows)


def _inproj(h, norm_w, mods, w, splits, dtypes, s_len, t):
    d, n = w.shape
    assert s_len % DTOK == 0
    srcs, src_specs, ctx_rows = _stream_specs(h, s_len)
    bsz = srcs[0].shape[0]
    return pl.pallas_call(
        functools.partial(_inproj_kernel, splits=splits, n_src=len(srcs), n_lat_tiles=s_len // DTOK,
                          ctx_rows=ctx_rows),
        grid=(bsz, pl.cdiv(t, DTOK)),
        in_specs=src_specs + [
            _const_spec((1, d)),
            _mod_spec(d, s_len // DTOK),
            _const_spec((d, n)),
        ],
        out_specs=[pl.BlockSpec((1, DTOK, hi - lo), lambda b, i: (b, i, 0)) for lo, hi in splits],
        out_shape=[jax.ShapeDtypeStruct((bsz, t, hi - lo), dt) for (lo, hi), dt in zip(splits, dtypes)],
        compiler_params=_cparams(("parallel", "parallel")),
        name="inproj",
    )(*srcs, norm_w.reshape(1, d), mods, w)


class _Mix:
    def __init__(self, arrays, specs, width, load):
        self.arrays, self.specs, self.width, self.load = arrays, specs, width, load


def _tok_spec(width, col_block=0):
    return pl.BlockSpec((1, DTOK, width), lambda b, i: (b, i, col_block))


def _mix_plain(a):
    return _Mix([a], [_tok_spec(a.shape[-1])], a.shape[-1], lambda refs, rows: refs[0][0, :rows].astype(BF16))


def _mix_gla(o_fwd, o_bwd, qkvg, norm_w):
    val = GLA_HEADS * GLA_DV

    def load(refs, rows):
        of_ref, ob_ref, gate_ref, nw_ref = refs
        o = of_ref[0, :rows] + ob_ref[0, :rows]
        gate = gate_ref[0, :rows]
        heads = [_rms(o[:, sl], nw_ref[...]) for sl in (slice(h * GLA_DV, (h + 1) * GLA_DV) for h in range(GLA_HEADS))]
        return (jnp.concatenate(heads, axis=1) * (gate * jax.nn.sigmoid(gate))).astype(BF16)

    return _Mix([o_fwd, o_bwd, qkvg, norm_w.reshape(1, GLA_DV)],
                [_tok_spec(val), _tok_spec(val), _tok_spec(val, col_block=2), _const_spec((1, GLA_DV))], val, load)


def _mix_lru(y_fwd, y_bwd, xg):
    ch = y_fwd.shape[-1]
    perm = _lru_row_permutation()

    def load(refs, rows):
        yf_ref, yb_ref, g_ref, perm_ref, unperm_ref = refs
        out = []
        for r in range(0, rows, TOK):
            y = yf_ref[0, r:r + TOK] + yb_ref[0, r:r + TOK]
            y = y * jax.nn.gelu(_dot(perm_ref[...], g_ref[0, r:r + TOK]))
            out.append(_dot(unperm_ref[...], y.astype(BF16)).astype(BF16))
        return out[0] if len(out) == 1 else jnp.concatenate(out, axis=0)

    return _Mix([y_fwd, y_bwd, xg, jnp.asarray(perm, BF16), jnp.asarray(perm.T, BF16)],
                [_tok_spec(ch), _tok_spec(ch), _tok_spec(ch, col_block=1), _const_spec((TOK, TOK)),
                 _const_spec((TOK, TOK))], ch, load)


def _mlp_kernel(*refs, ff_chunk, final, n_src, n_lat_tiles, ctx_rows, mixes):
    h_refs = refs[:n_src]
    pos = n_src
    mix_refs = []
    for n_arrays, _, _ in mixes:
        mix_refs.append(refs[pos:pos + n_arrays])
        pos += n_arrays
    mod_ref, wo_ref, nw_ref, w1_ref, w2_ref, fw_ref, o_ref = refs[pos:]

    def body(h_ref, rows):
        mod = mod_ref[0]
        y = None
        row0 = 0
        for (_, width, load), m_refs in zip(mixes, mix_refs):
            part = _dot(load(m_refs, rows), wo_ref[row0:row0 + width])
            y = part if y is None else y + part
            row0 += width
        h1 = h_ref[0] + mod[2:3] * y
        u = (_rms(h1, nw_ref[...]) * (1.0 + mod[4:5]) + mod[3:4]).astype(BF16)
        d_ff = w1_ref.shape[1]
        acc = jnp.zeros(h1.shape, F32)
        for c in range(d_ff // ff_chunk):
            sl = slice(c * ff_chunk, (c + 1) * ff_chunk)
            a = jnp.maximum(_dot(u, w1_ref[:, sl]), 0.0)
            acc = acc + _dot((a * a).astype(BF16), w2_ref[sl, :])
        h2 = h1 + mod[5:6] * acc
        if final:
            h2 = _rms(h2, fw_ref[...])
        o_ref[0, :rows] = h2

    _per_stream(body, h_refs, n_lat_tiles, ctx_rows)


def _mlp(h, mixes, mods, wo, norm_w, w1, w2, layer, final_w, s_len, out_rows, final):
    _, d, d_ff = w1.shape
    assert s_len % DTOK == 0 and wo.shape[0] == sum(m.width for m in mixes)
    srcs, src_specs, ctx_rows = _stream_specs(h, s_len)
    bsz = srcs[0].shape[0]
    return pl.pallas_call(
        functools.partial(_mlp_kernel, ff_chunk=FF_CHUNK, final=final, n_src=len(srcs), n_lat_tiles=s_len // DTOK,
                          ctx_rows=ctx_rows, mixes=tuple((len(m.arrays), m.width, m.load) for m in mixes)),
        grid=(bsz, pl.cdiv(out_rows, DTOK)),
        in_specs=src_specs + [s for m in mixes for s in m.specs] + [
            _mod_spec(d, s_len // DTOK),
            _const_spec(wo.shape), _const_spec((1, d)),
            _layer_spec((d, d_ff), layer), _layer_spec((d_ff, d), layer), _const_spec((1, d)),
        ],
        out_specs=_tok_spec(d),
        out_shape=jax.ShapeDtypeStruct((bsz, out_rows, d), F32),
        compiler_params=_cparams(("parallel", "parallel")),
        name="mlp",
    )(*srcs, *[a for m in mixes for a in m.arrays], mods, wo, norm_w.reshape(1, d), w1, w2, final_w.reshape(1, d))


NA_QROWS = TOK // GRID_W
NA_WIN = NA_KH + NA_QROWS


def _na_plan(rows):
    n_blk = rows // NA_QROWS
    r = np.arange(rows).reshape(n_blk, NA_QROWS)
    start = np.clip(r - NA_KH // 2, 0, rows - NA_KH)
    u0 = np.clip(r[:, 0] - NA_KH // 2, 0, rows - NA_WIN)
    key = u0[:, None, None] + np.arange(NA_WIN)[None, None, :]
    valid = (key >= start[:, :, None]) & (key < start[:, :, None] + NA_KH)
    d_row = np.where(valid, key - r[:, :, None] + (NA_KH - 1), 0)
    flat = np.concatenate([valid.reshape(n_blk, -1), d_row.reshape(n_blk, -1)], axis=1)
    _, first, pat = np.unique(flat, axis=0, return_index=True, return_inverse=True)
    return u0.astype(np.int32), pat.reshape(-1).astype(np.int32), valid[first], d_row[first]


def _na_bias_table(rpb, valid, d_row):
    n_pat = valid.shape[0]
    heads = rpb.shape[0]
    pairs = NA_WIN // 2
    n_dc = 2 * NA_KW - 1
    cols = np.arange(GRID_W)
    col_start = np.clip(cols - NA_KW // 2, 0, GRID_W - NA_KW)
    col_valid = (cols[None, :] >= col_start[:, None]) & (cols[None, :] < col_start[:, None] + NA_KW)
    d_col = np.clip(cols[None, :] - cols[:, None], 1 - NA_KW, NA_KW - 1) + (NA_KW - 1)
    row_sel = (d_row[..., None] == np.arange(2 * NA_KH - 1)).astype(np.float32)
    col_sel = (d_col[None] == np.arange(n_dc)[:, None, None]).astype(np.float32)
    n_k = n_dc + 2
    pair_sel = np.zeros((2, n_k, GRID_W, 2, GRID_W), np.float32)
    for u1 in range(2):
        pair_sel[u1, :n_dc, :, u1, :] = col_sel
        pair_sel[u1, n_dc, :, u1, :] = ~col_valid
        pair_sel[u1, n_dc + 1, :, u1, :] = 1.0
    pair_sel = pair_sel.reshape(2 * n_k, GRID_W, 2 * GRID_W)
    hp = lax.Precision.HIGHEST
    by_row = jnp.einsum('hrc,piur->phiuc', rpb, row_sel, precision=hp) * LOG2E
    col_mask = jnp.full((n_pat, heads, NA_QROWS, NA_WIN, 1), NEG, F32)
    row_mask = jnp.broadcast_to(jnp.where(valid, 0.0, NEG)[:, None, :, :, None], col_mask.shape)
    by_row = jnp.concatenate([by_row, col_mask, row_mask], axis=-1)
    by_row = by_row.reshape(n_pat, heads, NA_QROWS, pairs, 2 * n_k).transpose(0, 1, 3, 2, 4)
    bias = lax.dot_general(by_row, pair_sel, (((4,), (0,)), ((), ())), precision=hp)
    return bias


def _softmax_tiles(tiles):
    m = jnp.max(functools.reduce(jnp.maximum, tiles), axis=-1, keepdims=True)
    p = [jnp.exp2(t - m) for t in tiles]
    denom = jnp.sum(functools.reduce(jnp.add, p), axis=-1, keepdims=True)
    return [t.astype(BF16) for t in p], denom


def _lane_tiles(x):
    return [x[:, j:j + LANES] for j in range(0, x.shape[1], LANES)]


def _na_kernel(u0_ref, pat_ref, q_ref, k_ref, v_ref, kc_ref, vc_ref, bias_ref, o_ref, *, n_lat):
    i = pl.program_id(1)
    win = NA_WIN * GRID_W
    n_loc = win // LANES

    def attend(local):
        q = q_ref[0]
        kc = kc_ref[0]
        vc = vc_ref[0]
        if local:
            row0 = pl.multiple_of(u0_ref[i] * GRID_W, GRID_W)
            kw = k_ref[0, pl.ds(row0, win), :]
            vw = v_ref[0, pl.ds(row0, win), :]

        def scores(h):
            sl = slice(h * NA_DH, (h + 1) * NA_DH)
            s_ctx = _lane_tiles(_dot_nt(q[:, sl], kc[:, sl]))
            if not local:
                return s_ctx
            s_loc = _lane_tiles(_dot_nt(q[:, sl], kw[:, sl]))
            return [t + bias_ref[0, h, j].reshape(TOK, LANES) for j, t in enumerate(s_loc)] + s_ctx

        ahead = 1
        pending = [scores(h) for h in range(ahead)]
        for h in range(NA_HEADS):
            sl = slice(h * NA_DH, (h + 1) * NA_DH)
            tiles = pending.pop(0)
            if h + ahead < NA_HEADS:
                pending.append(scores(h + ahead))
            p, denom = _softmax_tiles(tiles)
            v_all = jnp.concatenate([vw[:, sl], vc[:, sl]], axis=0) if local else vc[:, sl]
            o = _dot(jnp.concatenate(p, axis=1), v_all)
            o_ref[0, :, sl] = (o / denom).astype(o_ref.dtype)

    pl.when(i < n_lat)(functools.partial(attend, True))
    pl.when(i >= n_lat)(functools.partial(attend, False))


def _na(qkv, rpb, s_len, c_len):
    bsz, t, _ = qkv.shape
    width = NA_HEADS * NA_DH
    rows = s_len // GRID_W
    n_lat = s_len // TOK
    n_steps = t // TOK
    assert rows >= NA_WIN and rows % NA_QROWS == 0 and c_len % TOK == 0 and s_len % c_len == 0
    u0, pat, valid, d_row = _na_plan(rows)
    bias = _na_bias_table(rpb, valid, d_row)
    ctx_steps = np.zeros(n_steps - n_lat, np.int32)
    u0 = jnp.asarray(np.concatenate([u0, ctx_steps]))
    pat = jnp.asarray(np.concatenate([pat, ctx_steps]))
    tok_spec = pl.BlockSpec((1, TOK, width), lambda b, i, u0_ref, pat_ref: (b, i, 0))
    ctx_blk = s_len // c_len
    return pl.pallas_call(
        functools.partial(_na_kernel, n_lat=n_lat),
        grid_spec=pltpu.PrefetchScalarGridSpec(
            num_scalar_prefetch=2,
            grid=(bsz, n_steps),
            in_specs=[
                tok_spec,
                pl.BlockSpec((1, s_len, width), lambda b, i, u0_ref, pat_ref: (b, 0, 1),
                             pipeline_mode=pl.Buffered(1)),
                pl.BlockSpec((1, s_len, width), lambda b, i, u0_ref, pat_ref: (b, 0, 2),
                             pipeline_mode=pl.Buffered(1)),
                pl.BlockSpec((1, c_len, width), lambda b, i, u0_ref, pat_ref: (b, ctx_blk, 1)),
                pl.BlockSpec((1, c_len, width), lambda b, i, u0_ref, pat_ref: (b, ctx_blk, 2)),
                pl.BlockSpec((1, NA_HEADS, NA_WIN // 2, NA_QROWS, GRID_W, 2 * GRID_W),
                             lambda b, i, u0_ref, pat_ref: (pat_ref[i], 0, 0, 0, 0, 0)),
            ],
            out_specs=tok_spec,
        ),
        out_shape=jax.ShapeDtypeStruct((bsz, t, width), BF16),
        compiler_params=_cparams(("parallel", "arbitrary")),
        name="na",
    )(u0, pat, qkv, qkv, qkv, qkv, qkv, bias)


def _lru_tile_index(i, n_lat, n_tiles, direction):
    if direction == 0:
        return (i + n_lat) % n_tiles
    return n_tiles - 1 - i


def _lru_row_permutation():
    seg = TOK // LRU_SEG
    p = np.arange(TOK)
    src = (p % LRU_SEG) * seg + p // LRU_SEG
    return (src[:, None] == np.arange(TOK)[None, :]).astype(np.float32)


def _shift_rows(group, boundary_row, down):
    sub = lax.broadcasted_iota(jnp.int32, group.shape, 0)
    if down:
        return jnp.where(sub == 0, boundary_row, pltpu.roll(group, 1, axis=0))
    return jnp.where(sub == LRU_SEG - 1, boundary_row, pltpu.roll(group, LRU_SEG - 1, axis=0))


def _lru_kernel(xf_ref, prevf_ref, nextf_ref, xb_ref, prevb_ref, nextb_ref, cw_ref, cb_ref, wg_ref, bg_ref, lam_ref,
                perm_ref, of_ref, ob_ref, carry_s, *, n_lat, n_tiles, ctx_sub):
    i = pl.program_id(1)
    ch = xf_ref.shape[-1]
    halo = prevf_ref.shape[1]
    seg = TOK // LRU_SEG
    grp = LRU_SEG

    @pl.when(i == 0)
    def _():
        carry_s[...] = jnp.zeros_like(carry_s)

    perm = perm_ref[...]
    cw = cw_ref[...]

    def sub_tile(direction, r, n_sub, state):
        x_ref, prev_ref, next_ref, o_ref = ((xf_ref, prevf_ref, nextf_ref, of_ref) if direction == 0 else
                                            (xb_ref, prevb_ref, nextb_ref, ob_ref))
        tile = _lru_tile_index(i, n_lat, n_tiles, direction)
        seq_first = jnp.logical_or(tile == 0, tile == n_lat)
        seq_last = jnp.logical_or(tile == n_lat - 1, tile == n_tiles - 1)
        rows = slice(r * TOK, (r + 1) * TOK)
        x = _dot(perm, x_ref[0, rows])
        if r == 0:
            prev = jnp.where(seq_first, 0.0, prev_ref[0].astype(F32))
        else:
            prev = x_ref[0, r * TOK - halo:r * TOK].astype(F32)
        if r == n_sub - 1:
            nxt = jnp.where(seq_last, 0.0, next_ref[0].astype(F32))
        else:
            nxt = x_ref[0, (r + 1) * TOK:(r + 1) * TOK + halo].astype(F32)
        wrap_m1 = _shift_rows(x[(seg - 1) * grp:], prev[halo - 1:halo], down=True)
        wrap_m2 = _shift_rows(x[(seg - 2) * grp:(seg - 1) * grp], prev[halo - 2:halo - 1], down=True)
        wrap_p1 = _shift_rows(x[:grp], nxt[0:1], down=False)
        x_m1 = jnp.concatenate([wrap_m1, x[:-grp]], axis=0)
        x_m2 = jnp.concatenate([wrap_m2, wrap_m1, x[:-2 * grp]], axis=0)
        x_p1 = jnp.concatenate([x[grp:], wrap_p1], axis=0)
        xc = cw[0:1] * x_m2 + cw[1:2] * x_m1 + cw[2:3] * x + cw[3:4] * x_p1 + cb_ref[...]

        gates = 0.5 * jnp.tanh(_dot(xc.astype(BF16), wg_ref[direction]) + bg_ref[direction]) + 0.5
        r_gate = gates[:, :ch]
        i_gate = gates[:, ch:]
        a = jnp.exp2(((-LRU_C * LOG2E) * _softplus(-lam_ref[direction])) * r_gate)
        gap = 1.0 - a * a
        root = jnp.where(gap > 0.0, gap * lax.rsqrt(gap), 0.0)
        bb = root * (i_gate * xc)

        steps = range(seg) if direction == 0 else range(seg - 1, -1, -1)
        h = jnp.zeros((grp, ch), F32)
        cum = jnp.ones((grp, ch), F32)
        h_loc = [None] * seg
        cum_loc = [None] * seg
        for j in steps:
            a_j = a[j * grp:(j + 1) * grp]
            h = a_j * h + bb[j * grp:(j + 1) * grp]
            cum = cum * a_j
            h_loc[j] = h
            cum_loc[j] = cum

        entering = [None] * LRU_SEG
        for s in (range(LRU_SEG) if direction == 0 else range(LRU_SEG - 1, -1, -1)):
            entering[s] = state
            state = h[s:s + 1] + cum[s:s + 1] * state
        entering = jnp.concatenate(entering, axis=0)
        o_ref[0, rows] = jnp.concatenate([h_loc[j] + cum_loc[j] * entering for j in range(seg)], axis=0)
        return state

    def run(n_sub):
        states = [carry_s[0], carry_s[1]]
        for r in range(n_sub):
            states[0] = sub_tile(0, r, n_sub, states[0])
            states[1] = sub_tile(1, n_sub - 1 - r, n_sub, states[1])
        carry_s[0] = states[0]
        carry_s[1] = states[1]

    n_blk = xf_ref.shape[1] // TOK
    if ctx_sub == n_blk:
        run(n_blk)
    else:
        pl.when(i == 0)(lambda: run(ctx_sub))
        pl.when(i > 0)(lambda: run(n_blk))


def _lru(xg, conv_w, conv_b, w_gate, b_gate, lam, s_len):
    bsz, t, two_ch = xg.shape
    ch = two_ch // 2
    n_lat, n_tiles, ctx_sub = _scan_blocks(s_len, t - s_len, MTOK)
    halo = 16
    per_tile = MTOK // halo
    tiles = [functools.partial(_lru_tile_index, n_lat=n_lat, n_tiles=n_tiles, direction=d) for d in range(2)]

    def stream_specs(tile_of):
        return [pl.BlockSpec((1, MTOK, ch), lambda b, i: (b, tile_of(i), 0)),
                pl.BlockSpec((1, halo, ch), lambda b, i: (b, jnp.maximum(tile_of(i) * per_tile - 1, 0), 0)),
                pl.BlockSpec((1, halo, ch),
                             lambda b, i: (b, jnp.minimum((tile_of(i) + 1) * per_tile, t // halo - 1), 0))]

    out_spec = lambda tile_of: pl.BlockSpec((1, MTOK, ch), lambda b, i: (b, tile_of(i), 0))
    return pl.pallas_call(
        functools.partial(_lru_kernel, n_lat=n_lat, n_tiles=n_tiles, ctx_sub=ctx_sub),
        grid=(bsz, n_tiles),
        in_specs=stream_specs(tiles[0]) + stream_specs(tiles[1]) + [
            _const_spec((4, ch)), _const_spec((1, ch)),
            _const_spec((2, ch, 2 * ch)), _const_spec((2, 1, 2 * ch)), _const_spec((2, 1, ch)),
            _const_spec((TOK, TOK)),
        ],
        out_specs=[out_spec(tiles[0]), out_spec(tiles[1])],
        out_shape=[jax.ShapeDtypeStruct((bsz, t, ch), F32)] * 2,
        scratch_shapes=[pltpu.VMEM((2, 1, ch), F32)],
        compiler_params=_cparams(("parallel", "arbitrary")),
        name="lru",
    )(xg, xg, xg, xg, xg, xg, conv_w, conv_b.reshape(1, ch), w_gate, b_gate, lam.reshape(2, 1, ch),
      jnp.asarray(_lru_row_permutation(), BF16))


def _block_diag(w):
    heads, n, _ = w.shape
    eye = jnp.eye(heads, dtype=w.dtype)
    return (eye[:, None, :, None] * w[:, :, None, :]).reshape(heads * n, heads * n)


def _gla_kernel(qf_ref, lrf_ref, qb_ref, lrb_ref, wa_ref, ba_ref, trif_ref, trib_ref, csum_ref, of_ref, ob_ref,
                state_s, *, ctx_sub):
    key = GLA_HEADS * GLA_DK
    val = GLA_HEADS * GLA_DV
    nchunk = TOK // GLA_CHUNK
    step = pl.program_id(1)

    @pl.when(step == 0)
    def _():
        state_s[...] = jnp.zeros_like(state_s)

    chunks = [slice(c * GLA_CHUNK, (c + 1) * GLA_CHUNK) for c in range(nchunk)]

    def prepare(direction, rows):
        qkvg_ref, lr_ref, tri_ref = (qf_ref, lrf_ref, trif_ref) if direction == 0 else (qb_ref, lrb_ref, trib_ref)
        tri = tri_ref[...]
        ends = [(c + 1) * GLA_CHUNK - 1 if direction == 0 else c * GLA_CHUNK for c in range(nchunk)]
        z = qkvg_ref[0, rows]
        q = z[:, :key] * (GLA_DK ** -0.5)
        k = z[:, key:2 * key]
        v = z[:, 2 * key:2 * key + val].astype(BF16)
        lr = lr_ref[0, rows][:, direction * GLA_RANK:(direction + 1) * GLA_RANK]
        logit = _dot(lr.astype(BF16), wa_ref[direction].astype(BF16)) + ba_ref[direction]
        log_a = (jnp.minimum(logit, 0.0) - jnp.log1p(jnp.exp(-jnp.abs(logit)))) / GLA_TAU

        g_hi, g_lo = _split_hi_lo(log_a)
        b = _dot(tri, g_hi) + _dot(tri, g_lo)
        q_in = (q * jnp.exp(b)).astype(BF16)
        k_in = (k * jnp.exp(-b)).astype(BF16)
        b_end = jnp.concatenate([jnp.broadcast_to(b[e:e + 1], (GLA_CHUNK, key)) for e in ends], axis=0)
        k_out_t = (k * jnp.exp(b_end - b)).T.astype(BF16)
        gt_hi, gt_lo = _split_hi_lo(log_a.T)
        decay = jnp.exp(_dot(gt_hi, csum_ref[...]) + _dot(gt_lo, csum_ref[...]))
        causal = tri > 0
        intra, update = [], []
        for h in range(GLA_HEADS):
            ks = slice(h * GLA_DK, (h + 1) * GLA_DK)
            v_h = v[:, h * GLA_DV:(h + 1) * GLA_DV]
            att = jnp.where(causal, _dot_nt(q_in[:, ks], k_in[:, ks]), 0.0)
            intra.append(_dot(att.astype(BF16), v_h))
            update.append([_dot(k_out_t[ks, rc], v_h[rc]) for rc in chunks])
        return q_in, decay, intra, update

    def recur(direction, rows, prepared, states):
        o_ref = of_ref if direction == 0 else ob_ref
        q_in, decay, intra, update = prepared
        chunk_order = range(nchunk) if direction == 0 else range(nchunk - 1, -1, -1)
        new_states = []
        for h in range(GLA_HEADS):
            ks = slice(h * GLA_DK, (h + 1) * GLA_DK)
            state = states[h]
            inter = [None] * nchunk
            for c in chunk_order:
                inter[c] = _dot(q_in[chunks[c], ks], state.astype(BF16))
                state = decay[ks, c * GLA_DV:(c + 1) * GLA_DV] * state + update[h][c]
            new_states.append(state)
            o_ref[0, rows, h * GLA_DV:(h + 1) * GLA_DV] = intra[h] + jnp.concatenate(inter, axis=0)
        return new_states

    def run(n_sub):
        subs = [slice(r * TOK, (r + 1) * TOK) for r in range(n_sub)]
        states = [[state_s[d, h] for h in range(GLA_HEADS)] for d in range(2)]
        work = [(d, subs[r] if d == 0 else subs[n_sub - 1 - r]) for r in range(n_sub) for d in range(2)]
        prepared = [prepare(d, rows) for d, rows in work]
        for (d, rows), prep in zip(work, prepared):
            states[d] = recur(d, rows, prep, states[d])
        for d in range(2):
            for h in range(GLA_HEADS):
                state_s[d, h] = states[d][h]

    n_blk = qf_ref.shape[1] // TOK
    if ctx_sub == n_blk:
        run(n_blk)
    else:
        pl.when(step == 0)(lambda: run(ctx_sub))
        pl.when(step > 0)(lambda: run(n_blk))


def _scan_blocks(s_len, c_len, blk):
    assert s_len % blk == 0 and c_len % TOK == 0 and c_len <= blk
    return s_len // blk, s_len // blk + 1, c_len // TOK


def _gla(qkvg, lr, wa2, ba, s_len):
    bsz, t, width = qkvg.shape
    key = GLA_HEADS * GLA_DK
    val = GLA_HEADS * GLA_DV
    n_lat, n_tiles, ctx_sub = _scan_blocks(s_len, t - s_len, MTOK)
    tiles = [functools.partial(_lru_tile_index, n_lat=n_lat, n_tiles=n_tiles, direction=d) for d in range(2)]
    pos = np.arange(TOK)
    same = (pos[:, None] // GLA_CHUNK) == (pos[None, :] // GLA_CHUNK)
    tri = [jnp.asarray(same & m, BF16) for m in (pos[None, :] <= pos[:, None], pos[None, :] >= pos[:, None])]
    chunk_sum = jnp.asarray((pos[:, None] // GLA_CHUNK) == (np.arange(TOK // GLA_CHUNK * GLA_DV)[None, :] // GLA_DV), BF16)
    blk = lambda width, tile_of: pl.BlockSpec((1, MTOK, width), lambda b, i: (b, tile_of(i), 0))
    return pl.pallas_call(
        functools.partial(_gla_kernel, ctx_sub=ctx_sub),
        grid=(bsz, n_tiles),
        in_specs=[
            blk(width, tiles[0]), blk(lr.shape[-1], tiles[0]), blk(width, tiles[1]), blk(lr.shape[-1], tiles[1]),
            _const_spec((2, GLA_RANK, key)), _const_spec((2, 1, key)),
            _const_spec((TOK, TOK)), _const_spec((TOK, TOK)), _const_spec(chunk_sum.shape),
        ],
        out_specs=[blk(val, tiles[0]), blk(val, tiles[1])],
        out_shape=[jax.ShapeDtypeStruct((bsz, t, val), F32)] * 2,
        scratch_shapes=[pltpu.VMEM((2, GLA_HEADS, GLA_DK, GLA_DV), F32)],
        compiler_params=_cparams(("parallel", "arbitrary")),
        name="gla",
    )(qkvg, lr, qkvg, lr, wa2, ba.reshape(2, 1, key), tri[0], tri[1], chunk_sum)


def _rope_tables(s_len, c_len, reps):
    quarter = GQA_DH // 4
    inv = ROPE_THETA ** (-np.arange(quarter, dtype=np.float32) / quarter)
    t = np.arange(s_len)
    ang_r = (t // GRID_W).astype(np.float32)[:, None] * inv[None, :]
    ang_c = (t % GRID_W).astype(np.float32)[:, None] * inv[None, :]
    cos = np.concatenate([np.cos(ang_r)] * 2 + [np.cos(ang_c)] * 2, axis=1)
    sin = np.concatenate([-np.sin(ang_r), np.sin(ang_r), -np.sin(ang_c), np.sin(ang_c)], axis=1)
    cos = np.concatenate([cos, np.ones((c_len, GQA_DH), np.float32)], axis=0)
    sin = np.concatenate([sin, np.zeros((c_len, GQA_DH), np.float32)], axis=0)
    return (jnp.asarray(np.tile(cos, (1, reps)), F32), jnp.asarray(np.tile(sin, (1, reps)), F32))


def _rope_swap(x):
    lane = lax.broadcasted_iota(jnp.int32, x.shape, 1)
    return jnp.where(lane % 32 < 16, pltpu.roll(x, 128 - 16, axis=1), pltpu.roll(x, 16, axis=1))


def _norm_rope(x, nw, cos, sin, gmean):
    hi, lo = _split_hi_lo(x * x)
    ms = _dot(hi, gmean) + _dot(lo, gmean)
    y = x * lax.rsqrt(ms + EPS) * nw
    slabs = [_rope_swap(y[:, j:j + 128]) for j in range(0, y.shape[1], 128)]
    swapped = slabs[0] if len(slabs) == 1 else jnp.concatenate(slabs, axis=1)
    return y * cos + swapped * sin


def _group_mean_matrix(n, group):
    idx = np.arange(n) // group
    return jnp.asarray((idx[:, None] == idx[None, :]) / group, BF16)


def _inproj_odd_kernel(h_ref, nw_ref, mod_ref, w_ref, qw_ref, kw_ref, cos_ref, sin_ref, gq_ref, gk_ref,
                       qkvg_ref, lr_ref, q_ref, k_ref, v_ref, *, gla_w):
    qd = GQA_HEADS * GQA_DH
    kd = GQA_KV_HEADS * GQA_DH
    mod = mod_ref[0]
    u = _rms(h_ref[0], nw_ref[...]) * (1.0 + mod[1:2]) + mod[0:1]
    z_all = _dot(u.astype(BF16), w_ref[...])
    qkvg_ref[0] = z_all[:, :gla_w]
    lr_ref[0] = z_all[:, gla_w + qd + 2 * kd:]
    z = z_all[:, gla_w:gla_w + qd + 2 * kd]
    cos_k = cos_ref[...]
    sin_k = sin_ref[...]
    reps = qd // kd
    cos_q = jnp.concatenate([cos_k] * reps, axis=1)
    sin_q = jnp.concatenate([sin_k] * reps, axis=1)
    q = _norm_rope(z[:, :qd], qw_ref[...], cos_q, sin_q, gq_ref[...]) * (LOG2E * GQA_DH ** -0.5)
    k = _norm_rope(z[:, qd:qd + kd], kw_ref[...], cos_k, sin_k, gk_ref[...])
    v = z[:, qd + kd:qd + 2 * kd]
    q_ref[0] = q.astype(BF16)
    ones = jnp.ones((z.shape[0], LANES - GQA_DH), BF16)
    for h in range(GQA_KV_HEADS):
        k_ref[0, h] = k[:, h * GQA_DH:(h + 1) * GQA_DH].astype(BF16)
        v_ref[0, h] = jnp.concatenate([v[:, h * GQA_DH:(h + 1) * GQA_DH].astype(BF16), ones], axis=1)


def _inproj_odd(h, norm_w, mods, w, q_norm_w, k_norm_w, s_len, c_len, gla_w):
    bsz, t, d = h.shape
    n = w.shape[1]
    qd = GQA_HEADS * GQA_DH
    kd = GQA_KV_HEADS * GQA_DH
    n_lr = n - gla_w - qd - 2 * kd
    cos, sin = _rope_tables(s_len, c_len, GQA_KV_HEADS)
    tok_spec = lambda width: pl.BlockSpec((1, DTOK, width), lambda b, i: (b, i, 0))
    head_spec = lambda width: pl.BlockSpec((1, GQA_KV_HEADS, DTOK, width), lambda b, i: (b, 0, i, 0))
    return pl.pallas_call(
        functools.partial(_inproj_odd_kernel, gla_w=gla_w),
        grid=(bsz, pl.cdiv(t, DTOK)),
        in_specs=[
            tok_spec(d), _const_spec((1, d)), _mod_spec(d, s_len // DTOK), _const_spec((d, n)),
            _const_spec((1, qd)), _const_spec((1, kd)),
            pl.BlockSpec((DTOK, kd), lambda b, i: (i, 0)),
            pl.BlockSpec((DTOK, kd), lambda b, i: (i, 0)),
            _const_spec((qd, qd)), _const_spec((kd, kd)),
        ],
        out_specs=[tok_spec(gla_w), tok_spec(n_lr), tok_spec(qd), head_spec(GQA_DH), head_spec(LANES)],
        out_shape=[jax.ShapeDtypeStruct((bsz, t, gla_w), F32),
                   jax.ShapeDtypeStruct((bsz, t, n_lr), F32),
                   jax.ShapeDtypeStruct((bsz, t, qd), BF16),
                   jax.ShapeDtypeStruct((bsz, GQA_KV_HEADS, t, GQA_DH), BF16),
                   jax.ShapeDtypeStruct((bsz, GQA_KV_HEADS, t, LANES), BF16)],
        compiler_params=_cparams(("parallel", "parallel")),
        name="inproj_odd",
    )(h, norm_w.reshape(1, d), mods, w,
      jnp.tile(q_norm_w, GQA_HEADS).reshape(1, qd), jnp.tile(k_norm_w, GQA_KV_HEADS).reshape(1, kd),
      cos, sin, _group_mean_matrix(qd, GQA_DH), _group_mean_matrix(kd, GQA_DH))


def _flash_kernel(q_ref, k_ref, v_ref, o_ref, m_s, acc_s, *, group, tk, sub):
    tq = q_ref.shape[1]
    n_kv = k_ref.shape[2] // tk
    units = [(g, r) for g in range(group) for r in range(0, tq, sub)]
    m_s[...] = jnp.full_like(m_s, -jnp.inf)
    acc_s[...] = jnp.zeros_like(acc_s)

    def kv_step(j, carry):
        rows_k = pl.ds(pl.multiple_of(j * tk, tk), tk)
        k = k_ref[0, 0, rows_k, :]
        v = v_ref[0, 0, rows_k, :]

        def scores(u):
            g, r = units[u]
            return _dot_nt(q_ref[0, r:r + sub, g * GQA_DH:(g + 1) * GQA_DH], k)

        s_next = scores(0)
        for u in range(len(units)):
            tiles = _lane_tiles(s_next)
            if u + 1 < len(units):
                s_next = scores(u + 1)
            m_prev = m_s[u]
            m_new = jnp.maximum(m_prev, jnp.max(functools.reduce(jnp.maximum, tiles), axis=-1, keepdims=True))
            p = [jnp.exp2(t - m_new).astype(BF16) for t in tiles]
            acc_s[u] = jnp.exp2(m_prev - m_new) * acc_s[u] + _dot(jnp.concatenate(p, axis=1), v)
            m_s[u] = m_new
        return carry

    lax.fori_loop(0, n_kv, kv_step, 0)
    for u, (g, r) in enumerate(units):
        acc = acc_s[u]
        out = acc[:, :GQA_DH] / acc[:, GQA_DH:GQA_DH + 1]
        o_ref[0, r:r + sub, g * GQA_DH:(g + 1) * GQA_DH] = out.astype(o_ref.dtype)


KV_TILE_MAX = 3072
FLASH_SUB = 512


def _kv_tile(t):
    return max(n for n in range(LANES, KV_TILE_MAX + 1, LANES) if t % n == 0)


def _flash(q, k, v, s_len, tq, tk):
    bsz, _, t, dh = k.shape
    sub = FLASH_SUB
    assert t % tk == 0 and s_len % tq == 0 and tq % sub == 0
    group = GQA_HEADS // GQA_KV_HEADS
    n_units = group * (tq // sub)
    return pl.pallas_call(
        functools.partial(_flash_kernel, group=group, tk=tk, sub=sub),
        grid=(bsz, GQA_KV_HEADS, s_len // tq),
        in_specs=[
            pl.BlockSpec((1, tq, group * dh), lambda b, h, i: (b, i, h)),
            pl.BlockSpec((1, 1, t, dh), lambda b, h, i: (b, h, 0, 0)),
            pl.BlockSpec((1, 1, t, LANES), lambda b, h, i: (b, h, 0, 0)),
        ],
        out_specs=pl.BlockSpec((1, tq, group * dh), lambda b, h, i: (b, i, h)),
        out_shape=jax.ShapeDtypeStruct((bsz, s_len, GQA_HEADS * dh), BF16),
        scratch_shapes=[pltpu.VMEM((n_units, sub, LANES), F32), pltpu.VMEM((n_units, sub, LANES), F32)],
        compiler_params=_cparams(("parallel", "parallel", "arbitrary")),
        name="gqa_flash",
    )(q, k, v)


def kernel(x, c, ctx, c_ctx, norm1_w, norm2_w, w_mod, b_mod, w_ff1, w_ff2,
           w_in_even, na_rpb, lru_conv_w, lru_conv_b, lru_wa, lru_ba, lru_wx, lru_bx, lru_lambda, w_out_even,
           w_in_odd, gla_wa2, gla_ba, gla_norm_w, gqa_q_norm_w, gqa_k_norm_w, w_out_odd, final_norm_w):
    bsz, s_len, d = x.shape
    c_len = ctx.shape[1]
    depth = w_mod.shape[0]
    assert s_len % TOK == 0 and c_len % TOK == 0 and depth == 2

    pad = (-(bsz + 1)) % 8
    cvec = jnp.concatenate([c_ctx[None], c, jnp.zeros((pad, d), F32)], axis=0)
    mods = _modulation(cvec, w_mod, b_mod)
    h = (x, ctx)
    t = s_len + c_len

    na_w = NA_HEADS * NA_DH
    ch = lru_lambda.shape[-1]
    col_scale = jnp.where(jnp.arange(w_in_even.shape[-1]) < na_w, LOG2E * NA_DH ** -0.5, 1.0)
    qkv, xg = _inproj(h, norm1_w[0], mods[0], (w_in_even[0] * col_scale).astype(BF16),
                      ((0, 3 * na_w), (3 * na_w, 3 * na_w + 2 * ch)), (BF16, BF16), s_len, t)
    na_out = _na(qkv, na_rpb[0], s_len, c_len)
    w_gate = jnp.stack([jnp.concatenate([_block_diag(lru_wa[0, dr]), _block_diag(lru_wx[0, dr])], axis=1)
                        for dr in range(2)])
    w_gate = (0.5 * w_gate).astype(BF16)
    b_gate = 0.5 * jnp.concatenate([lru_ba[0], lru_bx[0]], axis=-1).reshape(2, 1, 2 * ch)
    lru_f, lru_b = _lru(xg, lru_conv_w[0], lru_conv_b[0], w_gate, b_gate, lru_lambda[0], s_len)
    mixes = (_mix_plain(na_out), _mix_lru(lru_f, lru_b, xg))
    w_ff1_bf, w_ff2_bf = _to_bf16(w_ff1), _to_bf16(w_ff2)
    h = _mlp(h, mixes, mods[0], w_out_even[0].astype(BF16), norm2_w[0],
             w_ff1_bf, w_ff2_bf, 0, final_norm_w, s_len, s_len + c_len, final=False)

    key = GLA_HEADS * GLA_DK
    val = GLA_HEADS * GLA_DV
    gla_w = 2 * key + 2 * val
    w_in = w_in_odd[0]
    w_in = jnp.concatenate([w_in[:, :gla_w], w_in[:, gla_w + 2 * GLA_RANK:], w_in[:, gla_w:gla_w + 2 * GLA_RANK]],
                           axis=1).astype(BF16)
    qkvg, lr, q, k, v = _inproj_odd(h, norm1_w[1], mods[1], w_in, gqa_q_norm_w[0], gqa_k_norm_w[0],
                                    s_len, c_len, gla_w)
    gla_f, gla_b = _gla(qkvg, lr, gla_wa2[0], gla_ba[0], s_len)
    gqa_out = _flash(q, k, v, s_len, tq=2 * FLASH_SUB, tk=_kv_tile(t))
    mixes = (_mix_gla(gla_f, gla_b, qkvg, gla_norm_w[0]), _mix_plain(gqa_out))
    return _mlp(h, mixes, mods[1], w_out_odd[0].astype(BF16), norm2_w[1],
                w_ff1_bf, w_ff2_bf, 1, final_norm_w, s_len, s_len, final=True)
```

```python
import functools

import numpy as np
import jax
import jax.numpy as jnp
from jax import lax
from jax.experimental import pallas as pl
from jax.experimental.pallas import tpu as pltpu

F32 = jnp.float32
BF16 = jnp.bfloat16

EPS = 1e-6
GRID_W = 64
LANES = 128
N_MOD = 6
TOK = 256
DTOK = 512
MTOK = 1024
FF_CHUNK = 512
NEG = -1e30

NA_HEADS = 8
NA_DH = 64
NA_KH = 8
NA_KW = 16
LRU_C = 8.0
LRU_SEG = 8
GLA_HEADS = 4
GLA_DK = 64
GLA_DV = 128
GLA_RANK = 16
GLA_TAU = 16.0
GLA_CHUNK = 64
GQA_HEADS = 8
GQA_KV_HEADS = 2
GQA_DH = 64
ROPE_THETA = 10000.0
LOG2E = 1.4426950408889634

VMEM_LIMIT = 56 * 1024 * 1024


def _cparams(sem):
    return pltpu.CompilerParams(dimension_semantics=sem, vmem_limit_bytes=VMEM_LIMIT)


def _const_spec(shape):
    nd = len(shape)
    return pl.BlockSpec(shape, lambda *_: (0,) * nd, pipeline_mode=pl.Buffered(1))


def _layer_spec(shape, layer):
    nd = len(shape)
    return pl.BlockSpec((None,) + tuple(shape), lambda *_: (layer,) + (0,) * nd, pipeline_mode=pl.Buffered(1))


def _cast_kernel(x_ref, o_ref):
    o_ref[...] = x_ref[...].astype(o_ref.dtype)


def _to_bf16(w, row_blocks=2):
    layers, rows, cols = w.shape
    blk = rows // row_blocks
    spec = pl.BlockSpec((1, blk, cols), lambda l, i: (l, i, 0))
    return pl.pallas_call(
        _cast_kernel,
        grid=(layers, row_blocks),
        in_specs=[spec],
        out_specs=spec,
        out_shape=jax.ShapeDtypeStruct(w.shape, BF16),
        compiler_params=_cparams(("parallel", "parallel")),
        name="cast_bf16",
    )(w)


def _dot(a, b):
    return jnp.dot(a, b, preferred_element_type=F32)


def _dot_nt(a, b):
    return lax.dot_general(a, b, (((1,), (1,)), ((), ())), preferred_element_type=F32)


def _split_hi_lo(x):
    hi = x.astype(BF16)
    lo = (x - hi.astype(F32)).astype(BF16)
    return hi, lo


def _rms(x, w):
    ms = jnp.mean(x * x, axis=-1, keepdims=True)
    return x * lax.rsqrt(ms + EPS) * w


def _softplus(z):
    return jnp.maximum(z, 0.0) + jnp.log1p(jnp.exp(-jnp.abs(z)))


def _mod_kernel(c_ref, w_ref, b_ref, o_ref):
    c = c_ref[...]
    s = c * jax.nn.sigmoid(c)
    s_hi, s_lo = _split_hi_lo(s)
    w_hi, w_lo = _split_hi_lo(w_ref[0])
    o_ref[0] = _dot(s_hi, w_hi) + (_dot(s_hi, w_lo) + _dot(s_lo, w_hi)) + b_ref[0]


def _modulation(cvec, w_mod, b_mod):
    depth, d, _ = w_mod.shape
    rows = cvec.shape[0]
    out = pl.pallas_call(
        _mod_kernel,
        grid=(depth, N_MOD),
        in_specs=[
            pl.BlockSpec((rows, d), lambda i, j: (0, 0)),
            pl.BlockSpec((1, d, d), lambda i, j: (i, 0, j)),
            pl.BlockSpec((1, 1, d), lambda i, j: (i, 0, j)),
        ],
        out_specs=pl.BlockSpec((1, rows, d), lambda i, j: (i, 0, j)),
        out_shape=jax.ShapeDtypeStruct((depth, rows, N_MOD * d), F32),
        compiler_params=_cparams(("arbitrary", "arbitrary")),
        name="modulation",
    )(cvec, w_mod, b_mod.reshape(depth, 1, N_MOD * d))
    return out.reshape(depth, rows, N_MOD, d)


def _mod_spec(d, n_lat_tiles):
    return pl.BlockSpec((1, N_MOD, d), lambda b, t: (jnp.where(t >= n_lat_tiles, 0, b + 1), 0, 0))


def _stream_specs(h, s_len):
    n_lat = s_len // DTOK
    if not isinstance(h, tuple):
        return [h], [pl.BlockSpec((1, DTOK, h.shape[-1]), lambda b, i: (b, i, 0))], 0
    lat, ctx = h
    c_len, d = ctx.shape[1:]
    assert c_len <= DTOK and lat.shape[1] == s_len
    return ([lat, ctx],
            [pl.BlockSpec((1, DTOK, d), lambda b, i: (b, jnp.minimum(i, n_lat - 1), 0)),
             pl.BlockSpec((1, c_len, d), lambda b, i: (b, 0, 0))], c_len)


def _per_stream(body, h_refs, n_lat_tiles, ctx_rows):
    if len(h_refs) == 1:
        body(h_refs[0], DTOK)
        return
    i = pl.program_id(1)
    pl.when(i < n_lat_tiles)(lambda: body(h_refs[0], DTOK))
    pl.when(i >= n_lat_tiles)(lambda: body(h_refs[1], ctx_rows))


def _inproj_kernel(*refs, splits, n_src, n_lat_tiles, ctx_rows):
    h_refs = refs[:n_src]
    nw_ref, mod_ref, w_ref = refs[n_src:n_src + 3]
    o_refs = refs[n_src + 3:]

    def body(h_ref, rows):
        mod = mod_ref[0]
        u = _rms(h_ref[0], nw_ref[...]) * (1.0 + mod[1:2]) + mod[0:1]
        z = _dot(u.astype(BF16), w_ref[...])
        for o_ref, (lo, hi) in zip(o_refs, splits):
            o_ref[0, :rows] = z[:, lo:hi].astype(o_ref.dtype)

    _per_stream(body, h_refs, n_lat_tiles, ctx_rows)


def _inproj(h, norm_w, mods, w, splits, dtypes, s_len, t):
    d, n = w.shape
    assert s_len % DTOK == 0
    srcs, src_specs, ctx_rows = _stream_specs(h, s_len)
    bsz = srcs[0].shape[0]
    return pl.pallas_call(
        functools.partial(_inproj_kernel, splits=splits, n_src=len(srcs), n_lat_tiles=s_len // DTOK,
                          ctx_rows=ctx_rows),
        grid=(bsz, pl.cdiv(t, DTOK)),
        in_specs=src_specs + [
            _const_spec((1, d)),
            _mod_spec(d, s_len // DTOK),
            _const_spec((d, n)),
        ],
        out_specs=[pl.BlockSpec((1, DTOK, hi - lo), lambda b, i: (b, i, 0)) for lo, hi in splits],
        out_shape=[jax.ShapeDtypeStruct((bsz, t, hi - lo), dt) for (lo, hi), dt in zip(splits, dtypes)],
        compiler_params=_cparams(("parallel", "parallel")),
        name="inproj",
    )(*srcs, norm_w.reshape(1, d), mods, w)


class _Mix:
    def __init__(self, arrays, specs, width, load):
        self.arrays, self.specs, self.width, self.load = arrays, specs, width, load


def _tok_spec(width, col_block=0):
    return pl.BlockSpec((1, DTOK, width), lambda b, i: (b, i, col_block))


def _mix_plain(a):
    return _Mix([a], [_tok_spec(a.shape[-1])], a.shape[-1], lambda refs, rows: refs[0][0, :rows].astype(BF16))


def _mix_gla(o_fwd, o_bwd, qkvg, norm_w):
    val = GLA_HEADS * GLA_DV

    def load(refs, rows):
        of_ref, ob_ref, gate_ref, nw_ref = refs
        o = of_ref[0, :rows] + ob_ref[0, :rows]
        gate = gate_ref[0, :rows]
        heads = [_rms(o[:, sl], nw_ref[...]) for sl in (slice(h * GLA_DV, (h + 1) * GLA_DV) for h in range(GLA_HEADS))]
        return (jnp.concatenate(heads, axis=1) * (gate * jax.nn.sigmoid(gate))).astype(BF16)

    return _Mix([o_fwd, o_bwd, qkvg, norm_w.reshape(1, GLA_DV)],
                [_tok_spec(val), _tok_spec(val), _tok_spec(val, col_block=2), _const_spec((1, GLA_DV))], val, load)


def _mix_lru(y_fwd, y_bwd, xg):
    ch = y_fwd.shape[-1]
    perm = _lru_row_permutation()

    def load(refs, rows):
        yf_ref, yb_ref, g_ref, perm_ref, unperm_ref = refs
        out = []
        for r in range(0, rows, TOK):
            y = yf_ref[0, r:r + TOK] + yb_ref[0, r:r + TOK]
            y = y * jax.nn.gelu(_dot(perm_ref[...], g_ref[0, r:r + TOK]))
            out.append(_dot(unperm_ref[...], y.astype(BF16)).astype(BF16))
        return out[0] if len(out) == 1 else jnp.concatenate(out, axis=0)

    return _Mix([y_fwd, y_bwd, xg, jnp.asarray(perm, BF16), jnp.asarray(perm.T, BF16)],
                [_tok_spec(ch), _tok_spec(ch), _tok_spec(ch, col_block=1), _const_spec((TOK, TOK)),
                 _const_spec((TOK, TOK))], ch, load)


def _mlp_kernel(*refs, ff_chunk, final, n_src, n_lat_tiles, ctx_rows, mixes):
    h_refs = refs[:n_src]
    pos = n_src
    mix_refs = []
    for n_arrays, _, _ in mixes:
        mix_refs.append(refs[pos:pos + n_arrays])
        pos += n_arrays
    mod_ref, wo_ref, nw_ref, w1_ref, w2_ref, fw_ref, o_ref = refs[pos:]

    def body(h_ref, rows):
        mod = mod_ref[0]
        y = None
        row0 = 0
        for (_, width, load), m_refs in zip(mixes, mix_refs):
            part = _dot(load(m_refs, rows), wo_ref[row0:row0 + width])
            y = part if y is None else y + part
            row0 += width
        h1 = h_ref[0] + mod[2:3] * y
        u = (_rms(h1, nw_ref[...]) * (1.0 + mod[4:5]) + mod[3:4]).astype(BF16)
        d_ff = w1_ref.shape[1]
        acc = jnp.zeros(h1.shape, F32)
        for c in range(d_ff // ff_chunk):
            sl = slice(c * ff_chunk, (c + 1) * ff_chunk)
            a = jnp.maximum(_dot(u, w1_ref[:, sl]), 0.0)
            acc = acc + _dot((a * a).astype(BF16), w2_ref[sl, :])
        h2 = h1 + mod[5:6] * acc
        if final:
            h2 = _rms(h2, fw_ref[...])
        o_ref[0, :rows] = h2

    _per_stream(body, h_refs, n_lat_tiles, ctx_rows)


def _mlp(h, mixes, mods, wo, norm_w, w1, w2, layer, final_w, s_len, out_rows, final):
    _, d, d_ff = w1.shape
    assert s_len % DTOK == 0 and wo.shape[0] == sum(m.width for m in mixes)
    srcs, src_specs, ctx_rows = _stream_specs(h, s_len)
    bsz = srcs[0].shape[0]
    return pl.pallas_call(
        functools.partial(_mlp_kernel, ff_chunk=FF_CHUNK, final=final, n_src=len(srcs), n_lat_tiles=s_len // DTOK,
                          ctx_rows=ctx_rows, mixes=tuple((len(m.arrays), m.width, m.load) for m in mixes)),
        grid=(bsz, pl.cdiv(out_rows, DTOK)),
        in_specs=src_specs + [s for m in mixes for s in m.specs] + [
            _mod_spec(d, s_len // DTOK),
            _const_spec(wo.shape), _const_spec((1, d)),
            _layer_spec((d, d_ff), layer), _layer_spec((d_ff, d), layer), _const_spec((1, d)),
        ],
        out_specs=_tok_spec(d),
        out_shape=jax.ShapeDtypeStruct((bsz, out_rows, d), F32),
        compiler_params=_cparams(("parallel", "parallel")),
        name="mlp",
    )(*srcs, *[a for m in mixes for a in m.arrays], mods, wo, norm_w.reshape(1, d), w1, w2, final_w.reshape(1, d))


NA_QROWS = TOK // GRID_W
NA_WIN = NA_KH + NA_QROWS


def _na_plan(rows):
    n_blk = rows // NA_QROWS
    r = np.arange(rows).reshape(n_blk, NA_QROWS)
    start = np.clip(r - NA_KH // 2, 0, rows - NA_KH)
    u0 = np.clip(r[:, 0] - NA_KH // 2, 0, rows - NA_WIN)
    key = u0[:, None, None] + np.arange(NA_WIN)[None, None, :]
    valid = (key >= start[:, :, None]) & (key < start[:, :, None] + NA_KH)
    d_row = np.where(valid, key - r[:, :, None] + (NA_KH - 1), 0)
    flat = np.concatenate([valid.reshape(n_blk, -1), d_row.reshape(n_blk, -1)], axis=1)
    _, first, pat = np.unique(flat, axis=0, return_index=True, return_inverse=True)
    return u0.astype(np.int32), pat.reshape(-1).astype(np.int32), valid[first], d_row[first]


def _na_bias_table(rpb, valid, d_row):
    n_pat = valid.shape[0]
    heads = rpb.shape[0]
    pairs = NA_WIN // 2
    n_dc = 2 * NA_KW - 1
    cols = np.arange(GRID_W)
    col_start = np.clip(cols - NA_KW // 2, 0, GRID_W - NA_KW)
    col_valid = (cols[None, :] >= col_start[:, None]) & (cols[None, :] < col_start[:, None] + NA_KW)
    d_col = np.clip(cols[None, :] - cols[:, None], 1 - NA_KW, NA_KW - 1) + (NA_KW - 1)
    row_sel = (d_row[..., None] == np.arange(2 * NA_KH - 1)).astype(np.float32)
    col_sel = (d_col[None] == np.arange(n_dc)[:, None, None]).astype(np.float32)
    n_k = n_dc + 2
    pair_sel = np.zeros((2, n_k, GRID_W, 2, GRID_W), np.float32)
    for u1 in range(2):
        pair_sel[u1, :n_dc, :, u1, :] = col_sel
        pair_sel[u1, n_dc, :, u1, :] = ~col_valid
        pair_sel[u1, n_dc + 1, :, u1, :] = 1.0
    pair_sel = pair_sel.reshape(2 * n_k, GRID_W, 2 * GRID_W)
    hp = lax.Precision.HIGHEST
    by_row = jnp.einsum('hrc,piur->phiuc', rpb, row_sel, precision=hp) * LOG2E
    col_mask = jnp.full((n_pat, heads, NA_QROWS, NA_WIN, 1), NEG, F32)
    row_mask = jnp.broadcast_to(jnp.where(valid, 0.0, NEG)[:, None, :, :, None], col_mask.shape)
    by_row = jnp.concatenate([by_row, col_mask, row_mask], axis=-1)
    by_row = by_row.reshape(n_pat, heads, NA_QROWS, pairs, 2 * n_k).transpose(0, 1, 3, 2, 4)
    bias = lax.dot_general(by_row, pair_sel, (((4,), (0,)), ((), ())), precision=hp)
    return bias


def _softmax_tiles(tiles):
    m = jnp.max(functools.reduce(jnp.maximum, tiles), axis=-1, keepdims=True)
    p = [jnp.exp2(t - m) for t in tiles]
    denom = jnp.sum(functools.reduce(jnp.add, p), axis=-1, keepdims=True)
    return [t.astype(BF16) for t in p], denom


def _lane_tiles(x):
    return [x[:, j:j + LANES] for j in range(0, x.shape[1], LANES)]


def _na_kernel(u0_ref, pat_ref, q_ref, k_ref, v_ref, kc_ref, vc_ref, bias_ref, o_ref, *, n_lat):
    i = pl.program_id(1)
    win = NA_WIN * GRID_W
    n_loc = win // LANES

    def attend(local):
        q = q_ref[0]
        kc = kc_ref[0]
        vc = vc_ref[0]
        if local:
            row0 = pl.multiple_of(u0_ref[i] * GRID_W, GRID_W)
            kw = k_ref[0, pl.ds(row0, win), :]
            vw = v_ref[0, pl.ds(row0, win), :]

        def scores(h):
            sl = slice(h * NA_DH, (h + 1) * NA_DH)
            s_ctx = _lane_tiles(_dot_nt(q[:, sl], kc[:, sl]))
            if not local:
                return s_ctx
            s_loc = _lane_tiles(_dot_nt(q[:, sl], kw[:, sl]))
            return [t + bias_ref[0, h, j].reshape(TOK, LANES) for j, t in enumerate(s_loc)] + s_ctx

        ahead = 1
        pending = [scores(h) for h in range(ahead)]
        for h in range(NA_HEADS):
            sl = slice(h * NA_DH, (h + 1) * NA_DH)
            tiles = pending.pop(0)
            if h + ahead < NA_HEADS:
                pending.append(scores(h + ahead))
            p, denom = _softmax_tiles(tiles)
            v_all = jnp.concatenate([vw[:, sl], vc[:, sl]], axis=0) if local else vc[:, sl]
            o = _dot(jnp.concatenate(p, axis=1), v_all)
            o_ref[0, :, sl] = (o / denom).astype(o_ref.dtype)

    pl.when(i < n_lat)(functools.partial(attend, True))
    pl.when(i >= n_lat)(functools.partial(attend, False))


def _na(qkv, rpb, s_len, c_len):
    bsz, t, _ = qkv.shape
    width = NA_HEADS * NA_DH
    rows = s_len // GRID_W
    n_lat = s_len // TOK
    n_steps = t // TOK
    assert rows >= NA_WIN and rows % NA_QROWS == 0 and c_len % TOK == 0 and s_len % c_len == 0
    u0, pat, valid, d_row = _na_plan(rows)
    bias = _na_bias_table(rpb, valid, d_row)
    ctx_steps = np.zeros(n_steps - n_lat, np.int32)
    u0 = jnp.asarray(np.concatenate([u0, ctx_steps]))
    pat = jnp.asarray(np.concatenate([pat, ctx_steps]))
    tok_spec = pl.BlockSpec((1, TOK, width), lambda b, i, u0_ref, pat_ref: (b, i, 0))
    ctx_blk = s_len // c_len
    return pl.pallas_call(
        functools.partial(_na_kernel, n_lat=n_lat),
        grid_spec=pltpu.PrefetchScalarGridSpec(
            num_scalar_prefetch=2,
            grid=(bsz, n_steps),
            in_specs=[
                tok_spec,
                pl.BlockSpec((1, s_len, width), lambda b, i, u0_ref, pat_ref: (b, 0, 1),
                             pipeline_mode=pl.Buffered(1)),
                pl.BlockSpec((1, s_len, width), lambda b, i, u0_ref, pat_ref: (b, 0, 2),
                             pipeline_mode=pl.Buffered(1)),
                pl.BlockSpec((1, c_len, width), lambda b, i, u0_ref, pat_ref: (b, ctx_blk, 1)),
                pl.BlockSpec((1, c_len, width), lambda b, i, u0_ref, pat_ref: (b, ctx_blk, 2)),
                pl.BlockSpec((1, NA_HEADS, NA_WIN // 2, NA_QROWS, GRID_W, 2 * GRID_W),
                             lambda b, i, u0_ref, pat_ref: (pat_ref[i], 0, 0, 0, 0, 0)),
            ],
            out_specs=tok_spec,
        ),
        out_shape=jax.ShapeDtypeStruct((bsz, t, width), BF16),
        compiler_params=_cparams(("parallel", "arbitrary")),
        name="na",
    )(u0, pat, qkv, qkv, qkv, qkv, qkv, bias)


def _lru_tile_index(i, n_lat, n_tiles, direction):
    if direction == 0:
        return (i + n_lat) % n_tiles
    return n_tiles - 1 - i


def _lru_row_permutation():
    seg = TOK // LRU_SEG
    p = np.arange(TOK)
    src = (p % LRU_SEG) * seg + p // LRU_SEG
    return (src[:, None] == np.arange(TOK)[None, :]).astype(np.float32)


def _shift_rows(group, boundary_row, down):
    sub = lax.broadcasted_iota(jnp.int32, group.shape, 0)
    if down:
        return jnp.where(sub == 0, boundary_row, pltpu.roll(group, 1, axis=0))
    return jnp.where(sub == LRU_SEG - 1, boundary_row, pltpu.roll(group, LRU_SEG - 1, axis=0))


def _lru_kernel(xf_ref, prevf_ref, nextf_ref, xb_ref, prevb_ref, nextb_ref, cw_ref, cb_ref, wg_ref, bg_ref, lam_ref,
                perm_ref, of_ref, ob_ref, carry_s, *, n_lat, n_tiles, ctx_sub):
    i = pl.program_id(1)
    ch = xf_ref.shape[-1]
    halo = prevf_ref.shape[1]
    seg = TOK // LRU_SEG
    grp = LRU_SEG

    @pl.when(i == 0)
    def _():
        carry_s[...] = jnp.zeros_like(carry_s)

    perm = perm_ref[...]
    cw = cw_ref[...]

    def sub_tile(direction, r, n_sub, state):
        x_ref, prev_ref, next_ref, o_ref = ((xf_ref, prevf_ref, nextf_ref, of_ref) if direction == 0 else
                                            (xb_ref, prevb_ref, nextb_ref, ob_ref))
        tile = _lru_tile_index(i, n_lat, n_tiles, direction)
        seq_first = jnp.logical_or(tile == 0, tile == n_lat)
        seq_last = jnp.logical_or(tile == n_lat - 1, tile == n_tiles - 1)
        rows = slice(r * TOK, (r + 1) * TOK)
        x = _dot(perm, x_ref[0, rows])
        if r == 0:
            prev = jnp.where(seq_first, 0.0, prev_ref[0].astype(F32))
        else:
            prev = x_ref[0, r * TOK - halo:r * TOK].astype(F32)
        if r == n_sub - 1:
            nxt = jnp.where(seq_last, 0.0, next_ref[0].astype(F32))
        else:
            nxt = x_ref[0, (r + 1) * TOK:(r + 1) * TOK + halo].astype(F32)
        wrap_m1 = _shift_rows(x[(seg - 1) * grp:], prev[halo - 1:halo], down=True)
        wrap_m2 = _shift_rows(x[(seg - 2) * grp:(seg - 1) * grp], prev[halo - 2:halo - 1], down=True)
        wrap_p1 = _shift_rows(x[:grp], nxt[0:1], down=False)
        x_m1 = jnp.concatenate([wrap_m1, x[:-grp]], axis=0)
        x_m2 = jnp.concatenate([wrap_m2, wrap_m1, x[:-2 * grp]], axis=0)
        x_p1 = jnp.concatenate([x[grp:], wrap_p1], axis=0)
        xc = cw[0:1] * x_m2 + cw[1:2] * x_m1 + cw[2:3] * x + cw[3:4] * x_p1 + cb_ref[...]

        gates = 0.5 * jnp.tanh(_dot(xc.astype(BF16), wg_ref[direction]) + bg_ref[direction]) + 0.5
        r_gate = gates[:, :ch]
        i_gate = gates[:, ch:]
        a = jnp.exp2(((-LRU_C * LOG2E) * _softplus(-lam_ref[direction])) * r_gate)
        gap = 1.0 - a * a
        root = jnp.where(gap > 0.0, gap * lax.rsqrt(gap), 0.0)
        bb = root * (i_gate * xc)

        steps = range(seg) if direction == 0 else range(seg - 1, -1, -1)
        h = jnp.zeros((grp, ch), F32)
        cum = jnp.ones((grp, ch), F32)
        h_loc = [None] * seg
        cum_loc = [None] * seg
        for j in steps:
            a_j = a[j * grp:(j + 1) * grp]
            h = a_j * h + bb[j * grp:(j + 1) * grp]
            cum = cum * a_j
            h_loc[j] = h
            cum_loc[j] = cum

        entering = [None] * LRU_SEG
        for s in (range(LRU_SEG) if direction == 0 else range(LRU_SEG - 1, -1, -1)):
            entering[s] = state
            state = h[s:s + 1] + cum[s:s + 1] * state
        entering = jnp.concatenate(entering, axis=0)
        o_ref[0, rows] = jnp.concatenate([h_loc[j] + cum_loc[j] * entering for j in range(seg)], axis=0)
        return state

    def run(n_sub):
        states = [carry_s[0], carry_s[1]]
        for r in range(n_sub):
            states[0] = sub_tile(0, r, n_sub, states[0])
            states[1] = sub_tile(1, n_sub - 1 - r, n_sub, states[1])
        carry_s[0] = states[0]
        carry_s[1] = states[1]

    n_blk = xf_ref.shape[1] // TOK
    if ctx_sub == n_blk:
        run(n_blk)
    else:
        pl.when(i == 0)(lambda: run(ctx_sub))
        pl.when(i > 0)(lambda: run(n_blk))


def _lru(xg, conv_w, conv_b, w_gate, b_gate, lam, s_len):
    bsz, t, two_ch = xg.shape
    ch = two_ch // 2
    n_lat, n_tiles, ctx_sub = _scan_blocks(s_len, t - s_len, MTOK)
    halo = 16
    per_tile = MTOK // halo
    tiles = [functools.partial(_lru_tile_index, n_lat=n_lat, n_tiles=n_tiles, direction=d) for d in range(2)]

    def stream_specs(tile_of):
        return [pl.BlockSpec((1, MTOK, ch), lambda b, i: (b, tile_of(i), 0)),
                pl.BlockSpec((1, halo, ch), lambda b, i: (b, jnp.maximum(tile_of(i) * per_tile - 1, 0), 0)),
                pl.BlockSpec((1, halo, ch),
                             lambda b, i: (b, jnp.minimum((tile_of(i) + 1) * per_tile, t // halo - 1), 0))]

    out_spec = lambda tile_of: pl.BlockSpec((1, MTOK, ch), lambda b, i: (b, tile_of(i), 0))
    return pl.pallas_call(
        functools.partial(_lru_kernel, n_lat=n_lat, n_tiles=n_tiles, ctx_sub=ctx_sub),
        grid=(bsz, n_tiles),
        in_specs=stream_specs(tiles[0]) + stream_specs(tiles[1]) + [
            _const_spec((4, ch)), _const_spec((1, ch)),
            _const_spec((2, ch, 2 * ch)), _const_spec((2, 1, 2 * ch)), _const_spec((2, 1, ch)),
            _const_spec((TOK, TOK)),
        ],
        out_specs=[out_spec(tiles[0]), out_spec(tiles[1])],
        out_shape=[jax.ShapeDtypeStruct((bsz, t, ch), F32)] * 2,
        scratch_shapes=[pltpu.VMEM((2, 1, ch), F32)],
        compiler_params=_cparams(("parallel", "arbitrary")),
        name="lru",
    )(xg, xg, xg, xg, xg, xg, conv_w, conv_b.reshape(1, ch), w_gate, b_gate, lam.reshape(2, 1, ch),
      jnp.asarray(_lru_row_permutation(), BF16))


def _block_diag(w):
    heads, n, _ = w.shape
    eye = jnp.eye(heads, dtype=w.dtype)
    return (eye[:, None, :, None] * w[:, :, None, :]).reshape(heads * n, heads * n)


def _gla_kernel(qf_ref, lrf_ref, qb_ref, lrb_ref, wa_ref, ba_ref, trif_ref, trib_ref, csum_ref, of_ref, ob_ref,
                state_s, *, ctx_sub):
    key = GLA_HEADS * GLA_DK
    val = GLA_HEADS * GLA_DV
    nchunk = TOK // GLA_CHUNK
    step = pl.program_id(1)

    @pl.when(step == 0)
    def _():
        state_s[...] = jnp.zeros_like(state_s)

    chunks = [slice(c * GLA_CHUNK, (c + 1) * GLA_CHUNK) for c in range(nchunk)]

    def prepare(direction, rows):
        qkvg_ref, lr_ref, tri_ref = (qf_ref, lrf_ref, trif_ref) if direction == 0 else (qb_ref, lrb_ref, trib_ref)
        tri = tri_ref[...]
        ends = [(c + 1) * GLA_CHUNK - 1 if direction == 0 else c * GLA_CHUNK for c in range(nchunk)]
        z = qkvg_ref[0, rows]
        q = z[:, :key] * (GLA_DK ** -0.5)
        k = z[:, key:2 * key]
        v = z[:, 2 * key:2 * key + val].astype(BF16)
        lr = lr_ref[0, rows][:, direction * GLA_RANK:(direction + 1) * GLA_RANK]
        logit = _dot(lr.astype(BF16), wa_ref[direction].astype(BF16)) + ba_ref[direction]
        log_a = (jnp.minimum(logit, 0.0) - jnp.log1p(jnp.exp(-jnp.abs(logit)))) / GLA_TAU

        g_hi, g_lo = _split_hi_lo(log_a)
        b = _dot(tri, g_hi) + _dot(tri, g_lo)
        q_in = (q * jnp.exp(b)).astype(BF16)
        k_in = (k * jnp.exp(-b)).astype(BF16)
        b_end = jnp.concatenate([jnp.broadcast_to(b[e:e + 1], (GLA_CHUNK, key)) for e in ends], axis=0)
        k_out_t = (k * jnp.exp(b_end - b)).T.astype(BF16)
        gt_hi, gt_lo = _split_hi_lo(log_a.T)
        decay = jnp.exp(_dot(gt_hi, csum_ref[...]) + _dot(gt_lo, csum_ref[...]))
        causal = tri > 0
        intra, update = [], []
        for h in range(GLA_HEADS):
            ks = slice(h * GLA_DK, (h + 1) * GLA_DK)
            v_h = v[:, h * GLA_DV:(h + 1) * GLA_DV]
            att = jnp.where(causal, _dot_nt(q_in[:, ks], k_in[:, ks]), 0.0)
            intra.append(_dot(att.astype(BF16), v_h))
            update.append([_dot(k_out_t[ks, rc], v_h[rc]) for rc in chunks])
        return q_in, decay, intra, update

    def recur(direction, rows, prepared, states):
        o_ref = of_ref if direction == 0 else ob_ref
        q_in, decay, intra, update = prepared
        chunk_order = range(nchunk) if direction == 0 else range(nchunk - 1, -1, -1)
        new_states = []
        for h in range(GLA_HEADS):
            ks = slice(h * GLA_DK, (h + 1) * GLA_DK)
            state = states[h]
            inter = [None] * nchunk
            for c in chunk_order:
                inter[c] = _dot(q_in[chunks[c], ks], state.astype(BF16))
                state = decay[ks, c * GLA_DV:(c + 1) * GLA_DV] * state + update[h][c]
            new_states.append(state)
            o_ref[0, rows, h * GLA_DV:(h + 1) * GLA_DV] = intra[h] + jnp.concatenate(inter, axis=0)
        return new_states

    def run(n_sub):
        subs = [slice(r * TOK, (r + 1) * TOK) for r in range(n_sub)]
        states = [[state_s[d, h] for h in range(GLA_HEADS)] for d in range(2)]
        work = [(d, subs[r] if d == 0 else subs[n_sub - 1 - r]) for r in range(n_sub) for d in range(2)]
        prepared = [prepare(d, rows) for d, rows in work]
        for (d, rows), prep in zip(work, prepared):
            states[d] = recur(d, rows, prep, states[d])
        for d in range(2):
            for h in range(GLA_HEADS):
                state_s[d, h] = states[d][h]

    n_blk = qf_ref.shape[1] // TOK
    if ctx_sub == n_blk:
        run(n_blk)
    else:
        pl.when(step == 0)(lambda: run(ctx_sub))
        pl.when(step > 0)(lambda: run(n_blk))


def _scan_blocks(s_len, c_len, blk):
    assert s_len % blk == 0 and c_len % TOK == 0 and c_len <= blk
    return s_len // blk, s_len // blk + 1, c_len // TOK


def _gla(qkvg, lr, wa2, ba, s_len):
    bsz, t, width = qkvg.shape
    key = GLA_HEADS * GLA_DK
    val = GLA_HEADS * GLA_DV
    n_lat, n_tiles, ctx_sub = _scan_blocks(s_len, t - s_len, MTOK)
    tiles = [functools.partial(_lru_tile_index, n_lat=n_lat, n_tiles=n_tiles, direction=d) for d in range(2)]
    pos = np.arange(TOK)
    same = (pos[:, None] // GLA_CHUNK) == (pos[None, :] // GLA_CHUNK)
    tri = [jnp.asarray(same & m, BF16) for m in (pos[None, :] <= pos[:, None], pos[None, :] >= pos[:, None])]
    chunk_sum = jnp.asarray((pos[:, None] // GLA_CHUNK) == (np.arange(TOK // GLA_CHUNK * GLA_DV)[None, :] // GLA_DV), BF16)
    blk = lambda width, tile_of: pl.BlockSpec((1, MTOK, width), lambda b, i: (b, tile_of(i), 0))
    return pl.pallas_call(
        functools.partial(_gla_kernel, ctx_sub=ctx_sub),
        grid=(bsz, n_tiles),
        in_specs=[
            blk(width, tiles[0]), blk(lr.shape[-1], tiles[0]), blk(width, tiles[1]), blk(lr.shape[-1], tiles[1]),
            _const_spec((2, GLA_RANK, key)), _const_spec((2, 1, key)),
            _const_spec((TOK, TOK)), _const_spec((TOK, TOK)), _const_spec(chunk_sum.shape),
        ],
        out_specs=[blk(val, tiles[0]), blk(val, tiles[1])],
        out_shape=[jax.ShapeDtypeStruct((bsz, t, val), F32)] * 2,
        scratch_shapes=[pltpu.VMEM((2, GLA_HEADS, GLA_DK, GLA_DV), F32)],
        compiler_params=_cparams(("parallel", "arbitrary")),
        name="gla",
    )(qkvg, lr, qkvg, lr, wa2, ba.reshape(2, 1, key), tri[0], tri[1], chunk_sum)


def _rope_tables(s_len, c_len, reps):
    quarter = GQA_DH // 4
    inv = ROPE_THETA ** (-np.arange(quarter, dtype=np.float32) / quarter)
    t = np.arange(s_len)
    ang_r = (t // GRID_W).astype(np.float32)[:, None] * inv[None, :]
    ang_c = (t % GRID_W).astype(np.float32)[:, None] * inv[None, :]
    cos = np.concatenate([np.cos(ang_r)] * 2 + [np.cos(ang_c)] * 2, axis=1)
    sin = np.concatenate([-np.sin(ang_r), np.sin(ang_r), -np.sin(ang_c), np.sin(ang_c)], axis=1)
    cos = np.concatenate([cos, np.ones((c_len, GQA_DH), np.float32)], axis=0)
    sin = np.concatenate([sin, np.zeros((c_len, GQA_DH), np.float32)], axis=0)
    return (jnp.asarray(np.tile(cos, (1, reps)), F32), jnp.asarray(np.tile(sin, (1, reps)), F32))


def _rope_swap(x):
    lane = lax.broadcasted_iota(jnp.int32, x.shape, 1)
    return jnp.where(lane % 32 < 16, pltpu.roll(x, 128 - 16, axis=1), pltpu.roll(x, 16, axis=1))


def _norm_rope(x, nw, cos, sin, gmean):
    hi, lo = _split_hi_lo(x * x)
    ms = _dot(hi, gmean) + _dot(lo, gmean)
    y = x * lax.rsqrt(ms + EPS) * nw
    slabs = [_rope_swap(y[:, j:j + 128]) for j in range(0, y.shape[1], 128)]
    swapped = slabs[0] if len(slabs) == 1 else jnp.concatenate(slabs, axis=1)
    return y * cos + swapped * sin


def _group_mean_matrix(n, group):
    idx = np.arange(n) // group
    return jnp.asarray((idx[:, None] == idx[None, :]) / group, BF16)


def _inproj_odd_kernel(h_ref, nw_ref, mod_ref, w_ref, qw_ref, kw_ref, cos_ref, sin_ref, gq_ref, gk_ref,
                       qkvg_ref, lr_ref, q_ref, k_ref, v_ref, *, gla_w):
    qd = GQA_HEADS * GQA_DH
    kd = GQA_KV_HEADS * GQA_DH
    mod = mod_ref[0]
    u = _rms(h_ref[0], nw_ref[...]) * (1.0 + mod[1:2]) + mod[0:1]
    z_all = _dot(u.astype(BF16), w_ref[...])
    qkvg_ref[0] = z_all[:, :gla_w]
    lr_ref[0] = z_all[:, gla_w + qd + 2 * kd:]
    z = z_all[:, gla_w:gla_w + qd + 2 * kd]
    cos_k = cos_ref[...]
    sin_k = sin_ref[...]
    reps = qd // kd
    cos_q = jnp.concatenate([cos_k] * reps, axis=1)
    sin_q = jnp.concatenate([sin_k] * reps, axis=1)
    q = _norm_rope(z[:, :qd], qw_ref[...], cos_q, sin_q, gq_ref[...]) * (LOG2E * GQA_DH ** -0.5)
    k = _norm_rope(z[:, qd:qd + kd], kw_ref[...], cos_k, sin_k, gk_ref[...])
    v = z[:, qd + kd:qd + 2 * kd]
    q_ref[0] = q.astype(BF16)
    ones = jnp.ones((z.shape[0], LANES - GQA_DH), BF16)
    for h in range(GQA_KV_HEADS):
        k_ref[0, h] = k[:, h * GQA_DH:(h + 1) * GQA_DH].astype(BF16)
        v_ref[0, h] = jnp.concatenate([v[:, h * GQA_DH:(h + 1) * GQA_DH].astype(BF16), ones], axis=1)


def _inproj_odd(h, norm_w, mods, w, q_norm_w, k_norm_w, s_len, c_len, gla_w):
    bsz, t, d = h.shape
    n = w.shape[1]
    qd = GQA_HEADS * GQA_DH
    kd = GQA_KV_HEADS * GQA_DH
    n_lr = n - gla_w - qd - 2 * kd
    cos, sin = _rope_tables(s_len, c_len, GQA_KV_HEADS)
    tok_spec = lambda width: pl.BlockSpec((1, DTOK, width), lambda b, i: (b, i, 0))
    head_spec = lambda width: pl.BlockSpec((1, GQA_KV_HEADS, DTOK, width), lambda b, i: (b, 0, i, 0))
    return pl.pallas_call(
        functools.partial(_inproj_odd_kernel, gla_w=gla_w),
        grid=(bsz, pl.cdiv(t, DTOK)),
        in_specs=[
            tok_spec(d), _const_spec((1, d)), _mod_spec(d, s_len // DTOK), _const_spec((d, n)),
            _const_spec((1, qd)), _const_spec((1, kd)),
            pl.BlockSpec((DTOK, kd), lambda b, i: (i, 0)),
            pl.BlockSpec((DTOK, kd), lambda b, i: (i, 0)),
            _const_spec((qd, qd)), _const_spec((kd, kd)),
        ],
        out_specs=[tok_spec(gla_w), tok_spec(n_lr), tok_spec(qd), head_spec(GQA_DH), head_spec(LANES)],
        out_shape=[jax.ShapeDtypeStruct((bsz, t, gla_w), F32),
                   jax.ShapeDtypeStruct((bsz, t, n_lr), F32),
                   jax.ShapeDtypeStruct((bsz, t, qd), BF16),
                   jax.ShapeDtypeStruct((bsz, GQA_KV_HEADS, t, GQA_DH), BF16),
                   jax.ShapeDtypeStruct((bsz, GQA_KV_HEADS, t, LANES), BF16)],
        compiler_params=_cparams(("parallel", "parallel")),
        name="inproj_odd",
    )(h, norm_w.reshape(1, d), mods, w,
      jnp.tile(q_norm_w, GQA_HEADS).reshape(1, qd), jnp.tile(k_norm_w, GQA_KV_HEADS).reshape(1, kd),
      cos, sin, _group_mean_matrix(qd, GQA_DH), _group_mean_matrix(kd, GQA_DH))


def _flash_kernel(q_ref, k_ref, v_ref, o_ref, m_s, acc_s, *, group, tk, sub):
    tq = q_ref.shape[1]
    n_kv = k_ref.shape[2] // tk
    units = [(g, r) for g in range(group) for r in range(0, tq, sub)]
    m_s[...] = jnp.full_like(m_s, -jnp.inf)
    acc_s[...] = jnp.zeros_like(acc_s)

    def kv_step(j, carry):
        rows_k = pl.ds(pl.multiple_of(j * tk, tk), tk)
        k = k_ref[0, 0, rows_k, :]
        v = v_ref[0, 0, rows_k, :]

        def scores(u):
            g, r = units[u]
            return _dot_nt(q_ref[0, r:r + sub, g * GQA_DH:(g + 1) * GQA_DH], k)

        s_next = scores(0)
        for u in range(len(units)):
            tiles = _lane_tiles(s_next)
            if u + 1 < len(units):
                s_next = scores(u + 1)
            m_prev = m_s[u]
            m_new = jnp.maximum(m_prev, jnp.max(functools.reduce(jnp.maximum, tiles), axis=-1, keepdims=True))
            p = [jnp.exp2(t - m_new).astype(BF16) for t in tiles]
            acc_s[u] = jnp.exp2(m_prev - m_new) * acc_s[u] + _dot(jnp.concatenate(p, axis=1), v)
            m_s[u] = m_new
        return carry

    lax.fori_loop(0, n_kv, kv_step, 0)
    for u, (g, r) in enumerate(units):
        acc = acc_s[u]
        out = acc[:, :GQA_DH] / acc[:, GQA_DH:GQA_DH + 1]
        o_ref[0, r:r + sub, g * GQA_DH:(g + 1) * GQA_DH] = out.astype(o_ref.dtype)


KV_TILE_MAX = 3072
FLASH_SUB = 512


def _kv_tile(t):
    return max(n for n in range(LANES, KV_TILE_MAX + 1, LANES) if t % n == 0)


def _flash(q, k, v, s_len, tq, tk):
    bsz, _, t, dh = k.shape
    sub = FLASH_SUB
    assert t % tk == 0 and s_len % tq == 0 and tq % sub == 0
    group = GQA_HEADS // GQA_KV_HEADS
    n_units = group * (tq // sub)
    return pl.pallas_call(
        functools.partial(_flash_kernel, group=group, tk=tk, sub=sub),
        grid=(bsz, GQA_KV_HEADS, s_len // tq),
        in_specs=[
            pl.BlockSpec((1, tq, group * dh), lambda b, h, i: (b, i, h)),
            pl.BlockSpec((1, 1, t, dh), lambda b, h, i: (b, h, 0, 0)),
            pl.BlockSpec((1, 1, t, LANES), lambda b, h, i: (b, h, 0, 0)),
        ],
        out_specs=pl.BlockSpec((1, tq, group * dh), lambda b, h, i: (b, i, h)),
        out_shape=jax.ShapeDtypeStruct((bsz, s_len, GQA_HEADS * dh), BF16),
        scratch_shapes=[pltpu.VMEM((n_units, sub, LANES), F32), pltpu.VMEM((n_units, sub, LANES), F32)],
        compiler_params=_cparams(("parallel", "parallel", "arbitrary")),
        name="gqa_flash",
    )(q, k, v)


def kernel(x, c, ctx, c_ctx, norm1_w, norm2_w, w_mod, b_mod, w_ff1, w_ff2,
           w_in_even, na_rpb, lru_conv_w, lru_conv_b, lru_wa, lru_ba, lru_wx, lru_bx, lru_lambda, w_out_even,
           w_in_odd, gla_wa2, gla_ba, gla_norm_w, gqa_q_norm_w, gqa_k_norm_w, w_out_odd, final_norm_w):
    bsz, s_len, d = x.shape
    c_len = ctx.shape[1]
    depth = w_mod.shape[0]
    assert s_len % TOK == 0 and c_len % TOK == 0 and depth == 2

    pad = (-(bsz + 1)) % 8
    cvec = jnp.concatenate([c_ctx[None], c, jnp.zeros((pad, d), F32)], axis=0)
    mods = _modulation(cvec, w_mod, b_mod)
    h = (x, ctx)
    t = s_len + c_len

    na_w = NA_HEADS * NA_DH
    ch = lru_lambda.shape[-1]
    col_scale = jnp.where(jnp.arange(w_in_even.shape[-1]) < na_w, LOG2E * NA_DH ** -0.5, 1.0)
    qkv, xg = _inproj(h, norm1_w[0], mods[0], (w_in_even[0] * col_scale).astype(BF16),
                      ((0, 3 * na_w), (3 * na_w, 3 * na_w + 2 * ch)), (BF16, BF16), s_len, t)
    na_out = _na(qkv, na_rpb[0], s_len, c_len)
    w_gate = jnp.stack([jnp.concatenate([_block_diag(lru_wa[0, dr]), _block_diag(lru_wx[0, dr])], axis=1)
                        for dr in range(2)])
    w_gate = (0.5 * w_gate).astype(BF16)
    b_gate = 0.5 * jnp.concatenate([lru_ba[0], lru_bx[0]], axis=-1).reshape(2, 1, 2 * ch)
    lru_f, lru_b = _lru(xg, lru_conv_w[0], lru_conv_b[0], w_gate, b_gate, lru_lambda[0], s_len)
    mixes = (_mix_plain(na_out), _mix_lru(lru_f, lru_b, xg))
    w_ff1_bf, w_ff2_bf = _to_bf16(w_ff1), _to_bf16(w_ff2)
    h = _mlp(h, mixes, mods[0], w_out_even[0].astype(BF16), norm2_w[0],
             w_ff1_bf, w_ff2_bf, 0, final_norm_w, s_len, s_len + c_len, final=False)

    key = GLA_HEADS * GLA_DK
    val = GLA_HEADS * GLA_DV
    gla_w = 2 * key + 2 * val
    w_in = w_in_odd[0]
    w_in = jnp.concatenate([w_in[:, :gla_w], w_in[:, gla_w + 2 * GLA_RANK:], w_in[:, gla_w:gla_w + 2 * GLA_RANK]],
                           axis=1).astype(BF16)
    qkvg, lr, q, k, v = _inproj_odd(h, norm1_w[1], mods[1], w_in, gqa_q_norm_w[0], gqa_k_norm_w[0],
                                    s_len, c_len, gla_w)
    gla_f, gla_b = _gla(qkvg, lr, gla_wa2[0], gla_ba[0], s_len)
    gqa_out = _flash(q, k, v, s_len, tq=2 * FLASH_SUB, tk=_kv_tile(t))
    mixes = (_mix_gla(gla_f, gla_b, qkvg, gla_norm_w[0]), _mix_plain(gqa_out))
    return _mlp(h, mixes, mods[1], w_out_odd[0].astype(BF16), norm2_w[1],
                w_ff1_bf, w_ff2_bf, 1, final_norm_w, s_len, s_len, final=True)
```

```python
import functools

import numpy as np
import jax
import jax.numpy as jnp
from jax import lax
from jax.experimental import pallas as pl
from jax.experimental.pallas import tpu as pltpu

F32 = jnp.float32
BF16 = jnp.bfloat16

EPS = 1e-6
GRID_W = 64
LANES = 128
N_MOD = 6
TOK = 256
DTOK = 512
MTOK = 512
FF_CHUNK = 512
NEG = -1e30

NA_HEADS = 8
NA_DH = 64
NA_KH = 8
NA_KW = 16
LRU_C = 8.0
LRU_SEG = 8
GLA_HEADS = 4
GLA_DK = 64
GLA_DV = 128
GLA_RANK = 16
GLA_TAU = 16.0
GLA_CHUNK = 64
GQA_HEADS = 8
GQA_KV_HEADS = 2
GQA_DH = 64
ROPE_THETA = 10000.0
LOG2E = 1.4426950408889634

VMEM_LIMIT = 56 * 1024 * 1024


def _cparams(sem):
    return pltpu.CompilerParams(dimension_semantics=sem, vmem_limit_bytes=VMEM_LIMIT)


def _const_spec(shape):
    nd = len(shape)
    return pl.BlockSpec(shape, lambda *_: (0,) * nd, pipeline_mode=pl.Buffered(1))


def _layer_spec(shape, layer):
    nd = len(shape)
    return pl.BlockSpec((None,) + tuple(shape), lambda *_: (layer,) + (0,) * nd, pipeline_mode=pl.Buffered(1))


def _cast_kernel(x_ref, o_ref):
    o_ref[...] = x_ref[...].astype(o_ref.dtype)


def _to_bf16(w, row_blocks=2):
    layers, rows, cols = w.shape
    blk = rows // row_blocks
    spec = pl.BlockSpec((1, blk, cols), lambda l, i: (l, i, 0))
    return pl.pallas_call(
        _cast_kernel,
        grid=(layers, row_blocks),
        in_specs=[spec],
        out_specs=spec,
        out_shape=jax.ShapeDtypeStruct(w.shape, BF16),
        compiler_params=_cparams(("parallel", "parallel")),
        name="cast_bf16",
    )(w)


def _dot(a, b):
    return jnp.dot(a, b, preferred_element_type=F32)


def _dot_nt(a, b):
    return lax.dot_general(a, b, (((1,), (1,)), ((), ())), preferred_element_type=F32)


def _split_hi_lo(x):
    hi = x.astype(BF16)
    lo = (x - hi.astype(F32)).astype(BF16)
    return hi, lo


def _rms(x, w):
    ms = jnp.mean(x * x, axis=-1, keepdims=True)
    return x * lax.rsqrt(ms + EPS) * w


def _softplus(z):
    return jnp.maximum(z, 0.0) + jnp.log1p(jnp.exp(-jnp.abs(z)))


def _mod_kernel(c_ref, w_ref, b_ref, o_ref):
    c = c_ref[...]
    s = c * jax.nn.sigmoid(c)
    s_hi, s_lo = _split_hi_lo(s)
    w_hi, w_lo = _split_hi_lo(w_ref[0])
    o_ref[0] = _dot(s_hi, w_hi) + (_dot(s_hi, w_lo) + _dot(s_lo, w_hi)) + b_ref[0]


def _modulation(cvec, w_mod, b_mod):
    depth, d, _ = w_mod.shape
    rows = cvec.shape[0]
    out = pl.pallas_call(
        _mod_kernel,
        grid=(depth, N_MOD),
        in_specs=[
            pl.BlockSpec((rows, d), lambda i, j: (0, 0)),
            pl.BlockSpec((1, d, d), lambda i, j: (i, 0, j)),
            pl.BlockSpec((1, 1, d), lambda i, j: (i, 0, j)),
        ],
        out_specs=pl.BlockSpec((1, rows, d), lambda i, j: (i, 0, j)),
        out_shape=jax.ShapeDtypeStruct((depth, rows, N_MOD * d), F32),
        compiler_params=_cparams(("arbitrary", "arbitrary")),
        name="modulation",
    )(cvec, w_mod, b_mod.reshape(depth, 1, N_MOD * d))
    return out.reshape(depth, rows, N_MOD, d)


def _mod_spec(d, n_lat_tiles):
    return pl.BlockSpec((1, N_MOD, d), lambda b, t: (jnp.where(t >= n_lat_tiles, 0, b + 1), 0, 0))


def _stream_specs(h, s_len):
    n_lat = s_len // DTOK
    if not isinstance(h, tuple):
        return [h], [pl.BlockSpec((1, DTOK, h.shape[-1]), lambda b, i: (b, i, 0))], 0
    lat, ctx = h
    c_len, d = ctx.shape[1:]
    assert c_len <= DTOK and lat.shape[1] == s_len
    return ([lat, ctx],
            [pl.BlockSpec((1, DTOK, d), lambda b, i: (b, jnp.minimum(i, n_lat - 1), 0)),
             pl.BlockSpec((1, c_len, d), lambda b, i: (b, 0, 0))], c_len)


def _per_stream(body, h_refs, n_lat_tiles, ctx_rows):
    if len(h_refs) == 1:
        body(h_refs[0], DTOK)
        return
    i = pl.program_id(1)
    pl.when(i < n_lat_tiles)(lambda: body(h_refs[0], DTOK))
    pl.when(i >= n_lat_tiles)(lambda: body(h_refs[1], ctx_rows))


def _inproj_kernel(*refs, splits, n_src, n_lat_tiles, ctx_rows):
    h_refs = refs[:n_src]
    nw_ref, mod_ref, w_ref = refs[n_src:n_src + 3]
    o_refs = refs[n_src + 3:]

    def body(h_ref, rows):
        mod = mod_ref[0]
        u = _rms(h_ref[0], nw_ref[...]) * (1.0 + mod[1:2]) + mod[0:1]
        z = _dot(u.astype(BF16), w_ref[...])
        for o_ref, (lo, hi, ones_every) in zip(o_refs, splits):
            part = z[:, lo:hi].astype(o_ref.dtype)
            if ones_every:
                ones = jnp.ones((part.shape[0], ones_every), o_ref.dtype)
                part = jnp.concatenate([x for j in range(0, hi - lo, ones_every)
                                        for x in (part[:, j:j + ones_every], ones)], axis=1)
            o_ref[0, :rows] = part

    _per_stream(body, h_refs, n_lat_tiles, ctx_rows)


def _inproj(h, norm_w, mods, w, splits, dtypes, s_len, t):
    d, n = w.shape
    assert s_len % DTOK == 0
    srcs, src_specs, ctx_rows = _stream_specs(h, s_len)
    bsz = srcs[0].shape[0]
    return pl.pallas_call(
        functools.partial(_inproj_kernel, splits=splits, n_src=len(srcs), n_lat_tiles=s_len // DTOK,
                          ctx_rows=ctx_rows),
        grid=(bsz, pl.cdiv(t, DTOK)),
        in_specs=src_specs + [
            _const_spec((1, d)),
            _mod_spec(d, s_len // DTOK),
            _const_spec((d, n)),
        ],
        out_specs=[pl.BlockSpec((1, DTOK, (hi - lo) * (2 if ones else 1)), lambda b, i: (b, i, 0))
                   for lo, hi, ones in splits],
        out_shape=[jax.ShapeDtypeStruct((bsz, t, (hi - lo) * (2 if ones else 1)), dt)
                   for (lo, hi, ones), dt in zip(splits, dtypes)],
        compiler_params=_cparams(("parallel", "parallel")),
        name="inproj",
    )(*srcs, norm_w.reshape(1, d), mods, w)


class _Mix:
    def __init__(self, arrays, specs, width, load):
        self.arrays, self.specs, self.width, self.load = arrays, specs, width, load


def _tok_spec(width, col_block=0):
    return pl.BlockSpec((1, DTOK, width), lambda b, i: (b, i, col_block))


def _mix_plain(a):
    return _Mix([a], [_tok_spec(a.shape[-1])], a.shape[-1], lambda refs, rows: refs[0][0, :rows].astype(BF16))


def _mix_gla(o_fwd, o_bwd, qkvg, norm_w):
    val = GLA_HEADS * GLA_DV

    def load(refs, rows):
        of_ref, ob_ref, gate_ref, nw_ref = refs
        o = of_ref[0, :rows] + ob_ref[0, :rows]
        gate = gate_ref[0, :rows]
        heads = [_rms(o[:, sl], nw_ref[...]) for sl in (slice(h * GLA_DV, (h + 1) * GLA_DV) for h in range(GLA_HEADS))]
        return (jnp.concatenate(heads, axis=1) * (gate * jax.nn.sigmoid(gate))).astype(BF16)

    return _Mix([o_fwd, o_bwd, qkvg, norm_w.reshape(1, GLA_DV)],
                [_tok_spec(val), _tok_spec(val), _tok_spec(val, col_block=2), _const_spec((1, GLA_DV))], val, load)


def _mix_lru(y_fwd, y_bwd, xg):
    ch = y_fwd.shape[-1]
    perm = _lru_row_permutation()

    def load(refs, rows):
        yf_ref, yb_ref, g_ref, perm_ref, unperm_ref = refs
        out = []
        for r in range(0, rows, TOK):
            y = yf_ref[0, r:r + TOK] + yb_ref[0, r:r + TOK]
            y = y * jax.nn.gelu(_dot(perm_ref[...], g_ref[0, r:r + TOK]))
            out.append(_dot(unperm_ref[...], y.astype(BF16)).astype(BF16))
        return out[0] if len(out) == 1 else jnp.concatenate(out, axis=0)

    return _Mix([y_fwd, y_bwd, xg, jnp.asarray(perm, BF16), jnp.asarray(perm.T, BF16)],
                [_tok_spec(ch), _tok_spec(ch), _tok_spec(ch, col_block=1), _const_spec((TOK, TOK)),
                 _const_spec((TOK, TOK))], ch, load)


def _mlp_kernel(*refs, ff_chunk, final, n_src, n_lat_tiles, ctx_rows, mixes):
    h_refs = refs[:n_src]
    pos = n_src
    mix_refs = []
    for n_arrays, _, _ in mixes:
        mix_refs.append(refs[pos:pos + n_arrays])
        pos += n_arrays
    mod_ref, wo_ref, nw_ref, w1_ref, w2_ref, fw_ref, o_ref = refs[pos:]

    def body(h_ref, rows):
        mod = mod_ref[0]
        y = None
        row0 = 0
        for (_, width, load), m_refs in zip(mixes, mix_refs):
            part = _dot(load(m_refs, rows), wo_ref[row0:row0 + width])
            y = part if y is None else y + part
            row0 += width
        h1 = h_ref[0] + mod[2:3] * y
        u = (_rms(h1, nw_ref[...]) * (1.0 + mod[4:5]) + mod[3:4]).astype(BF16)
        d_ff = w1_ref.shape[1]
        acc = jnp.zeros(h1.shape, F32)
        for c in range(d_ff // ff_chunk):
            sl = slice(c * ff_chunk, (c + 1) * ff_chunk)
            a = jnp.maximum(_dot(u, w1_ref[:, sl]), 0.0)
            acc = acc + _dot((a * a).astype(BF16), w2_ref[sl, :])
        h2 = h1 + mod[5:6] * acc
        if final:
            h2 = _rms(h2, fw_ref[...])
        o_ref[0, :rows] = h2

    _per_stream(body, h_refs, n_lat_tiles, ctx_rows)


def _mlp(h, mixes, mods, wo, norm_w, w1, w2, layer, final_w, s_len, out_rows, final):
    _, d, d_ff = w1.shape
    assert s_len % DTOK == 0 and wo.shape[0] == sum(m.width for m in mixes)
    srcs, src_specs, ctx_rows = _stream_specs(h, s_len)
    bsz = srcs[0].shape[0]
    return pl.pallas_call(
        functools.partial(_mlp_kernel, ff_chunk=FF_CHUNK, final=final, n_src=len(srcs), n_lat_tiles=s_len // DTOK,
                          ctx_rows=ctx_rows, mixes=tuple((len(m.arrays), m.width, m.load) for m in mixes)),
        grid=(bsz, pl.cdiv(out_rows, DTOK)),
        in_specs=src_specs + [s for m in mixes for s in m.specs] + [
            _mod_spec(d, s_len // DTOK),
            _const_spec(wo.shape), _const_spec((1, d)),
            _layer_spec((d, d_ff), layer), _layer_spec((d_ff, d), layer), _const_spec((1, d)),
        ],
        out_specs=_tok_spec(d),
        out_shape=jax.ShapeDtypeStruct((bsz, out_rows, d), F32),
        compiler_params=_cparams(("parallel", "parallel")),
        name="mlp",
    )(*srcs, *[a for m in mixes for a in m.arrays], mods, wo, norm_w.reshape(1, d), w1, w2, final_w.reshape(1, d))


NA_QROWS = TOK // GRID_W
NA_WIN = NA_KH + NA_QROWS


def _na_plan(rows):
    n_blk = rows // NA_QROWS
    r = np.arange(rows).reshape(n_blk, NA_QROWS)
    start = np.clip(r - NA_KH // 2, 0, rows - NA_KH)
    u0 = np.clip(r[:, 0] - NA_KH // 2, 0, rows - NA_WIN)
    key = u0[:, None, None] + np.arange(NA_WIN)[None, None, :]
    valid = (key >= start[:, :, None]) & (key < start[:, :, None] + NA_KH)
    d_row = np.where(valid, key - r[:, :, None] + (NA_KH - 1), 0)
    flat = np.concatenate([valid.reshape(n_blk, -1), d_row.reshape(n_blk, -1)], axis=1)
    _, first, pat = np.unique(flat, axis=0, return_index=True, return_inverse=True)
    return u0.astype(np.int32), pat.reshape(-1).astype(np.int32), valid[first], d_row[first]


def _na_bias_table(rpb, valid, d_row):
    n_pat = valid.shape[0]
    heads = rpb.shape[0]
    pairs = NA_WIN // 2
    n_dc = 2 * NA_KW - 1
    cols = np.arange(GRID_W)
    col_start = np.clip(cols - NA_KW // 2, 0, GRID_W - NA_KW)
    col_valid = (cols[None, :] >= col_start[:, None]) & (cols[None, :] < col_start[:, None] + NA_KW)
    d_col = np.clip(cols[None, :] - cols[:, None], 1 - NA_KW, NA_KW - 1) + (NA_KW - 1)
    row_sel = (d_row[..., None] == np.arange(2 * NA_KH - 1)).astype(np.float32)
    col_sel = (d_col[None] == np.arange(n_dc)[:, None, None]).astype(np.float32)
    n_k = n_dc + 2
    pair_sel = np.zeros((2, n_k, GRID_W, 2, GRID_W), np.float32)
    for u1 in range(2):
        pair_sel[u1, :n_dc, :, u1, :] = col_sel
        pair_sel[u1, n_dc, :, u1, :] = ~col_valid
        pair_sel[u1, n_dc + 1, :, u1, :] = 1.0
    pair_sel = pair_sel.reshape(2 * n_k, GRID_W, 2 * GRID_W)
    hp = lax.Precision.HIGHEST
    by_row = jnp.einsum('hrc,piur->phiuc', rpb, row_sel, precision=hp) * LOG2E
    col_mask = jnp.full((n_pat, heads, NA_QROWS, NA_WIN, 1), NEG, F32)
    row_mask = jnp.broadcast_to(jnp.where(valid, 0.0, NEG)[:, None, :, :, None], col_mask.shape)
    by_row = jnp.concatenate([by_row, col_mask, row_mask], axis=-1)
    by_row = by_row.reshape(n_pat, heads, NA_QROWS, pairs, 2 * n_k).transpose(0, 1, 3, 2, 4)
    bias = lax.dot_general(by_row, pair_sel, (((4,), (0,)), ((), ())), precision=hp)
    return bias


def _softmax_tiles(tiles):
    m = jnp.max(functools.reduce(jnp.maximum, tiles), axis=-1, keepdims=True)
    return [jnp.exp2(t - m).astype(BF16) for t in tiles]


def _lane_tiles(x):
    return [x[:, j:j + LANES] for j in range(0, x.shape[1], LANES)]


def _na_kernel(u0_ref, pat_ref, q_ref, k_ref, v_ref, kc_ref, vc_ref, bias_ref, o_ref, *, n_lat):
    i = pl.program_id(1)
    win = NA_WIN * GRID_W
    n_loc = win // LANES

    def attend(local):
        q = q_ref[0]
        kc = kc_ref[0]
        vc = vc_ref[0]
        if local:
            row0 = pl.multiple_of(u0_ref[i] * GRID_W, GRID_W)
            kw = k_ref[0, pl.ds(row0, win), :]
            vw = v_ref[0, pl.ds(row0, win), :]

        def scores(h):
            sl = slice(h * NA_DH, (h + 1) * NA_DH)
            s_ctx = _lane_tiles(_dot_nt(q[:, sl], kc[:, sl]))
            if not local:
                return s_ctx
            s_loc = _lane_tiles(_dot_nt(q[:, sl], kw[:, sl]))
            return [t + bias_ref[0, h, j].reshape(TOK, LANES) for j, t in enumerate(s_loc)] + s_ctx

        ahead = 1
        pending = [scores(h) for h in range(ahead)]
        for h in range(NA_HEADS):
            sl = slice(h * NA_DH, (h + 1) * NA_DH)
            tiles = pending.pop(0)
            if h + ahead < NA_HEADS:
                pending.append(scores(h + ahead))
            p = _softmax_tiles(tiles)
            vs = slice(h * LANES, (h + 1) * LANES)
            v_all = jnp.concatenate([vw[:, vs], vc[:, vs]], axis=0) if local else vc[:, vs]
            o = _dot(jnp.concatenate(p, axis=1), v_all)
            o_ref[0, :, sl] = (o[:, :NA_DH] / o[:, NA_DH:NA_DH + 1]).astype(o_ref.dtype)

    pl.when(i < n_lat)(functools.partial(attend, True))
    pl.when(i >= n_lat)(functools.partial(attend, False))


def _na(qk, v1, rpb, s_len, c_len):
    bsz, t, _ = qk.shape
    width = NA_HEADS * NA_DH
    vwidth = NA_HEADS * LANES
    rows = s_len // GRID_W
    n_lat = s_len // TOK
    n_steps = t // TOK
    assert rows >= NA_WIN and rows % NA_QROWS == 0 and c_len % TOK == 0 and s_len % c_len == 0
    u0, pat, valid, d_row = _na_plan(rows)
    bias = _na_bias_table(rpb, valid, d_row)
    ctx_steps = np.zeros(n_steps - n_lat, np.int32)
    u0 = jnp.asarray(np.concatenate([u0, ctx_steps]))
    pat = jnp.asarray(np.concatenate([pat, ctx_steps]))
    tok_spec = pl.BlockSpec((1, TOK, width), lambda b, i, u0_ref, pat_ref: (b, i, 0))
    ctx_blk = s_len // c_len
    return pl.pallas_call(
        functools.partial(_na_kernel, n_lat=n_lat),
        grid_spec=pltpu.PrefetchScalarGridSpec(
            num_scalar_prefetch=2,
            grid=(bsz, n_steps),
            in_specs=[
                tok_spec,
                pl.BlockSpec((1, s_len, width), lambda b, i, u0_ref, pat_ref: (b, 0, 1),
                             pipeline_mode=pl.Buffered(1)),
                pl.BlockSpec((1, s_len, vwidth), lambda b, i, u0_ref, pat_ref: (b, 0, 0),
                             pipeline_mode=pl.Buffered(1)),
                pl.BlockSpec((1, c_len, width), lambda b, i, u0_ref, pat_ref: (b, ctx_blk, 1)),
                pl.BlockSpec((1, c_len, vwidth), lambda b, i, u0_ref, pat_ref: (b, ctx_blk, 0)),
                pl.BlockSpec((1, NA_HEADS, NA_WIN // 2, NA_QROWS, GRID_W, 2 * GRID_W),
                             lambda b, i, u0_ref, pat_ref: (pat_ref[i], 0, 0, 0, 0, 0)),
            ],
            out_specs=tok_spec,
        ),
        out_shape=jax.ShapeDtypeStruct((bsz, t, width), BF16),
        compiler_params=_cparams(("parallel", "arbitrary")),
        name="na",
    )(u0, pat, qk, qk, v1, qk, v1, bias)


def _lru_tile_index(i, n_lat, n_tiles, direction):
    if direction == 0:
        return (i + n_lat) % n_tiles
    return n_tiles - 1 - i


def _lru_row_permutation():
    seg = TOK // LRU_SEG
    p = np.arange(TOK)
    src = (p % LRU_SEG) * seg + p // LRU_SEG
    return (src[:, None] == np.arange(TOK)[None, :]).astype(np.float32)


def _shift_rows(group, boundary_row, down):
    sub = lax.broadcasted_iota(jnp.int32, group.shape, 0)
    if down:
        return jnp.where(sub == 0, boundary_row, pltpu.roll(group, 1, axis=0))
    return jnp.where(sub == LRU_SEG - 1, boundary_row, pltpu.roll(group, LRU_SEG - 1, axis=0))


def _lru_kernel(xf_ref, prevf_ref, nextf_ref, xb_ref, prevb_ref, nextb_ref, cw_ref, cb_ref, wg_ref, bg_ref, lam_ref,
                perm_ref, of_ref, ob_ref, carry_s, *, n_lat, n_tiles, ctx_sub):
    i = pl.program_id(1)
    ch = xf_ref.shape[-1]
    halo = prevf_ref.shape[1]
    seg = TOK // LRU_SEG
    grp = LRU_SEG

    @pl.when(i == 0)
    def _():
        carry_s[...] = jnp.zeros_like(carry_s)

    perm = perm_ref[...]
    cw = cw_ref[...]

    def sub_tile(direction, r, n_sub, state):
        x_ref, prev_ref, next_ref, o_ref = ((xf_ref, prevf_ref, nextf_ref, of_ref) if direction == 0 else
                                            (xb_ref, prevb_ref, nextb_ref, ob_ref))
        tile = _lru_tile_index(i, n_lat, n_tiles, direction)
        seq_first = jnp.logical_or(tile == 0, tile == n_lat)
        seq_last = jnp.logical_or(tile == n_lat - 1, tile == n_tiles - 1)
        rows = slice(r * TOK, (r + 1) * TOK)
        x = _dot(perm, x_ref[0, rows])
        if r == 0:
            prev = jnp.where(seq_first, 0.0, prev_ref[0].astype(F32))
        else:
            prev = x_ref[0, r * TOK - halo:r * TOK].astype(F32)
        if r == n_sub - 1:
            nxt = jnp.where(seq_last, 0.0, next_ref[0].astype(F32))
        else:
            nxt = x_ref[0, (r + 1) * TOK:(r + 1) * TOK + halo].astype(F32)
        wrap_m1 = _shift_rows(x[(seg - 1) * grp:], prev[halo - 1:halo], down=True)
        wrap_m2 = _shift_rows(x[(seg - 2) * grp:(seg - 1) * grp], prev[halo - 2:halo - 1], down=True)
        wrap_p1 = _shift_rows(x[:grp], nxt[0:1], down=False)
        x_m1 = jnp.concatenate([wrap_m1, x[:-grp]], axis=0)
        x_m2 = jnp.concatenate([wrap_m2, wrap_m1, x[:-2 * grp]], axis=0)
        x_p1 = jnp.concatenate([x[grp:], wrap_p1], axis=0)
        xc = cw[0:1] * x_m2 + cw[1:2] * x_m1 + cw[2:3] * x + cw[3:4] * x_p1 + cb_ref[...]

        gates = 0.5 * jnp.tanh(_dot(xc.astype(BF16), wg_ref[direction]) + bg_ref[direction]) + 0.5
        r_gate = gates[:, :ch]
        i_gate = gates[:, ch:]
        a = jnp.exp2(((-LRU_C * LOG2E) * _softplus(-lam_ref[direction])) * r_gate)
        gap = 1.0 - a * a
        root = jnp.where(gap > 0.0, gap * lax.rsqrt(gap), 0.0)
        bb = root * (i_gate * xc)

        steps = range(seg) if direction == 0 else range(seg - 1, -1, -1)
        h = jnp.zeros((grp, ch), F32)
        cum = jnp.ones((grp, ch), F32)
        h_loc = [None] * seg
        cum_loc = [None] * seg
        for j in steps:
            a_j = a[j * grp:(j + 1) * grp]
            h = a_j * h + bb[j * grp:(j + 1) * grp]
            cum = cum * a_j
            h_loc[j] = h
            cum_loc[j] = cum

        entering = [None] * LRU_SEG
        for s in (range(LRU_SEG) if direction == 0 else range(LRU_SEG - 1, -1, -1)):
            entering[s] = state
            state = h[s:s + 1] + cum[s:s + 1] * state
        entering = jnp.concatenate(entering, axis=0)
        o_ref[0, rows] = jnp.concatenate([h_loc[j] + cum_loc[j] * entering for j in range(seg)], axis=0)
        return state

    def run(n_sub):
        states = [carry_s[0], carry_s[1]]
        for r in range(n_sub):
            states[0] = sub_tile(0, r, n_sub, states[0])
            states[1] = sub_tile(1, n_sub - 1 - r, n_sub, states[1])
        carry_s[0] = states[0]
        carry_s[1] = states[1]

    n_blk = xf_ref.shape[1] // TOK
    if ctx_sub == n_blk:
        run(n_blk)
    else:
        pl.when(i == 0)(lambda: run(ctx_sub))
        pl.when(i > 0)(lambda: run(n_blk))


def _lru(xg, conv_w, conv_b, w_gate, b_gate, lam, s_len):
    bsz, t, two_ch = xg.shape
    ch = two_ch // 2
    n_lat, n_tiles, ctx_sub = _scan_blocks(s_len, t - s_len, MTOK)
    halo = 16
    per_tile = MTOK // halo
    tiles = [functools.partial(_lru_tile_index, n_lat=n_lat, n_tiles=n_tiles, direction=d) for d in range(2)]

    def stream_specs(tile_of):
        return [pl.BlockSpec((1, MTOK, ch), lambda b, i: (b, tile_of(i), 0)),
                pl.BlockSpec((1, halo, ch), lambda b, i: (b, jnp.maximum(tile_of(i) * per_tile - 1, 0), 0)),
                pl.BlockSpec((1, halo, ch),
                             lambda b, i: (b, jnp.minimum((tile_of(i) + 1) * per_tile, t // halo - 1), 0))]

    out_spec = lambda tile_of: pl.BlockSpec((1, MTOK, ch), lambda b, i: (b, tile_of(i), 0))
    return pl.pallas_call(
        functools.partial(_lru_kernel, n_lat=n_lat, n_tiles=n_tiles, ctx_sub=ctx_sub),
        grid=(bsz, n_tiles),
        in_specs=stream_specs(tiles[0]) + stream_specs(tiles[1]) + [
            _const_spec((4, ch)), _const_spec((1, ch)),
            _const_spec((2, ch, 2 * ch)), _const_spec((2, 1, 2 * ch)), _const_spec((2, 1, ch)),
            _const_spec((TOK, TOK)),
        ],
        out_specs=[out_spec(tiles[0]), out_spec(tiles[1])],
        out_shape=[jax.ShapeDtypeStruct((bsz, t, ch), F32)] * 2,
        scratch_shapes=[pltpu.VMEM((2, 1, ch), F32)],
        compiler_params=_cparams(("parallel", "arbitrary")),
        name="lru",
    )(xg, xg, xg, xg, xg, xg, conv_w, conv_b.reshape(1, ch), w_gate, b_gate, lam.reshape(2, 1, ch),
      jnp.asarray(_lru_row_permutation(), BF16))


def _block_diag(w):
    heads, n, _ = w.shape
    eye = jnp.eye(heads, dtype=w.dtype)
    return (eye[:, None, :, None] * w[:, :, None, :]).reshape(heads * n, heads * n)


def _gla_kernel(qf_ref, lrf_ref, qb_ref, lrb_ref, wa_ref, ba_ref, trif_ref, trib_ref, csum_ref, of_ref, ob_ref,
                state_s, *, ctx_sub):
    key = GLA_HEADS * GLA_DK
    val = GLA_HEADS * GLA_DV
    nchunk = TOK // GLA_CHUNK
    step = pl.program_id(1)

    @pl.when(step == 0)
    def _():
        state_s[...] = jnp.zeros_like(state_s)

    chunks = [slice(c * GLA_CHUNK, (c + 1) * GLA_CHUNK) for c in range(nchunk)]

    def prepare(direction, rows):
        qkvg_ref, lr_ref, tri_ref = (qf_ref, lrf_ref, trif_ref) if direction == 0 else (qb_ref, lrb_ref, trib_ref)
        tri = tri_ref[...]
        ends = [(c + 1) * GLA_CHUNK - 1 if direction == 0 else c * GLA_CHUNK for c in range(nchunk)]
        z = qkvg_ref[0, rows]
        q = z[:, :key] * (GLA_DK ** -0.5)
        k = z[:, key:2 * key]
        v = z[:, 2 * key:2 * key + val].astype(BF16)
        lr = lr_ref[0, rows][:, direction * GLA_RANK:(direction + 1) * GLA_RANK]
        logit = _dot(lr.astype(BF16), wa_ref[direction].astype(BF16)) + ba_ref[direction]
        log_a = (jnp.minimum(logit, 0.0) - jnp.log1p(jnp.exp(-jnp.abs(logit)))) / GLA_TAU

        g_hi, g_lo = _split_hi_lo(log_a)
        b = _dot(tri, g_hi) + _dot(tri, g_lo)
        q_in = (q * jnp.exp(b)).astype(BF16)
        k_in = (k * jnp.exp(-b)).astype(BF16)
        b_end = jnp.concatenate([jnp.broadcast_to(b[e:e + 1], (GLA_CHUNK, key)) for e in ends], axis=0)
        k_out_t = (k * jnp.exp(b_end - b)).T.astype(BF16)
        gt_hi, gt_lo = _split_hi_lo(log_a.T)
        decay = jnp.exp(_dot(gt_hi, csum_ref[...]) + _dot(gt_lo, csum_ref[...]))
        causal = tri > 0
        intra, update = [], []
        for h in range(GLA_HEADS):
            ks = slice(h * GLA_DK, (h + 1) * GLA_DK)
            v_h = v[:, h * GLA_DV:(h + 1) * GLA_DV]
            att = jnp.where(causal, _dot_nt(q_in[:, ks], k_in[:, ks]), 0.0)
            intra.append(_dot(att.astype(BF16), v_h))
            update.append([_dot(k_out_t[ks, rc], v_h[rc]) for rc in chunks])
        return q_in, decay, intra, update

    def recur(direction, rows, prepared, states):
        o_ref = of_ref if direction == 0 else ob_ref
        q_in, decay, intra, update = prepared
        chunk_order = range(nchunk) if direction == 0 else range(nchunk - 1, -1, -1)
        new_states = []
        for h in range(GLA_HEADS):
            ks = slice(h * GLA_DK, (h + 1) * GLA_DK)
            state = states[h]
            inter = [None] * nchunk
            for c in chunk_order:
                inter[c] = _dot(q_in[chunks[c], ks], state.astype(BF16))
                state = decay[ks, c * GLA_DV:(c + 1) * GLA_DV] * state + update[h][c]
            new_states.append(state)
            o_ref[0, rows, h * GLA_DV:(h + 1) * GLA_DV] = intra[h] + jnp.concatenate(inter, axis=0)
        return new_states

    def run(n_sub):
        subs = [slice(r * TOK, (r + 1) * TOK) for r in range(n_sub)]
        states = [[state_s[d, h] for h in range(GLA_HEADS)] for d in range(2)]
        work = [(d, subs[r] if d == 0 else subs[n_sub - 1 - r]) for r in range(n_sub) for d in range(2)]
        prepared = [prepare(d, rows) for d, rows in work]
        for (d, rows), prep in zip(work, prepared):
            states[d] = recur(d, rows, prep, states[d])
        for d in range(2):
            for h in range(GLA_HEADS):
                state_s[d, h] = states[d][h]

    n_blk = qf_ref.shape[1] // TOK
    if ctx_sub == n_blk:
        run(n_blk)
    else:
        pl.when(step == 0)(lambda: run(ctx_sub))
        pl.when(step > 0)(lambda: run(n_blk))


def _scan_blocks(s_len, c_len, blk):
    assert s_len % blk == 0 and c_len % TOK == 0 and c_len <= blk
    return s_len // blk, s_len // blk + 1, c_len // TOK


def _gla(qkvg, lr, wa2, ba, s_len):
    bsz, t, width = qkvg.shape
    key = GLA_HEADS * GLA_DK
    val = GLA_HEADS * GLA_DV
    n_lat, n_tiles, ctx_sub = _scan_blocks(s_len, t - s_len, MTOK)
    tiles = [functools.partial(_lru_tile_index, n_lat=n_lat, n_tiles=n_tiles, direction=d) for d in range(2)]
    pos = np.arange(TOK)
    same = (pos[:, None] // GLA_CHUNK) == (pos[None, :] // GLA_CHUNK)
    tri = [jnp.asarray(same & m, BF16) for m in (pos[None, :] <= pos[:, None], pos[None, :] >= pos[:, None])]
    chunk_sum = jnp.asarray((pos[:, None] // GLA_CHUNK) == (np.arange(TOK // GLA_CHUNK * GLA_DV)[None, :] // GLA_DV), BF16)
    blk = lambda width, tile_of: pl.BlockSpec((1, MTOK, width), lambda b, i: (b, tile_of(i), 0))
    return pl.pallas_call(
        functools.partial(_gla_kernel, ctx_sub=ctx_sub),
        grid=(bsz, n_tiles),
        in_specs=[
            blk(width, tiles[0]), blk(lr.shape[-1], tiles[0]), blk(width, tiles[1]), blk(lr.shape[-1], tiles[1]),
            _const_spec((2, GLA_RANK, key)), _const_spec((2, 1, key)),
            _const_spec((TOK, TOK)), _const_spec((TOK, TOK)), _const_spec(chunk_sum.shape),
        ],
        out_specs=[blk(val, tiles[0]), blk(val, tiles[1])],
        out_shape=[jax.ShapeDtypeStruct((bsz, t, val), F32)] * 2,
        scratch_shapes=[pltpu.VMEM((2, GLA_HEADS, GLA_DK, GLA_DV), F32)],
        compiler_params=_cparams(("parallel", "arbitrary")),
        name="gla",
    )(qkvg, lr, qkvg, lr, wa2, ba.reshape(2, 1, key), tri[0], tri[1], chunk_sum)


def _rope_tables(s_len, c_len, reps):
    quarter = GQA_DH // 4
    inv = ROPE_THETA ** (-np.arange(quarter, dtype=np.float32) / quarter)
    t = np.arange(s_len)
    ang_r = (t // GRID_W).astype(np.float32)[:, None] * inv[None, :]
    ang_c = (t % GRID_W).astype(np.float32)[:, None] * inv[None, :]
    cos = np.concatenate([np.cos(ang_r)] * 2 + [np.cos(ang_c)] * 2, axis=1)
    sin = np.concatenate([-np.sin(ang_r), np.sin(ang_r), -np.sin(ang_c), np.sin(ang_c)], axis=1)
    cos = np.concatenate([cos, np.ones((c_len, GQA_DH), np.float32)], axis=0)
    sin = np.concatenate([sin, np.zeros((c_len, GQA_DH), np.float32)], axis=0)
    return (jnp.asarray(np.tile(cos, (1, reps)), F32), jnp.asarray(np.tile(sin, (1, reps)), F32))


def _rope_swap(x):
    lane = lax.broadcasted_iota(jnp.int32, x.shape, 1)
    return jnp.where(lane % 32 < 16, pltpu.roll(x, 128 - 16, axis=1), pltpu.roll(x, 16, axis=1))


def _norm_rope(x, nw, cos, sin, gmean):
    hi, lo = _split_hi_lo(x * x)
    ms = _dot(hi, gmean) + _dot(lo, gmean)
    y = x * lax.rsqrt(ms + EPS) * nw
    slabs = [_rope_swap(y[:, j:j + 128]) for j in range(0, y.shape[1], 128)]
    swapped = slabs[0] if len(slabs) == 1 else jnp.concatenate(slabs, axis=1)
    return y * cos + swapped * sin


def _group_mean_matrix(n, group):
    idx = np.arange(n) // group
    return jnp.asarray((idx[:, None] == idx[None, :]) / group, BF16)


def _inproj_odd_kernel(h_ref, nw_ref, mod_ref, w_ref, qw_ref, kw_ref, cos_ref, sin_ref, gq_ref, gk_ref,
                       qkvg_ref, lr_ref, q_ref, k_ref, v_ref, *, gla_w):
    qd = GQA_HEADS * GQA_DH
    kd = GQA_KV_HEADS * GQA_DH
    mod = mod_ref[0]
    u = _rms(h_ref[0], nw_ref[...]) * (1.0 + mod[1:2]) + mod[0:1]
    z_all = _dot(u.astype(BF16), w_ref[...])
    qkvg_ref[0] = z_all[:, :gla_w]
    lr_ref[0] = z_all[:, gla_w + qd + 2 * kd:]
    z = z_all[:, gla_w:gla_w + qd + 2 * kd]
    cos_k = cos_ref[...]
    sin_k = sin_ref[...]
    reps = qd // kd
    cos_q = jnp.concatenate([cos_k] * reps, axis=1)
    sin_q = jnp.concatenate([sin_k] * reps, axis=1)
    q = _norm_rope(z[:, :qd], qw_ref[...], cos_q, sin_q, gq_ref[...]) * (LOG2E * GQA_DH ** -0.5)
    k = _norm_rope(z[:, qd:qd + kd], kw_ref[...], cos_k, sin_k, gk_ref[...])
    v = z[:, qd + kd:qd + 2 * kd]
    q_ref[0] = q.astype(BF16)
    ones = jnp.ones((z.shape[0], LANES - GQA_DH), BF16)
    for h in range(GQA_KV_HEADS):
        k_ref[0, h] = k[:, h * GQA_DH:(h + 1) * GQA_DH].astype(BF16)
        v_ref[0, h] = jnp.concatenate([v[:, h * GQA_DH:(h + 1) * GQA_DH].astype(BF16), ones], axis=1)


def _inproj_odd(h, norm_w, mods, w, q_norm_w, k_norm_w, s_len, c_len, gla_w):
    bsz, t, d = h.shape
    n = w.shape[1]
    qd = GQA_HEADS * GQA_DH
    kd = GQA_KV_HEADS * GQA_DH
    n_lr = n - gla_w - qd - 2 * kd
    cos, sin = _rope_tables(s_len, c_len, GQA_KV_HEADS)
    tok_spec = lambda width: pl.BlockSpec((1, DTOK, width), lambda b, i: (b, i, 0))
    head_spec = lambda width: pl.BlockSpec((1, GQA_KV_HEADS, DTOK, width), lambda b, i: (b, 0, i, 0))
    return pl.pallas_call(
        functools.partial(_inproj_odd_kernel, gla_w=gla_w),
        grid=(bsz, pl.cdiv(t, DTOK)),
        in_specs=[
            tok_spec(d), _const_spec((1, d)), _mod_spec(d, s_len // DTOK), _const_spec((d, n)),
            _const_spec((1, qd)), _const_spec((1, kd)),
            pl.BlockSpec((DTOK, kd), lambda b, i: (i, 0)),
            pl.BlockSpec((DTOK, kd), lambda b, i: (i, 0)),
            _const_spec((qd, qd)), _const_spec((kd, kd)),
        ],
        out_specs=[tok_spec(gla_w), tok_spec(n_lr), tok_spec(qd), head_spec(GQA_DH), head_spec(LANES)],
        out_shape=[jax.ShapeDtypeStruct((bsz, t, gla_w), F32),
                   jax.ShapeDtypeStruct((bsz, t, n_lr), F32),
                   jax.ShapeDtypeStruct((bsz, t, qd), BF16),
                   jax.ShapeDtypeStruct((bsz, GQA_KV_HEADS, t, GQA_DH), BF16),
                   jax.ShapeDtypeStruct((bsz, GQA_KV_HEADS, t, LANES), BF16)],
        compiler_params=_cparams(("parallel", "parallel")),
        name="inproj_odd",
    )(h, norm_w.reshape(1, d), mods, w,
      jnp.tile(q_norm_w, GQA_HEADS).reshape(1, qd), jnp.tile(k_norm_w, GQA_KV_HEADS).reshape(1, kd),
      cos, sin, _group_mean_matrix(qd, GQA_DH), _group_mean_matrix(kd, GQA_DH))


def _flash_kernel(q_ref, k_ref, v_ref, o_ref, m_s, acc_s, *, group, tk, sub):
    tq = q_ref.shape[1]
    n_kv = k_ref.shape[2] // tk
    units = [(g, r) for g in range(group) for r in range(0, tq, sub)]
    m_s[...] = jnp.full_like(m_s, -jnp.inf)
    acc_s[...] = jnp.zeros_like(acc_s)

    def kv_step(j, carry):
        rows_k = pl.ds(pl.multiple_of(j * tk, tk), tk)
        k = k_ref[0, 0, rows_k, :]
        v = v_ref[0, 0, rows_k, :]

        def scores(u):
            g, r = units[u]
            return _dot_nt(q_ref[0, r:r + sub, g * GQA_DH:(g + 1) * GQA_DH], k)

        s_next = scores(0)
        for u in range(len(units)):
            tiles = _lane_tiles(s_next)
            if u + 1 < len(units):
                s_next = scores(u + 1)
            m_prev = m_s[u]
            m_new = jnp.maximum(m_prev, jnp.max(functools.reduce(jnp.maximum, tiles), axis=-1, keepdims=True))
            p = [jnp.exp2(t - m_new).astype(BF16) for t in tiles]
            acc_s[u] = jnp.exp2(m_prev - m_new) * acc_s[u] + _dot(jnp.concatenate(p, axis=1), v)
            m_s[u] = m_new
        return carry

    lax.fori_loop(0, n_kv, kv_step, 0)
    for u, (g, r) in enumerate(units):
        acc = acc_s[u]
        out = acc[:, :GQA_DH] / acc[:, GQA_DH:GQA_DH + 1]
        o_ref[0, r:r + sub, g * GQA_DH:(g + 1) * GQA_DH] = out.astype(o_ref.dtype)


KV_TILE_MAX = 3072
FLASH_SUB = 512


def _kv_tile(t):
    return max(n for n in range(LANES, KV_TILE_MAX + 1, LANES) if t % n == 0)


def _flash(q, k, v, s_len, tq, tk):
    bsz, _, t, dh = k.shape
    sub = FLASH_SUB
    assert t % tk == 0 and s_len % tq == 0 and tq % sub == 0
    group = GQA_HEADS // GQA_KV_HEADS
    n_units = group * (tq // sub)
    return pl.pallas_call(
        functools.partial(_flash_kernel, group=group, tk=tk, sub=sub),
        grid=(bsz, GQA_KV_HEADS, s_len // tq),
        in_specs=[
            pl.BlockSpec((1, tq, group * dh), lambda b, h, i: (b, i, h)),
            pl.BlockSpec((1, 1, t, dh), lambda b, h, i: (b, h, 0, 0)),
            pl.BlockSpec((1, 1, t, LANES), lambda b, h, i: (b, h, 0, 0)),
        ],
        out_specs=pl.BlockSpec((1, tq, group * dh), lambda b, h, i: (b, i, h)),
        out_shape=jax.ShapeDtypeStruct((bsz, s_len, GQA_HEADS * dh), BF16),
        scratch_shapes=[pltpu.VMEM((n_units, sub, LANES), F32), pltpu.VMEM((n_units, sub, LANES), F32)],
        compiler_params=_cparams(("parallel", "parallel", "arbitrary")),
        name="gqa_flash",
    )(q, k, v)


def kernel(x, c, ctx, c_ctx, norm1_w, norm2_w, w_mod, b_mod, w_ff1, w_ff2,
           w_in_even, na_rpb, lru_conv_w, lru_conv_b, lru_wa, lru_ba, lru_wx, lru_bx, lru_lambda, w_out_even,
           w_in_odd, gla_wa2, gla_ba, gla_norm_w, gqa_q_norm_w, gqa_k_norm_w, w_out_odd, final_norm_w):
    bsz, s_len, d = x.shape
    c_len = ctx.shape[1]
    depth = w_mod.shape[0]
    assert s_len % TOK == 0 and c_len % TOK == 0 and depth == 2

    pad = (-(bsz + 1)) % 8
    cvec = jnp.concatenate([c_ctx[None], c, jnp.zeros((pad, d), F32)], axis=0)
    mods = _modulation(cvec, w_mod, b_mod)
    h = (x, ctx)
    t = s_len + c_len

    na_w = NA_HEADS * NA_DH
    ch = lru_lambda.shape[-1]
    col_scale = jnp.where(jnp.arange(w_in_even.shape[-1]) < na_w, LOG2E * NA_DH ** -0.5, 1.0)
    qk, v1, xg = _inproj(h, norm1_w[0], mods[0], (w_in_even[0] * col_scale).astype(BF16),
                         ((0, 2 * na_w, 0), (2 * na_w, 3 * na_w, NA_DH), (3 * na_w, 3 * na_w + 2 * ch, 0)),
                         (BF16, BF16, BF16), s_len, t)
    na_out = _na(qk, v1, na_rpb[0], s_len, c_len)
    w_gate = jnp.stack([jnp.concatenate([_block_diag(lru_wa[0, dr]), _block_diag(lru_wx[0, dr])], axis=1)
                        for dr in range(2)])
    w_gate = (0.5 * w_gate).astype(BF16)
    b_gate = 0.5 * jnp.concatenate([lru_ba[0], lru_bx[0]], axis=-1).reshape(2, 1, 2 * ch)
    lru_f, lru_b = _lru(xg, lru_conv_w[0], lru_conv_b[0], w_gate, b_gate, lru_lambda[0], s_len)
    mixes = (_mix_plain(na_out), _mix_lru(lru_f, lru_b, xg))
    w_ff1_bf, w_ff2_bf = _to_bf16(w_ff1), _to_bf16(w_ff2)
    h = _mlp(h, mixes, mods[0], w_out_even[0].astype(BF16), norm2_w[0],
             w_ff1_bf, w_ff2_bf, 0, final_norm_w, s_len, s_len + c_len, final=False)

    key = GLA_HEADS * GLA_DK
    val = GLA_HEADS * GLA_DV
    gla_w = 2 * key + 2 * val
    w_in = w_in_odd[0]
    w_in = jnp.concatenate([w_in[:, :gla_w], w_in[:, gla_w + 2 * GLA_RANK:], w_in[:, gla_w:gla_w + 2 * GLA_RANK]],
                           axis=1).astype(BF16)
    qkvg, lr, q, k, v = _inproj_odd(h, norm1_w[1], mods[1], w_in, gqa_q_norm_w[0], gqa_k_norm_w[0],
                                    s_len, c_len, gla_w)
    gla_f, gla_b = _gla(qkvg, lr, gla_wa2[0], gla_ba[0], s_len)
    gqa_out = _flash(q, k, v, s_len, tq=2 * FLASH_SUB, tk=_kv_tile(t))
    mixes = (_mix_gla(gla_f, gla_b, qkvg, gla_norm_w[0]), _mix_plain(gqa_out))
    return _mlp(h, mixes, mods[1], w_out_odd[0].astype(BF16), norm2_w[1],
                w_ff1_bf, w_ff2_bf, 1, final_norm_w, s_len, s_len, final=True)
```
